```python
import math
import jax
import jax.numpy as jnp
from jax import lax
import numpy as np

D_MODEL = 1024
BATCH = 8
SEQ = 4096
DEPTH = 4

CTX_LEN = 256
GRID_W = 64
HEAD_DIM = 64
W_HY = D_MODEL // 4
W_RW = D_MODEL // 4
W_WA = D_MODEL // 4
W_FA = D_MODEL // 4
D_MIX = W_HY + W_RW + W_WA + W_FA
HY_EMB = 33
HY_BANDS = (HY_EMB - 1) // 2
HY_FFN = 64
HY_FAST_DECAY = 0.3
HY_SLOW_DECAY = 1.5
HY_DECAY_TARGET = 1e-2
RW_HEADS = W_RW // HEAD_DIM
RW_LORA_W = 64
RW_LORA_A = 64
RW_GN_EPS = 64e-5
WA_HEADS = W_WA // HEAD_DIM
WA_KV = WA_HEADS // 2
WINDOW = 128
Q_BLOCK = 128
FA_HEADS = W_FA // HEAD_DIM
FA_KV = FA_HEADS // 2
ROPE_THETA = 10000.0
NORM_EPS = 1e-6
NEG_INF = -1e30
SPLITS = (3 * W_HY, W_HY, 3 * W_RW + RW_LORA_W + RW_LORA_A, W_RW,
          W_WA + 2 * WA_KV * HEAD_DIM, W_WA, W_FA + 2 * FA_KV * HEAD_DIM, W_FA)
D_IN = sum(SPLITS)
SPLIT_IDX = tuple(int(i) for i in np.cumsum(SPLITS)[:-1])

kernel_name = 'hymba_style_flow_hybrid_block'


def rms_norm(x, g):
    xf = x.astype(jnp.float32)
    y = xf * lax.rsqrt(jnp.mean(xf * xf, axis=-1, keepdims=True) + NORM_EPS)
    return (y * g.astype(jnp.float32)).astype(x.dtype)


def short_conv(z, w):
    L = z.shape[1]
    zp = jnp.pad(z, ((0, 0), (1, 1), (0, 0)))
    return zp[:, :L] * w[0] + zp[:, 1:L + 1] * w[1] + zp[:, 2:] * w[2]


def heads(t):
    return t.reshape(t.shape[:-1] + (-1, HEAD_DIM))


def group_q(t, n_kv):
    return t.reshape(t.shape[:-1] + (n_kv, -1, HEAD_DIM))


def grid_rope_tables(rows):
    row = jnp.repeat(jnp.arange(rows, dtype=jnp.float32), GRID_W)
    col = jnp.tile(jnp.arange(GRID_W, dtype=jnp.float32), rows)
    n_freq = HEAD_DIM // 4
    inv_freq = ROPE_THETA ** (-jnp.arange(n_freq, dtype=jnp.float32) / n_freq)
    ang = jnp.stack([row[:, None] * inv_freq, col[:, None] * inv_freq], axis=1)
    return jnp.cos(ang), jnp.sin(ang)


def apply_rope(x, cos, sin):
    L = x.shape[1]
    bshape = (1, L) + (1,) * (x.ndim - 3) + cos.shape[1:]
    c = cos.reshape(bshape)
    s = sin.reshape(bshape)
    xr = x.astype(jnp.float32).reshape(x.shape[:-1] + (2, 2, HEAD_DIM // 4))
    xa, xb = xr[..., 0, :], xr[..., 1, :]
    out = jnp.stack([xa * c - xb * s, xb * c + xa * s], axis=-2)
    return out.reshape(x.shape).astype(x.dtype)


def hyena_filters(L, fw1, fb1, freq, fw2, fb2, fw3):
    f32 = lambda a: a.astype(jnp.float32)
    t = jnp.linspace(0.0, 1.0, L, dtype=jnp.float32)[:, None]
    w = (2.0 * math.pi / L) * jnp.arange(L, dtype=jnp.float32)[:, None]
    f = jnp.linspace(1e-4, HY_BANDS - 1, HY_BANDS, dtype=jnp.float32)[None, :]
    z = jnp.concatenate([t, jnp.cos(f * w), jnp.sin(f * w)], axis=-1)
    h = jnp.sin(f32(freq) * (z @ f32(fw1) + f32(fb1)))
    h = jnp.sin(f32(freq) * (h @ f32(fw2) + f32(fb2)))
    h = (h @ f32(fw3)).reshape(L, 2, 2, W_HY)
    max_decay = math.log(HY_DECAY_TARGET) / HY_FAST_DECAY
    min_decay = math.log(HY_DECAY_TARGET) / HY_SLOW_DECAY
    deltas = jnp.linspace(min_decay, max_decay, W_HY, dtype=jnp.float32)
    h = h * jnp.exp(-t * jnp.abs(deltas))[:, None, None, :]
    return h / jnp.sum(jnp.abs(h), axis=(0, 2), keepdims=True)


def two_sided_fftconv(u, h_fwd, h_bwd, bias):
    L = u.shape[1]
    k = jnp.concatenate([h_fwd, jnp.zeros_like(h_fwd[:1]), h_bwd[:0:-1]], axis=0)
    U = jnp.fft.rfft(u, n=2 * L, axis=1)
    K = jnp.fft.rfft(k, n=2 * L, axis=0)
    y = jnp.fft.irfft(U * K[None], n=2 * L, axis=1)[:, :L]
    return y + u * bias


def hyena_branch(z, conv_w, fw1, fb1, freq, fw2, fb2, fw3, bias):
    h = hyena_filters(z.shape[1], fw1, fb1, freq, fw2, fb2, fw3)
    zc = short_conv(z.astype(jnp.float32), conv_w.astype(jnp.float32))
    v, x1, x2 = jnp.split(zc, 3, axis=-1)
    bias = bias.astype(jnp.float32)
    y = x1 * two_sided_fftconv(v, h[:, 0, 0], h[:, 0, 1], bias[0])
    return x2 * two_sided_fftconv(y, h[:, 1, 0], h[:, 1, 1], bias[1])


def rwkv_prep(z, conv_w, w0, w_up, a0, a_up, k_k, k_a):
    z = short_conv(z.astype(jnp.float32), conv_w.astype(jnp.float32))
    r, k, v, w_low, a_low = jnp.split(
        z, [W_RW, 2 * W_RW, 3 * W_RW, 3 * W_RW + RW_LORA_W], axis=-1)
    kk = heads(k * k_k)
    kk = kk / jnp.maximum(jnp.linalg.norm(kk, axis=-1, keepdims=True), 1e-12)
    w_low = jnp.tanh(w_low)
    per_dir = []
    for d in range(2):
        w_log = -jax.nn.softplus(-(w0[d] + w_low @ w_up[d])) - 0.5
        a = jax.nn.sigmoid(a0[d] + a_low @ a_up[d])
        k_d = k * (1.0 + (a - 1.0) * k_a)
        per_dir.append((heads(jnp.exp(-jnp.exp(w_log))), heads(a), heads(k_d)))
    return heads(r), heads(v), kk, per_dir


def rwkv7_scan(s0, r, w, k, v, kk, a, reverse):
    def step(S, inp):
        r_t, w_t, k_t, v_t, kk_t, a_t = inp
        sa = jnp.einsum('bhvk,bhk->bhv', S, -kk_t)
        S = (S * w_t[:, :, None, :] + sa[..., None] * (kk_t * a_t)[:, :, None, :]
             + v_t[..., None] * k_t[:, :, None, :])
        return S, jnp.einsum('bhvk,bhk->bhv', S, r_t)
    xs = tuple(jnp.moveaxis(t, 1, 0) for t in (r, w, k, v, kk, a))
    S, ys = lax.scan(step, s0, xs, reverse=reverse)
    return S, jnp.moveaxis(ys, 0, 1)


def rwkv_readout(y, r, v, k_fwd, k_bwd, r_k, ln_g, ln_b):
    mu = jnp.mean(y, axis=-1, keepdims=True)
    var = jnp.mean(jnp.square(y - mu), axis=-1, keepdims=True)
    yn = ((y - mu) * lax.rsqrt(var + RW_GN_EPS)).reshape(y.shape[:-2] + (-1,)) * ln_g + ln_b
    bonus = jnp.sum(r * (k_fwd + k_bwd) * heads(r_k), axis=-1, keepdims=True) * v
    return yn + bonus.reshape(yn.shape)


def rwkv_mixer(z_ctx, z_lat, conv_w, w0, w_up, a0, a_up, k_k, k_a, r_k, ln_g, ln_b, with_ctx_out):
    rc, vc, kkc, dirs_c = rwkv_prep(z_ctx, conv_w, w0, w_up, a0, a_up, k_k, k_a)
    rl, vl, kkl, dirs_l = rwkv_prep(z_lat, conv_w, w0, w_up, a0, a_up, k_k, k_a)
    s0 = jnp.zeros((z_lat.shape[0], RW_HEADS, HEAD_DIM, HEAD_DIM), jnp.float32)
    ys_c, ys_l = [], []
    for d, reverse in enumerate((False, True)):
        wc, ac, kc = dirs_c[d]
        wl, al, kl = dirs_l[d]
        s_ctx, y_c = rwkv7_scan(s0, rc, wc, kc, vc, kkc, ac, reverse)
        _, y_l = rwkv7_scan(s_ctx, rl, wl, kl, vl, kkl, al, reverse)
        ys_c.append(y_c)
        ys_l.append(y_l)
    out_l = rwkv_readout(ys_l[0] + ys_l[1], rl, vl, dirs_l[0][2], dirs_l[1][2], r_k, ln_g, ln_b)
    if not with_ctx_out:
        return out_l, None
    out_c = rwkv_readout(ys_c[0] + ys_c[1], rc, vc, dirs_c[0][2], dirs_c[1][2], r_k, ln_g, ln_b)
    return out_l, out_c


def full_attention(q, k, v, sink):
    s = jnp.einsum('bqkgd,bskd->bkgqs', q, k, preferred_element_type=jnp.float32) * HEAD_DIM ** -0.5
    if sink is not None:
        B, KV, G, Q, _ = s.shape
        sk = jnp.broadcast_to(sink.astype(jnp.float32).reshape(1, KV, G, 1, 1), (B, KV, G, Q, 1))
        p = jax.nn.softmax(jnp.concatenate([s, sk], axis=-1), axis=-1)[..., :-1]
    else:
        p = jax.nn.softmax(s, axis=-1)
    return jnp.einsum('bkgqs,bskd->bqkgd', p, v.astype(jnp.float32))


def window_attention(q, k, v, kc, vc, sink):
    B, L, KV, G, HD = q.shape
    nb = L // Q_BLOCK
    side = WINDOW // Q_BLOCK
    band = (2 * side + 1) * Q_BLOCK
    qb = q.reshape(B, nb, Q_BLOCK, KV, G, HD)

    def banded(t):
        tp = jnp.pad(t, ((0, 0), (WINDOW, WINDOW), (0, 0), (0, 0))).reshape(
            B, nb + 2 * side, Q_BLOCK, KV, HD)
        return jnp.concatenate([tp[:, j:j + nb] for j in range(2 * side + 1)], axis=2)

    kw, vw = banded(k), banded(v)
    scale = HEAD_DIM ** -0.5
    s_loc = jnp.einsum('bnqkgd,bnskd->bnkgqs', qb, kw, preferred_element_type=jnp.float32) * scale
    qpos = jnp.arange(nb)[:, None] * Q_BLOCK + jnp.arange(Q_BLOCK)[None, :]
    kpos = jnp.arange(nb)[:, None] * Q_BLOCK - WINDOW + jnp.arange(band)[None, :]
    valid = ((jnp.abs(qpos[:, :, None] - kpos[:, None, :]) <= WINDOW)
             & (kpos[:, None, :] >= 0) & (kpos[:, None, :] < L))
    s_loc = jnp.where(valid[None, :, None, None], s_loc, NEG_INF)
    s_ctx = jnp.einsum('bnqkgd,bckd->bnkgqc', qb, kc, preferred_element_type=jnp.float32) * scale
    sk = jnp.broadcast_to(sink.astype(jnp.float32).reshape(1, 1, KV, G, 1, 1),
                          (B, nb, KV, G, Q_BLOCK, 1))
    p = jax.nn.softmax(jnp.concatenate([s_loc, s_ctx, sk], axis=-1), axis=-1)
    out = (jnp.einsum('bnkgqs,bnskd->bnqkgd', p[..., :band], vw.astype(jnp.float32))
           + jnp.einsum('bnkgqc,bckd->bnqkgd', p[..., band:-1], vc.astype(jnp.float32)))
    return out.reshape(B, L, KV * G * HD)


def dense_attention(q, k_all, v_all):
    B, L, KV, G, HD = q.shape
    nb = L // Q_BLOCK
    qb = jnp.moveaxis(q.reshape(B, nb, Q_BLOCK, KV, G, HD), 1, 0)
    out = lax.map(lambda qi: full_attention(qi, k_all, v_all, None), qb)
    return jnp.moveaxis(out, 0, 1).reshape(B, L, KV * G * HD)


def mixer_layer(x, ctx, c, c_ctx, mod_w, mod_b, norm_g, w_in, w_out,
                hy_conv, hy_fw1, hy_fb1, hy_freq, hy_fw2, hy_fb2, hy_fw3, hy_bias,
                rw_conv, rw_w0, rw_w_up, rw_a0, rw_a_up, rw_k_k, rw_k_a, rw_r_k, rw_ln_g, rw_ln_b,
                wa_sink, fa_q_norm, fa_k_norm, rope_cos, rope_sin, with_ctx_out):
    dt = x.dtype
    B, L, _ = x.shape
    C = ctx.shape[1]
    shift, scale, gate = jnp.split(jax.nn.silu(c) @ mod_w + mod_b, 3, axis=-1)
    shift_c, scale_c, gate_c = jnp.split(jax.nn.silu(c_ctx) @ mod_w + mod_b, 3, axis=-1)
    h_l = rms_norm(x, norm_g) * (1.0 + scale[:, None]) + shift[:, None]
    h_c = rms_norm(ctx, norm_g) * (1.0 + scale_c) + shift_c
    hy_l, hyg_l, rw_l, rwg_l, wa_l, wag_l, fa_l, fag_l = jnp.split(h_l @ w_in, SPLIT_IDX, axis=-1)
    hy_c, hyg_c, rw_c, rwg_c, wa_c, wag_c, fa_c, fag_c = jnp.split(h_c @ w_in, SPLIT_IDX, axis=-1)

    a_l = hyena_branch(hy_l, hy_conv, hy_fw1, hy_fb1, hy_freq, hy_fw2, hy_fb2, hy_fw3, hy_bias) * jax.nn.silu(hyg_l)

    b_l, b_c = rwkv_mixer(rw_c, rw_l, rw_conv, rw_w0, rw_w_up, rw_a0, rw_a_up, rw_k_k, rw_k_a,
                          rw_r_k, rw_ln_g, rw_ln_b, with_ctx_out)
    b_l = b_l * jax.nn.silu(rwg_l)

    wa_idx = [W_WA, W_WA + WA_KV * HEAD_DIM]
    q_w, k_w, v_w = jnp.split(wa_l, wa_idx, axis=-1)
    q_cw, k_cw, v_cw = jnp.split(wa_c, wa_idx, axis=-1)
    k_cw, v_cw = heads(k_cw), heads(v_cw)
    c_l = window_attention(apply_rope(group_q(q_w, WA_KV), rope_cos, rope_sin),
                           apply_rope(heads(k_w), rope_cos, rope_sin), heads(v_w),
                           k_cw, v_cw, wa_sink) * jax.nn.silu(wag_l)

    fa_idx = [W_FA, W_FA + FA_KV * HEAD_DIM]
    q_f, k_f, v_f = jnp.split(fa_l, fa_idx, axis=-1)
    q_cf, k_cf, v_cf = jnp.split(fa_c, fa_idx, axis=-1)
    k_cf, v_cf = rms_norm(heads(k_cf), fa_k_norm), heads(v_cf)
    q_f = apply_rope(rms_norm(group_q(q_f, FA_KV), fa_q_norm), rope_cos, rope_sin)
    k_f = apply_rope(rms_norm(heads(k_f), fa_k_norm), rope_cos, rope_sin)
    k_all = jnp.concatenate([k_cf, k_f], axis=1)
    v_all = jnp.concatenate([v_cf, heads(v_f)], axis=1)
    d_l = dense_attention(q_f, k_all, v_all) * jax.nn.silu(fag_l)

    mix_l = jnp.concatenate([a_l, b_l, c_l, d_l], axis=-1).astype(dt)
    x = x + (gate[:, None] * (mix_l @ w_out)).astype(dt)
    if not with_ctx_out:
        return x, None

    a_c = hyena_branch(hy_c, hy_conv, hy_fw1, hy_fb1, hy_freq, hy_fw2, hy_fb2, hy_fw3, hy_bias) * jax.nn.silu(hyg_c)
    b_c = b_c * jax.nn.silu(rwg_c)
    c_c = full_attention(group_q(q_cw, WA_KV), k_cw, v_cw, wa_sink).reshape(B, C, W_WA) * jax.nn.silu(wag_c)
    d_c = full_attention(rms_norm(group_q(q_cf, FA_KV), fa_q_norm), k_cf, v_cf, None).reshape(B, C, W_FA) * jax.nn.silu(fag_c)
    mix_c = jnp.concatenate([a_c, b_c, c_c, d_c], axis=-1).astype(ctx.dtype)
    ctx = ctx + (gate_c * (mix_c @ w_out)).astype(ctx.dtype)
    return x, ctx


def setup_inputs(seed: int = 0) -> dict:
    key = jax.random.key(seed)
    ks = jax.random.split(key, 31)
    f32 = jnp.float32

    def nrm(k, shape, s):
        return jax.random.normal(k, shape, f32) * s

    return {
        'x': nrm(ks[0], (BATCH, SEQ, D_MODEL), 1.0),
        'c': nrm(ks[1], (BATCH, D_MODEL), 1.0),
        'ctx': nrm(ks[2], (BATCH, CTX_LEN, D_MODEL), 1.0),
        'c_ctx': nrm(ks[3], (D_MODEL,), 1.0),
        'mod_w': nrm(ks[4], (DEPTH, D_MODEL, 3 * D_MODEL), 0.5 * D_MODEL ** -0.5),
        'mod_b': nrm(ks[5], (DEPTH, 3 * D_MODEL), 0.02),
        'norm_g': 1.0 + nrm(ks[6], (DEPTH, D_MODEL), 0.02),
        'w_in': nrm(ks[7], (DEPTH, D_MODEL, D_IN), D_MODEL ** -0.5),
        'w_out': nrm(ks[8], (DEPTH, D_MIX, D_MODEL), D_MIX ** -0.5),
        'hy_conv': nrm(ks[9], (DEPTH, 3, 3 * W_HY), 3 ** -0.5),
        'hy_fw1': nrm(ks[10], (DEPTH, HY_EMB, HY_FFN), HY_EMB ** -0.5),
        'hy_fb1': nrm(ks[11], (DEPTH, HY_FFN), 0.1),
        'hy_freq': 1.0 + nrm(ks[12], (DEPTH, HY_FFN), 0.1),
        'hy_fw2': nrm(ks[13], (DEPTH, HY_FFN, HY_FFN), HY_FFN ** -0.5),
        'hy_fb2': nrm(ks[14], (DEPTH, HY_FFN), 0.1),
        'hy_fw3': nrm(ks[15], (DEPTH, HY_FFN, 4 * W_HY), HY_FFN ** -0.5),
        'hy_bias': nrm(ks[16], (DEPTH, 2, W_HY), 1.0),
        'rw_conv': nrm(ks[17], (DEPTH, 3, 3 * W_RW + RW_LORA_W + RW_LORA_A), 3 ** -0.5),
        'rw_w0': jnp.linspace(-6.0, -1.0, W_RW, dtype=f32) + nrm(ks[18], (DEPTH, 2, W_RW), 0.1),
        'rw_w_up': nrm(ks[19], (DEPTH, 2, RW_LORA_W, W_RW), 0.1),
        'rw_a0': nrm(ks[20], (DEPTH, 2, W_RW), 0.1),
        'rw_a_up': nrm(ks[21], (DEPTH, 2, RW_LORA_A, W_RW), 0.1),
        'rw_k_k': 0.85 + nrm(ks[22], (DEPTH, W_RW), 0.02),
        'rw_k_a': 1.0 + nrm(ks[23], (DEPTH, W_RW), 0.02),
        'rw_r_k': nrm(ks[24], (DEPTH, W_RW), 0.1),
        'rw_ln_g': 1.0 + nrm(ks[25], (DEPTH, W_RW), 0.02),
        'rw_ln_b': nrm(ks[26], (DEPTH, W_RW), 0.02),
        'wa_sink': nrm(ks[27], (DEPTH, WA_HEADS), 0.5),
        'fa_q_norm': 1.0 + nrm(ks[28], (DEPTH, HEAD_DIM), 0.02),
        'fa_k_norm': 1.0 + nrm(ks[29], (DEPTH, HEAD_DIM), 0.02),
        'final_g': 1.0 + nrm(ks[30], (D_MODEL,), 0.02),
    }


def reference(x, c, ctx, c_ctx, mod_w, mod_b, norm_g, w_in, w_out,
              hy_conv, hy_fw1, hy_fb1, hy_freq, hy_fw2, hy_fb2, hy_fw3, hy_bias,
              rw_conv, rw_w0, rw_w_up, rw_a0, rw_a_up, rw_k_k, rw_k_a, rw_r_k, rw_ln_g, rw_ln_b,
              wa_sink, fa_q_norm, fa_k_norm, final_g):
    rows = x.shape[1] // GRID_W
    rope_cos, rope_sin = grid_rope_tables(rows)
    for l in range(DEPTH):
        x, ctx = mixer_layer(
            x, ctx, c, c_ctx, mod_w[l], mod_b[l], norm_g[l], w_in[l], w_out[l],
            hy_conv[l], hy_fw1[l], hy_fb1[l], hy_freq[l], hy_fw2[l], hy_fb2[l], hy_fw3[l], hy_bias[l],
            rw_conv[l], rw_w0[l], rw_w_up[l], rw_a0[l], rw_a_up[l], rw_k_k[l], rw_k_a[l], rw_r_k[l],
            rw_ln_g[l], rw_ln_b[l], wa_sink[l], fa_q_norm[l], fa_k_norm[l], rope_cos, rope_sin,
            l < DEPTH - 1)
    return rms_norm(x, final_g)
```

```python
import functools
import math

import jax
import jax.numpy as jnp
import numpy as np
from jax import lax
from jax.experimental import pallas as pl
from jax.experimental.pallas import tpu as pltpu

HEAD_DIM = 64
GRID_W = 64
WINDOW = 128
NORM_EPS = 1e-6
RW_GN_EPS = 64e-5
NEG_INF = -1e30
ROPE_THETA = 10000.0
HY_FAST_DECAY = 0.3
HY_SLOW_DECAY = 1.5
HY_DECAY_TARGET = 1e-2

ROW_TILE = 256
LANES = 128
FFT_N1 = 64
RW_CHUNK = 64
VMEM_LIMIT = 60 * 1024 * 1024

F32 = jnp.float32
BF16 = jnp.bfloat16
HI = lax.Precision.HIGHEST


def _dot(a, b):
    return jnp.dot(a, b, preferred_element_type=F32, precision=HI)


def _dot_nt(a, b):
    return lax.dot_general(a, b, (((1,), (1,)), ((), ())), preferred_element_type=F32, precision=HI)


def _dot_tn(a, b):
    return lax.dot_general(a, b, (((0,), (0,)), ((), ())), preferred_element_type=F32, precision=HI)


def _cparams(*sem):
    return pltpu.CompilerParams(dimension_semantics=sem, vmem_limit_bytes=VMEM_LIMIT)


def _const_spec(shape):
    return pl.BlockSpec(shape, lambda *_: (0,) * len(shape), pipeline_mode=pl.Buffered(1))


def _silu(x):
    return x * (1.0 / (1.0 + jnp.exp(-x)))


def _mod_kernel(c_ref, w_ref, b_ref, o_ref):
    o_ref[0] = _dot(_silu(c_ref[...]), w_ref[0]) + b_ref[0]


def _modulation(cond, mod_w, mod_b):
    depth, d, d3 = mod_w.shape
    rows = cond.shape[0]
    return pl.pallas_call(
        _mod_kernel,
        out_shape=jax.ShapeDtypeStruct((depth, rows, d3), F32),
        grid=(depth,),
        in_specs=[pl.BlockSpec((rows, d), lambda l: (0, 0)),
                  pl.BlockSpec((1, d, d3), lambda l: (l, 0, 0)),
                  pl.BlockSpec((1, 1, d3), lambda l: (l, 0, 0))],
        out_specs=pl.BlockSpec((1, rows, d3), lambda l: (l, 0, 0)),
        compiler_params=_cparams("arbitrary"),
        name="modulation",
    )(cond, mod_w, mod_b.reshape(depth, 1, d3))


def _blockreal(m):
    return np.block([[m.real, -m.imag], [m.imag, m.real]])


@functools.lru_cache(maxsize=None)
def _fft_tables(seq_len):
    n = 2 * seq_len
    n1, n2 = FFT_N1, n // FFT_N1
    h1 = n1 // 2
    j2 = np.arange(n2)[:, None, None]
    k1 = np.arange(n1)[None, :, None]
    t1 = np.exp(-2j * np.pi * (j2 * k1 / n + k1 * np.arange(n1)[None, None, :] / n1))
    t1_data = np.stack([_blockreal(t1[j][:, :h1]) for j in range(n2)])
    t1_real = np.concatenate([t1.real, t1.imag], axis=1)
    f2 = np.exp(-2j * np.pi * np.outer(np.arange(n2), np.arange(n2)) / n2)
    f2_fwd = _blockreal(f2)
    f2_inv = _blockreal(np.conj(f2))
    t4 = np.exp(2j * np.pi * (np.arange(h1)[None, :, None] * np.arange(n1)[None, None, :] / n1
                              + j2 * np.arange(n1)[None, None, :] / n)) / n
    t4 = np.stack([_blockreal(t4[j]) for j in range(n2)])
    as32 = lambda a: jnp.asarray(a, dtype=F32)
    return as32(t1_data), as32(t1_real), as32(f2_fwd), as32(f2_inv), as32(t4)


def _spectrum_kernel(k_ref, t1_ref, f2_ref, o_ref, a_ref, *, n1, n2):
    def stage1(j, carry):
        rows = k_ref[0, pl.ds(j, n1, stride=n2), :]
        a_ref[pl.ds(j, 2 * n1, stride=n2), :] = _dot(t1_ref[j], rows)
        return carry
    lax.fori_loop(0, n2, stage1, 0)

    def stage2(i, carry):
        re = a_ref[pl.ds(pl.multiple_of(i * n2, n2), n2), :]
        im = a_ref[pl.ds(pl.multiple_of((n1 + i) * n2, n2), n2), :]
        o_ref[0, i] = _dot(f2_ref[...], jnp.concatenate([re, im], axis=0))
        return carry
    lax.fori_loop(0, n1, stage2, 0)


def _filter_spectrum(kfilt):
    g, n, w = kfilt.shape
    n1, n2 = FFT_N1, n // FFT_N1
    _, t1_real, f2_fwd, _, _ = _fft_tables(n // 2)
    const = _const_spec
    return pl.pallas_call(
        functools.partial(_spectrum_kernel, n1=n1, n2=n2),
        out_shape=jax.ShapeDtypeStruct((g, n1, 2 * n2, w), F32),
        grid=(g, w // LANES),
        in_specs=[pl.BlockSpec((1, n, LANES), lambda gi, j: (gi, 0, j)),
                  const((n2, 2 * n1, n1)), const((2 * n2, 2 * n2))],
        out_specs=pl.BlockSpec((1, n1, 2 * n2, LANES), lambda gi, j: (gi, 0, 0, j)),
        scratch_shapes=[pltpu.VMEM((2 * n1 * n2, LANES), F32)],
        compiler_params=_cparams("arbitrary", "arbitrary"),
        name="hyena_filter_spectrum",
    )(kfilt, t1_real, f2_fwd)


def _fftconv_kernel(u_ref, m_ref, spec_ref, bias_ref, t1_ref, f2f_ref, f2i_ref, t4_ref, o_ref, a_ref,
                    *, n1, n2):
    h1 = n1 // 2

    def stage1(j, carry):
        za = u_ref[0, pl.ds(j, h1, stride=n2), :]
        zb = u_ref[1, pl.ds(j, h1, stride=n2), :]
        a_ref[pl.ds(j, 2 * n1, stride=n2), :] = _dot(t1_ref[j], jnp.concatenate([za, zb], axis=0))
        return carry
    lax.fori_loop(0, n2, stage1, 0)

    def stage2(i, carry):
        re_rows = pl.ds(pl.multiple_of(i * n2, n2), n2)
        im_rows = pl.ds(pl.multiple_of((n1 + i) * n2, n2), n2)
        x = _dot(f2f_ref[...], jnp.concatenate([a_ref[re_rows, :], a_ref[im_rows, :]], axis=0))
        xr, xi = x[:n2], x[n2:]
        kr, ki = spec_ref[0, i, :n2, :], spec_ref[0, i, n2:, :]
        y = jnp.concatenate([xr * kr - xi * ki, xr * ki + xi * kr], axis=0)
        b = _dot(f2i_ref[...], y)
        a_ref[re_rows, :] = b[:n2]
        a_ref[im_rows, :] = b[n2:]
        return carry
    lax.fori_loop(0, n1, stage2, 0)

    bias = bias_ref[0]

    def stage4(j, carry):
        y = _dot(t4_ref[j], a_ref[pl.ds(j, 2 * n1, stride=n2), :])
        rows = pl.ds(j, h1, stride=n2)
        for p in range(2):
            u = u_ref[p, rows, :]
            o_ref[p, rows, :] = m_ref[p, rows, :] * (y[p * h1:(p + 1) * h1] + bias * u)
        return carry
    lax.fori_loop(0, n2, stage4, 0)


def _fftconv_gated(u, u_col, mult, mult_col, spec, conv_idx, bias, seq_len):
    bsz = u.shape[0]
    w = spec.shape[-1]
    n = 2 * seq_len
    n1, n2 = FFT_N1, n // FFT_N1
    t1_data, _, f2_fwd, f2_inv, t4 = _fft_tables(seq_len)
    const = _const_spec
    return pl.pallas_call(
        functools.partial(_fftconv_kernel, n1=n1, n2=n2),
        out_shape=jax.ShapeDtypeStruct((bsz, seq_len, w), F32),
        grid=(w // LANES, bsz // 2),
        in_specs=[pl.BlockSpec((2, seq_len, LANES), lambda j, p: (p, 0, u_col + j)),
                  pl.BlockSpec((2, seq_len, LANES), lambda j, p: (p, 0, mult_col + j)),
                  pl.BlockSpec((1, n1, 2 * n2, LANES), lambda j, p: (conv_idx, 0, 0, j),
                               pipeline_mode=pl.Buffered(1)),
                  pl.BlockSpec((1, 1, LANES), lambda j, p: (conv_idx, 0, j)),
                  const((n2, 2 * n1, n1)), const((2 * n2, 2 * n2)), const((2 * n2, 2 * n2)),
                  const((n2, n1, 2 * n1))],
        out_specs=pl.BlockSpec((2, seq_len, LANES), lambda j, p: (p, 0, j)),
        scratch_shapes=[pltpu.VMEM((2 * n1 * n2, LANES), F32)],
        compiler_params=_cparams("arbitrary", "arbitrary"),
        name="hyena_fftconv",
    )(u, mult, spec, bias, t1_data, f2_fwd, f2_inv, t4)


@functools.lru_cache(maxsize=None)
def _small_fft_tables(seq_len):
    n = 2 * seq_len
    f = np.exp(-2j * np.pi * np.outer(np.arange(n), np.arange(n)) / n)
    fwd = _blockreal(f[:, :seq_len])
    real = np.concatenate([f.real, f.imag], axis=0)
    inv = _blockreal(np.conj(f)[:seq_len, :] / n)
    as32 = lambda a: jnp.asarray(a, dtype=F32)
    return as32(fwd), as32(real), as32(inv)


def _ctx_hyena_kernel(v_ref, x1_ref, x2_ref, k_ref, bias_ref, fwd_ref, real_ref, inv_ref, o_ref, *, seq_len):
    n = 2 * seq_len

    def conv(ua, ub, g):
        spec = _dot(real_ref[...], k_ref[g])
        x = _dot(fwd_ref[...], jnp.concatenate([ua, ub], axis=0))
        xr, xi, kr, ki = x[:n], x[n:], spec[:n], spec[n:]
        y = _dot(inv_ref[...], jnp.concatenate([xr * kr - xi * ki, xr * ki + xi * kr], axis=0))
        b = bias_ref[g]
        return y[:seq_len] + b * ua, y[seq_len:] + b * ub

    c1a, c1b = conv(v_ref[0], v_ref[1], 0)
    y1a, y1b = x1_ref[0] * c1a, x1_ref[1] * c1b
    c2a, c2b = conv(y1a, y1b, 1)
    o_ref[0] = x2_ref[0] * c2a
    o_ref[1] = x2_ref[1] * c2b


def _ctx_hyena(hyc, row_block, kfilt, bias, seq_len, w):
    bsz = hyc.shape[0]
    n = 2 * seq_len
    fwd, real, inv = _small_fft_tables(seq_len)
    nt = w // LANES
    col = lambda c0: pl.BlockSpec((2, seq_len, LANES), lambda j, p: (p, row_block, c0 + j))
    return pl.pallas_call(
        functools.partial(_ctx_hyena_kernel, seq_len=seq_len),
        out_shape=jax.ShapeDtypeStruct((bsz, seq_len, w), F32),
        grid=(nt, bsz // 2),
        in_specs=[col(0), col(nt), col(2 * nt),
                  pl.BlockSpec((2, n, LANES), lambda j, p: (0, 0, j)),
                  pl.BlockSpec((2, 1, LANES), lambda j, p: (0, 0, j)),
                  _const_spec((2 * n, 2 * seq_len)), _const_spec((2 * n, n)), _const_spec((2 * seq_len, 2 * n))],
        out_specs=pl.BlockSpec((2, seq_len, LANES), lambda j, p: (p, 0, j)),
        compiler_params=_cparams("arbitrary", "arbitrary"),
        name="hyena_ctx",
    )(hyc, hyc, hyc, kfilt, bias, fwd, real, inv)


def _rms(x, g):
    return x * lax.rsqrt(jnp.mean(x * x, axis=-1, keepdims=True) + NORM_EPS) * g


def _inproj_kernel(x_ref, mod_ref, g_ref, w_ref, *out_refs, widths):
    shift, scale = mod_ref[0, 0, 0:1, :], mod_ref[0, 0, 1:2, :]
    h = _rms(x_ref[0], g_ref[...]) * (1.0 + scale) + shift
    z = jnp.dot(h.astype(BF16), w_ref[...], preferred_element_type=F32)
    off = 0
    for o_ref, wd in zip(out_refs[:-1], widths[:-1]):
        o_ref[0] = z[:, off:off + wd]
        off += wd
    out_refs[-1][0] = _silu(z[:, off:])


def _inproj(xs, mods, norm_g, w_perm, widths, n_lat_tiles):
    bsz, s, d = xs.shape
    nt = s // ROW_TILE
    return pl.pallas_call(
        functools.partial(_inproj_kernel, widths=widths),
        out_shape=[jax.ShapeDtypeStruct((bsz, s, wd), F32) for wd in widths],
        grid=(bsz, nt),
        in_specs=[pl.BlockSpec((1, ROW_TILE, d), lambda b, t: (b, t, 0)),
                  pl.BlockSpec((1, 1, 3, d), lambda b, t: (b, t // n_lat_tiles, 0, 0)),
                  pl.BlockSpec((1, d), lambda b, t: (0, 0)),
                  _const_spec(w_perm.shape)],
        out_specs=[pl.BlockSpec((1, ROW_TILE, wd), lambda b, t: (b, t, 0)) for wd in widths],
        compiler_params=_cparams("arbitrary", "arbitrary"),
        name="in_projection",
    )(xs, mods, norm_g.reshape(1, d), w_perm)


def _halo_specs(width, n_tiles):
    per = ROW_TILE // 8
    prev = pl.BlockSpec((1, 8, width), lambda b, t: (b, jnp.maximum(t * per - 1, 0), 0))
    nxt = pl.BlockSpec((1, 8, width), lambda b, t: (b, jnp.minimum((t + 1) * per, n_tiles * per - 1), 0))
    return prev, nxt


def _short_conv_tile(z, prev8, next8, w, t, n_lat_tiles, n_tiles):
    first = jnp.logical_or(t == 0, t == n_lat_tiles)
    last = jnp.logical_or(t == n_lat_tiles - 1, t == n_tiles - 1)
    above = jnp.where(first, 0.0, prev8[7:8, :])
    below = jnp.where(last, 0.0, next8[0:1, :])
    row = lax.broadcasted_iota(jnp.int32, z.shape, 0)
    zm1 = jnp.where(row == 0, above, pltpu.roll(z, 1, 0))
    zp1 = jnp.where(row == z.shape[0] - 1, below, pltpu.roll(z, z.shape[0] - 1, 0))
    return zm1 * w[0:1, :] + z * w[1:2, :] + zp1 * w[2:3, :]


def _short_conv_kernel(z_ref, p_ref, n_ref, w_ref, o_ref, *, n_lat_tiles, n_tiles):
    o_ref[0] = _short_conv_tile(z_ref[0], p_ref[0], n_ref[0], w_ref[...], pl.program_id(1),
                                n_lat_tiles, n_tiles)


def _short_conv(z, w, n_lat_tiles):
    bsz, s, width = z.shape
    nt = s // ROW_TILE
    prev, nxt = _halo_specs(width, nt)
    tile = pl.BlockSpec((1, ROW_TILE, width), lambda b, t: (b, t, 0))
    return pl.pallas_call(
        functools.partial(_short_conv_kernel, n_lat_tiles=n_lat_tiles, n_tiles=nt),
        out_shape=jax.ShapeDtypeStruct(z.shape, F32),
        grid=(bsz, nt),
        in_specs=[tile, prev, nxt, pl.BlockSpec((3, width), lambda b, t: (0, 0))],
        out_specs=tile,
        compiler_params=_cparams("arbitrary", "arbitrary"),
        name="short_conv",
    )(z, z, z, w)


def _outproj_kernel(a_lat_ref, a_ctx_ref, b_ref, c_ref, d_ref, gt_ref, x_ref, mod_ref, w_ref, fg_ref, o_ref,
                    *, n_lat_tiles, final):
    is_ctx = pl.program_id(1) >= n_lat_tiles
    a = jnp.where(is_ctx, a_ctx_ref[0], a_lat_ref[0])
    mix = jnp.concatenate([a, b_ref[0], c_ref[0], d_ref[0]], axis=-1) * gt_ref[0]
    y = jnp.dot(mix.astype(BF16), w_ref[...], preferred_element_type=F32)
    x = x_ref[0] + mod_ref[0, 0, 2:3, :] * y
    o_ref[0] = _rms(x, fg_ref[...]) if final else x


def _outproj(a_lat, a_ctx, bmix, cmix, dmix, gates, xs, mods, w_out, final_g, n_lat_tiles, final):
    bsz, s, d = xs.shape
    wb = a_lat.shape[-1]
    nt = n_lat_tiles if final else s // ROW_TILE
    tile = lambda wd: pl.BlockSpec((1, ROW_TILE, wd), lambda b, t: (b, t, 0))
    return pl.pallas_call(
        functools.partial(_outproj_kernel, n_lat_tiles=n_lat_tiles, final=final),
        out_shape=jax.ShapeDtypeStruct((bsz, nt * ROW_TILE, d), F32),
        grid=(bsz, nt),
        in_specs=[pl.BlockSpec((1, ROW_TILE, wb), lambda b, t: (b, jnp.minimum(t, n_lat_tiles - 1), 0)),
                  pl.BlockSpec((1, ROW_TILE, wb), lambda b, t: (b, 0, 0)),
                  tile(wb), tile(wb), tile(wb), tile(4 * wb), tile(d),
                  pl.BlockSpec((1, 1, 3, d), lambda b, t: (b, t // n_lat_tiles, 0, 0)),
                  _const_spec(w_out.shape),
                  pl.BlockSpec((1, d), lambda b, t: (0, 0))],
        out_specs=tile(d),
        compiler_params=_cparams("arbitrary", "arbitrary"),
        name="out_projection",
    )(a_lat, a_ctx, bmix, cmix, dmix, gates, xs, mods, w_out, final_g.reshape(1, d))


def _rope_tables(seq_len, ctx_len):
    rows = seq_len // GRID_W
    row = jnp.repeat(jnp.arange(rows, dtype=F32), GRID_W)
    col = jnp.tile(jnp.arange(GRID_W, dtype=F32), rows)
    n_freq = HEAD_DIM // 4
    inv_freq = ROPE_THETA ** (-jnp.arange(n_freq, dtype=F32) / n_freq)
    ar, ac = row[:, None] * inv_freq, col[:, None] * inv_freq
    zero = jnp.zeros_like(ar)
    cos = jnp.concatenate([jnp.cos(ar), jnp.cos(ar), jnp.cos(ac), jnp.cos(ac)], axis=-1)
    sin_hi = jnp.concatenate([-jnp.sin(ar), zero, -jnp.sin(ac), zero], axis=-1)
    sin_lo = jnp.concatenate([zero, jnp.sin(ar), zero, jnp.sin(ac)], axis=-1)
    pad = lambda t, v: jnp.concatenate([t, jnp.full((ctx_len, HEAD_DIM), v, F32)], axis=0)
    return pad(cos, 1.0), pad(sin_hi, 0.0), pad(sin_lo, 0.0)


def _rope(x, cos, sin_hi, sin_lo):
    q = HEAD_DIM // 4
    w = x.shape[-1]
    return x * cos + pltpu.roll(x, w - q, 1) * sin_hi + pltpu.roll(x, q, 1) * sin_lo


def _head_mean_sq(x, bd):
    return _dot(x * x, bd)


def _qkprep_kernel(wa_ref, fa_ref, cos_ref, shi_ref, slo_ref, qg_ref, kg_ref, bd_ref,
                   qw_ref, kw_ref, vw_ref, qf_ref, kf_ref, vf_ref, *, wq, wk):
    nq, nk = wq // HEAD_DIM, wk // HEAD_DIM
    cos, shi, slo = cos_ref[...], shi_ref[...], slo_ref[...]
    tab = lambda t, n: jnp.concatenate([t] * n, axis=-1)
    cq, hq, lq = tab(cos, nq), tab(shi, nq), tab(slo, nq)
    ck, hk, lk = tab(cos, nk), tab(shi, nk), tab(slo, nk)
    scale = HEAD_DIM ** -0.5

    def emit(ref, val, n):
        for h in range(n):
            ref[0, h] = val[:, h * HEAD_DIM:(h + 1) * HEAD_DIM].astype(ref.dtype)

    wa = wa_ref[0]
    emit(qw_ref, _rope(wa[:, :wq], cq, hq, lq) * scale, nq)
    emit(kw_ref, _rope(wa[:, wq:wq + wk], ck, hk, lk), nk)
    emit(vw_ref, wa[:, wq + wk:], nk)
    fa = fa_ref[0]
    q, k = fa[:, :wq], fa[:, wq:wq + wk]
    bd = bd_ref[...]
    q = q * lax.rsqrt(_head_mean_sq(q, bd) + NORM_EPS) * qg_ref[...]
    k = k * lax.rsqrt(_head_mean_sq(k, bd[:wk, :wk]) + NORM_EPS) * kg_ref[...]
    emit(qf_ref, _rope(q, cq, hq, lq) * scale, nq)
    emit(kf_ref, _rope(k, ck, hk, lk), nk)
    emit(vf_ref, fa[:, wq + wk:], nk)


def _head_block_diag(width, value):
    h = np.arange(width) // HEAD_DIM
    return jnp.asarray((h[:, None] == h[None, :]) * value, dtype=F32)


def _qkprep(wa, fa, rope, q_gain, k_gain, wq, wk):
    bsz, s, _ = wa.shape
    nq, nk = wq // HEAD_DIM, wk // HEAD_DIM
    nt = s // ROW_TILE
    tile = lambda wd: pl.BlockSpec((1, ROW_TILE, wd), lambda b, t: (b, t, 0))
    tab = pl.BlockSpec((ROW_TILE, HEAD_DIM), lambda b, t: (t, 0))
    hm = lambda n: pl.BlockSpec((1, n, ROW_TILE, HEAD_DIM), lambda b, t: (b, 0, t, 0))
    shp = lambda n: jax.ShapeDtypeStruct((bsz, n, s, HEAD_DIM), BF16)
    return pl.pallas_call(
        functools.partial(_qkprep_kernel, wq=wq, wk=wk),
        out_shape=[shp(nq), shp(nk), shp(nk), shp(nq), shp(nk), shp(nk)],
        grid=(bsz, nt),
        in_specs=[tile(wq + 2 * wk), tile(wq + 2 * wk), tab, tab, tab,
                  pl.BlockSpec((1, wq), lambda b, t: (0, 0)), pl.BlockSpec((1, wk), lambda b, t: (0, 0)),
                  _const_spec((wq, wq))],
        out_specs=[hm(nq), hm(nk), hm(nk), hm(nq), hm(nk), hm(nk)],
        compiler_params=_cparams("arbitrary", "arbitrary"),
        name="qk_prep",
    )(wa, fa, *rope, jnp.tile(q_gain, nq).reshape(1, wq), jnp.tile(k_gain, nk).reshape(1, wk),
      _head_block_diag(wq, 1.0 / HEAD_DIM))


def _window_attn_kernel(q_ref, k_ref, v_ref, sink_ref, o_ref, *, seq_len, ctx_len, tq):
    t = pl.program_id(2)
    n_lat = seq_len // tq
    g = q_ref.shape[1]
    band = 2 * WINDOW + tq
    q = q_ref[0].reshape(g * tq, HEAD_DIM)
    kc, vc = k_ref[0, 0, seq_len:seq_len + ctx_len, :], v_ref[0, 0, seq_len:seq_len + ctx_len, :]
    start = pl.multiple_of(jnp.clip((t - 1) * tq, 0, seq_len - band), tq)
    kb, vb = k_ref[0, 0, pl.ds(start, band), :], v_ref[0, 0, pl.ds(start, band), :]
    nt_dot = lambda a, b: lax.dot_general(a, b, (((1,), (1,)), ((), ())), preferred_element_type=F32)
    s_ctx = nt_dot(q, kc)
    s_loc = nt_dot(q, kb)
    qpos = t * tq + lax.broadcasted_iota(jnp.int32, (g, tq, band), 1).reshape(g * tq, band)
    kpos = start + lax.broadcasted_iota(jnp.int32, (g * tq, band), 1)
    valid = jnp.logical_and(jnp.abs(qpos - kpos) <= WINDOW, t < n_lat)
    s_loc = jnp.where(valid, s_loc, NEG_INF)
    sink = sink_ref[0]
    m = jnp.maximum(jnp.maximum(jnp.max(s_ctx, axis=-1, keepdims=True),
                                jnp.max(s_loc, axis=-1, keepdims=True)), sink)
    p_ctx, p_loc = jnp.exp(s_ctx - m), jnp.exp(s_loc - m)
    denom = (jnp.sum(p_ctx, axis=-1, keepdims=True) + jnp.sum(p_loc, axis=-1, keepdims=True)
             + jnp.exp(sink - m))
    out = (jnp.dot(p_ctx.astype(BF16), vc, preferred_element_type=F32)
           + jnp.dot(p_loc.astype(BF16), vb, preferred_element_type=F32)) / denom
    o_ref[0] = jnp.concatenate([out[h * tq:(h + 1) * tq] for h in range(g)], axis=-1)


def _window_attention(q, k, v, sink, seq_len, ctx_len):
    bsz, nq, s, _ = q.shape
    nkv = k.shape[1]
    g = nq // nkv
    tq = WINDOW
    sink_rows = jnp.repeat(sink.astype(F32).reshape(nkv, g), tq, axis=1).reshape(nkv, g * tq, 1)
    kv = pl.BlockSpec((1, 1, s, HEAD_DIM), lambda b, h, t: (b, h, 0, 0))
    return pl.pallas_call(
        functools.partial(_window_attn_kernel, seq_len=seq_len, ctx_len=ctx_len, tq=tq),
        out_shape=jax.ShapeDtypeStruct((bsz, s, nq * HEAD_DIM), F32),
        grid=(bsz, nkv, s // tq),
        in_specs=[pl.BlockSpec((1, g, tq, HEAD_DIM), lambda b, h, t: (b, h, t, 0)), kv, kv,
                  pl.BlockSpec((1, g * tq, 1), lambda b, h, t: (h, 0, 0))],
        out_specs=pl.BlockSpec((1, tq, g * HEAD_DIM), lambda b, h, t: (b, t, h)),
        compiler_params=_cparams("arbitrary", "arbitrary", "arbitrary"),
        name="window_attention",
    )(q, k, v, sink_rows)


def _dense_attn_kernel(q_ref, k_ref, v_ref, o_ref, *, seq_len, tq, tk):
    t = pl.program_id(2)
    n_lat = seq_len // tq
    g = q_ref.shape[1]
    n_kblocks = k_ref.shape[2] // tk
    q = q_ref[0].reshape(g * tq, HEAD_DIM)
    first = jnp.where(t < n_lat, 0, seq_len // tk)

    def body(j, carry):
        m, l, acc = carry
        rows = pl.ds(pl.multiple_of(j * tk, tk), tk)
        s = lax.dot_general(q, k_ref[0, 0, rows, :], (((1,), (1,)), ((), ())), preferred_element_type=F32)
        m_new = jnp.maximum(m, jnp.max(s, axis=-1, keepdims=True))
        alpha = jnp.exp(m - m_new)
        p = jnp.exp(s - m_new)
        l = alpha * l + jnp.sum(p, axis=-1, keepdims=True)
        acc = alpha * acc + jnp.dot(p.astype(BF16), v_ref[0, 0, rows, :], preferred_element_type=F32)
        return m_new, l, acc

    init = (jnp.full((g * tq, 1), NEG_INF, F32), jnp.zeros((g * tq, 1), F32),
            jnp.zeros((g * tq, HEAD_DIM), F32))
    _, l, acc = lax.fori_loop(first, n_kblocks, body, init)
    out = acc / l
    o_ref[0] = jnp.concatenate([out[h * tq:(h + 1) * tq] for h in range(g)], axis=-1)


def _dense_attention(q, k, v, seq_len):
    bsz, nq, s, _ = q.shape
    nkv = k.shape[1]
    g = nq // nkv
    tq = tk = ROW_TILE
    kv = pl.BlockSpec((1, 1, s, HEAD_DIM), lambda b, h, t: (b, h, 0, 0))
    return pl.pallas_call(
        functools.partial(_dense_attn_kernel, seq_len=seq_len, tq=tq, tk=tk),
        out_shape=jax.ShapeDtypeStruct((bsz, s, nq * HEAD_DIM), F32),
        grid=(bsz, nkv, s // tq),
        in_specs=[pl.BlockSpec((1, g, tq, HEAD_DIM), lambda b, h, t: (b, h, t, 0)), kv, kv],
        out_specs=pl.BlockSpec((1, tq, g * HEAD_DIM), lambda b, h, t: (b, t, h)),
        compiler_params=_cparams("arbitrary", "arbitrary", "arbitrary"),
        name="dense_attention",
    )(q, k, v)


def _softplus(x):
    return jnp.maximum(x, 0.0) + jnp.log(1.0 + jnp.exp(-jnp.abs(x)))


def _rwkv_par_kernel(z_ref, prev_ref, next_ref, cw_ref, w0_ref, wup_ref, a0_ref, aup_ref, kk_ref, ka_ref,
                     bd_ref, trif_ref, trib_ref, ones_ref,
                     rp_ref, yv_ref, p_ref, q_ref, rvk_ref, s_ref, wt_ref,
                     *, n_lat_tiles, n_tiles, w, lora):
    tc = RW_CHUNK
    nc, nh = ROW_TILE // tc, w // HEAD_DIM
    z = _short_conv_tile(z_ref[0], prev_ref[0], next_ref[0], cw_ref[...], pl.program_id(1),
                         n_lat_tiles, n_tiles)
    r, k, v = z[:, :w], z[:, w:2 * w], z[:, 2 * w:3 * w]
    w_low = jnp.tanh(z[:, 3 * w:3 * w + lora])
    a_low = z[:, 3 * w + lora:]
    kk = k * kk_ref[...]
    kk = kk / jnp.maximum(jnp.sqrt(_dot(kk * kk, bd_ref[...])), 1e-12)
    ksum = jnp.zeros_like(k)
    for d in range(2):
        w_log = -_softplus(-(w0_ref[d:d + 1, :] + _dot(w_low, wup_ref[d]))) - 0.5
        lw = -jnp.exp(w_log)
        a = 1.0 / (1.0 + jnp.exp(-(a0_ref[d:d + 1, :] + _dot(a_low, aup_ref[d]))))
        kd = k * (1.0 + (a - 1.0) * ka_ref[...])
        ksum = ksum + kd
        c = _dot(trif_ref[...] if d == 0 else trib_ref[...], lw)
        ctot = _dot(ones_ref[...], lw)
        e_neg, e_rem = jnp.exp(-c), jnp.exp(ctot - c)
        b = kk * a
        arrays = (-kk * jnp.exp(c - lw), r * jnp.exp(c), b * e_neg, kd * e_neg, b * e_rem, kd * e_rem, v)
        w_tot = jnp.exp(ctot)
        for ci in range(nc):
            for h in range(nh):
                slot = (d * nc + ci) * nh + h
                rows, cols = slice(ci * tc, (ci + 1) * tc), slice(h * HEAD_DIM, (h + 1) * HEAD_DIM)
                for ai, arr in enumerate(arrays):
                    s_ref[ai, slot] = arr[rows, cols]
                wt_ref[slot] = w_tot[ci * tc:ci * tc + 8, cols]
    rvk_ref[0] = jnp.concatenate([r, v, ksum], axis=-1)

    row = lax.broadcasted_iota(jnp.int32, (tc, tc), 0)
    col = lax.broadcasted_iota(jnp.int32, (tc, tc), 1)
    eye = row == col

    def level_mask(s):
        return jnp.logical_and(row // (2 * s) == col // (2 * s), row // s != col // s)

    def chunk(slot, carry):
        at, rt, bh, kh, bc, kc, vv = (s_ref[ai, slot] for ai in range(7))
        d = slot // (nc * nh)
        fwd = d == 0
        before = jnp.logical_or(jnp.logical_and(fwd, row > col),
                                jnp.logical_and(jnp.logical_not(fwd), row < col))
        upto = jnp.logical_or(before, eye)
        big = _dot_nt(jnp.concatenate([at, rt], axis=0), jnp.concatenate([bh, kh], axis=0))
        a_ab = jnp.where(before, big[:tc, :tc], 0.0)
        a_ak = jnp.where(before, big[:tc, tc:], 0.0)
        a_rb = jnp.where(upto, big[tc:, :tc], 0.0)
        a_rk = jnp.where(upto, big[tc:, tc:], 0.0)
        x = jnp.where(eye, 1.0, jnp.where(level_mask(1), a_ab, 0.0))
        s = 2
        while s < tc:
            x = x + _dot(_dot(x, jnp.where(level_mask(s), a_ab, 0.0)), x)
            s *= 2
        at2 = _dot(x, at)
        uv = _dot(x, _dot(a_ak, vv))
        h = slot % nh
        rows = pl.ds(pl.multiple_of(((slot // nh) % nc) * tc, tc), tc)
        rp_ref[0, d, h, rows, :] = rt + _dot(a_rb, at2)
        yv_ref[0, d, h, rows, :] = _dot(a_rb, uv) + _dot(a_rk, vv)
        w_tot = wt_ref[slot][0:1, :]
        p_ref[0, d, h, (slot // nh) % nc] = jnp.where(eye, w_tot, 0.0) + _dot_tn(bc, at2)
        q_ref[0, d, h, (slot // nh) % nc] = _dot_tn(bc, uv) + _dot_tn(kc, vv)
        return carry
    lax.fori_loop(0, 2 * nc * nh, chunk, 0)


def _chunk_matrices():
    t = np.arange(ROW_TILE)
    same = (t[:, None] // RW_CHUNK) == (t[None, :] // RW_CHUNK)
    as32 = lambda m: jnp.asarray(m, dtype=F32)
    return (as32(same & (t[None, :] <= t[:, None])), as32(same & (t[None, :] >= t[:, None])), as32(same))


def _rwkv_par(rw, conv_w, w0, w_up, a0, a_up, k_k, k_a, w, n_lat_tiles):
    bsz, s, width = rw.shape
    nt = s // ROW_TILE
    nh, nc = w // HEAD_DIM, ROW_TILE // RW_CHUNK
    lora = w_up.shape[1]
    prev, nxt = _halo_specs(width, nt)
    full = lambda a: pl.BlockSpec(a.shape, lambda b, t: (0,) * a.ndim)
    trif, trib, ones = _chunk_matrices()
    bd = _head_block_diag(w, 1.0)
    vec = lambda a: a.reshape(1, w)
    seq = jax.ShapeDtypeStruct((bsz, 2, nh, s, HEAD_DIM), F32)
    mat = jax.ShapeDtypeStruct((bsz, 2, nh, s // RW_CHUNK, HEAD_DIM, HEAD_DIM), F32)
    seq_spec = pl.BlockSpec((1, 2, nh, ROW_TILE, HEAD_DIM), lambda b, t: (b, 0, 0, t, 0))
    mat_spec = pl.BlockSpec((1, 2, nh, nc, HEAD_DIM, HEAD_DIM), lambda b, t: (b, 0, 0, t, 0, 0))
    consts = (conv_w, w0, w_up, a0, a_up, vec(k_k), vec(k_a), bd, trif, trib, ones)
    return pl.pallas_call(
        functools.partial(_rwkv_par_kernel, n_lat_tiles=n_lat_tiles, n_tiles=nt, w=w, lora=lora),
        out_shape=[seq, seq, mat, mat, jax.ShapeDtypeStruct((bsz, s, 3 * w), F32)],
        grid=(bsz, nt),
        in_specs=[pl.BlockSpec((1, ROW_TILE, width), lambda b, t: (b, t, 0)), prev, nxt]
                 + [full(a) for a in consts],
        out_specs=[seq_spec, seq_spec, mat_spec, mat_spec,
                   pl.BlockSpec((1, ROW_TILE, 3 * w), lambda b, t: (b, t, 0))],
        scratch_shapes=[pltpu.VMEM((7, 2 * nc * nh, RW_CHUNK, HEAD_DIM), F32),
                        pltpu.VMEM((2 * nc * nh, 8, HEAD_DIM), F32)],
        compiler_params=_cparams("arbitrary", "arbitrary"),
        name="rwkv_chunk_prep",
    )(rw, rw, rw, *consts)


def _rwkv_seq_kernel(rpf_ref, yvf_ref, pf_ref, qf_ref, rpb_ref, yvb_ref, pb_ref, qb_ref,
                     yf_ref, yb_ref, g_ref):
    @pl.when(pl.program_id(1) == 0)
    def _():
        g_ref[...] = jnp.zeros_like(g_ref)

    nh = g_ref.shape[1]
    for d, (rp, yv, p, q, y) in enumerate(((rpf_ref, yvf_ref, pf_ref, qf_ref, yf_ref),
                                           (rpb_ref, yvb_ref, pb_ref, qb_ref, yb_ref))):
        for h in range(nh):
            g = g_ref[d, h]
            y[0, h] = _dot(rp[0, 0, h], g) + yv[0, 0, h]
            g_ref[d, h] = _dot(p[0, 0, h, 0], g) + q[0, 0, h, 0]


def _rwkv_seq(rp, yv, pm, qm, n_lat_chunks):
    bsz, _, nh, s, _ = rp.shape
    n_chunks = s // RW_CHUNK
    n_ctx = n_chunks - n_lat_chunks
    order = (lambda i: jnp.where(i < n_ctx, n_lat_chunks + i, i - n_ctx),
             lambda i: n_chunks - 1 - i)
    seq = lambda d: pl.BlockSpec((1, 1, nh, RW_CHUNK, HEAD_DIM), lambda b, i: (b, d, 0, order[d](i), 0))
    mat = lambda d: pl.BlockSpec((1, 1, nh, 1, HEAD_DIM, HEAD_DIM),
                                 lambda b, i: (b, d, 0, order[d](i), 0, 0))
    out = lambda d: pl.BlockSpec((1, nh, RW_CHUNK, HEAD_DIM), lambda b, i: (b, 0, order[d](i), 0))
    shp = jax.ShapeDtypeStruct((bsz, nh, s, HEAD_DIM), F32)
    return pl.pallas_call(
        _rwkv_seq_kernel,
        out_shape=[shp, shp],
        grid=(bsz, n_chunks),
        in_specs=[seq(0), seq(0), mat(0), mat(0), seq(1), seq(1), mat(1), mat(1)],
        out_specs=[out(0), out(1)],
        scratch_shapes=[pltpu.VMEM((2, nh, HEAD_DIM, HEAD_DIM), F32)],
        compiler_params=_cparams("arbitrary", "arbitrary"),
        name="rwkv_state_scan",
    )(rp, yv, pm, qm, rp, yv, pm, qm)


def _rwkv_out_kernel(yf_ref, yb_ref, rvk_ref, rk_ref, g_ref, b_ref, o_ref, *, w):
    nh = w // HEAD_DIM
    rvk = rvk_ref[0]
    r, v, ksum = rvk[:, :w], rvk[:, w:2 * w], rvk[:, 2 * w:]
    rkk = r * ksum * rk_ref[...]
    normed, bonus = [], []
    for h in range(nh):
        cols = slice(h * HEAD_DIM, (h + 1) * HEAD_DIM)
        y = yf_ref[0, h] + yb_ref[0, h]
        mu = jnp.mean(y, axis=-1, keepdims=True)
        var = jnp.mean(jnp.square(y - mu), axis=-1, keepdims=True)
        normed.append((y - mu) * lax.rsqrt(var + RW_GN_EPS))
        bonus.append(jnp.sum(rkk[:, cols], axis=-1, keepdims=True) * v[:, cols])
    o_ref[0] = (jnp.concatenate(normed, axis=-1) * g_ref[...] + b_ref[...]
                + jnp.concatenate(bonus, axis=-1))


def _rwkv_out(yf, yb, rvk, r_k, ln_g, ln_b, w):
    bsz, nh, s, _ = yf.shape
    hm = pl.BlockSpec((1, nh, ROW_TILE, HEAD_DIM), lambda b, t: (b, 0, t, 0))
    vec = pl.BlockSpec((1, w), lambda b, t: (0, 0))
    return pl.pallas_call(
        functools.partial(_rwkv_out_kernel, w=w),
        out_shape=jax.ShapeDtypeStruct((bsz, s, w), F32),
        grid=(bsz, s // ROW_TILE),
        in_specs=[hm, hm, pl.BlockSpec((1, ROW_TILE, 3 * w), lambda b, t: (b, t, 0)), vec, vec, vec],
        out_specs=pl.BlockSpec((1, ROW_TILE, w), lambda b, t: (b, t, 0)),
        compiler_params=_cparams("arbitrary", "arbitrary"),
        name="rwkv_readout",
    )(yf, yb, rvk, r_k.reshape(1, w), ln_g.reshape(1, w), ln_b.reshape(1, w))


def _hyena_two_sided_filters(seq_len, fw1, fb1, freq, fw2, fb2, fw3, width):
    bands = (fw1.shape[0] - 1) // 2
    mm = functools.partial(jnp.matmul, precision=HI)
    t = jnp.linspace(0.0, 1.0, seq_len, dtype=F32)[:, None]
    wpos = (2.0 * math.pi / seq_len) * jnp.arange(seq_len, dtype=F32)[:, None]
    f = jnp.linspace(1e-4, bands - 1, bands, dtype=F32)[None, :]
    z = jnp.concatenate([t, jnp.cos(f * wpos), jnp.sin(f * wpos)], axis=-1)
    h = jnp.sin(freq * (mm(z, fw1) + fb1))
    h = jnp.sin(freq * (mm(h, fw2) + fb2))
    h = mm(h, fw3).reshape(seq_len, 2, 2, width)
    max_decay = math.log(HY_DECAY_TARGET) / HY_FAST_DECAY
    min_decay = math.log(HY_DECAY_TARGET) / HY_SLOW_DECAY
    deltas = jnp.linspace(min_decay, max_decay, width, dtype=F32)
    h = h * jnp.exp(-t * jnp.abs(deltas))[:, None, None, :]
    h = h / jnp.sum(jnp.abs(h), axis=(0, 2), keepdims=True)
    fwd, bwd = h[:, :, 0], h[:, :, 1]
    k = jnp.concatenate([fwd, jnp.zeros_like(fwd[:1]), bwd[:0:-1]], axis=0)
    return jnp.moveaxis(k, 1, 0)


def kernel(x, c, ctx, c_ctx, mod_w, mod_b, norm_g, w_in, w_out, hy_conv, hy_fw1, hy_fb1, hy_freq, hy_fw2,
           hy_fb2, hy_fw3, hy_bias, rw_conv, rw_w0, rw_w_up, rw_a0, rw_a_up, rw_k_k, rw_k_a, rw_r_k,
           rw_ln_g, rw_ln_b, wa_sink, fa_q_norm, fa_k_norm, final_g):
    bsz, seq_len, d = x.shape
    ctx_len = ctx.shape[1]
    depth = w_in.shape[0]
    w_hy = hy_bias.shape[-1]
    w_rw = rw_w0.shape[-1]
    n_wa_heads = wa_sink.shape[-1]
    w_q = n_wa_heads * HEAD_DIM
    w_kv = w_q // 2
    lora = rw_w_up.shape[2] + rw_a_up.shape[2]
    branch_w = (3 * w_hy, 3 * w_rw + lora, w_q + 2 * w_kv, w_q + 2 * w_kv)
    gate_w = (w_hy, w_rw, w_q, w_q)
    assert seq_len % ROW_TILE == 0 and ctx_len % ROW_TILE == 0 and bsz % 2 == 0
    n_lat_tiles = seq_len // ROW_TILE

    starts = np.cumsum([0] + [bw + gw for bw, gw in zip(branch_w, gate_w)])
    cols = np.concatenate([np.arange(s0, s0 + bw) for s0, bw in zip(starts, branch_w)]
                          + [np.arange(s0 + bw, s0 + bw + gw) for s0, bw, gw in zip(starts, branch_w, gate_w)])
    w_in_p = w_in[:, :, cols].astype(BF16)
    w_out_b = w_out.astype(BF16)
    widths = branch_w + (sum(gate_w),)

    pad_rows = (-(bsz + 1)) % 8
    cond = jnp.concatenate([c, c_ctx[None], jnp.zeros((pad_rows, d), F32)], axis=0)
    mod = _modulation(cond, mod_w, mod_b)
    mod_lat = mod[:, :bsz].reshape(depth, bsz, 3, d)
    mod_ctx = jnp.broadcast_to(mod[:, bsz].reshape(depth, 1, 3, d), (depth, bsz, 3, d))
    mods = jnp.stack([mod_lat, mod_ctx], axis=2)

    rope = _rope_tables(seq_len, ctx_len)
    xs = jnp.concatenate([x, ctx], axis=1)
    for l in range(depth):
        last = l == depth - 1
        hy, rw, wa, fa, gates = _inproj(xs, mods[l], norm_g[l], w_in_p[l], widths, n_lat_tiles)

        hyc = _short_conv(hy, hy_conv[l], n_lat_tiles)
        filt = functools.partial(_hyena_two_sided_filters, fw1=hy_fw1[l], fb1=hy_fb1[l], freq=hy_freq[l],
                                 fw2=hy_fw2[l], fb2=hy_fb2[l], fw3=hy_fw3[l], width=w_hy)
        bias = hy_bias[l].reshape(2, 1, w_hy)
        spec = _filter_spectrum(filt(seq_len))
        nt_hy = w_hy // LANES
        y1 = _fftconv_gated(hyc, 0, hyc, nt_hy, spec, 0, bias, seq_len)
        a_lat = _fftconv_gated(y1, 0, hyc, 2 * nt_hy, spec, 1, bias, seq_len)
        a_ctx = a_lat if last else _ctx_hyena(hyc, seq_len // ctx_len, filt(ctx_len), bias, ctx_len, w_hy)

        rp, yv, pm, qm, rvk = _rwkv_par(rw, rw_conv[l], rw_w0[l], rw_w_up[l], rw_a0[l], rw_a_up[l],
                                        rw_k_k[l], rw_k_a[l], w_rw, n_lat_tiles)
        yf, yb = _rwkv_seq(rp, yv, pm, qm, seq_len // RW_CHUNK)
        b_mix = _rwkv_out(yf, yb, rvk, rw_r_k[l], rw_ln_g[l], rw_ln_b[l], w_rw)

        qw, kw, vw, qf, kf, vf = _qkprep(wa, fa, rope, fa_q_norm[l], fa_k_norm[l], w_q, w_kv)
        c_mix = _window_attention(qw, kw, vw, wa_sink[l], seq_len, ctx_len)
        d_mix = _dense_attention(qf, kf, vf, seq_len)

        xs = _outproj(a_lat, a_ctx, b_mix, c_mix, d_mix, gates, xs, mods[l], w_out_b[l], final_g,
                      n_lat_tiles, last)
    return xs
```

```python
import functools
import math

import jax
import jax.numpy as jnp
import numpy as np
from jax import lax
from jax.experimental import pallas as pl
from jax.experimental.pallas import tpu as pltpu

HEAD_DIM = 64
GRID_W = 64
WINDOW = 128
NORM_EPS = 1e-6
RW_GN_EPS = 64e-5
NEG_INF = -1e30
ROPE_THETA = 10000.0
HY_FAST_DECAY = 0.3
HY_SLOW_DECAY = 1.5
HY_DECAY_TARGET = 1e-2

ROW_TILE = 256
LANES = 128
FFT_N1 = 64
RW_CHUNK = 64
VMEM_LIMIT = 60 * 1024 * 1024

F32 = jnp.float32
BF16 = jnp.bfloat16
HI = lax.Precision.HIGHEST


def _dot(a, b):
    return jnp.dot(a, b, preferred_element_type=F32, precision=HI)


def _dot_nt(a, b):
    return lax.dot_general(a, b, (((1,), (1,)), ((), ())), preferred_element_type=F32, precision=HI)


def _dot_tn(a, b):
    return lax.dot_general(a, b, (((0,), (0,)), ((), ())), preferred_element_type=F32, precision=HI)


def _cparams(*sem):
    return pltpu.CompilerParams(dimension_semantics=sem, vmem_limit_bytes=VMEM_LIMIT)


def _const_spec(shape):
    return pl.BlockSpec(shape, lambda *_: (0,) * len(shape), pipeline_mode=pl.Buffered(1))


def _silu(x):
    return x * (1.0 / (1.0 + jnp.exp(-x)))


def _mod_kernel(c_ref, w_ref, b_ref, o_ref):
    o_ref[0] = _dot(_silu(c_ref[...]), w_ref[0]) + b_ref[0]


def _modulation(cond, mod_w, mod_b):
    depth, d, d3 = mod_w.shape
    rows = cond.shape[0]
    return pl.pallas_call(
        _mod_kernel,
        out_shape=jax.ShapeDtypeStruct((depth, rows, d3), F32),
        grid=(depth,),
        in_specs=[pl.BlockSpec((rows, d), lambda l: (0, 0)),
                  pl.BlockSpec((1, d, d3), lambda l: (l, 0, 0)),
                  pl.BlockSpec((1, 1, d3), lambda l: (l, 0, 0))],
        out_specs=pl.BlockSpec((1, rows, d3), lambda l: (l, 0, 0)),
        compiler_params=_cparams("arbitrary"),
        name="modulation",
    )(cond, mod_w, mod_b.reshape(depth, 1, d3))


def _blockreal(m):
    return np.block([[m.real, -m.imag], [m.imag, m.real]])


@functools.lru_cache(maxsize=None)
def _fft_tables(seq_len):
    n = 2 * seq_len
    n1, n2 = FFT_N1, n // FFT_N1
    h1 = n1 // 2
    j2 = np.arange(n2)[:, None, None]
    k1 = np.arange(n1)[None, :, None]
    t1 = np.exp(-2j * np.pi * (j2 * k1 / n + k1 * np.arange(n1)[None, None, :] / n1))
    t1_data = np.stack([_blockreal(t1[j][:, :h1]) for j in range(n2)])
    t1_real = np.concatenate([t1.real, t1.imag], axis=1)
    f2 = np.exp(-2j * np.pi * np.outer(np.arange(n2), np.arange(n2)) / n2)
    f2_fwd = _blockreal(f2)
    f2_inv = _blockreal(np.conj(f2))
    t4 = np.exp(2j * np.pi * (np.arange(h1)[None, :, None] * np.arange(n1)[None, None, :] / n1
                              + j2 * np.arange(n1)[None, None, :] / n)) / n
    t4 = np.stack([_blockreal(t4[j]) for j in range(n2)])
    as32 = lambda a: jnp.asarray(a, dtype=F32)
    return as32(t1_data), as32(t1_real), as32(f2_fwd), as32(f2_inv), as32(t4)


def _spectrum_kernel(k_ref, t1_ref, f2_ref, o_ref, a_ref, *, n1, n2):
    def stage1(j, carry):
        rows = k_ref[0, pl.ds(j, n1, stride=n2), :]
        a_ref[pl.ds(j, 2 * n1, stride=n2), :] = _dot(t1_ref[j], rows)
        return carry
    lax.fori_loop(0, n2, stage1, 0)

    def stage2(i, carry):
        re = a_ref[pl.ds(pl.multiple_of(i * n2, n2), n2), :]
        im = a_ref[pl.ds(pl.multiple_of((n1 + i) * n2, n2), n2), :]
        o_ref[0, i] = _dot(f2_ref[...], jnp.concatenate([re, im], axis=0))
        return carry
    lax.fori_loop(0, n1, stage2, 0)


def _filter_spectrum(kfilt):
    g, n, w = kfilt.shape
    n1, n2 = FFT_N1, n // FFT_N1
    _, t1_real, f2_fwd, _, _ = _fft_tables(n // 2)
    const = _const_spec
    return pl.pallas_call(
        functools.partial(_spectrum_kernel, n1=n1, n2=n2),
        out_shape=jax.ShapeDtypeStruct((g, n1, 2 * n2, w), F32),
        grid=(g, w // LANES),
        in_specs=[pl.BlockSpec((1, n, LANES), lambda gi, j: (gi, 0, j)),
                  const((n2, 2 * n1, n1)), const((2 * n2, 2 * n2))],
        out_specs=pl.BlockSpec((1, n1, 2 * n2, LANES), lambda gi, j: (gi, 0, 0, j)),
        scratch_shapes=[pltpu.VMEM((2 * n1 * n2, LANES), F32)],
        compiler_params=_cparams("arbitrary", "arbitrary"),
        name="hyena_filter_spectrum",
    )(kfilt, t1_real, f2_fwd)


def _fftconv_kernel(u_ref, m_ref, spec_ref, bias_ref, t1_ref, f2f_ref, f2i_ref, t4_ref, o_ref, a_ref,
                    *, n1, n2):
    h1 = n1 // 2

    def stage1(j, carry):
        za = u_ref[0, pl.ds(j, h1, stride=n2), :]
        zb = u_ref[1, pl.ds(j, h1, stride=n2), :]
        a_ref[pl.ds(j, 2 * n1, stride=n2), :] = _dot(t1_ref[j], jnp.concatenate([za, zb], axis=0))
        return carry
    lax.fori_loop(0, n2, stage1, 0)

    def stage2(i, carry):
        re_rows = pl.ds(pl.multiple_of(i * n2, n2), n2)
        im_rows = pl.ds(pl.multiple_of((n1 + i) * n2, n2), n2)
        x = _dot(f2f_ref[...], jnp.concatenate([a_ref[re_rows, :], a_ref[im_rows, :]], axis=0))
        xr, xi = x[:n2], x[n2:]
        kr, ki = spec_ref[0, i, :n2, :], spec_ref[0, i, n2:, :]
        y = jnp.concatenate([xr * kr - xi * ki, xr * ki + xi * kr], axis=0)
        b = _dot(f2i_ref[...], y)
        a_ref[re_rows, :] = b[:n2]
        a_ref[im_rows, :] = b[n2:]
        return carry
    lax.fori_loop(0, n1, stage2, 0)

    bias = bias_ref[0]

    def stage4(j, carry):
        y = _dot(t4_ref[j], a_ref[pl.ds(j, 2 * n1, stride=n2), :])
        rows = pl.ds(j, h1, stride=n2)
        for p in range(2):
            u = u_ref[p, rows, :]
            o_ref[p, rows, :] = m_ref[p, rows, :] * (y[p * h1:(p + 1) * h1] + bias * u)
        return carry
    lax.fori_loop(0, n2, stage4, 0)


def _fftconv_gated(u, u_col, mult, mult_col, spec, conv_idx, bias, seq_len):
    bsz = u.shape[0]
    w = spec.shape[-1]
    n = 2 * seq_len
    n1, n2 = FFT_N1, n // FFT_N1
    t1_data, _, f2_fwd, f2_inv, t4 = _fft_tables(seq_len)
    const = _const_spec
    return pl.pallas_call(
        functools.partial(_fftconv_kernel, n1=n1, n2=n2),
        out_shape=jax.ShapeDtypeStruct((bsz, seq_len, w), F32),
        grid=(w // LANES, bsz // 2),
        in_specs=[pl.BlockSpec((2, seq_len, LANES), lambda j, p: (p, 0, u_col + j)),
                  pl.BlockSpec((2, seq_len, LANES), lambda j, p: (p, 0, mult_col + j)),
                  pl.BlockSpec((1, n1, 2 * n2, LANES), lambda j, p: (conv_idx, 0, 0, j),
                               pipeline_mode=pl.Buffered(1)),
                  pl.BlockSpec((1, 1, LANES), lambda j, p: (conv_idx, 0, j)),
                  const((n2, 2 * n1, n1)), const((2 * n2, 2 * n2)), const((2 * n2, 2 * n2)),
                  const((n2, n1, 2 * n1))],
        out_specs=pl.BlockSpec((2, seq_len, LANES), lambda j, p: (p, 0, j)),
        scratch_shapes=[pltpu.VMEM((2 * n1 * n2, LANES), F32)],
        compiler_params=_cparams("arbitrary", "arbitrary"),
        name="hyena_fftconv",
    )(u, mult, spec, bias, t1_data, f2_fwd, f2_inv, t4)


@functools.lru_cache(maxsize=None)
def _small_fft_tables(seq_len):
    n = 2 * seq_len
    f = np.exp(-2j * np.pi * np.outer(np.arange(n), np.arange(n)) / n)
    fwd = _blockreal(f[:, :seq_len])
    real = np.concatenate([f.real, f.imag], axis=0)
    inv = _blockreal(np.conj(f)[:seq_len, :] / n)
    as32 = lambda a: jnp.asarray(a, dtype=F32)
    return as32(fwd), as32(real), as32(inv)


def _ctx_hyena_kernel(v_ref, x1_ref, x2_ref, k_ref, bias_ref, fwd_ref, real_ref, inv_ref, o_ref, *, seq_len):
    n = 2 * seq_len

    def conv(ua, ub, g):
        spec = _dot(real_ref[...], k_ref[g])
        x = _dot(fwd_ref[...], jnp.concatenate([ua, ub], axis=0))
        xr, xi, kr, ki = x[:n], x[n:], spec[:n], spec[n:]
        y = _dot(inv_ref[...], jnp.concatenate([xr * kr - xi * ki, xr * ki + xi * kr], axis=0))
        b = bias_ref[g]
        return y[:seq_len] + b * ua, y[seq_len:] + b * ub

    c1a, c1b = conv(v_ref[0], v_ref[1], 0)
    y1a, y1b = x1_ref[0] * c1a, x1_ref[1] * c1b
    c2a, c2b = conv(y1a, y1b, 1)
    o_ref[0] = x2_ref[0] * c2a
    o_ref[1] = x2_ref[1] * c2b


def _ctx_hyena(hyc, row_block, kfilt, bias, seq_len, w):
    bsz = hyc.shape[0]
    n = 2 * seq_len
    fwd, real, inv = _small_fft_tables(seq_len)
    nt = w // LANES
    col = lambda c0: pl.BlockSpec((2, seq_len, LANES), lambda j, p: (p, row_block, c0 + j))
    return pl.pallas_call(
        functools.partial(_ctx_hyena_kernel, seq_len=seq_len),
        out_shape=jax.ShapeDtypeStruct((bsz, seq_len, w), F32),
        grid=(nt, bsz // 2),
        in_specs=[col(0), col(nt), col(2 * nt),
                  pl.BlockSpec((2, n, LANES), lambda j, p: (0, 0, j)),
                  pl.BlockSpec((2, 1, LANES), lambda j, p: (0, 0, j)),
                  _const_spec((2 * n, 2 * seq_len)), _const_spec((2 * n, n)), _const_spec((2 * seq_len, 2 * n))],
        out_specs=pl.BlockSpec((2, seq_len, LANES), lambda j, p: (p, 0, j)),
        compiler_params=_cparams("arbitrary", "arbitrary"),
        name="hyena_ctx",
    )(hyc, hyc, hyc, kfilt, bias, fwd, real, inv)


def _rms(x, g):
    return x * lax.rsqrt(jnp.mean(x * x, axis=-1, keepdims=True) + NORM_EPS) * g


def _inproj_kernel(x_ref, mod_ref, g_ref, w_ref, *out_refs, widths):
    shift, scale = mod_ref[0, 0, 0:1, :], mod_ref[0, 0, 1:2, :]
    h = _rms(x_ref[0], g_ref[...]) * (1.0 + scale) + shift
    z = jnp.dot(h.astype(BF16), w_ref[...], preferred_element_type=F32)
    off = 0
    for o_ref, wd in zip(out_refs[:-1], widths[:-1]):
        o_ref[0] = z[:, off:off + wd]
        off += wd
    out_refs[-1][0] = _silu(z[:, off:])


def _inproj(xs, mods, norm_g, w_perm, widths, n_lat_tiles):
    bsz, s, d = xs.shape
    nt = s // ROW_TILE
    return pl.pallas_call(
        functools.partial(_inproj_kernel, widths=widths),
        out_shape=[jax.ShapeDtypeStruct((bsz, s, wd), F32) for wd in widths],
        grid=(bsz, nt),
        in_specs=[pl.BlockSpec((1, ROW_TILE, d), lambda b, t: (b, t, 0)),
                  pl.BlockSpec((1, 1, 3, d), lambda b, t: (b, t // n_lat_tiles, 0, 0)),
                  pl.BlockSpec((1, d), lambda b, t: (0, 0)),
                  _const_spec(w_perm.shape)],
        out_specs=[pl.BlockSpec((1, ROW_TILE, wd), lambda b, t: (b, t, 0)) for wd in widths],
        compiler_params=_cparams("arbitrary", "arbitrary"),
        name="in_projection",
    )(xs, mods, norm_g.reshape(1, d), w_perm)


def _halo_specs(width, n_tiles):
    per = ROW_TILE // 8
    prev = pl.BlockSpec((1, 8, width), lambda b, t: (b, jnp.maximum(t * per - 1, 0), 0))
    nxt = pl.BlockSpec((1, 8, width), lambda b, t: (b, jnp.minimum((t + 1) * per, n_tiles * per - 1), 0))
    return prev, nxt


def _short_conv_tile(z, prev8, next8, w, t, n_lat_tiles, n_tiles):
    first = jnp.logical_or(t == 0, t == n_lat_tiles)
    last = jnp.logical_or(t == n_lat_tiles - 1, t == n_tiles - 1)
    above = jnp.where(first, 0.0, prev8[7:8, :])
    below = jnp.where(last, 0.0, next8[0:1, :])
    row = lax.broadcasted_iota(jnp.int32, z.shape, 0)
    zm1 = jnp.where(row == 0, above, pltpu.roll(z, 1, 0))
    zp1 = jnp.where(row == z.shape[0] - 1, below, pltpu.roll(z, z.shape[0] - 1, 0))
    return zm1 * w[0:1, :] + z * w[1:2, :] + zp1 * w[2:3, :]


def _short_conv_kernel(z_ref, p_ref, n_ref, w_ref, o_ref, *, n_lat_tiles, n_tiles):
    o_ref[0] = _short_conv_tile(z_ref[0], p_ref[0], n_ref[0], w_ref[...], pl.program_id(1),
                                n_lat_tiles, n_tiles)


def _short_conv(z, w, n_lat_tiles):
    bsz, s, width = z.shape
    nt = s // ROW_TILE
    prev, nxt = _halo_specs(width, nt)
    tile = pl.BlockSpec((1, ROW_TILE, width), lambda b, t: (b, t, 0))
    return pl.pallas_call(
        functools.partial(_short_conv_kernel, n_lat_tiles=n_lat_tiles, n_tiles=nt),
        out_shape=jax.ShapeDtypeStruct(z.shape, F32),
        grid=(bsz, nt),
        in_specs=[tile, prev, nxt, pl.BlockSpec((3, width), lambda b, t: (0, 0))],
        out_specs=tile,
        compiler_params=_cparams("arbitrary", "arbitrary"),
        name="short_conv",
    )(z, z, z, w)


def _outproj_kernel(a_lat_ref, a_ctx_ref, b_ref, c_ref, d_ref, gt_ref, x_ref, mod_ref, w_ref, fg_ref, o_ref,
                    *, n_lat_tiles, final):
    is_ctx = pl.program_id(1) >= n_lat_tiles
    a = jnp.where(is_ctx, a_ctx_ref[0], a_lat_ref[0])
    mix = jnp.concatenate([a, b_ref[0], c_ref[0], d_ref[0]], axis=-1) * gt_ref[0]
    y = jnp.dot(mix.astype(BF16), w_ref[...], preferred_element_type=F32)
    x = x_ref[0] + mod_ref[0, 0, 2:3, :] * y
    o_ref[0] = _rms(x, fg_ref[...]) if final else x


def _outproj(a_lat, a_ctx, bmix, cmix, dmix, gates, xs, mods, w_out, final_g, n_lat_tiles, final):
    bsz, s, d = xs.shape
    wb = a_lat.shape[-1]
    nt = n_lat_tiles if final else s // ROW_TILE
    tile = lambda wd: pl.BlockSpec((1, ROW_TILE, wd), lambda b, t: (b, t, 0))
    return pl.pallas_call(
        functools.partial(_outproj_kernel, n_lat_tiles=n_lat_tiles, final=final),
        out_shape=jax.ShapeDtypeStruct((bsz, nt * ROW_TILE, d), F32),
        grid=(bsz, nt),
        in_specs=[pl.BlockSpec((1, ROW_TILE, wb), lambda b, t: (b, jnp.minimum(t, n_lat_tiles - 1), 0)),
                  pl.BlockSpec((1, ROW_TILE, wb), lambda b, t: (b, 0, 0)),
                  tile(wb), tile(wb), tile(wb), tile(4 * wb), tile(d),
                  pl.BlockSpec((1, 1, 3, d), lambda b, t: (b, t // n_lat_tiles, 0, 0)),
                  _const_spec(w_out.shape),
                  pl.BlockSpec((1, d), lambda b, t: (0, 0))],
        out_specs=tile(d),
        compiler_params=_cparams("arbitrary", "arbitrary"),
        name="out_projection",
    )(a_lat, a_ctx, bmix, cmix, dmix, gates, xs, mods, w_out, final_g.reshape(1, d))


def _rope_tables(seq_len, ctx_len):
    rows = seq_len // GRID_W
    row = jnp.repeat(jnp.arange(rows, dtype=F32), GRID_W)
    col = jnp.tile(jnp.arange(GRID_W, dtype=F32), rows)
    n_freq = HEAD_DIM // 4
    inv_freq = ROPE_THETA ** (-jnp.arange(n_freq, dtype=F32) / n_freq)
    ar, ac = row[:, None] * inv_freq, col[:, None] * inv_freq
    zero = jnp.zeros_like(ar)
    cos = jnp.concatenate([jnp.cos(ar), jnp.cos(ar), jnp.cos(ac), jnp.cos(ac)], axis=-1)
    sin_hi = jnp.concatenate([-jnp.sin(ar), zero, -jnp.sin(ac), zero], axis=-1)
    sin_lo = jnp.concatenate([zero, jnp.sin(ar), zero, jnp.sin(ac)], axis=-1)
    pad = lambda t, v: jnp.concatenate([t, jnp.full((ctx_len, HEAD_DIM), v, F32)], axis=0)
    return pad(cos, 1.0), pad(sin_hi, 0.0), pad(sin_lo, 0.0)


def _rope(x, cos, sin_hi, sin_lo):
    q = HEAD_DIM // 4
    w = x.shape[-1]
    return x * cos + pltpu.roll(x, w - q, 1) * sin_hi + pltpu.roll(x, q, 1) * sin_lo


def _head_mean_sq(x, bd):
    return _dot(x * x, bd)


def _qkprep_kernel(wa_ref, fa_ref, cos_ref, shi_ref, slo_ref, qg_ref, kg_ref, bd_ref,
                   qw_ref, kw_ref, vw_ref, qf_ref, kf_ref, vf_ref, *, wq, wk):
    nq, nk = wq // HEAD_DIM, wk // HEAD_DIM
    cos, shi, slo = cos_ref[...], shi_ref[...], slo_ref[...]
    tab = lambda t, n: jnp.concatenate([t] * n, axis=-1)
    cq, hq, lq = tab(cos, nq), tab(shi, nq), tab(slo, nq)
    ck, hk, lk = tab(cos, nk), tab(shi, nk), tab(slo, nk)
    scale = HEAD_DIM ** -0.5

    def emit(ref, val, n):
        for h in range(n):
            ref[0, h] = val[:, h * HEAD_DIM:(h + 1) * HEAD_DIM].astype(ref.dtype)

    wa = wa_ref[0]
    emit(qw_ref, _rope(wa[:, :wq], cq, hq, lq) * scale, nq)
    emit(kw_ref, _rope(wa[:, wq:wq + wk], ck, hk, lk), nk)
    emit(vw_ref, wa[:, wq + wk:], nk)
    fa = fa_ref[0]
    q, k = fa[:, :wq], fa[:, wq:wq + wk]
    bd = bd_ref[...]
    q = q * lax.rsqrt(_head_mean_sq(q, bd) + NORM_EPS) * qg_ref[...]
    k = k * lax.rsqrt(_head_mean_sq(k, bd[:wk, :wk]) + NORM_EPS) * kg_ref[...]
    emit(qf_ref, _rope(q, cq, hq, lq) * scale, nq)
    emit(kf_ref, _rope(k, ck, hk, lk), nk)
    emit(vf_ref, fa[:, wq + wk:], nk)


def _head_block_diag(width, value):
    h = np.arange(width) // HEAD_DIM
    return jnp.asarray((h[:, None] == h[None, :]) * value, dtype=F32)


def _qkprep(wa, fa, rope, q_gain, k_gain, wq, wk):
    bsz, s, _ = wa.shape
    nq, nk = wq // HEAD_DIM, wk // HEAD_DIM
    nt = s // ROW_TILE
    tile = lambda wd: pl.BlockSpec((1, ROW_TILE, wd), lambda b, t: (b, t, 0))
    tab = pl.BlockSpec((ROW_TILE, HEAD_DIM), lambda b, t: (t, 0))
    hm = lambda n: pl.BlockSpec((1, n, ROW_TILE, HEAD_DIM), lambda b, t: (b, 0, t, 0))
    shp = lambda n: jax.ShapeDtypeStruct((bsz, n, s, HEAD_DIM), BF16)
    return pl.pallas_call(
        functools.partial(_qkprep_kernel, wq=wq, wk=wk),
        out_shape=[shp(nq), shp(nk), shp(nk), shp(nq), shp(nk), shp(nk)],
        grid=(bsz, nt),
        in_specs=[tile(wq + 2 * wk), tile(wq + 2 * wk), tab, tab, tab,
                  pl.BlockSpec((1, wq), lambda b, t: (0, 0)), pl.BlockSpec((1, wk), lambda b, t: (0, 0)),
                  _const_spec((wq, wq))],
        out_specs=[hm(nq), hm(nk), hm(nk), hm(nq), hm(nk), hm(nk)],
        compiler_params=_cparams("arbitrary", "arbitrary"),
        name="qk_prep",
    )(wa, fa, *rope, jnp.tile(q_gain, nq).reshape(1, wq), jnp.tile(k_gain, nk).reshape(1, wk),
      _head_block_diag(wq, 1.0 / HEAD_DIM))


def _window_attn_kernel(q_ref, k_ref, v_ref, sink_ref, o_ref, *, seq_len, ctx_len, tq):
    t = pl.program_id(2)
    n_lat = seq_len // tq
    g = q_ref.shape[1]
    band = 2 * WINDOW + tq
    q = q_ref[0].reshape(g * tq, HEAD_DIM)
    kc, vc = k_ref[0, 0, seq_len:seq_len + ctx_len, :], v_ref[0, 0, seq_len:seq_len + ctx_len, :]
    start = pl.multiple_of(jnp.clip((t - 1) * tq, 0, seq_len - band), tq)
    kb, vb = k_ref[0, 0, pl.ds(start, band), :], v_ref[0, 0, pl.ds(start, band), :]
    nt_dot = lambda a, b: lax.dot_general(a, b, (((1,), (1,)), ((), ())), preferred_element_type=F32)
    s_ctx = nt_dot(q, kc)
    s_loc = nt_dot(q, kb)
    qpos = t * tq + lax.broadcasted_iota(jnp.int32, (g, tq, band), 1).reshape(g * tq, band)
    kpos = start + lax.broadcasted_iota(jnp.int32, (g * tq, band), 1)
    valid = jnp.logical_and(jnp.abs(qpos - kpos) <= WINDOW, t < n_lat)
    s_loc = jnp.where(valid, s_loc, NEG_INF)
    sink = sink_ref[0]
    m = jnp.maximum(jnp.maximum(jnp.max(s_ctx, axis=-1, keepdims=True),
                                jnp.max(s_loc, axis=-1, keepdims=True)), sink)
    p_ctx, p_loc = jnp.exp(s_ctx - m), jnp.exp(s_loc - m)
    denom = (jnp.sum(p_ctx, axis=-1, keepdims=True) + jnp.sum(p_loc, axis=-1, keepdims=True)
             + jnp.exp(sink - m))
    out = (jnp.dot(p_ctx.astype(BF16), vc, preferred_element_type=F32)
           + jnp.dot(p_loc.astype(BF16), vb, preferred_element_type=F32)) / denom
    o_ref[0] = jnp.concatenate([out[h * tq:(h + 1) * tq] for h in range(g)], axis=-1)


def _window_attention(q, k, v, sink, seq_len, ctx_len):
    bsz, nq, s, _ = q.shape
    nkv = k.shape[1]
    g = nq // nkv
    tq = WINDOW
    sink_rows = jnp.repeat(sink.astype(F32).reshape(nkv, g), tq, axis=1).reshape(nkv, g * tq, 1)
    kv = pl.BlockSpec((1, 1, s, HEAD_DIM), lambda b, h, t: (b, h, 0, 0))
    return pl.pallas_call(
        functools.partial(_window_attn_kernel, seq_len=seq_len, ctx_len=ctx_len, tq=tq),
        out_shape=jax.ShapeDtypeStruct((bsz, s, nq * HEAD_DIM), F32),
        grid=(bsz, nkv, s // tq),
        in_specs=[pl.BlockSpec((1, g, tq, HEAD_DIM), lambda b, h, t: (b, h, t, 0)), kv, kv,
                  pl.BlockSpec((1, g * tq, 1), lambda b, h, t: (h, 0, 0))],
        out_specs=pl.BlockSpec((1, tq, g * HEAD_DIM), lambda b, h, t: (b, t, h)),
        compiler_params=_cparams("arbitrary", "arbitrary", "arbitrary"),
        name="window_attention",
    )(q, k, v, sink_rows)


def _dense_attn_kernel(q_ref, k_ref, v_ref, o_ref, *, seq_len, tq, tk):
    t = pl.program_id(2)
    n_lat = seq_len // tq
    g = q_ref.shape[1]
    n_kblocks = k_ref.shape[2] // tk
    q = q_ref[0].reshape(g * tq, HEAD_DIM)
    first = jnp.where(t < n_lat, 0, seq_len // tk)

    def body(j, carry):
        m, l, acc = carry
        rows = pl.ds(pl.multiple_of(j * tk, tk), tk)
        s = lax.dot_general(q, k_ref[0, 0, rows, :], (((1,), (1,)), ((), ())), preferred_element_type=F32)
        m_new = jnp.maximum(m, jnp.max(s, axis=-1, keepdims=True))
        alpha = jnp.exp(m - m_new)
        p = jnp.exp(s - m_new)
        l = alpha * l + jnp.sum(p, axis=-1, keepdims=True)
        acc = alpha * acc + jnp.dot(p.astype(BF16), v_ref[0, 0, rows, :], preferred_element_type=F32)
        return m_new, l, acc

    init = (jnp.full((g * tq, 1), NEG_INF, F32), jnp.zeros((g * tq, 1), F32),
            jnp.zeros((g * tq, HEAD_DIM), F32))
    _, l, acc = lax.fori_loop(first, n_kblocks, body, init)
    out = acc / l
    o_ref[0] = jnp.concatenate([out[h * tq:(h + 1) * tq] for h in range(g)], axis=-1)


def _dense_attention(q, k, v, seq_len):
    bsz, nq, s, _ = q.shape
    nkv = k.shape[1]
    g = nq // nkv
    tq = tk = ROW_TILE
    kv = pl.BlockSpec((1, 1, s, HEAD_DIM), lambda b, h, t: (b, h, 0, 0))
    return pl.pallas_call(
        functools.partial(_dense_attn_kernel, seq_len=seq_len, tq=tq, tk=tk),
        out_shape=jax.ShapeDtypeStruct((bsz, s, nq * HEAD_DIM), F32),
        grid=(bsz, nkv, s // tq),
        in_specs=[pl.BlockSpec((1, g, tq, HEAD_DIM), lambda b, h, t: (b, h, t, 0)), kv, kv],
        out_specs=pl.BlockSpec((1, tq, g * HEAD_DIM), lambda b, h, t: (b, t, h)),
        compiler_params=_cparams("arbitrary", "arbitrary", "arbitrary"),
        name="dense_attention",
    )(q, k, v)


def _softplus(x):
    return jnp.maximum(x, 0.0) + jnp.log(1.0 + jnp.exp(-jnp.abs(x)))


def _split(a):
    hi = a.astype(BF16).astype(F32)
    return hi, a - hi


def _lhs3(a):
    hi, lo = _split(a)
    return jnp.concatenate([hi, lo, hi], axis=-1).astype(BF16)


def _rhs3(b):
    hi, lo = _split(b)
    return jnp.concatenate([hi, hi, lo], axis=-2).astype(BF16)


def _bmm(l3, r3):
    return lax.dot_general(l3, r3, (((2,), (1,)), ((0,), (0,))), preferred_element_type=F32)


RW_SLOTS_PER_STEP = 8


def _rwkv_par_kernel(z_ref, prev_ref, next_ref, cw_ref, w0_ref, wup_ref, a0_ref, aup_ref, kk_ref, ka_ref,
                     bd_ref, trif_ref, trib_ref, ones_ref,
                     rp3_ref, yvq_ref, rvk_ref,
                     lbig_ref, rbig_ref, at_ref, rt_ref, vv3_ref, bct_ref, kct_ref, wt_ref,
                     *, n_lat_tiles, n_tiles, w, lora):
    tc = RW_CHUNK
    nc, nh = ROW_TILE // tc, w // HEAD_DIM
    nb = RW_SLOTS_PER_STEP
    z = _short_conv_tile(z_ref[0], prev_ref[0], next_ref[0], cw_ref[...], pl.program_id(1),
                         n_lat_tiles, n_tiles)
    r, k, v = z[:, :w], z[:, w:2 * w], z[:, 2 * w:3 * w]
    w_low = jnp.tanh(z[:, 3 * w:3 * w + lora])
    a_low = z[:, 3 * w + lora:]
    kk = k * kk_ref[...]
    kk = kk / jnp.maximum(jnp.sqrt(_dot(kk * kk, bd_ref[...])), 1e-12)
    ksum = jnp.zeros_like(k)
    for d in range(2):
        w_log = -_softplus(-(w0_ref[d:d + 1, :] + _dot(w_low, wup_ref[d]))) - 0.5
        lw = -jnp.exp(w_log)
        a = 1.0 / (1.0 + jnp.exp(-(a0_ref[d:d + 1, :] + _dot(a_low, aup_ref[d]))))
        kd = k * (1.0 + (a - 1.0) * ka_ref[...])
        ksum = ksum + kd
        c = _dot(trif_ref[...] if d == 0 else trib_ref[...], lw)
        ctot = _dot(ones_ref[...], lw)
        e_neg, e_rem = jnp.exp(-c), jnp.exp(ctot - c)
        b = kk * a
        at, rt = -kk * jnp.exp(c - lw), r * jnp.exp(c)
        w_tot = jnp.exp(ctot)
        at_s, rt_s, bh_s, kh_s, v_s = _split(at), _split(rt), _split(b * e_neg), _split(kd * e_neg), _split(v)
        bct_s, kct_s = _split((b * e_rem).T), _split((kd * e_rem).T)
        for ci in range(nc):
            for h in range(nh):
                slot = (d * nc + ci) * nh + h
                rows, cols = slice(ci * tc, (ci + 1) * tc), slice(h * HEAD_DIM, (h + 1) * HEAD_DIM)
                lanes3 = lambda s, order: jnp.concatenate([s[i][rows, cols] for i in order],
                                                          axis=-1).astype(BF16)
                lbig_ref[slot, :tc, :] = lanes3(at_s, (0, 1, 0))
                lbig_ref[slot, tc:, :] = lanes3(rt_s, (0, 1, 0))
                rbig_ref[slot, :tc, :] = lanes3(bh_s, (0, 0, 1))
                rbig_ref[slot, tc:, :] = lanes3(kh_s, (0, 0, 1))
                at_ref[slot] = at[rows, cols]
                rt_ref[slot] = rt[rows, cols]
                vv3_ref[slot] = jnp.concatenate([v_s[i][rows, cols] for i in (0, 0, 1)], axis=0).astype(BF16)
                bct_ref[slot] = jnp.concatenate([bct_s[i][cols, rows] for i in (0, 1, 0)], axis=-1).astype(BF16)
                kct_ref[slot] = jnp.concatenate([kct_s[i][cols, rows] for i in (0, 1, 0)], axis=-1).astype(BF16)
                wt_ref[slot] = w_tot[ci * tc:ci * tc + 8, cols]
    rvk_ref[0] = jnp.concatenate([r, v, ksum], axis=-1)

    row = lax.broadcasted_iota(jnp.int32, (tc, tc), 0)
    col = lax.broadcasted_iota(jnp.int32, (tc, tc), 1)
    eye = row == col

    def level_mask(s):
        return jnp.logical_and(row // (2 * s) == col // (2 * s), row // s != col // s)

    steps_per_dir = nc * nh // nb

    def chunk_group(g, carry):
        slots = pl.ds(pl.multiple_of(g * nb, nb), nb)
        d = g // steps_per_dir
        fwd = d == 0
        before = jnp.logical_or(jnp.logical_and(fwd, row > col),
                                jnp.logical_and(jnp.logical_not(fwd), row < col))
        upto = jnp.logical_or(before, eye)
        big = lax.dot_general(lbig_ref[slots], rbig_ref[slots], (((2,), (2,)), ((0,), (0,))),
                              preferred_element_type=F32)
        a_ab = jnp.where(before, big[:, :tc, :tc], 0.0)
        a_ak = jnp.where(before, big[:, :tc, tc:], 0.0)
        a_rb = jnp.where(upto, big[:, tc:, :tc], 0.0)
        a_rk = jnp.where(upto, big[:, tc:, tc:], 0.0)
        x = jnp.where(eye, 1.0, jnp.where(level_mask(1), a_ab, 0.0))
        s = 2
        while s < tc:
            half = _bmm(_lhs3(x), _rhs3(jnp.where(level_mask(s), a_ab, 0.0)))
            x = x + _bmm(_lhs3(half), _rhs3(x))
            s *= 2
        vv3 = vv3_ref[slots]
        akrk = _bmm(jnp.concatenate([_lhs3(a_ak), _lhs3(a_rk)], axis=1), vv3)
        xa3 = _rhs3(_bmm(_lhs3(x), _rhs3(jnp.concatenate([at_ref[slots], akrk[:, :tc]], axis=-1))))
        ra = _bmm(_lhs3(a_rb), xa3)
        pq = _bmm(bct_ref[slots], xa3)
        rp = rt_ref[slots] + ra[:, :, :HEAD_DIM]
        yv = ra[:, :, HEAD_DIM:] + akrk[:, tc:]
        p = jnp.where(eye, wt_ref[slots][:, 0:1, :], 0.0) + pq[:, :, :HEAD_DIM]
        q = pq[:, :, HEAD_DIM:] + _bmm(kct_ref[slots], vv3)
        rp3 = jnp.concatenate([_lhs3(rp), _lhs3(p)], axis=1)
        yvq = jnp.concatenate([yv, q], axis=1)
        for j in range(nb):
            ci = (g % steps_per_dir) * (nb // nh) + j // nh
            rp3_ref[0, d, j % nh, ci] = rp3[j]
            yvq_ref[0, d, j % nh, ci] = yvq[j]
        return carry
    lax.fori_loop(0, 2 * steps_per_dir, chunk_group, 0)


def _chunk_matrices():
    t = np.arange(ROW_TILE)
    same = (t[:, None] // RW_CHUNK) == (t[None, :] // RW_CHUNK)
    as32 = lambda m: jnp.asarray(m, dtype=F32)
    return (as32(same & (t[None, :] <= t[:, None])), as32(same & (t[None, :] >= t[:, None])), as32(same))


def _rwkv_par(rw, conv_w, w0, w_up, a0, a_up, k_k, k_a, w, n_lat_tiles):
    bsz, s, width = rw.shape
    nt = s // ROW_TILE
    nh, nc = w // HEAD_DIM, ROW_TILE // RW_CHUNK
    lora = w_up.shape[1]
    prev, nxt = _halo_specs(width, nt)
    full = lambda a: pl.BlockSpec(a.shape, lambda b, t: (0,) * a.ndim)
    trif, trib, ones = _chunk_matrices()
    bd = _head_block_diag(w, 1.0)
    vec = lambda a: a.reshape(1, w)
    n_chunks = s // RW_CHUNK
    nslots = 2 * nc * nh
    tc3 = 3 * RW_CHUNK
    consts = (conv_w, w0, w_up, a0, a_up, vec(k_k), vec(k_a), bd, trif, trib, ones)
    return pl.pallas_call(
        functools.partial(_rwkv_par_kernel, n_lat_tiles=n_lat_tiles, n_tiles=nt, w=w, lora=lora),
        out_shape=[jax.ShapeDtypeStruct((bsz, 2, nh, n_chunks, 2 * RW_CHUNK, 3 * HEAD_DIM), BF16),
                   jax.ShapeDtypeStruct((bsz, 2, nh, n_chunks, 2 * RW_CHUNK, HEAD_DIM), F32),
                   jax.ShapeDtypeStruct((bsz, s, 3 * w), F32)],
        grid=(bsz, nt),
        in_specs=[pl.BlockSpec((1, ROW_TILE, width), lambda b, t: (b, t, 0)), prev, nxt]
                 + [full(a) for a in consts],
        out_specs=[pl.BlockSpec((1, 2, nh, nc, 2 * RW_CHUNK, 3 * HEAD_DIM), lambda b, t: (b, 0, 0, t, 0, 0)),
                   pl.BlockSpec((1, 2, nh, nc, 2 * RW_CHUNK, HEAD_DIM), lambda b, t: (b, 0, 0, t, 0, 0)),
                   pl.BlockSpec((1, ROW_TILE, 3 * w), lambda b, t: (b, t, 0))],
        scratch_shapes=[pltpu.VMEM((nslots, 2 * RW_CHUNK, tc3), BF16),
                        pltpu.VMEM((nslots, 2 * RW_CHUNK, tc3), BF16),
                        pltpu.VMEM((nslots, RW_CHUNK, HEAD_DIM), F32),
                        pltpu.VMEM((nslots, RW_CHUNK, HEAD_DIM), F32),
                        pltpu.VMEM((nslots, tc3, HEAD_DIM), BF16),
                        pltpu.VMEM((nslots, HEAD_DIM, tc3), BF16),
                        pltpu.VMEM((nslots, HEAD_DIM, tc3), BF16),
                        pltpu.VMEM((nslots, 8, HEAD_DIM), F32)],
        compiler_params=_cparams("arbitrary", "arbitrary"),
        name="rwkv_chunk_prep",
    )(rw, rw, rw, *consts)


def _rwkv_seq_kernel(rp3f_ref, yvqf_ref, rp3b_ref, yvqb_ref, yf_ref, yb_ref, g_ref):
    @pl.when(pl.program_id(1) == 0)
    def _():
        g_ref[...] = jnp.zeros_like(g_ref)

    nh = g_ref.shape[1]
    tc = RW_CHUNK
    for d, (rp3, yvq, y) in enumerate(((rp3f_ref, yvqf_ref, yf_ref), (rp3b_ref, yvqb_ref, yb_ref))):
        for h in range(nh):
            out = jnp.dot(rp3[0, 0, h, 0], _rhs3(g_ref[d, h]), preferred_element_type=F32) + yvq[0, 0, h, 0]
            y[0, h] = out[:tc]
            g_ref[d, h] = out[tc:]


def _rwkv_seq(rp3, yvq, n_lat_chunks):
    bsz, _, nh, n_chunks = rp3.shape[:4]
    n_ctx = n_chunks - n_lat_chunks
    order = (lambda i: jnp.where(i < n_ctx, n_lat_chunks + i, i - n_ctx),
             lambda i: n_chunks - 1 - i)
    blk = lambda d, a: pl.BlockSpec((1, 1, nh, 1) + a.shape[4:], lambda b, i: (b, d, 0, order[d](i), 0, 0))
    out = lambda d: pl.BlockSpec((1, nh, RW_CHUNK, HEAD_DIM), lambda b, i: (b, 0, order[d](i), 0))
    shp = jax.ShapeDtypeStruct((bsz, nh, n_chunks * RW_CHUNK, HEAD_DIM), F32)
    return pl.pallas_call(
        _rwkv_seq_kernel,
        out_shape=[shp, shp],
        grid=(bsz, n_chunks),
        in_specs=[blk(0, rp3), blk(0, yvq), blk(1, rp3), blk(1, yvq)],
        out_specs=[out(0), out(1)],
        scratch_shapes=[pltpu.VMEM((2, nh, HEAD_DIM, HEAD_DIM), F32)],
        compiler_params=_cparams("arbitrary", "arbitrary"),
        name="rwkv_state_scan",
    )(rp3, yvq, rp3, yvq)


def _rwkv_out_kernel(yf_ref, yb_ref, rvk_ref, rk_ref, g_ref, b_ref, o_ref, *, w):
    nh = w // HEAD_DIM
    rvk = rvk_ref[0]
    r, v, ksum = rvk[:, :w], rvk[:, w:2 * w], rvk[:, 2 * w:]
    rkk = r * ksum * rk_ref[...]
    normed, bonus = [], []
    for h in range(nh):
        cols = slice(h * HEAD_DIM, (h + 1) * HEAD_DIM)
        y = yf_ref[0, h] + yb_ref[0, h]
        mu = jnp.mean(y, axis=-1, keepdims=True)
        var = jnp.mean(jnp.square(y - mu), axis=-1, keepdims=True)
        normed.append((y - mu) * lax.rsqrt(var + RW_GN_EPS))
        bonus.append(jnp.sum(rkk[:, cols], axis=-1, keepdims=True) * v[:, cols])
    o_ref[0] = (jnp.concatenate(normed, axis=-1) * g_ref[...] + b_ref[...]
                + jnp.concatenate(bonus, axis=-1))


def _rwkv_out(yf, yb, rvk, r_k, ln_g, ln_b, w):
    bsz, nh, s, _ = yf.shape
    hm = pl.BlockSpec((1, nh, ROW_TILE, HEAD_DIM), lambda b, t: (b, 0, t, 0))
    vec = pl.BlockSpec((1, w), lambda b, t: (0, 0))
    return pl.pallas_call(
        functools.partial(_rwkv_out_kernel, w=w),
        out_shape=jax.ShapeDtypeStruct((bsz, s, w), F32),
        grid=(bsz, s // ROW_TILE),
        in_specs=[hm, hm, pl.BlockSpec((1, ROW_TILE, 3 * w), lambda b, t: (b, t, 0)), vec, vec, vec],
        out_specs=pl.BlockSpec((1, ROW_TILE, w), lambda b, t: (b, t, 0)),
        compiler_params=_cparams("arbitrary", "arbitrary"),
        name="rwkv_readout",
    )(yf, yb, rvk, r_k.reshape(1, w), ln_g.reshape(1, w), ln_b.reshape(1, w))


def _hyena_two_sided_filters(seq_len, fw1, fb1, freq, fw2, fb2, fw3, width):
    bands = (fw1.shape[0] - 1) // 2
    mm = functools.partial(jnp.matmul, precision=HI)
    t = jnp.linspace(0.0, 1.0, seq_len, dtype=F32)[:, None]
    wpos = (2.0 * math.pi / seq_len) * jnp.arange(seq_len, dtype=F32)[:, None]
    f = jnp.linspace(1e-4, bands - 1, bands, dtype=F32)[None, :]
    z = jnp.concatenate([t, jnp.cos(f * wpos), jnp.sin(f * wpos)], axis=-1)
    h = jnp.sin(freq * (mm(z, fw1) + fb1))
    h = jnp.sin(freq * (mm(h, fw2) + fb2))
    h = mm(h, fw3).reshape(seq_len, 2, 2, width)
    max_decay = math.log(HY_DECAY_TARGET) / HY_FAST_DECAY
    min_decay = math.log(HY_DECAY_TARGET) / HY_SLOW_DECAY
    deltas = jnp.linspace(min_decay, max_decay, width, dtype=F32)
    h = h * jnp.exp(-t * jnp.abs(deltas))[:, None, None, :]
    h = h / jnp.sum(jnp.abs(h), axis=(0, 2), keepdims=True)
    fwd, bwd = h[:, :, 0], h[:, :, 1]
    k = jnp.concatenate([fwd, jnp.zeros_like(fwd[:1]), bwd[:0:-1]], axis=0)
    return jnp.moveaxis(k, 1, 0)


def kernel(x, c, ctx, c_ctx, mod_w, mod_b, norm_g, w_in, w_out, hy_conv, hy_fw1, hy_fb1, hy_freq, hy_fw2,
           hy_fb2, hy_fw3, hy_bias, rw_conv, rw_w0, rw_w_up, rw_a0, rw_a_up, rw_k_k, rw_k_a, rw_r_k,
           rw_ln_g, rw_ln_b, wa_sink, fa_q_norm, fa_k_norm, final_g):
    bsz, seq_len, d = x.shape
    ctx_len = ctx.shape[1]
    depth = w_in.shape[0]
    w_hy = hy_bias.shape[-1]
    w_rw = rw_w0.shape[-1]
    n_wa_heads = wa_sink.shape[-1]
    w_q = n_wa_heads * HEAD_DIM
    w_kv = w_q // 2
    lora = rw_w_up.shape[2] + rw_a_up.shape[2]
    branch_w = (3 * w_hy, 3 * w_rw + lora, w_q + 2 * w_kv, w_q + 2 * w_kv)
    gate_w = (w_hy, w_rw, w_q, w_q)
    assert seq_len % ROW_TILE == 0 and ctx_len % ROW_TILE == 0 and bsz % 2 == 0
    n_lat_tiles = seq_len // ROW_TILE

    starts = np.cumsum([0] + [bw + gw for bw, gw in zip(branch_w, gate_w)])
    cols = np.concatenate([np.arange(s0, s0 + bw) for s0, bw in zip(starts, branch_w)]
                          + [np.arange(s0 + bw, s0 + bw + gw) for s0, bw, gw in zip(starts, branch_w, gate_w)])
    w_in_p = w_in[:, :, cols].astype(BF16)
    w_out_b = w_out.astype(BF16)
    widths = branch_w + (sum(gate_w),)

    pad_rows = (-(bsz + 1)) % 8
    cond = jnp.concatenate([c, c_ctx[None], jnp.zeros((pad_rows, d), F32)], axis=0)
    mod = _modulation(cond, mod_w, mod_b)
    mod_lat = mod[:, :bsz].reshape(depth, bsz, 3, d)
    mod_ctx = jnp.broadcast_to(mod[:, bsz].reshape(depth, 1, 3, d), (depth, bsz, 3, d))
    mods = jnp.stack([mod_lat, mod_ctx], axis=2)

    rope = _rope_tables(seq_len, ctx_len)
    xs = jnp.concatenate([x, ctx], axis=1)
    for l in range(depth):
        last = l == depth - 1
        hy, rw, wa, fa, gates = _inproj(xs, mods[l], norm_g[l], w_in_p[l], widths, n_lat_tiles)

        hyc = _short_conv(hy, hy_conv[l], n_lat_tiles)
        filt = functools.partial(_hyena_two_sided_filters, fw1=hy_fw1[l], fb1=hy_fb1[l], freq=hy_freq[l],
                                 fw2=hy_fw2[l], fb2=hy_fb2[l], fw3=hy_fw3[l], width=w_hy)
        bias = hy_bias[l].reshape(2, 1, w_hy)
        spec = _filter_spectrum(filt(seq_len))
        nt_hy = w_hy // LANES
        y1 = _fftconv_gated(hyc, 0, hyc, nt_hy, spec, 0, bias, seq_len)
        a_lat = _fftconv_gated(y1, 0, hyc, 2 * nt_hy, spec, 1, bias, seq_len)
        a_ctx = a_lat if last else _ctx_hyena(hyc, seq_len // ctx_len, filt(ctx_len), bias, ctx_len, w_hy)

        rp3, yvq, rvk = _rwkv_par(rw, rw_conv[l], rw_w0[l], rw_w_up[l], rw_a0[l], rw_a_up[l],
                                  rw_k_k[l], rw_k_a[l], w_rw, n_lat_tiles)
        yf, yb = _rwkv_seq(rp3, yvq, seq_len // RW_CHUNK)
        b_mix = _rwkv_out(yf, yb, rvk, rw_r_k[l], rw_ln_g[l], rw_ln_b[l], w_rw)

        qw, kw, vw, qf, kf, vf = _qkprep(wa, fa, rope, fa_q_norm[l], fa_k_norm[l], w_q, w_kv)
        c_mix = _window_attention(qw, kw, vw, wa_sink[l], seq_len, ctx_len)
        d_mix = _dense_attention(qf, kf, vf, seq_len)

        xs = _outproj(a_lat, a_ctx, b_mix, c_mix, d_mix, gates, xs, mods[l], w_out_b[l], final_g,
                      n_lat_tiles, last)
    return xs
```

```python
import functools
import math

import jax
import jax.numpy as jnp
import numpy as np
from jax import lax
from jax.experimental import pallas as pl
from jax.experimental.pallas import tpu as pltpu

HEAD_DIM = 64
GRID_W = 64
WINDOW = 128
NORM_EPS = 1e-6
RW_GN_EPS = 64e-5
NEG_INF = -1e30
ROPE_THETA = 10000.0
HY_FAST_DECAY = 0.3
HY_SLOW_DECAY = 1.5
HY_DECAY_TARGET = 1e-2

ROW_TILE = 256
LANES = 128
FFT_N1 = 64
FFT_UNROLL = 8
FFT_PLANE_PAD = 8
RW_CHUNK = 64
VMEM_LIMIT = 60 * 1024 * 1024

F32 = jnp.float32
BF16 = jnp.bfloat16
HI = lax.Precision.HIGHEST


def _dot(a, b):
    return jnp.dot(a, b, preferred_element_type=F32, precision=HI)


def _dot_nt(a, b):
    return lax.dot_general(a, b, (((1,), (1,)), ((), ())), preferred_element_type=F32, precision=HI)


def _dot_tn(a, b):
    return lax.dot_general(a, b, (((0,), (0,)), ((), ())), preferred_element_type=F32, precision=HI)


def _cparams(*sem):
    return pltpu.CompilerParams(dimension_semantics=sem, vmem_limit_bytes=VMEM_LIMIT)


def _const_spec(shape):
    return pl.BlockSpec(shape, lambda *_: (0,) * len(shape), pipeline_mode=pl.Buffered(1))


def _silu(x):
    return x * (1.0 / (1.0 + jnp.exp(-x)))


def _mod_kernel(c_ref, w_ref, b_ref, o_ref):
    o_ref[0] = _dot(_silu(c_ref[...]), w_ref[0]) + b_ref[0]


def _modulation(cond, mod_w, mod_b):
    depth, d, d3 = mod_w.shape
    rows = cond.shape[0]
    return pl.pallas_call(
        _mod_kernel,
        out_shape=jax.ShapeDtypeStruct((depth, rows, d3), F32),
        grid=(depth,),
        in_specs=[pl.BlockSpec((rows, d), lambda l: (0, 0)),
                  pl.BlockSpec((1, d, d3), lambda l: (l, 0, 0)),
                  pl.BlockSpec((1, 1, d3), lambda l: (l, 0, 0))],
        out_specs=pl.BlockSpec((1, rows, d3), lambda l: (l, 0, 0)),
        compiler_params=_cparams("arbitrary"),
        name="modulation",
    )(cond, mod_w, mod_b.reshape(depth, 1, d3))


def _blockreal(m):
    return np.block([[m.real, -m.imag], [m.imag, m.real]])


@functools.lru_cache(maxsize=None)
def _fft_tables(seq_len):
    n = 2 * seq_len
    n1, n2 = FFT_N1, n // FFT_N1
    h1 = n1 // 2
    j2 = np.arange(n2)[:, None, None]
    k1 = np.arange(n1)[None, :, None]
    t1 = np.exp(-2j * np.pi * (j2 * k1 / n + k1 * np.arange(n1)[None, None, :] / n1))
    t1_data = np.stack([_blockreal(t1[j][:, :h1]) for j in range(n2)])
    t1_real = np.concatenate([t1.real, t1.imag], axis=1)
    f2 = np.exp(-2j * np.pi * np.outer(np.arange(n2), np.arange(n2)) / n2)
    f2_fwd = _blockreal(f2)
    f2_inv = _blockreal(np.conj(f2))
    t4 = np.exp(2j * np.pi * (np.arange(h1)[None, :, None] * np.arange(n1)[None, None, :] / n1
                              + j2 * np.arange(n1)[None, None, :] / n)) / n
    t4 = np.stack([_blockreal(t4[j]) for j in range(n2)])
    return tuple(_lhs3_table(t) for t in (t1_data, t1_real, f2_fwd, f2_inv, t4))


def _lhs3_table(m):
    hi = m.astype(np.float32).astype(BF16)
    lo = (m - hi.astype(np.float64)).astype(np.float32).astype(BF16)
    return np.concatenate([hi, lo, hi], axis=-1)


def _mm3(tbl3, x):
    return jnp.dot(tbl3, _rhs3(x), preferred_element_type=F32)


def _spectrum_kernel(k_ref, t1_ref, f2_ref, o_ref, a_ref, *, n1, n2):
    def stage1(j, carry):
        rows = k_ref[0, pl.ds(j, n1, stride=n2), :]
        a_ref[pl.ds(j, 2 * n1, stride=n2 + FFT_PLANE_PAD), :] = _mm3(t1_ref[j], rows)
        return carry
    lax.fori_loop(0, n2, stage1, 0, unroll=FFT_UNROLL)

    def stage2(i, carry):
        pitch = n2 + FFT_PLANE_PAD
        re = a_ref[pl.ds(pl.multiple_of(i * pitch, 8), n2), :]
        im = a_ref[pl.ds(pl.multiple_of((n1 + i) * pitch, 8), n2), :]
        o_ref[0, i] = _mm3(f2_ref[...], jnp.concatenate([re, im], axis=0))
        return carry
    lax.fori_loop(0, n1, stage2, 0, unroll=2)


def _filter_spectrum(kfilt):
    g, n, w = kfilt.shape
    n1, n2 = FFT_N1, n // FFT_N1
    _, t1_real, f2_fwd, _, _ = _fft_tables(n // 2)
    const = _const_spec
    return pl.pallas_call(
        functools.partial(_spectrum_kernel, n1=n1, n2=n2),
        out_shape=jax.ShapeDtypeStruct((g, n1, 2 * n2, w), F32),
        grid=(g, w // LANES),
        in_specs=[pl.BlockSpec((1, n, LANES), lambda gi, j: (gi, 0, j)),
                  const(t1_real.shape), const(f2_fwd.shape)],
        out_specs=pl.BlockSpec((1, n1, 2 * n2, LANES), lambda gi, j: (gi, 0, 0, j)),
        scratch_shapes=[pltpu.VMEM((2 * n1 * (n2 + FFT_PLANE_PAD), LANES), F32)],
        compiler_params=_cparams("arbitrary", "arbitrary"),
        name="hyena_filter_spectrum",
    )(kfilt, t1_real, f2_fwd)


def _fftconv_kernel(u_ref, m_ref, spec_ref, bias_ref, t1_ref, f2f_ref, f2i_ref, t4_ref, o_ref, a_ref,
                    *, n1, n2):
    h1 = n1 // 2

    def stage1(j, carry):
        za = u_ref[0, pl.ds(j, h1, stride=n2), :]
        zb = u_ref[1, pl.ds(j, h1, stride=n2), :]
        a_ref[pl.ds(j, 2 * n1, stride=n2 + FFT_PLANE_PAD), :] = _mm3(t1_ref[j], jnp.concatenate([za, zb], axis=0))
        return carry
    lax.fori_loop(0, n2, stage1, 0, unroll=FFT_UNROLL)

    def stage2(i, carry):
        pitch = n2 + FFT_PLANE_PAD
        re_rows = pl.ds(pl.multiple_of(i * pitch, 8), n2)
        im_rows = pl.ds(pl.multiple_of((n1 + i) * pitch, 8), n2)
        x = _mm3(f2f_ref[...], jnp.concatenate([a_ref[re_rows, :], a_ref[im_rows, :]], axis=0))
        xr, xi = x[:n2], x[n2:]
        kr, ki = spec_ref[0, i, :n2, :], spec_ref[0, i, n2:, :]
        y = jnp.concatenate([xr * kr - xi * ki, xr * ki + xi * kr], axis=0)
        b = _mm3(f2i_ref[...], y)
        a_ref[re_rows, :] = b[:n2]
        a_ref[im_rows, :] = b[n2:]
        return carry
    lax.fori_loop(0, n1, stage2, 0, unroll=2)

    bias = bias_ref[0]

    def stage4(j, carry):
        y = _mm3(t4_ref[j], a_ref[pl.ds(j, 2 * n1, stride=n2 + FFT_PLANE_PAD), :])
        rows = pl.ds(j, h1, stride=n2)
        for p in range(2):
            u = u_ref[p, rows, :]
            o_ref[p, rows, :] = m_ref[p, rows, :] * (y[p * h1:(p + 1) * h1] + bias * u)
        return carry
    lax.fori_loop(0, n2, stage4, 0, unroll=FFT_UNROLL)


def _fftconv_gated(u, u_col, mult, mult_col, spec, conv_idx, bias, seq_len):
    bsz = u.shape[0]
    w = spec.shape[-1]
    n = 2 * seq_len
    n1, n2 = FFT_N1, n // FFT_N1
    t1_data, _, f2_fwd, f2_inv, t4 = _fft_tables(seq_len)
    const = _const_spec
    return pl.pallas_call(
        functools.partial(_fftconv_kernel, n1=n1, n2=n2),
        out_shape=jax.ShapeDtypeStruct((bsz, seq_len, w), F32),
        grid=(w // LANES, bsz // 2),
        in_specs=[pl.BlockSpec((2, seq_len, LANES), lambda j, p: (p, 0, u_col + j)),
                  pl.BlockSpec((2, seq_len, LANES), lambda j, p: (p, 0, mult_col + j)),
                  pl.BlockSpec((1, n1, 2 * n2, LANES), lambda j, p: (conv_idx, 0, 0, j),
                               pipeline_mode=pl.Buffered(1)),
                  pl.BlockSpec((1, 1, LANES), lambda j, p: (conv_idx, 0, j)),
                  const(t1_data.shape), const(f2_fwd.shape), const(f2_inv.shape), const(t4.shape)],
        out_specs=pl.BlockSpec((2, seq_len, LANES), lambda j, p: (p, 0, j)),
        scratch_shapes=[pltpu.VMEM((2 * n1 * (n2 + FFT_PLANE_PAD), LANES), F32)],
        compiler_params=_cparams("arbitrary", "arbitrary"),
        name="hyena_fftconv",
    )(u, mult, spec, bias, t1_data, f2_fwd, f2_inv, t4)


@functools.lru_cache(maxsize=None)
def _small_fft_tables(seq_len):
    n = 2 * seq_len
    f = np.exp(-2j * np.pi * np.outer(np.arange(n), np.arange(n)) / n)
    fwd = _blockreal(f[:, :seq_len])
    real = np.concatenate([f.real, f.imag], axis=0)
    inv = _blockreal(np.conj(f)[:seq_len, :] / n)
    as32 = lambda a: jnp.asarray(a, dtype=F32)
    return as32(fwd), as32(real), as32(inv)


def _ctx_hyena_kernel(v_ref, x1_ref, x2_ref, k_ref, bias_ref, fwd_ref, real_ref, inv_ref, o_ref, *, seq_len):
    n = 2 * seq_len

    def conv(ua, ub, g):
        spec = _dot(real_ref[...], k_ref[g])
        x = _dot(fwd_ref[...], jnp.concatenate([ua, ub], axis=0))
        xr, xi, kr, ki = x[:n], x[n:], spec[:n], spec[n:]
        y = _dot(inv_ref[...], jnp.concatenate([xr * kr - xi * ki, xr * ki + xi * kr], axis=0))
        b = bias_ref[g]
        return y[:seq_len] + b * ua, y[seq_len:] + b * ub

    c1a, c1b = conv(v_ref[0], v_ref[1], 0)
    y1a, y1b = x1_ref[0] * c1a, x1_ref[1] * c1b
    c2a, c2b = conv(y1a, y1b, 1)
    o_ref[0] = x2_ref[0] * c2a
    o_ref[1] = x2_ref[1] * c2b


def _ctx_hyena(hyc, row_block, kfilt, bias, seq_len, w):
    bsz = hyc.shape[0]
    n = 2 * seq_len
    fwd, real, inv = _small_fft_tables(seq_len)
    nt = w // LANES
    col = lambda c0: pl.BlockSpec((2, seq_len, LANES), lambda j, p: (p, row_block, c0 + j))
    return pl.pallas_call(
        functools.partial(_ctx_hyena_kernel, seq_len=seq_len),
        out_shape=jax.ShapeDtypeStruct((bsz, seq_len, w), F32),
        grid=(nt, bsz // 2),
        in_specs=[col(0), col(nt), col(2 * nt),
                  pl.BlockSpec((2, n, LANES), lambda j, p: (0, 0, j)),
                  pl.BlockSpec((2, 1, LANES), lambda j, p: (0, 0, j)),
                  _const_spec((2 * n, 2 * seq_len)), _const_spec((2 * n, n)), _const_spec((2 * seq_len, 2 * n))],
        out_specs=pl.BlockSpec((2, seq_len, LANES), lambda j, p: (p, 0, j)),
        compiler_params=_cparams("arbitrary", "arbitrary"),
        name="hyena_ctx",
    )(hyc, hyc, hyc, kfilt, bias, fwd, real, inv)


def _rms(x, g):
    return x * lax.rsqrt(jnp.mean(x * x, axis=-1, keepdims=True) + NORM_EPS) * g


def _inproj_kernel(x_ref, mod_ref, g_ref, w_ref, *out_refs, widths):
    shift, scale = mod_ref[0, 0, 0:1, :], mod_ref[0, 0, 1:2, :]
    h = _rms(x_ref[0], g_ref[...]) * (1.0 + scale) + shift
    z = jnp.dot(h.astype(BF16), w_ref[...], preferred_element_type=F32)
    off = 0
    for o_ref, wd in zip(out_refs[:-1], widths[:-1]):
        o_ref[0] = z[:, off:off + wd]
        off += wd
    out_refs[-1][0] = _silu(z[:, off:])


def _inproj(xs, mods, norm_g, w_perm, widths, n_lat_tiles):
    bsz, s, d = xs.shape
    nt = s // ROW_TILE
    return pl.pallas_call(
        functools.partial(_inproj_kernel, widths=widths),
        out_shape=[jax.ShapeDtypeStruct((bsz, s, wd), F32) for wd in widths],
        grid=(bsz, nt),
        in_specs=[pl.BlockSpec((1, ROW_TILE, d), lambda b, t: (b, t, 0)),
                  pl.BlockSpec((1, 1, 3, d), lambda b, t: (b, t // n_lat_tiles, 0, 0)),
                  pl.BlockSpec((1, d), lambda b, t: (0, 0)),
                  _const_spec(w_perm.shape)],
        out_specs=[pl.BlockSpec((1, ROW_TILE, wd), lambda b, t: (b, t, 0)) for wd in widths],
        compiler_params=_cparams("arbitrary", "arbitrary"),
        name="in_projection",
    )(xs, mods, norm_g.reshape(1, d), w_perm)


def _halo_specs(width, n_tiles):
    per = ROW_TILE // 8
    prev = pl.BlockSpec((1, 8, width), lambda b, t: (b, jnp.maximum(t * per - 1, 0), 0))
    nxt = pl.BlockSpec((1, 8, width), lambda b, t: (b, jnp.minimum((t + 1) * per, n_tiles * per - 1), 0))
    return prev, nxt


def _short_conv_tile(z, prev8, next8, w, t, n_lat_tiles, n_tiles):
    first = jnp.logical_or(t == 0, t == n_lat_tiles)
    last = jnp.logical_or(t == n_lat_tiles - 1, t == n_tiles - 1)
    above = jnp.where(first, 0.0, prev8[7:8, :])
    below = jnp.where(last, 0.0, next8[0:1, :])
    row = lax.broadcasted_iota(jnp.int32, z.shape, 0)
    zm1 = jnp.where(row == 0, above, pltpu.roll(z, 1, 0))
    zp1 = jnp.where(row == z.shape[0] - 1, below, pltpu.roll(z, z.shape[0] - 1, 0))
    return zm1 * w[0:1, :] + z * w[1:2, :] + zp1 * w[2:3, :]


def _short_conv_kernel(z_ref, p_ref, n_ref, w_ref, o_ref, *, n_lat_tiles, n_tiles):
    o_ref[0] = _short_conv_tile(z_ref[0], p_ref[0], n_ref[0], w_ref[...], pl.program_id(1),
                                n_lat_tiles, n_tiles)


def _short_conv(z, w, n_lat_tiles):
    bsz, s, width = z.shape
    nt = s // ROW_TILE
    prev, nxt = _halo_specs(width, nt)
    tile = pl.BlockSpec((1, ROW_TILE, width), lambda b, t: (b, t, 0))
    return pl.pallas_call(
        functools.partial(_short_conv_kernel, n_lat_tiles=n_lat_tiles, n_tiles=nt),
        out_shape=jax.ShapeDtypeStruct(z.shape, F32),
        grid=(bsz, nt),
        in_specs=[tile, prev, nxt, pl.BlockSpec((3, width), lambda b, t: (0, 0))],
        out_specs=tile,
        compiler_params=_cparams("arbitrary", "arbitrary"),
        name="short_conv",
    )(z, z, z, w)


def _outproj_kernel(a_lat_ref, a_ctx_ref, b_ref, c_ref, d_ref, gt_ref, x_ref, mod_ref, w_ref, fg_ref, o_ref,
                    *, n_lat_tiles, final):
    is_ctx = pl.program_id(1) >= n_lat_tiles
    a = jnp.where(is_ctx, a_ctx_ref[0], a_lat_ref[0])
    mix = jnp.concatenate([a, b_ref[0], c_ref[0], d_ref[0]], axis=-1) * gt_ref[0]
    y = jnp.dot(mix.astype(BF16), w_ref[...], preferred_element_type=F32)
    x = x_ref[0] + mod_ref[0, 0, 2:3, :] * y
    o_ref[0] = _rms(x, fg_ref[...]) if final else x


def _outproj(a_lat, a_ctx, bmix, cmix, dmix, gates, xs, mods, w_out, final_g, n_lat_tiles, final):
    bsz, s, d = xs.shape
    wb = a_lat.shape[-1]
    nt = n_lat_tiles if final else s // ROW_TILE
    tile = lambda wd: pl.BlockSpec((1, ROW_TILE, wd), lambda b, t: (b, t, 0))
    return pl.pallas_call(
        functools.partial(_outproj_kernel, n_lat_tiles=n_lat_tiles, final=final),
        out_shape=jax.ShapeDtypeStruct((bsz, nt * ROW_TILE, d), F32),
        grid=(bsz, nt),
        in_specs=[pl.BlockSpec((1, ROW_TILE, wb), lambda b, t: (b, jnp.minimum(t, n_lat_tiles - 1), 0)),
                  pl.BlockSpec((1, ROW_TILE, wb), lambda b, t: (b, 0, 0)),
                  tile(wb), tile(wb), tile(wb), tile(4 * wb), tile(d),
                  pl.BlockSpec((1, 1, 3, d), lambda b, t: (b, t // n_lat_tiles, 0, 0)),
                  _const_spec(w_out.shape),
                  pl.BlockSpec((1, d), lambda b, t: (0, 0))],
        out_specs=tile(d),
        compiler_params=_cparams("arbitrary", "arbitrary"),
        name="out_projection",
    )(a_lat, a_ctx, bmix, cmix, dmix, gates, xs, mods, w_out, final_g.reshape(1, d))


def _rope_tables(seq_len, ctx_len):
    rows = seq_len // GRID_W
    row = jnp.repeat(jnp.arange(rows, dtype=F32), GRID_W)
    col = jnp.tile(jnp.arange(GRID_W, dtype=F32), rows)
    n_freq = HEAD_DIM // 4
    inv_freq = ROPE_THETA ** (-jnp.arange(n_freq, dtype=F32) / n_freq)
    ar, ac = row[:, None] * inv_freq, col[:, None] * inv_freq
    zero = jnp.zeros_like(ar)
    cos = jnp.concatenate([jnp.cos(ar), jnp.cos(ar), jnp.cos(ac), jnp.cos(ac)], axis=-1)
    sin_hi = jnp.concatenate([-jnp.sin(ar), zero, -jnp.sin(ac), zero], axis=-1)
    sin_lo = jnp.concatenate([zero, jnp.sin(ar), zero, jnp.sin(ac)], axis=-1)
    pad = lambda t, v: jnp.concatenate([t, jnp.full((ctx_len, HEAD_DIM), v, F32)], axis=0)
    return pad(cos, 1.0), pad(sin_hi, 0.0), pad(sin_lo, 0.0)


def _rope(x, cos, sin_hi, sin_lo):
    q = HEAD_DIM // 4
    w = x.shape[-1]
    return x * cos + pltpu.roll(x, w - q, 1) * sin_hi + pltpu.roll(x, q, 1) * sin_lo


def _head_mean_sq(x, bd):
    return _dot(x * x, bd)


def _qkprep_kernel(wa_ref, fa_ref, cos_ref, shi_ref, slo_ref, qg_ref, kg_ref, bd_ref,
                   qw_ref, kw_ref, vw_ref, qf_ref, kf_ref, vf_ref, *, wq, wk):
    nq, nk = wq // HEAD_DIM, wk // HEAD_DIM
    cos, shi, slo = cos_ref[...], shi_ref[...], slo_ref[...]
    tab = lambda t, n: jnp.concatenate([t] * n, axis=-1)
    cq, hq, lq = tab(cos, nq), tab(shi, nq), tab(slo, nq)
    ck, hk, lk = tab(cos, nk), tab(shi, nk), tab(slo, nk)
    scale = HEAD_DIM ** -0.5

    def emit(ref, val, n):
        for h in range(n):
            ref[0, h] = val[:, h * HEAD_DIM:(h + 1) * HEAD_DIM].astype(ref.dtype)

    wa = wa_ref[0]
    emit(qw_ref, _rope(wa[:, :wq], cq, hq, lq) * scale, nq)
    emit(kw_ref, _rope(wa[:, wq:wq + wk], ck, hk, lk), nk)
    emit(vw_ref, wa[:, wq + wk:], nk)
    fa = fa_ref[0]
    q, k = fa[:, :wq], fa[:, wq:wq + wk]
    bd = bd_ref[...]
    q = q * lax.rsqrt(_head_mean_sq(q, bd) + NORM_EPS) * qg_ref[...]
    k = k * lax.rsqrt(_head_mean_sq(k, bd[:wk, :wk]) + NORM_EPS) * kg_ref[...]
    emit(qf_ref, _rope(q, cq, hq, lq) * scale, nq)
    emit(kf_ref, _rope(k, ck, hk, lk), nk)
    v = fa[:, wq + wk:]
    lane = lax.broadcasted_iota(jnp.int32, (v.shape[0], LANES - HEAD_DIM), 1)
    ones_pad = jnp.where(lane == 0, 1.0, 0.0)
    for h in range(nk):
        vf_ref[0, h] = jnp.concatenate([v[:, h * HEAD_DIM:(h + 1) * HEAD_DIM], ones_pad],
                                       axis=-1).astype(vf_ref.dtype)


def _head_block_diag(width, value):
    h = np.arange(width) // HEAD_DIM
    return jnp.asarray((h[:, None] == h[None, :]) * value, dtype=F32)


def _qkprep(wa, fa, rope, q_gain, k_gain, wq, wk):
    bsz, s, _ = wa.shape
    nq, nk = wq // HEAD_DIM, wk // HEAD_DIM
    nt = s // ROW_TILE
    tile = lambda wd: pl.BlockSpec((1, ROW_TILE, wd), lambda b, t: (b, t, 0))
    tab = pl.BlockSpec((ROW_TILE, HEAD_DIM), lambda b, t: (t, 0))
    hm = lambda n, wd=HEAD_DIM: pl.BlockSpec((1, n, ROW_TILE, wd), lambda b, t: (b, 0, t, 0))
    shp = lambda n, wd=HEAD_DIM: jax.ShapeDtypeStruct((bsz, n, s, wd), BF16)
    return pl.pallas_call(
        functools.partial(_qkprep_kernel, wq=wq, wk=wk),
        out_shape=[shp(nq), shp(nk), shp(nk), shp(nq), shp(nk), shp(nk, LANES)],
        grid=(bsz, nt),
        in_specs=[tile(wq + 2 * wk), tile(wq + 2 * wk), tab, tab, tab,
                  pl.BlockSpec((1, wq), lambda b, t: (0, 0)), pl.BlockSpec((1, wk), lambda b, t: (0, 0)),
                  _const_spec((wq, wq))],
        out_specs=[hm(nq), hm(nk), hm(nk), hm(nq), hm(nk), hm(nk, LANES)],
        compiler_params=_cparams("arbitrary", "arbitrary"),
        name="qk_prep",
    )(wa, fa, *rope, jnp.tile(q_gain, nq).reshape(1, wq), jnp.tile(k_gain, nk).reshape(1, wk),
      _head_block_diag(wq, 1.0 / HEAD_DIM))


def _window_attn_kernel(q_ref, k_ref, v_ref, sink_ref, o_ref, *, seq_len, ctx_len, tq):
    t = pl.program_id(2)
    n_lat = seq_len // tq
    g = q_ref.shape[1]
    band = 2 * WINDOW + tq
    q = q_ref[0].reshape(g * tq, HEAD_DIM)
    kc, vc = k_ref[0, 0, seq_len:seq_len + ctx_len, :], v_ref[0, 0, seq_len:seq_len + ctx_len, :]
    start = pl.multiple_of(jnp.clip((t - 1) * tq, 0, seq_len - band), tq)
    kb, vb = k_ref[0, 0, pl.ds(start, band), :], v_ref[0, 0, pl.ds(start, band), :]
    nt_dot = lambda a, b: lax.dot_general(a, b, (((1,), (1,)), ((), ())), preferred_element_type=F32)
    s_ctx = nt_dot(q, kc)
    s_loc = nt_dot(q, kb)
    qpos = t * tq + lax.broadcasted_iota(jnp.int32, (g, tq, band), 1).reshape(g * tq, band)
    kpos = start + lax.broadcasted_iota(jnp.int32, (g * tq, band), 1)
    valid = jnp.logical_and(jnp.abs(qpos - kpos) <= WINDOW, t < n_lat)
    s_loc = jnp.where(valid, s_loc, NEG_INF)
    sink = sink_ref[0]
    m = jnp.maximum(jnp.maximum(jnp.max(s_ctx, axis=-1, keepdims=True),
                                jnp.max(s_loc, axis=-1, keepdims=True)), sink)
    p_ctx, p_loc = jnp.exp(s_ctx - m), jnp.exp(s_loc - m)
    denom = (jnp.sum(p_ctx, axis=-1, keepdims=True) + jnp.sum(p_loc, axis=-1, keepdims=True)
             + jnp.exp(sink - m))
    out = (jnp.dot(p_ctx.astype(BF16), vc, preferred_element_type=F32)
           + jnp.dot(p_loc.astype(BF16), vb, preferred_element_type=F32)) / denom
    o_ref[0] = jnp.concatenate([out[h * tq:(h + 1) * tq] for h in range(g)], axis=-1)


def _window_attention(q, k, v, sink, seq_len, ctx_len):
    bsz, nq, s, _ = q.shape
    nkv = k.shape[1]
    g = nq // nkv
    tq = WINDOW
    sink_rows = jnp.repeat(sink.astype(F32).reshape(nkv, g), tq, axis=1).reshape(nkv, g * tq, 1)
    kv = pl.BlockSpec((1, 1, s, HEAD_DIM), lambda b, h, t: (b, h, 0, 0))
    return pl.pallas_call(
        functools.partial(_window_attn_kernel, seq_len=seq_len, ctx_len=ctx_len, tq=tq),
        out_shape=jax.ShapeDtypeStruct((bsz, s, nq * HEAD_DIM), F32),
        grid=(bsz, nkv, s // tq),
        in_specs=[pl.BlockSpec((1, g, tq, HEAD_DIM), lambda b, h, t: (b, h, t, 0)), kv, kv,
                  pl.BlockSpec((1, g * tq, 1), lambda b, h, t: (h, 0, 0))],
        out_specs=pl.BlockSpec((1, tq, g * HEAD_DIM), lambda b, h, t: (b, t, h)),
        compiler_params=_cparams("arbitrary", "arbitrary", "arbitrary"),
        name="window_attention",
    )(q, k, v, sink_rows)


DENSE_KEY_BLOCK = 1024


def _dense_attn_kernel(q_ref, k_ref, v_ref, o_ref, *, seq_len, tq, tk):
    t = pl.program_id(2)
    n_lat = seq_len // tq
    g = q_ref.shape[1]
    ctx_len = k_ref.shape[2] - seq_len
    q = q_ref[0].reshape(g * tq, HEAD_DIM)

    def step(carry, start, size):
        m, acc = carry
        s = lax.dot_general(q, k_ref[0, 0, start:start + size, :], (((1,), (1,)), ((), ())),
                            preferred_element_type=F32)
        m_new = jnp.maximum(m, jnp.max(s, axis=-1, keepdims=True))
        p = jnp.exp(s - m_new).astype(BF16)
        pv = jnp.dot(p, v_ref[0, 0, start:start + size, :], preferred_element_type=F32)
        return m_new, jnp.exp(m - m_new) * acc + pv

    def finish(carry):
        _, acc = carry
        out = acc[:, :HEAD_DIM] / acc[:, HEAD_DIM:HEAD_DIM + 1]
        o_ref[0] = jnp.concatenate([out[h * tq:(h + 1) * tq] for h in range(g)], axis=-1)

    init = (jnp.full((g * tq, 1), NEG_INF, F32), jnp.zeros((g * tq, v_ref.shape[-1]), F32))

    @pl.when(t < n_lat)
    def _():
        carry = step(init, seq_len, ctx_len)
        for j in range(seq_len // tk):
            carry = step(carry, j * tk, tk)
        finish(carry)

    @pl.when(t >= n_lat)
    def _():
        finish(step(init, seq_len, ctx_len))


def _dense_attention(q, k, v, seq_len):
    bsz, nq, s, _ = q.shape
    nkv = k.shape[1]
    g = nq // nkv
    tq = ROW_TILE
    tk = math.gcd(seq_len, DENSE_KEY_BLOCK)
    kv = lambda a: pl.BlockSpec((1, 1, s, a.shape[-1]), lambda b, h, t: (b, h, 0, 0))
    return pl.pallas_call(
        functools.partial(_dense_attn_kernel, seq_len=seq_len, tq=tq, tk=tk),
        out_shape=jax.ShapeDtypeStruct((bsz, s, nq * HEAD_DIM), F32),
        grid=(bsz, nkv, s // tq),
        in_specs=[pl.BlockSpec((1, g, tq, HEAD_DIM), lambda b, h, t: (b, h, t, 0)), kv(k), kv(v)],
        out_specs=pl.BlockSpec((1, tq, g * HEAD_DIM), lambda b, h, t: (b, t, h)),
        compiler_params=_cparams("arbitrary", "arbitrary", "arbitrary"),
        name="dense_attention",
    )(q, k, v)


def _softplus(x):
    return jnp.maximum(x, 0.0) + jnp.log(1.0 + jnp.exp(-jnp.abs(x)))


def _split(a):
    hi = a.astype(BF16).astype(F32)
    return hi, a - hi


def _lhs3(a):
    hi, lo = _split(a)
    return jnp.concatenate([hi, lo, hi], axis=-1).astype(BF16)


def _rhs3(b):
    hi, lo = _split(b)
    return jnp.concatenate([hi, hi, lo], axis=-2).astype(BF16)


def _bmm(l3, r3):
    return lax.dot_general(l3, r3, (((2,), (1,)), ((0,), (0,))), preferred_element_type=F32)


RW_SLOTS_PER_STEP = 8


def _rwkv_par_kernel(z_ref, prev_ref, next_ref, cw_ref, w0_ref, wup_ref, a0_ref, aup_ref, kk_ref, ka_ref,
                     bd_ref, trif_ref, trib_ref, ones_ref,
                     rp3_ref, yvq_ref, rvk_ref,
                     lbig_ref, rbig_ref, at_ref, rt_ref, vv3_ref, bct_ref, kct_ref, wt_ref,
                     *, n_lat_tiles, n_tiles, w, lora):
    tc = RW_CHUNK
    nc, nh = ROW_TILE // tc, w // HEAD_DIM
    nb = RW_SLOTS_PER_STEP
    z = _short_conv_tile(z_ref[0], prev_ref[0], next_ref[0], cw_ref[...], pl.program_id(1),
                         n_lat_tiles, n_tiles)
    r, k, v = z[:, :w], z[:, w:2 * w], z[:, 2 * w:3 * w]
    w_low = jnp.tanh(z[:, 3 * w:3 * w + lora])
    a_low = z[:, 3 * w + lora:]
    kk = k * kk_ref[...]
    kk = kk / jnp.maximum(jnp.sqrt(_dot(kk * kk, bd_ref[...])), 1e-12)
    ksum = jnp.zeros_like(k)
    for d in range(2):
        w_log = -_softplus(-(w0_ref[d:d + 1, :] + _dot(w_low, wup_ref[d]))) - 0.5
        lw = -jnp.exp(w_log)
        a = 1.0 / (1.0 + jnp.exp(-(a0_ref[d:d + 1, :] + _dot(a_low, aup_ref[d]))))
        kd = k * (1.0 + (a - 1.0) * ka_ref[...])
        ksum = ksum + kd
        c = _dot(trif_ref[...] if d == 0 else trib_ref[...], lw)
        ctot = _dot(ones_ref[...], lw)
        e_neg, e_rem = jnp.exp(-c), jnp.exp(ctot - c)
        b = kk * a
        at, rt = -kk * jnp.exp(c - lw), r * jnp.exp(c)
        w_tot = jnp.exp(ctot)
        at_s, rt_s, bh_s, kh_s, v_s = _split(at), _split(rt), _split(b * e_neg), _split(kd * e_neg), _split(v)
        bct_s, kct_s = _split((b * e_rem).T), _split((kd * e_rem).T)
        for ci in range(nc):
            for h in range(nh):
                slot = (d * nc + ci) * nh + h
                rows, cols = slice(ci * tc, (ci + 1) * tc), slice(h * HEAD_DIM, (h + 1) * HEAD_DIM)
                lanes3 = lambda s, order: jnp.concatenate([s[i][rows, cols] for i in order],
                                                          axis=-1).astype(BF16)
                lbig_ref[slot, :tc, :] = lanes3(at_s, (0, 1, 0))
                lbig_ref[slot, tc:, :] = lanes3(rt_s, (0, 1, 0))
                rbig_ref[slot, :tc, :] = lanes3(bh_s, (0, 0, 1))
                rbig_ref[slot, tc:, :] = lanes3(kh_s, (0, 0, 1))
                at_ref[slot] = at[rows, cols]
                rt_ref[slot] = rt[rows, cols]
                vv3_ref[slot] = jnp.concatenate([v_s[i][rows, cols] for i in (0, 0, 1)], axis=0).astype(BF16)
                bct_ref[slot] = jnp.concatenate([bct_s[i][cols, rows] for i in (0, 1, 0)], axis=-1).astype(BF16)
                kct_ref[slot] = jnp.concatenate([kct_s[i][cols, rows] for i in (0, 1, 0)], axis=-1).astype(BF16)
                wt_ref[slot] = w_tot[ci * tc:ci * tc + 8, cols]
    rvk_ref[0] = jnp.concatenate([r, v, ksum], axis=-1)

    row = lax.broadcasted_iota(jnp.int32, (tc, tc), 0)
    col = lax.broadcasted_iota(jnp.int32, (tc, tc), 1)
    eye = row == col

    def level_mask(s):
        return jnp.logical_and(row // (2 * s) == col // (2 * s), row // s != col // s)

    steps_per_dir = nc * nh // nb

    def chunk_group(g, carry):
        slots = pl.ds(pl.multiple_of(g * nb, nb), nb)
        d = g // steps_per_dir
        fwd = d == 0
        before = jnp.logical_or(jnp.logical_and(fwd, row > col),
                                jnp.logical_and(jnp.logical_not(fwd), row < col))
        upto = jnp.logical_or(before, eye)
        big = lax.dot_general(lbig_ref[slots], rbig_ref[slots], (((2,), (2,)), ((0,), (0,))),
                              preferred_element_type=F32)
        a_ab = jnp.where(before, big[:, :tc, :tc], 0.0)
        a_ak = jnp.where(before, big[:, :tc, tc:], 0.0)
        a_rb = jnp.where(upto, big[:, tc:, :tc], 0.0)
        a_rk = jnp.where(upto, big[:, tc:, tc:], 0.0)
        x = jnp.where(eye, 1.0, jnp.where(level_mask(1), a_ab, 0.0))
        s = 2
        while s < tc:
            half = _bmm(_lhs3(x), _rhs3(jnp.where(level_mask(s), a_ab, 0.0)))
            x = x + _bmm(_lhs3(half), _rhs3(x))
            s *= 2
        vv3 = vv3_ref[slots]
        akrk = _bmm(jnp.concatenate([_lhs3(a_ak), _lhs3(a_rk)], axis=1), vv3)
        xa3 = _rhs3(_bmm(_lhs3(x), _rhs3(jnp.concatenate([at_ref[slots], akrk[:, :tc]], axis=-1))))
        ra = _bmm(_lhs3(a_rb), xa3)
        pq = _bmm(bct_ref[slots], xa3)
        rp = rt_ref[slots] + ra[:, :, :HEAD_DIM]
        yv = ra[:, :, HEAD_DIM:] + akrk[:, tc:]
        p = jnp.where(eye, wt_ref[slots][:, 0:1, :], 0.0) + pq[:, :, :HEAD_DIM]
        q = pq[:, :, HEAD_DIM:] + _bmm(kct_ref[slots], vv3)
        rp3 = jnp.concatenate([_lhs3(rp), _lhs3(p)], axis=1)
        yvq = jnp.concatenate([yv, q], axis=1)
        for j in range(nb):
            ci = (g % steps_per_dir) * (nb // nh) + j // nh
            rp3_ref[0, d, j % nh, ci] = rp3[j]
            yvq_ref[0, d, j % nh, ci] = yvq[j]
        return carry
    lax.fori_loop(0, 2 * steps_per_dir, chunk_group, 0)


def _chunk_matrices():
    t = np.arange(ROW_TILE)
    same = (t[:, None] // RW_CHUNK) == (t[None, :] // RW_CHUNK)
    as32 = lambda m: jnp.asarray(m, dtype=F32)
    return (as32(same & (t[None, :] <= t[:, None])), as32(same & (t[None, :] >= t[:, None])), as32(same))


def _rwkv_par(rw, conv_w, w0, w_up, a0, a_up, k_k, k_a, w, n_lat_tiles):
    bsz, s, width = rw.shape
    nt = s // ROW_TILE
    nh, nc = w // HEAD_DIM, ROW_TILE // RW_CHUNK
    lora = w_up.shape[1]
    prev, nxt = _halo_specs(width, nt)
    full = lambda a: pl.BlockSpec(a.shape, lambda b, t: (0,) * a.ndim)
    trif, trib, ones = _chunk_matrices()
    bd = _head_block_diag(w, 1.0)
    vec = lambda a: a.reshape(1, w)
    n_chunks = s // RW_CHUNK
    nslots = 2 * nc * nh
    tc3 = 3 * RW_CHUNK
    consts = (conv_w, w0, w_up, a0, a_up, vec(k_k), vec(k_a), bd, trif, trib, ones)
    return pl.pallas_call(
        functools.partial(_rwkv_par_kernel, n_lat_tiles=n_lat_tiles, n_tiles=nt, w=w, lora=lora),
        out_shape=[jax.ShapeDtypeStruct((bsz, 2, nh, n_chunks, 2 * RW_CHUNK, 3 * HEAD_DIM), BF16),
                   jax.ShapeDtypeStruct((bsz, 2, nh, n_chunks, 2 * RW_CHUNK, HEAD_DIM), F32),
                   jax.ShapeDtypeStruct((bsz, s, 3 * w), F32)],
        grid=(bsz, nt),
        in_specs=[pl.BlockSpec((1, ROW_TILE, width), lambda b, t: (b, t, 0)), prev, nxt]
                 + [full(a) for a in consts],
        out_specs=[pl.BlockSpec((1, 2, nh, nc, 2 * RW_CHUNK, 3 * HEAD_DIM), lambda b, t: (b, 0, 0, t, 0, 0)),
                   pl.BlockSpec((1, 2, nh, nc, 2 * RW_CHUNK, HEAD_DIM), lambda b, t: (b, 0, 0, t, 0, 0)),
                   pl.BlockSpec((1, ROW_TILE, 3 * w), lambda b, t: (b, t, 0))],
        scratch_shapes=[pltpu.VMEM((nslots, 2 * RW_CHUNK, tc3), BF16),
                        pltpu.VMEM((nslots, 2 * RW_CHUNK, tc3), BF16),
                        pltpu.VMEM((nslots, RW_CHUNK, HEAD_DIM), F32),
                        pltpu.VMEM((nslots, RW_CHUNK, HEAD_DIM), F32),
                        pltpu.VMEM((nslots, tc3, HEAD_DIM), BF16),
                        pltpu.VMEM((nslots, HEAD_DIM, tc3), BF16),
                        pltpu.VMEM((nslots, HEAD_DIM, tc3), BF16),
                        pltpu.VMEM((nslots, 8, HEAD_DIM), F32)],
        compiler_params=_cparams("arbitrary", "arbitrary"),
        name="rwkv_chunk_prep",
    )(rw, rw, rw, *consts)


def _rwkv_seq_kernel(rp3f_ref, yvqf_ref, rp3b_ref, yvqb_ref, yf_ref, yb_ref, g_ref):
    @pl.when(pl.program_id(1) == 0)
    def _():
        g_ref[...] = jnp.zeros_like(g_ref)

    nh = g_ref.shape[1]
    tc = RW_CHUNK
    for d, (rp3, yvq, y) in enumerate(((rp3f_ref, yvqf_ref, yf_ref), (rp3b_ref, yvqb_ref, yb_ref))):
        for h in range(nh):
            out = jnp.dot(rp3[0, 0, h, 0], _rhs3(g_ref[d, h]), preferred_element_type=F32) + yvq[0, 0, h, 0]
            y[0, h] = out[:tc]
            g_ref[d, h] = out[tc:]


def _rwkv_seq(rp3, yvq, n_lat_chunks):
    bsz, _, nh, n_chunks = rp3.shape[:4]
    n_ctx = n_chunks - n_lat_chunks
    order = (lambda i: jnp.where(i < n_ctx, n_lat_chunks + i, i - n_ctx),
             lambda i: n_chunks - 1 - i)
    blk = lambda d, a: pl.BlockSpec((1, 1, nh, 1) + a.shape[4:], lambda b, i: (b, d, 0, order[d](i), 0, 0))
    out = lambda d: pl.BlockSpec((1, nh, RW_CHUNK, HEAD_DIM), lambda b, i: (b, 0, order[d](i), 0))
    shp = jax.ShapeDtypeStruct((bsz, nh, n_chunks * RW_CHUNK, HEAD_DIM), F32)
    return pl.pallas_call(
        _rwkv_seq_kernel,
        out_shape=[shp, shp],
        grid=(bsz, n_chunks),
        in_specs=[blk(0, rp3), blk(0, yvq), blk(1, rp3), blk(1, yvq)],
        out_specs=[out(0), out(1)],
        scratch_shapes=[pltpu.VMEM((2, nh, HEAD_DIM, HEAD_DIM), F32)],
        compiler_params=_cparams("arbitrary", "arbitrary"),
        name="rwkv_state_scan",
    )(rp3, yvq, rp3, yvq)


def _rwkv_out_kernel(yf_ref, yb_ref, rvk_ref, rk_ref, g_ref, b_ref, o_ref, *, w):
    nh = w // HEAD_DIM
    rvk = rvk_ref[0]
    r, v, ksum = rvk[:, :w], rvk[:, w:2 * w], rvk[:, 2 * w:]
    rkk = r * ksum * rk_ref[...]
    normed, bonus = [], []
    for h in range(nh):
        cols = slice(h * HEAD_DIM, (h + 1) * HEAD_DIM)
        y = yf_ref[0, h] + yb_ref[0, h]
        mu = jnp.mean(y, axis=-1, keepdims=True)
        var = jnp.mean(jnp.square(y - mu), axis=-1, keepdims=True)
        normed.append((y - mu) * lax.rsqrt(var + RW_GN_EPS))
        bonus.append(jnp.sum(rkk[:, cols], axis=-1, keepdims=True) * v[:, cols])
    o_ref[0] = (jnp.concatenate(normed, axis=-1) * g_ref[...] + b_ref[...]
                + jnp.concatenate(bonus, axis=-1))


def _rwkv_out(yf, yb, rvk, r_k, ln_g, ln_b, w):
    bsz, nh, s, _ = yf.shape
    hm = pl.BlockSpec((1, nh, ROW_TILE, HEAD_DIM), lambda b, t: (b, 0, t, 0))
    vec = pl.BlockSpec((1, w), lambda b, t: (0, 0))
    return pl.pallas_call(
        functools.partial(_rwkv_out_kernel, w=w),
        out_shape=jax.ShapeDtypeStruct((bsz, s, w), F32),
        grid=(bsz, s // ROW_TILE),
        in_specs=[hm, hm, pl.BlockSpec((1, ROW_TILE, 3 * w), lambda b, t: (b, t, 0)), vec, vec, vec],
        out_specs=pl.BlockSpec((1, ROW_TILE, w), lambda b, t: (b, t, 0)),
        compiler_params=_cparams("arbitrary", "arbitrary"),
        name="rwkv_readout",
    )(yf, yb, rvk, r_k.reshape(1, w), ln_g.reshape(1, w), ln_b.reshape(1, w))


def _hyena_two_sided_filters(seq_len, fw1, fb1, freq, fw2, fb2, fw3, width):
    bands = (fw1.shape[0] - 1) // 2
    mm = functools.partial(jnp.matmul, precision=HI)
    t = jnp.linspace(0.0, 1.0, seq_len, dtype=F32)[:, None]
    wpos = (2.0 * math.pi / seq_len) * jnp.arange(seq_len, dtype=F32)[:, None]
    f = jnp.linspace(1e-4, bands - 1, bands, dtype=F32)[None, :]
    z = jnp.concatenate([t, jnp.cos(f * wpos), jnp.sin(f * wpos)], axis=-1)
    h = jnp.sin(freq * (mm(z, fw1) + fb1))
    h = jnp.sin(freq * (mm(h, fw2) + fb2))
    h = mm(h, fw3).reshape(seq_len, 2, 2, width)
    max_decay = math.log(HY_DECAY_TARGET) / HY_FAST_DECAY
    min_decay = math.log(HY_DECAY_TARGET) / HY_SLOW_DECAY
    deltas = jnp.linspace(min_decay, max_decay, width, dtype=F32)
    h = h * jnp.exp(-t * jnp.abs(deltas))[:, None, None, :]
    h = h / jnp.sum(jnp.abs(h), axis=(0, 2), keepdims=True)
    fwd, bwd = h[:, :, 0], h[:, :, 1]
    k = jnp.concatenate([fwd, jnp.zeros_like(fwd[:1]), bwd[:0:-1]], axis=0)
    return jnp.moveaxis(k, 1, 0)


def kernel(x, c, ctx, c_ctx, mod_w, mod_b, norm_g, w_in, w_out, hy_conv, hy_fw1, hy_fb1, hy_freq, hy_fw2,
           hy_fb2, hy_fw3, hy_bias, rw_conv, rw_w0, rw_w_up, rw_a0, rw_a_up, rw_k_k, rw_k_a, rw_r_k,
           rw_ln_g, rw_ln_b, wa_sink, fa_q_norm, fa_k_norm, final_g):
    bsz, seq_len, d = x.shape
    ctx_len = ctx.shape[1]
    depth = w_in.shape[0]
    w_hy = hy_bias.shape[-1]
    w_rw = rw_w0.shape[-1]
    n_wa_heads = wa_sink.shape[-1]
    w_q = n_wa_heads * HEAD_DIM
    w_kv = w_q // 2
    lora = rw_w_up.shape[2] + rw_a_up.shape[2]
    branch_w = (3 * w_hy, 3 * w_rw + lora, w_q + 2 * w_kv, w_q + 2 * w_kv)
    gate_w = (w_hy, w_rw, w_q, w_q)
    assert seq_len % ROW_TILE == 0 and ctx_len % ROW_TILE == 0 and bsz % 2 == 0
    n_lat_tiles = seq_len // ROW_TILE

    starts = np.cumsum([0] + [bw + gw for bw, gw in zip(branch_w, gate_w)])
    cols = np.concatenate([np.arange(s0, s0 + bw) for s0, bw in zip(starts, branch_w)]
                          + [np.arange(s0 + bw, s0 + bw + gw) for s0, bw, gw in zip(starts, branch_w, gate_w)])
    w_in_p = w_in[:, :, cols].astype(BF16)
    w_out_b = w_out.astype(BF16)
    widths = branch_w + (sum(gate_w),)

    pad_rows = (-(bsz + 1)) % 8
    cond = jnp.concatenate([c, c_ctx[None], jnp.zeros((pad_rows, d), F32)], axis=0)
    mod = _modulation(cond, mod_w, mod_b)
    mod_lat = mod[:, :bsz].reshape(depth, bsz, 3, d)
    mod_ctx = jnp.broadcast_to(mod[:, bsz].reshape(depth, 1, 3, d), (depth, bsz, 3, d))
    mods = jnp.stack([mod_lat, mod_ctx], axis=2)

    rope = _rope_tables(seq_len, ctx_len)
    xs = jnp.concatenate([x, ctx], axis=1)
    for l in range(depth):
        last = l == depth - 1
        hy, rw, wa, fa, gates = _inproj(xs, mods[l], norm_g[l], w_in_p[l], widths, n_lat_tiles)

        hyc = _short_conv(hy, hy_conv[l], n_lat_tiles)
        filt = functools.partial(_hyena_two_sided_filters, fw1=hy_fw1[l], fb1=hy_fb1[l], freq=hy_freq[l],
                                 fw2=hy_fw2[l], fb2=hy_fb2[l], fw3=hy_fw3[l], width=w_hy)
        bias = hy_bias[l].reshape(2, 1, w_hy)
        spec = _filter_spectrum(filt(seq_len))
        nt_hy = w_hy // LANES
        y1 = _fftconv_gated(hyc, 0, hyc, nt_hy, spec, 0, bias, seq_len)
        a_lat = _fftconv_gated(y1, 0, hyc, 2 * nt_hy, spec, 1, bias, seq_len)
        a_ctx = a_lat if last else _ctx_hyena(hyc, seq_len // ctx_len, filt(ctx_len), bias, ctx_len, w_hy)

        rp3, yvq, rvk = _rwkv_par(rw, rw_conv[l], rw_w0[l], rw_w_up[l], rw_a0[l], rw_a_up[l],
                                  rw_k_k[l], rw_k_a[l], w_rw, n_lat_tiles)
        yf, yb = _rwkv_seq(rp3, yvq, seq_len // RW_CHUNK)
        b_mix = _rwkv_out(yf, yb, rvk, rw_r_k[l], rw_ln_g[l], rw_ln_b[l], w_rw)

        qw, kw, vw, qf, kf, vf = _qkprep(wa, fa, rope, fa_q_norm[l], fa_k_norm[l], w_q, w_kv)
        c_mix = _window_attention(qw, kw, vw, wa_sink[l], seq_len, ctx_len)
        d_mix = _dense_attention(qf, kf, vf, seq_len)

        xs = _outproj(a_lat, a_ctx, b_mix, c_mix, d_mix, gates, xs, mods[l], w_out_b[l], final_g,
                      n_lat_tiles, last)
    return xs
```

```python
import functools
import math

import jax
import jax.numpy as jnp
import numpy as np
from jax import lax
from jax.experimental import pallas as pl
from jax.experimental.pallas import tpu as pltpu

HEAD_DIM = 64
GRID_W = 64
WINDOW = 128
NORM_EPS = 1e-6
RW_GN_EPS = 64e-5
NEG_INF = -1e30
ROPE_THETA = 10000.0
HY_FAST_DECAY = 0.3
HY_SLOW_DECAY = 1.5
HY_DECAY_TARGET = 1e-2

ROW_TILE = 256
LANES = 128
FFT_N1 = 64
FFT_UNROLL = 8
FFT_PLANE_PAD = 8
RW_CHUNK = 64
VMEM_LIMIT = 60 * 1024 * 1024

F32 = jnp.float32
BF16 = jnp.bfloat16
HI = lax.Precision.HIGHEST


def _dot(a, b):
    return jnp.dot(a, b, preferred_element_type=F32, precision=HI)


def _dot_nt(a, b):
    return lax.dot_general(a, b, (((1,), (1,)), ((), ())), preferred_element_type=F32, precision=HI)


def _dot_tn(a, b):
    return lax.dot_general(a, b, (((0,), (0,)), ((), ())), preferred_element_type=F32, precision=HI)


def _cparams(*sem):
    return pltpu.CompilerParams(dimension_semantics=sem, vmem_limit_bytes=VMEM_LIMIT)


def _const_spec(shape):
    return pl.BlockSpec(shape, lambda *_: (0,) * len(shape), pipeline_mode=pl.Buffered(1))


def _silu(x):
    return x * (1.0 / (1.0 + jnp.exp(-x)))


def _mod_kernel(c_ref, w_ref, b_ref, o_ref):
    o_ref[0] = _dot(_silu(c_ref[...]), w_ref[0]) + b_ref[0]


def _modulation(cond, mod_w, mod_b):
    depth, d, d3 = mod_w.shape
    rows = cond.shape[0]
    return pl.pallas_call(
        _mod_kernel,
        out_shape=jax.ShapeDtypeStruct((depth, rows, d3), F32),
        grid=(depth,),
        in_specs=[pl.BlockSpec((rows, d), lambda l: (0, 0)),
                  pl.BlockSpec((1, d, d3), lambda l: (l, 0, 0)),
                  pl.BlockSpec((1, 1, d3), lambda l: (l, 0, 0))],
        out_specs=pl.BlockSpec((1, rows, d3), lambda l: (l, 0, 0)),
        compiler_params=_cparams("arbitrary"),
        name="modulation",
    )(cond, mod_w, mod_b.reshape(depth, 1, d3))


def _blockreal(m):
    return np.block([[m.real, -m.imag], [m.imag, m.real]])


@functools.lru_cache(maxsize=None)
def _fft_tables(seq_len):
    n = 2 * seq_len
    n1, n2 = FFT_N1, n // FFT_N1
    h1 = n1 // 2
    j2 = np.arange(n2)[:, None, None]
    k1 = np.arange(n1)[None, :, None]
    t1 = np.exp(-2j * np.pi * (j2 * k1 / n + k1 * np.arange(n1)[None, None, :] / n1))
    t1_data = np.stack([_blockreal(t1[j][:, :h1]) for j in range(n2)])
    t1_real = np.concatenate([t1.real, t1.imag], axis=1)
    f2 = np.exp(-2j * np.pi * np.outer(np.arange(n2), np.arange(n2)) / n2)
    f2_fwd = _blockreal(f2)
    f2_inv = _blockreal(np.conj(f2))
    t4 = np.exp(2j * np.pi * (np.arange(h1)[None, :, None] * np.arange(n1)[None, None, :] / n1
                              + j2 * np.arange(n1)[None, None, :] / n)) / n
    t4 = np.stack([_blockreal(t4[j]) for j in range(n2)])
    return tuple(_lhs3_table(t) for t in (t1_data, t1_real, f2_fwd, f2_inv, t4))


def _lhs3_table(m):
    hi = m.astype(np.float32).astype(BF16)
    lo = (m - hi.astype(np.float64)).astype(np.float32).astype(BF16)
    return np.concatenate([hi, lo, hi], axis=-1)


def _mm3(tbl3, x):
    return jnp.dot(tbl3, _rhs3(x), preferred_element_type=F32)


def _spectrum_kernel(k_ref, t1_ref, f2_ref, o_ref, a_ref, *, n1, n2):
    def stage1(j, carry):
        rows = k_ref[0, pl.ds(j, n1, stride=n2), :]
        a_ref[pl.ds(j, 2 * n1, stride=n2 + FFT_PLANE_PAD), :] = _mm3(t1_ref[j], rows)
        return carry
    lax.fori_loop(0, n2, stage1, 0, unroll=FFT_UNROLL)

    def stage2(i, carry):
        pitch = n2 + FFT_PLANE_PAD
        re = a_ref[pl.ds(pl.multiple_of(i * pitch, 8), n2), :]
        im = a_ref[pl.ds(pl.multiple_of((n1 + i) * pitch, 8), n2), :]
        o_ref[0, i] = _mm3(f2_ref[...], jnp.concatenate([re, im], axis=0))
        return carry
    lax.fori_loop(0, n1, stage2, 0, unroll=2)


def _filter_spectrum(kfilt):
    g, n, w = kfilt.shape
    n1, n2 = FFT_N1, n // FFT_N1
    _, t1_real, f2_fwd, _, _ = _fft_tables(n // 2)
    const = _const_spec
    return pl.pallas_call(
        functools.partial(_spectrum_kernel, n1=n1, n2=n2),
        out_shape=jax.ShapeDtypeStruct((g, n1, 2 * n2, w), F32),
        grid=(g, w // LANES),
        in_specs=[pl.BlockSpec((1, n, LANES), lambda gi, j: (gi, 0, j)),
                  const(t1_real.shape), const(f2_fwd.shape)],
        out_specs=pl.BlockSpec((1, n1, 2 * n2, LANES), lambda gi, j: (gi, 0, 0, j)),
        scratch_shapes=[pltpu.VMEM((2 * n1 * (n2 + FFT_PLANE_PAD), LANES), F32)],
        compiler_params=_cparams("arbitrary", "arbitrary"),
        name="hyena_filter_spectrum",
    )(kfilt, t1_real, f2_fwd)


def _fftconv_kernel(u_ref, m_ref, spec_ref, bias_ref, t1_ref, f2f_ref, f2i_ref, t4_ref, o_ref, a_ref,
                    *, n1, n2):
    h1 = n1 // 2

    def stage1(j, carry):
        za = u_ref[0, pl.ds(j, h1, stride=n2), :]
        zb = u_ref[1, pl.ds(j, h1, stride=n2), :]
        a_ref[pl.ds(j, 2 * n1, stride=n2 + FFT_PLANE_PAD), :] = _mm3(t1_ref[j], jnp.concatenate([za, zb], axis=0))
        return carry
    lax.fori_loop(0, n2, stage1, 0, unroll=FFT_UNROLL)

    def stage2(i, carry):
        pitch = n2 + FFT_PLANE_PAD
        re_rows = pl.ds(pl.multiple_of(i * pitch, 8), n2)
        im_rows = pl.ds(pl.multiple_of((n1 + i) * pitch, 8), n2)
        x = _mm3(f2f_ref[...], jnp.concatenate([a_ref[re_rows, :], a_ref[im_rows, :]], axis=0))
        xr, xi = x[:n2], x[n2:]
        kr, ki = spec_ref[0, i, :n2, :], spec_ref[0, i, n2:, :]
        y = jnp.concatenate([xr * kr - xi * ki, xr * ki + xi * kr], axis=0)
        b = _mm3(f2i_ref[...], y)
        a_ref[re_rows, :] = b[:n2]
        a_ref[im_rows, :] = b[n2:]
        return carry
    lax.fori_loop(0, n1, stage2, 0, unroll=2)

    bias = bias_ref[0]

    def stage4(j, carry):
        y = _mm3(t4_ref[j], a_ref[pl.ds(j, 2 * n1, stride=n2 + FFT_PLANE_PAD), :])
        rows = pl.ds(j, h1, stride=n2)
        for p in range(2):
            u = u_ref[p, rows, :]
            o_ref[p, rows, :] = m_ref[p, rows, :] * (y[p * h1:(p + 1) * h1] + bias * u)
        return carry
    lax.fori_loop(0, n2, stage4, 0, unroll=FFT_UNROLL)


def _fftconv_gated(u, u_col, mult, mult_col, spec, conv_idx, bias, seq_len):
    bsz = u.shape[0]
    w = spec.shape[-1]
    n = 2 * seq_len
    n1, n2 = FFT_N1, n // FFT_N1
    t1_data, _, f2_fwd, f2_inv, t4 = _fft_tables(seq_len)
    const = _const_spec
    return pl.pallas_call(
        functools.partial(_fftconv_kernel, n1=n1, n2=n2),
        out_shape=jax.ShapeDtypeStruct((bsz, seq_len, w), F32),
        grid=(w // LANES, bsz // 2),
        in_specs=[pl.BlockSpec((2, seq_len, LANES), lambda j, p: (p, 0, u_col + j)),
                  pl.BlockSpec((2, seq_len, LANES), lambda j, p: (p, 0, mult_col + j)),
                  pl.BlockSpec((1, n1, 2 * n2, LANES), lambda j, p: (conv_idx, 0, 0, j),
                               pipeline_mode=pl.Buffered(1)),
                  pl.BlockSpec((1, 1, LANES), lambda j, p: (conv_idx, 0, j)),
                  const(t1_data.shape), const(f2_fwd.shape), const(f2_inv.shape), const(t4.shape)],
        out_specs=pl.BlockSpec((2, seq_len, LANES), lambda j, p: (p, 0, j)),
        scratch_shapes=[pltpu.VMEM((2 * n1 * (n2 + FFT_PLANE_PAD), LANES), F32)],
        compiler_params=_cparams("arbitrary", "arbitrary"),
        name="hyena_fftconv",
    )(u, mult, spec, bias, t1_data, f2_fwd, f2_inv, t4)


@functools.lru_cache(maxsize=None)
def _small_fft_tables(seq_len):
    n = 2 * seq_len
    f = np.exp(-2j * np.pi * np.outer(np.arange(n), np.arange(n)) / n)
    fwd = _blockreal(f[:, :seq_len])
    real = np.concatenate([f.real, f.imag], axis=0)
    inv = _blockreal(np.conj(f)[:seq_len, :] / n)
    as32 = lambda a: jnp.asarray(a, dtype=F32)
    return as32(fwd), as32(real), as32(inv)


def _ctx_hyena_kernel(v_ref, x1_ref, x2_ref, k_ref, bias_ref, fwd_ref, real_ref, inv_ref, o_ref, *, seq_len):
    n = 2 * seq_len

    def conv(ua, ub, g):
        spec = _dot(real_ref[...], k_ref[g])
        x = _dot(fwd_ref[...], jnp.concatenate([ua, ub], axis=0))
        xr, xi, kr, ki = x[:n], x[n:], spec[:n], spec[n:]
        y = _dot(inv_ref[...], jnp.concatenate([xr * kr - xi * ki, xr * ki + xi * kr], axis=0))
        b = bias_ref[g]
        return y[:seq_len] + b * ua, y[seq_len:] + b * ub

    c1a, c1b = conv(v_ref[0], v_ref[1], 0)
    y1a, y1b = x1_ref[0] * c1a, x1_ref[1] * c1b
    c2a, c2b = conv(y1a, y1b, 1)
    o_ref[0] = x2_ref[0] * c2a
    o_ref[1] = x2_ref[1] * c2b


def _ctx_hyena(hyc, row_block, kfilt, bias, seq_len, w):
    bsz = hyc.shape[0]
    n = 2 * seq_len
    fwd, real, inv = _small_fft_tables(seq_len)
    nt = w // LANES
    col = lambda c0: pl.BlockSpec((2, seq_len, LANES), lambda j, p: (p, row_block, c0 + j))
    return pl.pallas_call(
        functools.partial(_ctx_hyena_kernel, seq_len=seq_len),
        out_shape=jax.ShapeDtypeStruct((bsz, seq_len, w), F32),
        grid=(nt, bsz // 2),
        in_specs=[col(0), col(nt), col(2 * nt),
                  pl.BlockSpec((2, n, LANES), lambda j, p: (0, 0, j)),
                  pl.BlockSpec((2, 1, LANES), lambda j, p: (0, 0, j)),
                  _const_spec((2 * n, 2 * seq_len)), _const_spec((2 * n, n)), _const_spec((2 * seq_len, 2 * n))],
        out_specs=pl.BlockSpec((2, seq_len, LANES), lambda j, p: (p, 0, j)),
        compiler_params=_cparams("arbitrary", "arbitrary"),
        name="hyena_ctx",
    )(hyc, hyc, hyc, kfilt, bias, fwd, real, inv)


def _rms(x, g):
    return x * lax.rsqrt(jnp.mean(x * x, axis=-1, keepdims=True) + NORM_EPS) * g


def _inproj_kernel(x_ref, mod_ref, g_ref, w_ref, *out_refs, widths):
    shift, scale = mod_ref[0, 0, 0:1, :], mod_ref[0, 0, 1:2, :]
    h = _rms(x_ref[0], g_ref[...]) * (1.0 + scale) + shift
    z = jnp.dot(h.astype(BF16), w_ref[...], preferred_element_type=F32)
    off = 0
    for o_ref, wd in zip(out_refs[:-1], widths[:-1]):
        o_ref[0] = z[:, off:off + wd]
        off += wd
    out_refs[-1][0] = _silu(z[:, off:])


def _inproj(xs, mods, norm_g, w_perm, widths, n_lat_tiles):
    bsz, s, d = xs.shape
    nt = s // ROW_TILE
    return pl.pallas_call(
        functools.partial(_inproj_kernel, widths=widths),
        out_shape=[jax.ShapeDtypeStruct((bsz, s, wd), F32) for wd in widths],
        grid=(bsz, nt),
        in_specs=[pl.BlockSpec((1, ROW_TILE, d), lambda b, t: (b, t, 0)),
                  pl.BlockSpec((1, 1, 3, d), lambda b, t: (b, t // n_lat_tiles, 0, 0)),
                  pl.BlockSpec((1, d), lambda b, t: (0, 0)),
                  _const_spec(w_perm.shape)],
        out_specs=[pl.BlockSpec((1, ROW_TILE, wd), lambda b, t: (b, t, 0)) for wd in widths],
        compiler_params=_cparams("arbitrary", "arbitrary"),
        name="in_projection",
    )(xs, mods, norm_g.reshape(1, d), w_perm)


def _halo_specs(width, n_tiles):
    per = ROW_TILE // 8
    prev = pl.BlockSpec((1, 8, width), lambda b, t: (b, jnp.maximum(t * per - 1, 0), 0))
    nxt = pl.BlockSpec((1, 8, width), lambda b, t: (b, jnp.minimum((t + 1) * per, n_tiles * per - 1), 0))
    return prev, nxt


def _short_conv_tile(z, prev8, next8, w, t, n_lat_tiles, n_tiles):
    first = jnp.logical_or(t == 0, t == n_lat_tiles)
    last = jnp.logical_or(t == n_lat_tiles - 1, t == n_tiles - 1)
    above = jnp.where(first, 0.0, prev8[7:8, :])
    below = jnp.where(last, 0.0, next8[0:1, :])
    row = lax.broadcasted_iota(jnp.int32, z.shape, 0)
    zm1 = jnp.where(row == 0, above, pltpu.roll(z, 1, 0))
    zp1 = jnp.where(row == z.shape[0] - 1, below, pltpu.roll(z, z.shape[0] - 1, 0))
    return zm1 * w[0:1, :] + z * w[1:2, :] + zp1 * w[2:3, :]


def _short_conv_kernel(z_ref, p_ref, n_ref, w_ref, o_ref, *, n_lat_tiles, n_tiles):
    o_ref[0] = _short_conv_tile(z_ref[0], p_ref[0], n_ref[0], w_ref[...], pl.program_id(1),
                                n_lat_tiles, n_tiles)


def _short_conv(z, w, n_lat_tiles):
    bsz, s, width = z.shape
    nt = s // ROW_TILE
    prev, nxt = _halo_specs(width, nt)
    tile = pl.BlockSpec((1, ROW_TILE, width), lambda b, t: (b, t, 0))
    return pl.pallas_call(
        functools.partial(_short_conv_kernel, n_lat_tiles=n_lat_tiles, n_tiles=nt),
        out_shape=jax.ShapeDtypeStruct(z.shape, F32),
        grid=(bsz, nt),
        in_specs=[tile, prev, nxt, pl.BlockSpec((3, width), lambda b, t: (0, 0))],
        out_specs=tile,
        compiler_params=_cparams("arbitrary", "arbitrary"),
        name="short_conv",
    )(z, z, z, w)


def _outproj_kernel(a_lat_ref, a_ctx_ref, b_ref, c_ref, d_ref, gt_ref, x_ref, mod_ref, w_ref, fg_ref, o_ref,
                    *, n_lat_tiles, final):
    is_ctx = pl.program_id(1) >= n_lat_tiles
    a = jnp.where(is_ctx, a_ctx_ref[0], a_lat_ref[0])
    mix = jnp.concatenate([a, b_ref[0], c_ref[0], d_ref[0]], axis=-1) * gt_ref[0]
    y = jnp.dot(mix.astype(BF16), w_ref[...], preferred_element_type=F32)
    x = x_ref[0] + mod_ref[0, 0, 2:3, :] * y
    o_ref[0] = _rms(x, fg_ref[...]) if final else x


def _outproj(a_lat, a_ctx, bmix, cmix, dmix, gates, xs, mods, w_out, final_g, n_lat_tiles, final):
    bsz, s, d = xs.shape
    wb = a_lat.shape[-1]
    nt = n_lat_tiles if final else s // ROW_TILE
    tile = lambda wd: pl.BlockSpec((1, ROW_TILE, wd), lambda b, t: (b, t, 0))
    return pl.pallas_call(
        functools.partial(_outproj_kernel, n_lat_tiles=n_lat_tiles, final=final),
        out_shape=jax.ShapeDtypeStruct((bsz, nt * ROW_TILE, d), F32),
        grid=(bsz, nt),
        in_specs=[pl.BlockSpec((1, ROW_TILE, wb), lambda b, t: (b, jnp.minimum(t, n_lat_tiles - 1), 0)),
                  pl.BlockSpec((1, ROW_TILE, wb), lambda b, t: (b, 0, 0)),
                  tile(wb), tile(wb), tile(wb), tile(4 * wb), tile(d),
                  pl.BlockSpec((1, 1, 3, d), lambda b, t: (b, t // n_lat_tiles, 0, 0)),
                  _const_spec(w_out.shape),
                  pl.BlockSpec((1, d), lambda b, t: (0, 0))],
        out_specs=tile(d),
        compiler_params=_cparams("arbitrary", "arbitrary"),
        name="out_projection",
    )(a_lat, a_ctx, bmix, cmix, dmix, gates, xs, mods, w_out, final_g.reshape(1, d))


def _rope_tables(seq_len, ctx_len):
    rows = seq_len // GRID_W
    row = jnp.repeat(jnp.arange(rows, dtype=F32), GRID_W)
    col = jnp.tile(jnp.arange(GRID_W, dtype=F32), rows)
    n_freq = HEAD_DIM // 4
    inv_freq = ROPE_THETA ** (-jnp.arange(n_freq, dtype=F32) / n_freq)
    ar, ac = row[:, None] * inv_freq, col[:, None] * inv_freq
    zero = jnp.zeros_like(ar)
    cos = jnp.concatenate([jnp.cos(ar), jnp.cos(ar), jnp.cos(ac), jnp.cos(ac)], axis=-1)
    sin_hi = jnp.concatenate([-jnp.sin(ar), zero, -jnp.sin(ac), zero], axis=-1)
    sin_lo = jnp.concatenate([zero, jnp.sin(ar), zero, jnp.sin(ac)], axis=-1)
    pad = lambda t, v: jnp.concatenate([t, jnp.full((ctx_len, HEAD_DIM), v, F32)], axis=0)
    return pad(cos, 1.0), pad(sin_hi, 0.0), pad(sin_lo, 0.0)


def _rope(x, cos, sin_hi, sin_lo):
    q = HEAD_DIM // 4
    w = x.shape[-1]
    return x * cos + pltpu.roll(x, w - q, 1) * sin_hi + pltpu.roll(x, q, 1) * sin_lo


def _head_mean_sq(x, bd3):
    return _mm_exact_rhs(x * x, bd3)


def _qkprep_kernel(wa_ref, fa_ref, cos_ref, shi_ref, slo_ref, qg_ref, kg_ref, bdq_ref, bdk_ref,
                   qw_ref, kw_ref, vw_ref, qf_ref, kf_ref, vf_ref, *, wq, wk):
    nq, nk = wq // HEAD_DIM, wk // HEAD_DIM
    cos, shi, slo = cos_ref[...], shi_ref[...], slo_ref[...]
    tab = lambda t, n: jnp.concatenate([t] * n, axis=-1)
    cq, hq, lq = tab(cos, nq), tab(shi, nq), tab(slo, nq)
    ck, hk, lk = tab(cos, nk), tab(shi, nk), tab(slo, nk)
    scale = HEAD_DIM ** -0.5

    def emit(ref, val, n):
        for h in range(n):
            ref[0, h] = val[:, h * HEAD_DIM:(h + 1) * HEAD_DIM].astype(ref.dtype)

    wa = wa_ref[0]
    emit(qw_ref, _rope(wa[:, :wq], cq, hq, lq) * scale, nq)
    emit(kw_ref, _rope(wa[:, wq:wq + wk], ck, hk, lk), nk)
    emit(vw_ref, wa[:, wq + wk:], nk)
    fa = fa_ref[0]
    q, k = fa[:, :wq], fa[:, wq:wq + wk]
    q = q * lax.rsqrt(_head_mean_sq(q, bdq_ref[...]) + NORM_EPS) * qg_ref[...]
    k = k * lax.rsqrt(_head_mean_sq(k, bdk_ref[...]) + NORM_EPS) * kg_ref[...]
    emit(qf_ref, _rope(q, cq, hq, lq) * scale, nq)
    emit(kf_ref, _rope(k, ck, hk, lk), nk)
    v = fa[:, wq + wk:]
    lane = lax.broadcasted_iota(jnp.int32, (v.shape[0], LANES - HEAD_DIM), 1)
    ones_pad = jnp.where(lane == 0, 1.0, 0.0)
    for h in range(nk):
        vf_ref[0, h] = jnp.concatenate([v[:, h * HEAD_DIM:(h + 1) * HEAD_DIM], ones_pad],
                                       axis=-1).astype(vf_ref.dtype)


def _head_block_diag(width, value):
    h = np.arange(width) // HEAD_DIM
    bd = ((h[:, None] == h[None, :]) * value).astype(np.float32).astype(BF16)
    assert np.all(bd.astype(np.float32) == (h[:, None] == h[None, :]) * value)
    return np.concatenate([bd, bd, bd], axis=0)


def _qkprep(wa, fa, rope, q_gain, k_gain, wq, wk):
    bsz, s, _ = wa.shape
    nq, nk = wq // HEAD_DIM, wk // HEAD_DIM
    nt = s // ROW_TILE
    tile = lambda wd: pl.BlockSpec((1, ROW_TILE, wd), lambda b, t: (b, t, 0))
    tab = pl.BlockSpec((ROW_TILE, HEAD_DIM), lambda b, t: (t, 0))
    hm = lambda n, wd=HEAD_DIM: pl.BlockSpec((1, n, ROW_TILE, wd), lambda b, t: (b, 0, t, 0))
    shp = lambda n, wd=HEAD_DIM: jax.ShapeDtypeStruct((bsz, n, s, wd), BF16)
    return pl.pallas_call(
        functools.partial(_qkprep_kernel, wq=wq, wk=wk),
        out_shape=[shp(nq), shp(nk), shp(nk), shp(nq), shp(nk), shp(nk, LANES)],
        grid=(bsz, nt),
        in_specs=[tile(wq + 2 * wk), tile(wq + 2 * wk), tab, tab, tab,
                  pl.BlockSpec((1, wq), lambda b, t: (0, 0)), pl.BlockSpec((1, wk), lambda b, t: (0, 0)),
                  _const_spec((3 * wq, wq)), _const_spec((3 * wk, wk))],
        out_specs=[hm(nq), hm(nk), hm(nk), hm(nq), hm(nk), hm(nk, LANES)],
        compiler_params=_cparams("arbitrary", "arbitrary"),
        name="qk_prep",
    )(wa, fa, *rope, jnp.tile(q_gain, nq).reshape(1, wq), jnp.tile(k_gain, nk).reshape(1, wk),
      _head_block_diag(wq, 1.0 / HEAD_DIM), _head_block_diag(wk, 1.0 / HEAD_DIM))


def _window_attn_kernel(q_ref, k_ref, v_ref, sink_ref, o_ref, *, seq_len, ctx_len, tq):
    t = pl.program_id(2)
    n_lat = seq_len // tq
    g = q_ref.shape[1]
    band = 2 * WINDOW + tq
    q = q_ref[0].reshape(g * tq, HEAD_DIM)
    kc, vc = k_ref[0, 0, seq_len:seq_len + ctx_len, :], v_ref[0, 0, seq_len:seq_len + ctx_len, :]
    start = pl.multiple_of(jnp.clip((t - 1) * tq, 0, seq_len - band), tq)
    kb, vb = k_ref[0, 0, pl.ds(start, band), :], v_ref[0, 0, pl.ds(start, band), :]
    nt_dot = lambda a, b: lax.dot_general(a, b, (((1,), (1,)), ((), ())), preferred_element_type=F32)
    s_ctx = nt_dot(q, kc)
    s_loc = nt_dot(q, kb)
    qpos = t * tq + lax.broadcasted_iota(jnp.int32, (g, tq, band), 1).reshape(g * tq, band)
    kpos = start + lax.broadcasted_iota(jnp.int32, (g * tq, band), 1)
    valid = jnp.logical_and(jnp.abs(qpos - kpos) <= WINDOW, t < n_lat)
    s_loc = jnp.where(valid, s_loc, NEG_INF)
    sink = sink_ref[0]
    m = jnp.maximum(jnp.maximum(jnp.max(s_ctx, axis=-1, keepdims=True),
                                jnp.max(s_loc, axis=-1, keepdims=True)), sink)
    p_ctx, p_loc = jnp.exp(s_ctx - m), jnp.exp(s_loc - m)
    denom = (jnp.sum(p_ctx, axis=-1, keepdims=True) + jnp.sum(p_loc, axis=-1, keepdims=True)
             + jnp.exp(sink - m))
    out = (jnp.dot(p_ctx.astype(BF16), vc, preferred_element_type=F32)
           + jnp.dot(p_loc.astype(BF16), vb, preferred_element_type=F32)) / denom
    o_ref[0] = jnp.concatenate([out[h * tq:(h + 1) * tq] for h in range(g)], axis=-1)


def _window_attention(q, k, v, sink, seq_len, ctx_len):
    bsz, nq, s, _ = q.shape
    nkv = k.shape[1]
    g = nq // nkv
    tq = WINDOW
    sink_rows = jnp.repeat(sink.astype(F32).reshape(nkv, g), tq, axis=1).reshape(nkv, g * tq, 1)
    kv = pl.BlockSpec((1, 1, s, HEAD_DIM), lambda b, h, t: (b, h, 0, 0))
    return pl.pallas_call(
        functools.partial(_window_attn_kernel, seq_len=seq_len, ctx_len=ctx_len, tq=tq),
        out_shape=jax.ShapeDtypeStruct((bsz, s, nq * HEAD_DIM), F32),
        grid=(bsz, nkv, s // tq),
        in_specs=[pl.BlockSpec((1, g, tq, HEAD_DIM), lambda b, h, t: (b, h, t, 0)), kv, kv,
                  pl.BlockSpec((1, g * tq, 1), lambda b, h, t: (h, 0, 0))],
        out_specs=pl.BlockSpec((1, tq, g * HEAD_DIM), lambda b, h, t: (b, t, h)),
        compiler_params=_cparams("arbitrary", "arbitrary", "arbitrary"),
        name="window_attention",
    )(q, k, v, sink_rows)


DENSE_KEY_BLOCK = 1024


def _dense_attn_kernel(q_ref, k_ref, v_ref, o_ref, *, seq_len, tq, tk):
    t = pl.program_id(2)
    n_lat = seq_len // tq
    g = q_ref.shape[1]
    ctx_len = k_ref.shape[2] - seq_len
    q = q_ref[0].reshape(g * tq, HEAD_DIM)

    def step(carry, start, size):
        m, acc = carry
        s = lax.dot_general(q, k_ref[0, 0, start:start + size, :], (((1,), (1,)), ((), ())),
                            preferred_element_type=F32)
        m_new = jnp.maximum(m, jnp.max(s, axis=-1, keepdims=True))
        p = jnp.exp(s - m_new).astype(BF16)
        pv = jnp.dot(p, v_ref[0, 0, start:start + size, :], preferred_element_type=F32)
        return m_new, jnp.exp(m - m_new) * acc + pv

    def finish(carry):
        _, acc = carry
        out = acc[:, :HEAD_DIM] / acc[:, HEAD_DIM:HEAD_DIM + 1]
        o_ref[0] = jnp.concatenate([out[h * tq:(h + 1) * tq] for h in range(g)], axis=-1)

    init = (jnp.full((g * tq, 1), NEG_INF, F32), jnp.zeros((g * tq, v_ref.shape[-1]), F32))

    @pl.when(t < n_lat)
    def _():
        carry = step(init, seq_len, ctx_len)
        for j in range(seq_len // tk):
            carry = step(carry, j * tk, tk)
        finish(carry)

    @pl.when(t >= n_lat)
    def _():
        finish(step(init, seq_len, ctx_len))


def _dense_attention(q, k, v, seq_len):
    bsz, nq, s, _ = q.shape
    nkv = k.shape[1]
    g = nq // nkv
    tq = ROW_TILE
    tk = math.gcd(seq_len, DENSE_KEY_BLOCK)
    kv = lambda a: pl.BlockSpec((1, 1, s, a.shape[-1]), lambda b, h, t: (b, h, 0, 0))
    return pl.pallas_call(
        functools.partial(_dense_attn_kernel, seq_len=seq_len, tq=tq, tk=tk),
        out_shape=jax.ShapeDtypeStruct((bsz, s, nq * HEAD_DIM), F32),
        grid=(bsz, nkv, s // tq),
        in_specs=[pl.BlockSpec((1, g, tq, HEAD_DIM), lambda b, h, t: (b, h, t, 0)), kv(k), kv(v)],
        out_specs=pl.BlockSpec((1, tq, g * HEAD_DIM), lambda b, h, t: (b, t, h)),
        compiler_params=_cparams("arbitrary", "arbitrary", "arbitrary"),
        name="dense_attention",
    )(q, k, v)


def _softplus(x):
    return jnp.maximum(x, 0.0) + jnp.log(1.0 + jnp.exp(-jnp.abs(x)))


def _split(a):
    bits = lax.bitcast_convert_type(a, jnp.uint32) & jnp.uint32(0xFFFF0000)
    hi = lax.bitcast_convert_type(bits, F32)
    return hi, a - hi


def _rhs3(b):
    hi, lo = _split(b)
    return jnp.concatenate([hi, hi, lo], axis=-2).astype(BF16)


def _split3(a):
    hi, rest = _split(a)
    mid, lo = _split(rest)
    return hi, mid, lo


def _mm_exact_lhs(tbl3, x):
    return jnp.dot(tbl3, jnp.concatenate(_split3(x), axis=0).astype(BF16), preferred_element_type=F32)


def _mm_exact_rhs(x, tbl3):
    return jnp.dot(jnp.concatenate(_split3(x), axis=-1).astype(BF16), tbl3, preferred_element_type=F32)


def _mm_split(a, b):
    ah, al = _split(a)
    return jnp.dot(jnp.concatenate([ah, al, ah], axis=-1).astype(BF16), _rhs3(b), preferred_element_type=F32)


RW_DEPTH = 1


def _lhs4(a):
    return a.astype(BF16)


def _rhs4(b):
    return b.astype(BF16)


def _rhs4_nt(b):
    return b.astype(BF16)


def _bmm(lf, rf):
    return lax.dot_general(lf, rf, (((2,), (1,)), ((0,), (0,))), preferred_element_type=F32)


RW_SLOTS_PER_STEP = 16


def _rwkv_par_kernel(z_ref, prev_ref, next_ref, cw_ref, w0_ref, wup_ref, a0_ref, aup_ref, kk_ref, ka_ref,
                     bd_ref, cum_ref,
                     rp3_ref, yvq_ref, rvk_ref,
                     lbig_ref, rbig_ref, at_ref, rt_ref, vv3_ref, bct_ref, kct_ref, wt_ref,
                     *, n_lat_tiles, n_tiles, w, lora):
    tc = RW_CHUNK
    nc, nh = ROW_TILE // tc, w // HEAD_DIM
    nb = RW_SLOTS_PER_STEP
    z = _short_conv_tile(z_ref[0], prev_ref[0], next_ref[0], cw_ref[...], pl.program_id(1),
                         n_lat_tiles, n_tiles)
    r, k, v = z[:, :w], z[:, w:2 * w], z[:, 2 * w:3 * w]
    w_low = jnp.tanh(z[:, 3 * w:3 * w + lora])
    a_low = z[:, 3 * w + lora:]
    kk = k * kk_ref[...]
    kk = kk / jnp.maximum(jnp.sqrt(_mm_exact_rhs(kk * kk, bd_ref[...])), 1e-12)
    ksum = jnp.zeros_like(k)
    for d in range(2):
        w_log = -_softplus(-(w0_ref[d:d + 1, :] + _mm_split(w_low, wup_ref[d]))) - 0.5
        lw = -jnp.exp(w_log)
        a = 1.0 / (1.0 + jnp.exp(-(a0_ref[d:d + 1, :] + _mm_split(a_low, aup_ref[d]))))
        kd = k * (1.0 + (a - 1.0) * ka_ref[...])
        ksum = ksum + kd
        sums = _mm_exact_lhs(cum_ref[d], lw)
        c, ctot = sums[:ROW_TILE], sums[ROW_TILE:]
        e_neg, e_rem = jnp.exp(-c), jnp.exp(ctot - c)
        b = kk * a
        at, rt = -kk * jnp.exp(c - lw), r * jnp.exp(c)
        w_tot = jnp.exp(ctot)
        bh, kh = b * e_neg, kd * e_neg
        bct, kct = (b * e_rem).T, (kd * e_rem).T
        for ci in range(nc):
            for h in range(nh):
                slot = (d * nc + ci) * nh + h
                rows, cols = slice(ci * tc, (ci + 1) * tc), slice(h * HEAD_DIM, (h + 1) * HEAD_DIM)
                lbig_ref[slot, :tc, :] = _lhs4(at[rows, cols])
                lbig_ref[slot, tc:, :] = _lhs4(rt[rows, cols])
                rbig_ref[slot, :tc, :] = _rhs4_nt(bh[rows, cols])
                rbig_ref[slot, tc:, :] = _rhs4_nt(kh[rows, cols])
                at_ref[slot] = at[rows, cols]
                rt_ref[slot] = rt[rows, cols]
                vv3_ref[slot] = _rhs4(v[rows, cols])
                bct_ref[slot] = _lhs4(bct[cols, rows])
                kct_ref[slot] = _lhs4(kct[cols, rows])
                wt_ref[slot] = w_tot[ci * tc:ci * tc + 8, cols]
    rvk_ref[0] = jnp.concatenate([r, v, ksum], axis=-1)

    row = lax.broadcasted_iota(jnp.int32, (tc, tc), 0)
    col = lax.broadcasted_iota(jnp.int32, (tc, tc), 1)
    eye = row == col

    def level_mask(s):
        return jnp.logical_and(row // (2 * s) == col // (2 * s), row // s != col // s)

    steps_per_dir = nc * nh // nb

    def chunk_group(g, carry):
        slots = pl.ds(pl.multiple_of(g * nb, nb), nb)
        d = g // steps_per_dir
        fwd = d == 0
        before = jnp.logical_or(jnp.logical_and(fwd, row > col),
                                jnp.logical_and(jnp.logical_not(fwd), row < col))
        upto = jnp.logical_or(before, eye)
        big = lax.dot_general(lbig_ref[slots], rbig_ref[slots], (((2,), (2,)), ((0,), (0,))),
                              preferred_element_type=F32)
        a_ab = jnp.where(before, big[:, :tc, :tc], 0.0)
        a_ak = jnp.where(before, big[:, :tc, tc:], 0.0)
        a_rb = jnp.where(upto, big[:, tc:, :tc], 0.0)
        a_rk = jnp.where(upto, big[:, tc:, tc:], 0.0)
        x = jnp.where(eye, 1.0, jnp.where(level_mask(1), a_ab, 0.0))
        s = 2
        while s < tc:
            half = _bmm(_lhs4(x), _rhs4(jnp.where(level_mask(s), a_ab, 0.0)))
            x = x + _bmm(_lhs4(half), _rhs4(x))
            s *= 2
        vv3 = vv3_ref[slots]
        akrk = _bmm(jnp.concatenate([_lhs4(a_ak), _lhs4(a_rk)], axis=1), vv3)
        xa3 = _rhs4(_bmm(_lhs4(x), _rhs4(jnp.concatenate([at_ref[slots], akrk[:, :tc]], axis=-1))))
        ra = _bmm(_lhs4(a_rb), xa3)
        pq = _bmm(bct_ref[slots], xa3)
        rp = rt_ref[slots] + ra[:, :, :HEAD_DIM]
        yv = ra[:, :, HEAD_DIM:] + akrk[:, tc:]
        p = jnp.where(eye, wt_ref[slots][:, 0:1, :], 0.0) + pq[:, :, :HEAD_DIM]
        q = pq[:, :, HEAD_DIM:] + _bmm(kct_ref[slots], vv3)
        rp3 = jnp.concatenate([_lhs4(rp), _lhs4(p)], axis=1)
        yvq = jnp.concatenate([yv, q], axis=1)
        for j in range(nb):
            ci = (g % steps_per_dir) * (nb // nh) + j // nh
            rp3_ref[0, d, j % nh, ci] = rp3[j]
            yvq_ref[0, d, j % nh, ci] = yvq[j]
        return carry
    lax.fori_loop(0, 2 * steps_per_dir, chunk_group, 0, unroll=True)


def _chunk_matrices():
    t = np.arange(ROW_TILE)
    same = (t[:, None] // RW_CHUNK) == (t[None, :] // RW_CHUNK)
    tabs = []
    for run in (same & (t[None, :] <= t[:, None]), same & (t[None, :] >= t[:, None])):
        m = np.concatenate([run, same], axis=0).astype(np.float32).astype(BF16)
        tabs.append(np.concatenate([m, m, m], axis=1))
    return np.stack(tabs)


def _rwkv_par(rw, conv_w, w0, w_up, a0, a_up, k_k, k_a, w, n_lat_tiles):
    bsz, s, width = rw.shape
    nt = s // ROW_TILE
    nh, nc = w // HEAD_DIM, ROW_TILE // RW_CHUNK
    lora = w_up.shape[1]
    prev, nxt = _halo_specs(width, nt)
    full = lambda a: pl.BlockSpec(a.shape, lambda b, t: (0,) * a.ndim)
    cum = _chunk_matrices()
    bd = _head_block_diag(w, 1.0)
    vec = lambda a: a.reshape(1, w)
    n_chunks = s // RW_CHUNK
    nslots = 2 * nc * nh
    tc3 = RW_DEPTH * RW_CHUNK
    consts = (conv_w, w0, w_up, a0, a_up, vec(k_k), vec(k_a), bd, cum)
    return pl.pallas_call(
        functools.partial(_rwkv_par_kernel, n_lat_tiles=n_lat_tiles, n_tiles=nt, w=w, lora=lora),
        out_shape=[jax.ShapeDtypeStruct((bsz, 2, nh, n_chunks, 2 * RW_CHUNK, RW_DEPTH * HEAD_DIM), BF16),
                   jax.ShapeDtypeStruct((bsz, 2, nh, n_chunks, 2 * RW_CHUNK, HEAD_DIM), F32),
                   jax.ShapeDtypeStruct((bsz, s, 3 * w), F32)],
        grid=(bsz, nt),
        in_specs=[pl.BlockSpec((1, ROW_TILE, width), lambda b, t: (b, t, 0)), prev, nxt]
                 + [full(a) for a in consts],
        out_specs=[pl.BlockSpec((1, 2, nh, nc, 2 * RW_CHUNK, RW_DEPTH * HEAD_DIM), lambda b, t: (b, 0, 0, t, 0, 0)),
                   pl.BlockSpec((1, 2, nh, nc, 2 * RW_CHUNK, HEAD_DIM), lambda b, t: (b, 0, 0, t, 0, 0)),
                   pl.BlockSpec((1, ROW_TILE, 3 * w), lambda b, t: (b, t, 0))],
        scratch_shapes=[pltpu.VMEM((nslots, 2 * RW_CHUNK, tc3), BF16),
                        pltpu.VMEM((nslots, 2 * RW_CHUNK, tc3), BF16),
                        pltpu.VMEM((nslots, RW_CHUNK, HEAD_DIM), F32),
                        pltpu.VMEM((nslots, RW_CHUNK, HEAD_DIM), F32),
                        pltpu.VMEM((nslots, tc3, HEAD_DIM), BF16),
                        pltpu.VMEM((nslots, HEAD_DIM, tc3), BF16),
                        pltpu.VMEM((nslots, HEAD_DIM, tc3), BF16),
                        pltpu.VMEM((nslots, 8, HEAD_DIM), F32)],
        compiler_params=_cparams("arbitrary", "arbitrary"),
        name="rwkv_chunk_prep",
    )(rw, rw, rw, *consts)


def _rwkv_seq_kernel(rp3f_ref, yvqf_ref, rp3b_ref, yvqb_ref, yf_ref, yb_ref, g_ref):
    @pl.when(pl.program_id(1) == 0)
    def _():
        g_ref[...] = jnp.zeros_like(g_ref)

    nh = g_ref.shape[1]
    tc = RW_CHUNK
    grp = rp3f_ref.shape[3]
    for step in range(grp):
        for d, (rp3, yvq, y) in enumerate(((rp3f_ref, yvqf_ref, yf_ref), (rp3b_ref, yvqb_ref, yb_ref))):
            ci = step if d == 0 else grp - 1 - step
            for h in range(nh):
                out = (jnp.dot(rp3[0, 0, h, ci], _rhs4(g_ref[d, h]), preferred_element_type=F32)
                       + yvq[0, 0, h, ci])
                y[0, h, ci * tc:(ci + 1) * tc, :] = out[:tc]
                g_ref[d, h] = out[tc:]


RW_SCAN_GROUP = 4


def _rwkv_seq(rp3, yvq, n_lat_chunks):
    bsz, _, nh, n_chunks = rp3.shape[:4]
    grp = RW_SCAN_GROUP
    assert n_lat_chunks % grp == 0 and n_chunks % grp == 0
    n_groups, n_lat, n_ctx = n_chunks // grp, n_lat_chunks // grp, (n_chunks - n_lat_chunks) // grp
    order = (lambda i: jnp.where(i < n_ctx, n_lat + i, i - n_ctx),
             lambda i: n_groups - 1 - i)
    blk = lambda d, a: pl.BlockSpec((1, 1, nh, grp) + a.shape[4:], lambda b, i: (b, d, 0, order[d](i), 0, 0))
    out = lambda d: pl.BlockSpec((1, nh, grp * RW_CHUNK, HEAD_DIM), lambda b, i: (b, 0, order[d](i), 0))
    shp = jax.ShapeDtypeStruct((bsz, nh, n_chunks * RW_CHUNK, HEAD_DIM), F32)
    return pl.pallas_call(
        _rwkv_seq_kernel,
        out_shape=[shp, shp],
        grid=(bsz, n_groups),
        in_specs=[blk(0, rp3), blk(0, yvq), blk(1, rp3), blk(1, yvq)],
        out_specs=[out(0), out(1)],
        scratch_shapes=[pltpu.VMEM((2, nh, HEAD_DIM, HEAD_DIM), F32)],
        compiler_params=_cparams("arbitrary", "arbitrary"),
        name="rwkv_state_scan",
    )(rp3, yvq, rp3, yvq)


def _rwkv_out_kernel(yf_ref, yb_ref, rvk_ref, rk_ref, g_ref, b_ref, o_ref, *, w):
    nh = w // HEAD_DIM
    rvk = rvk_ref[0]
    r, v, ksum = rvk[:, :w], rvk[:, w:2 * w], rvk[:, 2 * w:]
    rkk = r * ksum * rk_ref[...]
    normed, bonus = [], []
    for h in range(nh):
        cols = slice(h * HEAD_DIM, (h + 1) * HEAD_DIM)
        y = yf_ref[0, h] + yb_ref[0, h]
        mu = jnp.mean(y, axis=-1, keepdims=True)
        var = jnp.mean(jnp.square(y - mu), axis=-1, keepdims=True)
        normed.append((y - mu) * lax.rsqrt(var + RW_GN_EPS))
        bonus.append(jnp.sum(rkk[:, cols], axis=-1, keepdims=True) * v[:, cols])
    o_ref[0] = (jnp.concatenate(normed, axis=-1) * g_ref[...] + b_ref[...]
                + jnp.concatenate(bonus, axis=-1))


def _rwkv_out(yf, yb, rvk, r_k, ln_g, ln_b, w):
    bsz, nh, s, _ = yf.shape
    hm = pl.BlockSpec((1, nh, ROW_TILE, HEAD_DIM), lambda b, t: (b, 0, t, 0))
    vec = pl.BlockSpec((1, w), lambda b, t: (0, 0))
    return pl.pallas_call(
        functools.partial(_rwkv_out_kernel, w=w),
        out_shape=jax.ShapeDtypeStruct((bsz, s, w), F32),
        grid=(bsz, s // ROW_TILE),
        in_specs=[hm, hm, pl.BlockSpec((1, ROW_TILE, 3 * w), lambda b, t: (b, t, 0)), vec, vec, vec],
        out_specs=pl.BlockSpec((1, ROW_TILE, w), lambda b, t: (b, t, 0)),
        compiler_params=_cparams("arbitrary", "arbitrary"),
        name="rwkv_readout",
    )(yf, yb, rvk, r_k.reshape(1, w), ln_g.reshape(1, w), ln_b.reshape(1, w))


def _hyena_two_sided_filters(seq_len, fw1, fb1, freq, fw2, fb2, fw3, width):
    bands = (fw1.shape[0] - 1) // 2
    mm = functools.partial(jnp.matmul, precision=HI)
    t = jnp.linspace(0.0, 1.0, seq_len, dtype=F32)[:, None]
    wpos = (2.0 * math.pi / seq_len) * jnp.arange(seq_len, dtype=F32)[:, None]
    f = jnp.linspace(1e-4, bands - 1, bands, dtype=F32)[None, :]
    z = jnp.concatenate([t, jnp.cos(f * wpos), jnp.sin(f * wpos)], axis=-1)
    h = jnp.sin(freq * (mm(z, fw1) + fb1))
    h = jnp.sin(freq * (mm(h, fw2) + fb2))
    h = mm(h, fw3).reshape(seq_len, 2, 2, width)
    max_decay = math.log(HY_DECAY_TARGET) / HY_FAST_DECAY
    min_decay = math.log(HY_DECAY_TARGET) / HY_SLOW_DECAY
    deltas = jnp.linspace(min_decay, max_decay, width, dtype=F32)
    h = h * jnp.exp(-t * jnp.abs(deltas))[:, None, None, :]
    h = h / jnp.sum(jnp.abs(h), axis=(0, 2), keepdims=True)
    fwd, bwd = h[:, :, 0], h[:, :, 1]
    k = jnp.concatenate([fwd, jnp.zeros_like(fwd[:1]), bwd[:0:-1]], axis=0)
    return jnp.moveaxis(k, 1, 0)


def kernel(x, c, ctx, c_ctx, mod_w, mod_b, norm_g, w_in, w_out, hy_conv, hy_fw1, hy_fb1, hy_freq, hy_fw2,
           hy_fb2, hy_fw3, hy_bias, rw_conv, rw_w0, rw_w_up, rw_a0, rw_a_up, rw_k_k, rw_k_a, rw_r_k,
           rw_ln_g, rw_ln_b, wa_sink, fa_q_norm, fa_k_norm, final_g):
    bsz, seq_len, d = x.shape
    ctx_len = ctx.shape[1]
    depth = w_in.shape[0]
    w_hy = hy_bias.shape[-1]
    w_rw = rw_w0.shape[-1]
    n_wa_heads = wa_sink.shape[-1]
    w_q = n_wa_heads * HEAD_DIM
    w_kv = w_q // 2
    lora = rw_w_up.shape[2] + rw_a_up.shape[2]
    branch_w = (3 * w_hy, 3 * w_rw + lora, w_q + 2 * w_kv, w_q + 2 * w_kv)
    gate_w = (w_hy, w_rw, w_q, w_q)
    assert seq_len % ROW_TILE == 0 and ctx_len % ROW_TILE == 0 and bsz % 2 == 0
    n_lat_tiles = seq_len // ROW_TILE

    starts = np.cumsum([0] + [bw + gw for bw, gw in zip(branch_w, gate_w)])
    cols = np.concatenate([np.arange(s0, s0 + bw) for s0, bw in zip(starts, branch_w)]
                          + [np.arange(s0 + bw, s0 + bw + gw) for s0, bw, gw in zip(starts, branch_w, gate_w)])
    w_in_p = w_in[:, :, cols].astype(BF16)
    w_out_b = w_out.astype(BF16)
    widths = branch_w + (sum(gate_w),)

    pad_rows = (-(bsz + 1)) % 8
    cond = jnp.concatenate([c, c_ctx[None], jnp.zeros((pad_rows, d), F32)], axis=0)
    mod = _modulation(cond, mod_w, mod_b)
    mod_lat = mod[:, :bsz].reshape(depth, bsz, 3, d)
    mod_ctx = jnp.broadcast_to(mod[:, bsz].reshape(depth, 1, 3, d), (depth, bsz, 3, d))
    mods = jnp.stack([mod_lat, mod_ctx], axis=2)

    rope = _rope_tables(seq_len, ctx_len)
    xs = jnp.concatenate([x, ctx], axis=1)
    for l in range(depth):
        last = l == depth - 1
        hy, rw, wa, fa, gates = _inproj(xs, mods[l], norm_g[l], w_in_p[l], widths, n_lat_tiles)

        hyc = _short_conv(hy, hy_conv[l], n_lat_tiles)
        filt = functools.partial(_hyena_two_sided_filters, fw1=hy_fw1[l], fb1=hy_fb1[l], freq=hy_freq[l],
                                 fw2=hy_fw2[l], fb2=hy_fb2[l], fw3=hy_fw3[l], width=w_hy)
        bias = hy_bias[l].reshape(2, 1, w_hy)
        spec = _filter_spectrum(filt(seq_len))
        nt_hy = w_hy // LANES
        y1 = _fftconv_gated(hyc, 0, hyc, nt_hy, spec, 0, bias, seq_len)
        a_lat = _fftconv_gated(y1, 0, hyc, 2 * nt_hy, spec, 1, bias, seq_len)
        a_ctx = a_lat if last else _ctx_hyena(hyc, seq_len // ctx_len, filt(ctx_len), bias, ctx_len, w_hy)

        rp3, yvq, rvk = _rwkv_par(rw, rw_conv[l], rw_w0[l], rw_w_up[l], rw_a0[l], rw_a_up[l],
                                  rw_k_k[l], rw_k_a[l], w_rw, n_lat_tiles)
        yf, yb = _rwkv_seq(rp3, yvq, seq_len // RW_CHUNK)
        b_mix = _rwkv_out(yf, yb, rvk, rw_r_k[l], rw_ln_g[l], rw_ln_b[l], w_rw)

        qw, kw, vw, qf, kf, vf = _qkprep(wa, fa, rope, fa_q_norm[l], fa_k_norm[l], w_q, w_kv)
        c_mix = _window_attention(qw, kw, vw, wa_sink[l], seq_len, ctx_len)
        d_mix = _dense_attention(qf, kf, vf, seq_len)

        xs = _outproj(a_lat, a_ctx, b_mix, c_mix, d_mix, gates, xs, mods[l], w_out_b[l], final_g,
                      n_lat_tiles, last)
    return xs
```

```python
import functools
import math

import jax
import jax.numpy as jnp
import numpy as np
from jax import lax
from jax.experimental import pallas as pl
from jax.experimental.pallas import tpu as pltpu

HEAD_DIM = 64
GRID_W = 64
WINDOW = 128
NORM_EPS = 1e-6
RW_GN_EPS = 64e-5
NEG_INF = -1e30
ROPE_THETA = 10000.0
HY_FAST_DECAY = 0.3
HY_SLOW_DECAY = 1.5
HY_DECAY_TARGET = 1e-2

ROW_TILE = 256
LANES = 128
FFT_N1 = 64
FFT_UNROLL = 8
FFT_PLANE_PAD = 8
RW_CHUNK = 64
VMEM_LIMIT = 60 * 1024 * 1024

F32 = jnp.float32
BF16 = jnp.bfloat16
HI = lax.Precision.HIGHEST


def _dot(a, b):
    return jnp.dot(a, b, preferred_element_type=F32, precision=HI)


def _dot_nt(a, b):
    return lax.dot_general(a, b, (((1,), (1,)), ((), ())), preferred_element_type=F32, precision=HI)


def _dot_tn(a, b):
    return lax.dot_general(a, b, (((0,), (0,)), ((), ())), preferred_element_type=F32, precision=HI)


def _cparams(*sem):
    return pltpu.CompilerParams(dimension_semantics=sem, vmem_limit_bytes=VMEM_LIMIT)


def _const_spec(shape):
    return pl.BlockSpec(shape, lambda *_: (0,) * len(shape), pipeline_mode=pl.Buffered(1))


def _silu(x):
    return x * (1.0 / (1.0 + jnp.exp(-x)))


def _mod_kernel(c_ref, w_ref, b_ref, o_ref):
    o_ref[0] = _dot(_silu(c_ref[...]), w_ref[0]) + b_ref[0]


def _modulation(cond, mod_w, mod_b):
    depth, d, d3 = mod_w.shape
    rows = cond.shape[0]
    return pl.pallas_call(
        _mod_kernel,
        out_shape=jax.ShapeDtypeStruct((depth, rows, d3), F32),
        grid=(depth,),
        in_specs=[pl.BlockSpec((rows, d), lambda l: (0, 0)),
                  pl.BlockSpec((1, d, d3), lambda l: (l, 0, 0)),
                  pl.BlockSpec((1, 1, d3), lambda l: (l, 0, 0))],
        out_specs=pl.BlockSpec((1, rows, d3), lambda l: (l, 0, 0)),
        compiler_params=_cparams("arbitrary"),
        name="modulation",
    )(cond, mod_w, mod_b.reshape(depth, 1, d3))


def _blockreal(m):
    return np.block([[m.real, -m.imag], [m.imag, m.real]])


@functools.lru_cache(maxsize=None)
def _fft_tables(seq_len):
    n = 2 * seq_len
    n1, n2 = FFT_N1, n // FFT_N1
    h1 = n1 // 2
    j2 = np.arange(n2)[:, None, None]
    k1 = np.arange(n1)[None, :, None]
    t1 = np.exp(-2j * np.pi * (j2 * k1 / n + k1 * np.arange(n1)[None, None, :] / n1))
    t1_data = np.stack([_blockreal(t1[j][:, :h1]) for j in range(n2)])
    t1_real = np.concatenate([t1.real, t1.imag], axis=1)
    f2 = np.exp(-2j * np.pi * np.outer(np.arange(n2), np.arange(n2)) / n2)
    f2_fwd = _blockreal(f2)
    f2_inv = _blockreal(np.conj(f2))
    t4 = np.exp(2j * np.pi * (np.arange(h1)[None, :, None] * np.arange(n1)[None, None, :] / n1
                              + j2 * np.arange(n1)[None, None, :] / n)) / n
    t4 = np.stack([_blockreal(t4[j]) for j in range(n2)])
    return tuple(_lhs3_table(t) for t in (t1_data, t1_real, f2_fwd, f2_inv, t4))


def _lhs3_table(m):
    hi = m.astype(np.float32).astype(BF16)
    lo = (m - hi.astype(np.float64)).astype(np.float32).astype(BF16)
    return np.concatenate([hi, lo, hi], axis=-1)


def _mm3(tbl3, x):
    return jnp.dot(tbl3, _rhs3(x), preferred_element_type=F32)


def _spectrum_kernel(k_ref, t1_ref, f2_ref, o_ref, a_ref, *, n1, n2):
    def stage1(j, carry):
        rows = k_ref[0, pl.ds(j, n1, stride=n2), :]
        a_ref[pl.ds(j, 2 * n1, stride=n2 + FFT_PLANE_PAD), :] = _mm3(t1_ref[j], rows)
        return carry
    lax.fori_loop(0, n2, stage1, 0, unroll=FFT_UNROLL)

    def stage2(i, carry):
        pitch = n2 + FFT_PLANE_PAD
        re = a_ref[pl.ds(pl.multiple_of(i * pitch, 8), n2), :]
        im = a_ref[pl.ds(pl.multiple_of((n1 + i) * pitch, 8), n2), :]
        o_ref[0, i] = _mm3(f2_ref[...], jnp.concatenate([re, im], axis=0))
        return carry
    lax.fori_loop(0, n1, stage2, 0, unroll=2)


def _filter_spectrum(kfilt):
    g, n, w = kfilt.shape
    n1, n2 = FFT_N1, n // FFT_N1
    _, t1_real, f2_fwd, _, _ = _fft_tables(n // 2)
    const = _const_spec
    return pl.pallas_call(
        functools.partial(_spectrum_kernel, n1=n1, n2=n2),
        out_shape=jax.ShapeDtypeStruct((g, n1, 2 * n2, w), F32),
        grid=(g, w // LANES),
        in_specs=[pl.BlockSpec((1, n, LANES), lambda gi, j: (gi, 0, j)),
                  const(t1_real.shape), const(f2_fwd.shape)],
        out_specs=pl.BlockSpec((1, n1, 2 * n2, LANES), lambda gi, j: (gi, 0, 0, j)),
        scratch_shapes=[pltpu.VMEM((2 * n1 * (n2 + FFT_PLANE_PAD), LANES), F32)],
        compiler_params=_cparams("arbitrary", "arbitrary"),
        name="hyena_filter_spectrum",
    )(kfilt, t1_real, f2_fwd)


def _fftconv_kernel(u_ref, m_ref, spec_ref, bias_ref, t1_ref, f2f_ref, f2i_ref, t4_ref, o_ref, a_ref,
                    *, n1, n2):
    h1 = n1 // 2

    def stage1(j, carry):
        za = u_ref[0, pl.ds(j, h1, stride=n2), :]
        zb = u_ref[1, pl.ds(j, h1, stride=n2), :]
        a_ref[pl.ds(j, 2 * n1, stride=n2 + FFT_PLANE_PAD), :] = _mm3(t1_ref[j], jnp.concatenate([za, zb], axis=0))
        return carry
    lax.fori_loop(0, n2, stage1, 0, unroll=FFT_UNROLL)

    def stage2(i, carry):
        pitch = n2 + FFT_PLANE_PAD
        re_rows = pl.ds(pl.multiple_of(i * pitch, 8), n2)
        im_rows = pl.ds(pl.multiple_of((n1 + i) * pitch, 8), n2)
        x = _mm3(f2f_ref[...], jnp.concatenate([a_ref[re_rows, :], a_ref[im_rows, :]], axis=0))
        xr, xi = x[:n2], x[n2:]
        kr, ki = spec_ref[0, i, :n2, :], spec_ref[0, i, n2:, :]
        y = jnp.concatenate([xr * kr - xi * ki, xr * ki + xi * kr], axis=0)
        b = _mm3(f2i_ref[...], y)
        a_ref[re_rows, :] = b[:n2]
        a_ref[im_rows, :] = b[n2:]
        return carry
    lax.fori_loop(0, n1, stage2, 0, unroll=2)

    bias = bias_ref[0]

    def stage4(j, carry):
        y = _mm3(t4_ref[j], a_ref[pl.ds(j, 2 * n1, stride=n2 + FFT_PLANE_PAD), :])
        rows = pl.ds(j, h1, stride=n2)
        for p in range(2):
            u = u_ref[p, rows, :]
            o_ref[p, rows, :] = m_ref[p, rows, :] * (y[p * h1:(p + 1) * h1] + bias * u)
        return carry
    lax.fori_loop(0, n2, stage4, 0, unroll=FFT_UNROLL)


def _fftconv_gated(u, u_col, mult, mult_col, spec, conv_idx, bias, seq_len):
    bsz = u.shape[0]
    w = spec.shape[-1]
    n = 2 * seq_len
    n1, n2 = FFT_N1, n // FFT_N1
    t1_data, _, f2_fwd, f2_inv, t4 = _fft_tables(seq_len)
    const = _const_spec
    return pl.pallas_call(
        functools.partial(_fftconv_kernel, n1=n1, n2=n2),
        out_shape=jax.ShapeDtypeStruct((bsz, seq_len, w), F32),
        grid=(w // LANES, bsz // 2),
        in_specs=[pl.BlockSpec((2, seq_len, LANES), lambda j, p: (p, 0, u_col + j)),
                  pl.BlockSpec((2, seq_len, LANES), lambda j, p: (p, 0, mult_col + j)),
                  pl.BlockSpec((1, n1, 2 * n2, LANES), lambda j, p: (conv_idx, 0, 0, j),
                               pipeline_mode=pl.Buffered(1)),
                  pl.BlockSpec((1, 1, LANES), lambda j, p: (conv_idx, 0, j)),
                  const(t1_data.shape), const(f2_fwd.shape), const(f2_inv.shape), const(t4.shape)],
        out_specs=pl.BlockSpec((2, seq_len, LANES), lambda j, p: (p, 0, j)),
        scratch_shapes=[pltpu.VMEM((2 * n1 * (n2 + FFT_PLANE_PAD), LANES), F32)],
        compiler_params=_cparams("arbitrary", "arbitrary"),
        name="hyena_fftconv",
    )(u, mult, spec, bias, t1_data, f2_fwd, f2_inv, t4)


@functools.lru_cache(maxsize=None)
def _small_fft_tables(seq_len):
    n = 2 * seq_len
    f = np.exp(-2j * np.pi * np.outer(np.arange(n), np.arange(n)) / n)
    fwd = _blockreal(f[:, :seq_len])
    real = np.concatenate([f.real, f.imag], axis=0)
    inv = _blockreal(np.conj(f)[:seq_len, :] / n)
    as32 = lambda a: jnp.asarray(a, dtype=F32)
    return as32(fwd), as32(real), as32(inv)


def _ctx_hyena_kernel(v_ref, x1_ref, x2_ref, k_ref, bias_ref, fwd_ref, real_ref, inv_ref, o_ref, *, seq_len):
    n = 2 * seq_len

    def conv(ua, ub, g):
        spec = _dot(real_ref[...], k_ref[g])
        x = _dot(fwd_ref[...], jnp.concatenate([ua, ub], axis=0))
        xr, xi, kr, ki = x[:n], x[n:], spec[:n], spec[n:]
        y = _dot(inv_ref[...], jnp.concatenate([xr * kr - xi * ki, xr * ki + xi * kr], axis=0))
        b = bias_ref[g]
        return y[:seq_len] + b * ua, y[seq_len:] + b * ub

    c1a, c1b = conv(v_ref[0], v_ref[1], 0)
    y1a, y1b = x1_ref[0] * c1a, x1_ref[1] * c1b
    c2a, c2b = conv(y1a, y1b, 1)
    o_ref[0] = x2_ref[0] * c2a
    o_ref[1] = x2_ref[1] * c2b


def _ctx_hyena(hyc, row_block, kfilt, bias, seq_len, w):
    bsz = hyc.shape[0]
    n = 2 * seq_len
    fwd, real, inv = _small_fft_tables(seq_len)
    nt = w // LANES
    col = lambda c0: pl.BlockSpec((2, seq_len, LANES), lambda j, p: (p, row_block, c0 + j))
    return pl.pallas_call(
        functools.partial(_ctx_hyena_kernel, seq_len=seq_len),
        out_shape=jax.ShapeDtypeStruct((bsz, seq_len, w), F32),
        grid=(nt, bsz // 2),
        in_specs=[col(0), col(nt), col(2 * nt),
                  pl.BlockSpec((2, n, LANES), lambda j, p: (0, 0, j)),
                  pl.BlockSpec((2, 1, LANES), lambda j, p: (0, 0, j)),
                  _const_spec((2 * n, 2 * seq_len)), _const_spec((2 * n, n)), _const_spec((2 * seq_len, 2 * n))],
        out_specs=pl.BlockSpec((2, seq_len, LANES), lambda j, p: (p, 0, j)),
        compiler_params=_cparams("arbitrary", "arbitrary"),
        name="hyena_ctx",
    )(hyc, hyc, hyc, kfilt, bias, fwd, real, inv)


def _rms(x, g):
    return x * lax.rsqrt(jnp.mean(x * x, axis=-1, keepdims=True) + NORM_EPS) * g


def _inproj_kernel(x_ref, mod_ref, g_ref, w_ref, cos_ref, shi_ref, slo_ref, qg_ref, kg_ref, bdq_ref, bdk_ref,
                   hy_ref, rw_ref, gt_ref, qw_ref, kw_ref, vw_ref, qf_ref, kf_ref, vf_ref, *, widths, wq, wk):
    shift, scale = mod_ref[0, 0, 0:1, :], mod_ref[0, 0, 1:2, :]
    h = _rms(x_ref[0], g_ref[...]) * (1.0 + scale) + shift
    z = jnp.dot(h.astype(BF16), w_ref[...], preferred_element_type=F32)
    offs = np.cumsum((0,) + tuple(widths))
    part = lambda i: z[:, offs[i]:offs[i + 1]]
    hy_ref[0] = part(0)
    rw_ref[0] = part(1)
    gt_ref[0] = _silu(part(4))
    _qk_emit(part(2), part(3), cos_ref[...], shi_ref[...], slo_ref[...], qg_ref[...], kg_ref[...],
             bdq_ref[...], bdk_ref[...], qw_ref, kw_ref, vw_ref, qf_ref, kf_ref, vf_ref, wq, wk)


def _inproj(xs, mods, norm_g, w_perm, widths, n_lat_tiles, rope, q_gain, k_gain, wq, wk):
    bsz, s, d = xs.shape
    nt = s // ROW_TILE
    nq, nk = wq // HEAD_DIM, wk // HEAD_DIM
    tile = lambda wd: pl.BlockSpec((1, ROW_TILE, wd), lambda b, t: (b, t, 0))
    tab = pl.BlockSpec((ROW_TILE, HEAD_DIM), lambda b, t: (t, 0))
    hm = lambda n, wd=HEAD_DIM: pl.BlockSpec((1, n, ROW_TILE, wd), lambda b, t: (b, 0, t, 0))
    flat = lambda wd: jax.ShapeDtypeStruct((bsz, s, wd), F32)
    heads = lambda n, wd=HEAD_DIM: jax.ShapeDtypeStruct((bsz, n, s, wd), BF16)
    return pl.pallas_call(
        functools.partial(_inproj_kernel, widths=widths, wq=wq, wk=wk),
        out_shape=[flat(widths[0]), flat(widths[1]), flat(widths[4]),
                   heads(nq), heads(nk), heads(nk, LANES), heads(nq), heads(nk), heads(nk, LANES)],
        grid=(bsz, nt),
        in_specs=[tile(d),
                  pl.BlockSpec((1, 1, 3, d), lambda b, t: (b, t // n_lat_tiles, 0, 0)),
                  pl.BlockSpec((1, d), lambda b, t: (0, 0)),
                  _const_spec(w_perm.shape), tab, tab, tab,
                  pl.BlockSpec((1, wq), lambda b, t: (0, 0)), pl.BlockSpec((1, wk), lambda b, t: (0, 0)),
                  _const_spec((3 * wq, wq)), _const_spec((3 * wk, wk))],
        out_specs=[tile(widths[0]), tile(widths[1]), tile(widths[4]),
                   hm(nq), hm(nk), hm(nk, LANES), hm(nq), hm(nk), hm(nk, LANES)],
        compiler_params=_cparams("arbitrary", "arbitrary"),
        name="in_projection",
    )(xs, mods, norm_g.reshape(1, d), w_perm, *rope,
      jnp.tile(q_gain, nq).reshape(1, wq), jnp.tile(k_gain, nk).reshape(1, wk),
      _head_block_diag(wq, 1.0 / HEAD_DIM), _head_block_diag(wk, 1.0 / HEAD_DIM))


def _halo_specs(width, n_tiles):
    per = ROW_TILE // 8
    prev = pl.BlockSpec((1, 8, width), lambda b, t: (b, jnp.maximum(t * per - 1, 0), 0))
    nxt = pl.BlockSpec((1, 8, width), lambda b, t: (b, jnp.minimum((t + 1) * per, n_tiles * per - 1), 0))
    return prev, nxt


def _short_conv_tile(z, prev8, next8, w, t, n_lat_tiles, n_tiles):
    first = jnp.logical_or(t == 0, t == n_lat_tiles)
    last = jnp.logical_or(t == n_lat_tiles - 1, t == n_tiles - 1)
    above = jnp.where(first, 0.0, prev8[7:8, :])
    below = jnp.where(last, 0.0, next8[0:1, :])
    row = lax.broadcasted_iota(jnp.int32, z.shape, 0)
    zm1 = jnp.where(row == 0, above, pltpu.roll(z, 1, 0))
    zp1 = jnp.where(row == z.shape[0] - 1, below, pltpu.roll(z, z.shape[0] - 1, 0))
    return zm1 * w[0:1, :] + z * w[1:2, :] + zp1 * w[2:3, :]


def _short_conv_kernel(z_ref, p_ref, n_ref, w_ref, o_ref, *, n_lat_tiles, n_tiles):
    o_ref[0] = _short_conv_tile(z_ref[0], p_ref[0], n_ref[0], w_ref[...], pl.program_id(1),
                                n_lat_tiles, n_tiles)


def _short_conv(z, w, n_lat_tiles):
    bsz, s, width = z.shape
    nt = s // ROW_TILE
    prev, nxt = _halo_specs(width, nt)
    tile = pl.BlockSpec((1, ROW_TILE, width), lambda b, t: (b, t, 0))
    return pl.pallas_call(
        functools.partial(_short_conv_kernel, n_lat_tiles=n_lat_tiles, n_tiles=nt),
        out_shape=jax.ShapeDtypeStruct(z.shape, F32),
        grid=(bsz, nt),
        in_specs=[tile, prev, nxt, pl.BlockSpec((3, width), lambda b, t: (0, 0))],
        out_specs=tile,
        compiler_params=_cparams("arbitrary", "arbitrary"),
        name="short_conv",
    )(z, z, z, w)


def _outproj_kernel(a_lat_ref, a_ctx_ref, b_ref, c_ref, d_ref, gt_ref, x_ref, mod_ref, w_ref, fg_ref, o_ref,
                    *, n_lat_tiles, final):
    is_ctx = pl.program_id(1) >= n_lat_tiles
    a = jnp.where(is_ctx, a_ctx_ref[0], a_lat_ref[0])
    mix = jnp.concatenate([a, b_ref[0], c_ref[0], d_ref[0]], axis=-1) * gt_ref[0]
    y = jnp.dot(mix.astype(BF16), w_ref[...], preferred_element_type=F32)
    x = x_ref[0] + mod_ref[0, 0, 2:3, :] * y
    o_ref[0] = _rms(x, fg_ref[...]) if final else x


def _outproj(a_lat, a_ctx, bmix, cmix, dmix, gates, xs, mods, w_out, final_g, n_lat_tiles, final):
    bsz, s, d = xs.shape
    wb = a_lat.shape[-1]
    nt = n_lat_tiles if final else s // ROW_TILE
    tile = lambda wd: pl.BlockSpec((1, ROW_TILE, wd), lambda b, t: (b, t, 0))
    return pl.pallas_call(
        functools.partial(_outproj_kernel, n_lat_tiles=n_lat_tiles, final=final),
        out_shape=jax.ShapeDtypeStruct((bsz, nt * ROW_TILE, d), F32),
        grid=(bsz, nt),
        in_specs=[pl.BlockSpec((1, ROW_TILE, wb), lambda b, t: (b, jnp.minimum(t, n_lat_tiles - 1), 0)),
                  pl.BlockSpec((1, ROW_TILE, wb), lambda b, t: (b, 0, 0)),
                  tile(wb), tile(wb), tile(wb), tile(4 * wb), tile(d),
                  pl.BlockSpec((1, 1, 3, d), lambda b, t: (b, t // n_lat_tiles, 0, 0)),
                  _const_spec(w_out.shape),
                  pl.BlockSpec((1, d), lambda b, t: (0, 0))],
        out_specs=tile(d),
        compiler_params=_cparams("arbitrary", "arbitrary"),
        name="out_projection",
    )(a_lat, a_ctx, bmix, cmix, dmix, gates, xs, mods, w_out, final_g.reshape(1, d))


def _rope_tables(seq_len, ctx_len):
    rows = seq_len // GRID_W
    row = jnp.repeat(jnp.arange(rows, dtype=F32), GRID_W)
    col = jnp.tile(jnp.arange(GRID_W, dtype=F32), rows)
    n_freq = HEAD_DIM // 4
    inv_freq = ROPE_THETA ** (-jnp.arange(n_freq, dtype=F32) / n_freq)
    ar, ac = row[:, None] * inv_freq, col[:, None] * inv_freq
    zero = jnp.zeros_like(ar)
    cos = jnp.concatenate([jnp.cos(ar), jnp.cos(ar), jnp.cos(ac), jnp.cos(ac)], axis=-1)
    sin_hi = jnp.concatenate([-jnp.sin(ar), zero, -jnp.sin(ac), zero], axis=-1)
    sin_lo = jnp.concatenate([zero, jnp.sin(ar), zero, jnp.sin(ac)], axis=-1)
    pad = lambda t, v: jnp.concatenate([t, jnp.full((ctx_len, HEAD_DIM), v, F32)], axis=0)
    return pad(cos, 1.0), pad(sin_hi, 0.0), pad(sin_lo, 0.0)


def _rope(x, cos, sin_hi, sin_lo):
    q = HEAD_DIM // 4
    w = x.shape[-1]
    return x * cos + pltpu.roll(x, w - q, 1) * sin_hi + pltpu.roll(x, q, 1) * sin_lo


def _head_mean_sq(x, bd3):
    return _mm_exact_rhs(x * x, bd3)


def _qk_emit(wa, fa, cos, shi, slo, q_gain, k_gain, bdq, bdk, qw_ref, kw_ref, vw_ref, qf_ref, kf_ref, vf_ref,
             wq, wk):
    nq, nk = wq // HEAD_DIM, wk // HEAD_DIM
    tab = lambda t, n: jnp.concatenate([t] * n, axis=-1)
    cq, hq, lq = tab(cos, nq), tab(shi, nq), tab(slo, nq)
    ck, hk, lk = tab(cos, nk), tab(shi, nk), tab(slo, nk)
    scale = HEAD_DIM ** -0.5

    def emit(ref, val, n):
        for h in range(n):
            ref[0, h] = val[:, h * HEAD_DIM:(h + 1) * HEAD_DIM].astype(ref.dtype)

    def emit_values(ref, v):
        lane = lax.broadcasted_iota(jnp.int32, (v.shape[0], LANES - HEAD_DIM), 1)
        ones_pad = jnp.where(lane == 0, 1.0, 0.0)
        for h in range(nk):
            ref[0, h] = jnp.concatenate([v[:, h * HEAD_DIM:(h + 1) * HEAD_DIM], ones_pad],
                                        axis=-1).astype(ref.dtype)

    emit(qw_ref, _rope(wa[:, :wq], cq, hq, lq) * scale, nq)
    emit(kw_ref, _rope(wa[:, wq:wq + wk], ck, hk, lk), nk)
    emit_values(vw_ref, wa[:, wq + wk:])
    q, k = fa[:, :wq], fa[:, wq:wq + wk]
    q = q * lax.rsqrt(_head_mean_sq(q, bdq) + NORM_EPS) * q_gain
    k = k * lax.rsqrt(_head_mean_sq(k, bdk) + NORM_EPS) * k_gain
    emit(qf_ref, _rope(q, cq, hq, lq) * scale, nq)
    emit(kf_ref, _rope(k, ck, hk, lk), nk)
    emit_values(vf_ref, fa[:, wq + wk:])


def _head_block_diag(width, value):
    h = np.arange(width) // HEAD_DIM
    bd = ((h[:, None] == h[None, :]) * value).astype(np.float32).astype(BF16)
    assert np.all(bd.astype(np.float32) == (h[:, None] == h[None, :]) * value)
    return np.concatenate([bd, bd, bd], axis=0)


def _window_attn_kernel(q_ref, k_ref, v_ref, sink_ref, o_ref, *, seq_len, ctx_len, tq):
    t = pl.program_id(2)
    n_lat = seq_len // tq
    g = q_ref.shape[1]
    sub = WINDOW
    band = 2 * WINDOW + sub
    kc, vc = k_ref[0, 0, seq_len:seq_len + ctx_len, :], v_ref[0, 0, seq_len:seq_len + ctx_len, :]
    nt_dot = lambda a, b: lax.dot_general(a, b, (((1,), (1,)), ((), ())), preferred_element_type=F32)
    for j in range(tq // sub):
        q = q_ref[0, :, j * sub:(j + 1) * sub, :].reshape(g * sub, HEAD_DIM)
        first = t * tq + j * sub
        start = pl.multiple_of(jnp.clip(first - WINDOW, 0, seq_len - band), WINDOW)
        kb, vb = k_ref[0, 0, pl.ds(start, band), :], v_ref[0, 0, pl.ds(start, band), :]
        s_ctx = nt_dot(q, kc)
        s_loc = nt_dot(q, kb)
        qpos = first + lax.broadcasted_iota(jnp.int32, (g, sub, band), 1).reshape(g * sub, band)
        kpos = start + lax.broadcasted_iota(jnp.int32, (g * sub, band), 1)
        valid = jnp.logical_and(jnp.abs(qpos - kpos) <= WINDOW, t < n_lat)
        s_loc = jnp.where(valid, s_loc, NEG_INF)
        sink = sink_ref[0, :, j * sub:(j + 1) * sub, :].reshape(g * sub, 1)
        m = jnp.maximum(jnp.maximum(jnp.max(s_ctx, axis=-1, keepdims=True),
                                    jnp.max(s_loc, axis=-1, keepdims=True)), sink)
        p_ctx, p_loc = jnp.exp(s_ctx - m), jnp.exp(s_loc - m)
        acc = (jnp.dot(p_ctx.astype(BF16), vc, preferred_element_type=F32)
               + jnp.dot(p_loc.astype(BF16), vb, preferred_element_type=F32))
        denom = acc[:, HEAD_DIM:HEAD_DIM + 1] + jnp.exp(sink - m)
        out = acc[:, :HEAD_DIM] / denom
        o_ref[0, j * sub:(j + 1) * sub, :] = jnp.concatenate(
            [out[h * sub:(h + 1) * sub] for h in range(g)], axis=-1)


def _window_attention(q, k, v, sink, seq_len, ctx_len):
    bsz, nq, s, _ = q.shape
    nkv = k.shape[1]
    g = nq // nkv
    tq = ROW_TILE
    sink_rows = jnp.broadcast_to(sink.astype(F32).reshape(nkv, g, 1, 1), (nkv, g, tq, 1))
    kv = lambda a: pl.BlockSpec((1, 1, s, a.shape[-1]), lambda b, h, t: (b, h, 0, 0))
    return pl.pallas_call(
        functools.partial(_window_attn_kernel, seq_len=seq_len, ctx_len=ctx_len, tq=tq),
        out_shape=jax.ShapeDtypeStruct((bsz, s, nq * HEAD_DIM), F32),
        grid=(bsz, nkv, s // tq),
        in_specs=[pl.BlockSpec((1, g, tq, HEAD_DIM), lambda b, h, t: (b, h, t, 0)), kv(k), kv(v),
                  pl.BlockSpec((1, g, tq, 1), lambda b, h, t: (h, 0, 0, 0))],
        out_specs=pl.BlockSpec((1, tq, g * HEAD_DIM), lambda b, h, t: (b, t, h)),
        compiler_params=_cparams("arbitrary", "arbitrary", "arbitrary"),
        name="window_attention",
    )(q, k, v, sink_rows)


DENSE_KEY_BLOCK = 1024


def _dense_attn_kernel(q_ref, k_ref, v_ref, o_ref, *, seq_len, tq, tk):
    t = pl.program_id(2)
    n_lat = seq_len // tq
    g = q_ref.shape[1]
    ctx_len = k_ref.shape[2] - seq_len
    q = q_ref[0].reshape(g * tq, HEAD_DIM)

    def step(carry, start, size):
        m, acc = carry
        s = lax.dot_general(q, k_ref[0, 0, start:start + size, :], (((1,), (1,)), ((), ())),
                            preferred_element_type=F32)
        m_new = jnp.maximum(m, jnp.max(s, axis=-1, keepdims=True))
        p = jnp.exp(s - m_new).astype(BF16)
        pv = jnp.dot(p, v_ref[0, 0, start:start + size, :], preferred_element_type=F32)
        return m_new, jnp.exp(m - m_new) * acc + pv

    def finish(carry):
        _, acc = carry
        out = acc[:, :HEAD_DIM] / acc[:, HEAD_DIM:HEAD_DIM + 1]
        o_ref[0] = jnp.concatenate([out[h * tq:(h + 1) * tq] for h in range(g)], axis=-1)

    init = (jnp.full((g * tq, 1), NEG_INF, F32), jnp.zeros((g * tq, v_ref.shape[-1]), F32))

    @pl.when(t < n_lat)
    def _():
        carry = step(init, seq_len, ctx_len)
        for j in range(seq_len // tk):
            carry = step(carry, j * tk, tk)
        finish(carry)

    @pl.when(t >= n_lat)
    def _():
        finish(step(init, seq_len, ctx_len))


def _dense_attention(q, k, v, seq_len):
    bsz, nq, s, _ = q.shape
    nkv = k.shape[1]
    g = nq // nkv
    tq = ROW_TILE
    tk = math.gcd(seq_len, DENSE_KEY_BLOCK)
    kv = lambda a: pl.BlockSpec((1, 1, s, a.shape[-1]), lambda b, h, t: (b, h, 0, 0))
    return pl.pallas_call(
        functools.partial(_dense_attn_kernel, seq_len=seq_len, tq=tq, tk=tk),
        out_shape=jax.ShapeDtypeStruct((bsz, s, nq * HEAD_DIM), F32),
        grid=(bsz, nkv, s // tq),
        in_specs=[pl.BlockSpec((1, g, tq, HEAD_DIM), lambda b, h, t: (b, h, t, 0)), kv(k), kv(v)],
        out_specs=pl.BlockSpec((1, tq, g * HEAD_DIM), lambda b, h, t: (b, t, h)),
        compiler_params=_cparams("arbitrary", "arbitrary", "arbitrary"),
        name="dense_attention",
    )(q, k, v)


def _softplus(x):
    return jnp.maximum(x, 0.0) + jnp.log(1.0 + jnp.exp(-jnp.abs(x)))


def _split(a):
    bits = lax.bitcast_convert_type(a, jnp.uint32) & jnp.uint32(0xFFFF0000)
    hi = lax.bitcast_convert_type(bits, F32)
    return hi, a - hi


def _rhs3(b):
    hi, lo = _split(b)
    return jnp.concatenate([hi, hi, lo], axis=-2).astype(BF16)


def _split3(a):
    hi, rest = _split(a)
    mid, lo = _split(rest)
    return hi, mid, lo


def _mm_exact_lhs(tbl3, x):
    return jnp.dot(tbl3, jnp.concatenate(_split3(x), axis=0).astype(BF16), preferred_element_type=F32)


def _mm_exact_rhs(x, tbl3):
    return jnp.dot(jnp.concatenate(_split3(x), axis=-1).astype(BF16), tbl3, preferred_element_type=F32)


def _mm_split(a, b):
    ah, al = _split(a)
    return jnp.dot(jnp.concatenate([ah, al, ah], axis=-1).astype(BF16), _rhs3(b), preferred_element_type=F32)


RW_DEPTH = 1


def _lhs4(a):
    return a.astype(BF16)


def _rhs4(b):
    return b.astype(BF16)


def _rhs4_nt(b):
    return b.astype(BF16)


def _bmm(lf, rf):
    return lax.dot_general(lf, rf, (((2,), (1,)), ((0,), (0,))), preferred_element_type=F32)


RW_SLOTS_PER_STEP = 16


def _rwkv_par_kernel(z_ref, prev_ref, next_ref, cw_ref, w0_ref, wup_ref, a0_ref, aup_ref, kk_ref, ka_ref,
                     bd_ref, cum_ref,
                     rp3_ref, yvq_ref, rvk_ref,
                     lbig_ref, rbig_ref, at_ref, rt_ref, vv3_ref, bct_ref, kct_ref, wt_ref,
                     *, n_lat_tiles, n_tiles, w, lora):
    tc = RW_CHUNK
    nc, nh = ROW_TILE // tc, w // HEAD_DIM
    nb = RW_SLOTS_PER_STEP
    z = _short_conv_tile(z_ref[0], prev_ref[0], next_ref[0], cw_ref[...], pl.program_id(1),
                         n_lat_tiles, n_tiles)
    r, k, v = z[:, :w], z[:, w:2 * w], z[:, 2 * w:3 * w]
    w_low = jnp.tanh(z[:, 3 * w:3 * w + lora])
    a_low = z[:, 3 * w + lora:]
    kk = k * kk_ref[...]
    kk = kk / jnp.maximum(jnp.sqrt(_mm_exact_rhs(kk * kk, bd_ref[...])), 1e-12)
    ksum = jnp.zeros_like(k)
    for d in range(2):
        w_log = -_softplus(-(w0_ref[d:d + 1, :] + _mm_split(w_low, wup_ref[d]))) - 0.5
        lw = -jnp.exp(w_log)
        a = 1.0 / (1.0 + jnp.exp(-(a0_ref[d:d + 1, :] + _mm_split(a_low, aup_ref[d]))))
        kd = k * (1.0 + (a - 1.0) * ka_ref[...])
        ksum = ksum + kd
        sums = _mm_exact_lhs(cum_ref[d], lw)
        c, ctot = sums[:ROW_TILE], sums[ROW_TILE:]
        e_neg, e_rem = jnp.exp(-c), jnp.exp(ctot - c)
        b = kk * a
        at, rt = -kk * jnp.exp(c - lw), r * jnp.exp(c)
        w_tot = jnp.exp(ctot)
        bh, kh = b * e_neg, kd * e_neg
        bct, kct = (b * e_rem).T, (kd * e_rem).T
        for ci in range(nc):
            for h in range(nh):
                slot = (d * nc + ci) * nh + h
                rows, cols = slice(ci * tc, (ci + 1) * tc), slice(h * HEAD_DIM, (h + 1) * HEAD_DIM)
                lbig_ref[slot, :tc, :] = _lhs4(at[rows, cols])
                lbig_ref[slot, tc:, :] = _lhs4(rt[rows, cols])
                rbig_ref[slot, :tc, :] = _rhs4_nt(bh[rows, cols])
                rbig_ref[slot, tc:, :] = _rhs4_nt(kh[rows, cols])
                at_ref[slot] = at[rows, cols]
                rt_ref[slot] = rt[rows, cols]
                vv3_ref[slot] = _rhs4(v[rows, cols])
                bct_ref[slot] = _lhs4(bct[cols, rows])
                kct_ref[slot] = _lhs4(kct[cols, rows])
                wt_ref[slot] = w_tot[ci * tc:ci * tc + 8, cols]
    rvk_ref[0] = jnp.concatenate([r, v, ksum], axis=-1)

    row = lax.broadcasted_iota(jnp.int32, (tc, tc), 0)
    col = lax.broadcasted_iota(jnp.int32, (tc, tc), 1)
    eye = row == col

    def level_mask(s):
        return jnp.logical_and(row // (2 * s) == col // (2 * s), row // s != col // s)

    steps_per_dir = nc * nh // nb

    def chunk_group(g, carry):
        slots = pl.ds(pl.multiple_of(g * nb, nb), nb)
        d = g // steps_per_dir
        fwd = d == 0
        before = jnp.logical_or(jnp.logical_and(fwd, row > col),
                                jnp.logical_and(jnp.logical_not(fwd), row < col))
        upto = jnp.logical_or(before, eye)
        big = lax.dot_general(lbig_ref[slots], rbig_ref[slots], (((2,), (2,)), ((0,), (0,))),
                              preferred_element_type=F32)
        a_ab = jnp.where(before, big[:, :tc, :tc], 0.0)
        a_ak = jnp.where(before, big[:, :tc, tc:], 0.0)
        a_rb = jnp.where(upto, big[:, tc:, :tc], 0.0)
        a_rk = jnp.where(upto, big[:, tc:, tc:], 0.0)
        x = jnp.where(eye, 1.0, jnp.where(level_mask(1), a_ab, 0.0))
        s = 2
        while s < tc:
            half = _bmm(_lhs4(x), _rhs4(jnp.where(level_mask(s), a_ab, 0.0)))
            x = x + _bmm(_lhs4(half), _rhs4(x))
            s *= 2
        vv3 = vv3_ref[slots]
        akrk = _bmm(jnp.concatenate([_lhs4(a_ak), _lhs4(a_rk)], axis=1), vv3)
        xa3 = _rhs4(_bmm(_lhs4(x), _rhs4(jnp.concatenate([at_ref[slots], akrk[:, :tc]], axis=-1))))
        ra = _bmm(_lhs4(a_rb), xa3)
        pq = _bmm(bct_ref[slots], xa3)
        rp = rt_ref[slots] + ra[:, :, :HEAD_DIM]
        yv = ra[:, :, HEAD_DIM:] + akrk[:, tc:]
        p = jnp.where(eye[:HEAD_DIM, :HEAD_DIM], wt_ref[slots][:, 0:1, :], 0.0) + pq[:, :, :HEAD_DIM]
        q = pq[:, :, HEAD_DIM:] + _bmm(kct_ref[slots], vv3)
        rp3 = jnp.concatenate([_lhs4(rp), _lhs4(p)], axis=1)
        yvq = jnp.concatenate([yv, q], axis=1)
        for j in range(nb):
            ci = (g % steps_per_dir) * (nb // nh) + j // nh
            rp3_ref[0, d, j % nh, ci] = rp3[j]
            yvq_ref[0, d, j % nh, ci] = yvq[j]
        return carry
    lax.fori_loop(0, 2 * steps_per_dir, chunk_group, 0, unroll=True)


def _chunk_matrices():
    t = np.arange(ROW_TILE)
    same = (t[:, None] // RW_CHUNK) == (t[None, :] // RW_CHUNK)
    tabs = []
    for run in (same & (t[None, :] <= t[:, None]), same & (t[None, :] >= t[:, None])):
        m = np.concatenate([run, same], axis=0).astype(np.float32).astype(BF16)
        tabs.append(np.concatenate([m, m, m], axis=1))
    return np.stack(tabs)


def _rwkv_par(rw, conv_w, w0, w_up, a0, a_up, k_k, k_a, w, n_lat_tiles):
    bsz, s, width = rw.shape
    nt = s // ROW_TILE
    nh, nc = w // HEAD_DIM, ROW_TILE // RW_CHUNK
    lora = w_up.shape[1]
    prev, nxt = _halo_specs(width, nt)
    full = lambda a: pl.BlockSpec(a.shape, lambda b, t: (0,) * a.ndim)
    cum = _chunk_matrices()
    bd = _head_block_diag(w, 1.0)
    vec = lambda a: a.reshape(1, w)
    n_chunks = s // RW_CHUNK
    nslots = 2 * nc * nh
    tc, hd = RW_CHUNK, HEAD_DIM
    consts = (conv_w, w0, w_up, a0, a_up, vec(k_k), vec(k_a), bd, cum)
    return pl.pallas_call(
        functools.partial(_rwkv_par_kernel, n_lat_tiles=n_lat_tiles, n_tiles=nt, w=w, lora=lora),
        out_shape=[jax.ShapeDtypeStruct((bsz, 2, nh, n_chunks, tc + hd, RW_DEPTH * hd), BF16),
                   jax.ShapeDtypeStruct((bsz, 2, nh, n_chunks, tc + hd, hd), F32),
                   jax.ShapeDtypeStruct((bsz, s, 3 * w), F32)],
        grid=(bsz, nt),
        in_specs=[pl.BlockSpec((1, ROW_TILE, width), lambda b, t: (b, t, 0)), prev, nxt]
                 + [full(a) for a in consts],
        out_specs=[pl.BlockSpec((1, 2, nh, nc, tc + hd, RW_DEPTH * hd), lambda b, t: (b, 0, 0, t, 0, 0)),
                   pl.BlockSpec((1, 2, nh, nc, tc + hd, hd), lambda b, t: (b, 0, 0, t, 0, 0)),
                   pl.BlockSpec((1, ROW_TILE, 3 * w), lambda b, t: (b, t, 0))],
        scratch_shapes=[pltpu.VMEM((nslots, 2 * tc, RW_DEPTH * hd), BF16),
                        pltpu.VMEM((nslots, 2 * tc, RW_DEPTH * hd), BF16),
                        pltpu.VMEM((nslots, tc, hd), F32),
                        pltpu.VMEM((nslots, tc, hd), F32),
                        pltpu.VMEM((nslots, RW_DEPTH * tc, hd), BF16),
                        pltpu.VMEM((nslots, hd, RW_DEPTH * tc), BF16),
                        pltpu.VMEM((nslots, hd, RW_DEPTH * tc), BF16),
                        pltpu.VMEM((nslots, 8, HEAD_DIM), F32)],
        compiler_params=_cparams("arbitrary", "arbitrary"),
        name="rwkv_chunk_prep",
    )(rw, rw, rw, *consts)


def _rwkv_seq_kernel(rp3f_ref, yvqf_ref, rp3b_ref, yvqb_ref, yf_ref, yb_ref, g_ref):
    @pl.when(pl.program_id(1) == 0)
    def _():
        g_ref[...] = jnp.zeros_like(g_ref)

    nh = g_ref.shape[1]
    tc = RW_CHUNK
    grp = rp3f_ref.shape[3]
    for step in range(grp):
        for d, (rp3, yvq, y) in enumerate(((rp3f_ref, yvqf_ref, yf_ref), (rp3b_ref, yvqb_ref, yb_ref))):
            ci = step if d == 0 else grp - 1 - step
            for h in range(nh):
                out = (jnp.dot(rp3[0, 0, h, ci], _rhs4(g_ref[d, h]), preferred_element_type=F32)
                       + yvq[0, 0, h, ci])
                y[0, h, ci * tc:(ci + 1) * tc, :] = out[:tc]
                g_ref[d, h] = out[tc:]


RW_SCAN_GROUP = 4


def _rwkv_seq(rp3, yvq, n_lat_chunks):
    bsz, _, nh, n_chunks = rp3.shape[:4]
    grp = RW_SCAN_GROUP
    assert n_lat_chunks % grp == 0 and n_chunks % grp == 0
    n_groups, n_lat, n_ctx = n_chunks // grp, n_lat_chunks // grp, (n_chunks - n_lat_chunks) // grp
    order = (lambda i: jnp.where(i < n_ctx, n_lat + i, i - n_ctx),
             lambda i: n_groups - 1 - i)
    blk = lambda d, a: pl.BlockSpec((1, 1, nh, grp) + a.shape[4:], lambda b, i: (b, d, 0, order[d](i), 0, 0))
    out = lambda d: pl.BlockSpec((1, nh, grp * RW_CHUNK, HEAD_DIM), lambda b, i: (b, 0, order[d](i), 0))
    shp = jax.ShapeDtypeStruct((bsz, nh, n_chunks * RW_CHUNK, HEAD_DIM), F32)
    return pl.pallas_call(
        _rwkv_seq_kernel,
        out_shape=[shp, shp],
        grid=(bsz, n_groups),
        in_specs=[blk(0, rp3), blk(0, yvq), blk(1, rp3), blk(1, yvq)],
        out_specs=[out(0), out(1)],
        scratch_shapes=[pltpu.VMEM((2, nh, HEAD_DIM, HEAD_DIM), F32)],
        compiler_params=_cparams("arbitrary", "arbitrary"),
        name="rwkv_state_scan",
    )(rp3, yvq, rp3, yvq)


def _rwkv_out_kernel(yf_ref, yb_ref, rvk_ref, rk_ref, g_ref, b_ref, o_ref, *, w):
    nh = w // HEAD_DIM
    rvk = rvk_ref[0]
    r, v, ksum = rvk[:, :w], rvk[:, w:2 * w], rvk[:, 2 * w:]
    rkk = r * ksum * rk_ref[...]
    normed, bonus = [], []
    for h in range(nh):
        cols = slice(h * HEAD_DIM, (h + 1) * HEAD_DIM)
        y = yf_ref[0, h] + yb_ref[0, h]
        mu = jnp.mean(y, axis=-1, keepdims=True)
        var = jnp.mean(jnp.square(y - mu), axis=-1, keepdims=True)
        normed.append((y - mu) * lax.rsqrt(var + RW_GN_EPS))
        bonus.append(jnp.sum(rkk[:, cols], axis=-1, keepdims=True) * v[:, cols])
    o_ref[0] = (jnp.concatenate(normed, axis=-1) * g_ref[...] + b_ref[...]
                + jnp.concatenate(bonus, axis=-1))


def _rwkv_out(yf, yb, rvk, r_k, ln_g, ln_b, w):
    bsz, nh, s, _ = yf.shape
    hm = pl.BlockSpec((1, nh, ROW_TILE, HEAD_DIM), lambda b, t: (b, 0, t, 0))
    vec = pl.BlockSpec((1, w), lambda b, t: (0, 0))
    return pl.pallas_call(
        functools.partial(_rwkv_out_kernel, w=w),
        out_shape=jax.ShapeDtypeStruct((bsz, s, w), F32),
        grid=(bsz, s // ROW_TILE),
        in_specs=[hm, hm, pl.BlockSpec((1, ROW_TILE, 3 * w), lambda b, t: (b, t, 0)), vec, vec, vec],
        out_specs=pl.BlockSpec((1, ROW_TILE, w), lambda b, t: (b, t, 0)),
        compiler_params=_cparams("arbitrary", "arbitrary"),
        name="rwkv_readout",
    )(yf, yb, rvk, r_k.reshape(1, w), ln_g.reshape(1, w), ln_b.reshape(1, w))


def _hyena_two_sided_filters(seq_len, fw1, fb1, freq, fw2, fb2, fw3, width):
    bands = (fw1.shape[0] - 1) // 2
    mm = functools.partial(jnp.matmul, precision=HI)
    n = jnp.arange(2 * seq_len)
    pos = jnp.where(n < seq_len, n, 2 * seq_len - n) % seq_len
    is_fwd = (n < seq_len)[:, None, None]
    live = (n != seq_len)[:, None, None]
    t = jnp.linspace(0.0, 1.0, seq_len, dtype=F32)[pos][:, None]
    wpos = (2.0 * math.pi / seq_len) * pos.astype(F32)[:, None]
    f = jnp.linspace(1e-4, bands - 1, bands, dtype=F32)[None, :]
    z = jnp.concatenate([t, jnp.cos(f * wpos), jnp.sin(f * wpos)], axis=-1)
    h = jnp.sin(freq * (mm(z, fw1) + fb1))
    h = jnp.sin(freq * (mm(h, fw2) + fb2))
    h = mm(h, fw3).reshape(2 * seq_len, 2, 2, width)
    max_decay = math.log(HY_DECAY_TARGET) / HY_FAST_DECAY
    min_decay = math.log(HY_DECAY_TARGET) / HY_SLOW_DECAY
    deltas = jnp.linspace(min_decay, max_decay, width, dtype=F32)
    h = jnp.where(is_fwd, h[:, :, 0], h[:, :, 1]) * jnp.exp(-t * jnp.abs(deltas))[:, None, :]
    norm = jnp.sum(jnp.abs(h), axis=0, keepdims=True)
    return jnp.moveaxis(jnp.where(live, h / norm, 0.0), 1, 0)


def kernel(x, c, ctx, c_ctx, mod_w, mod_b, norm_g, w_in, w_out, hy_conv, hy_fw1, hy_fb1, hy_freq, hy_fw2,
           hy_fb2, hy_fw3, hy_bias, rw_conv, rw_w0, rw_w_up, rw_a0, rw_a_up, rw_k_k, rw_k_a, rw_r_k,
           rw_ln_g, rw_ln_b, wa_sink, fa_q_norm, fa_k_norm, final_g):
    bsz, seq_len, d = x.shape
    ctx_len = ctx.shape[1]
    depth = w_in.shape[0]
    w_hy = hy_bias.shape[-1]
    w_rw = rw_w0.shape[-1]
    n_wa_heads = wa_sink.shape[-1]
    w_q = n_wa_heads * HEAD_DIM
    w_kv = w_q // 2
    lora = rw_w_up.shape[2] + rw_a_up.shape[2]
    branch_w = (3 * w_hy, 3 * w_rw + lora, w_q + 2 * w_kv, w_q + 2 * w_kv)
    gate_w = (w_hy, w_rw, w_q, w_q)
    assert seq_len % ROW_TILE == 0 and ctx_len % ROW_TILE == 0 and bsz % 2 == 0
    n_lat_tiles = seq_len // ROW_TILE

    starts = np.cumsum([0] + [bw + gw for bw, gw in zip(branch_w, gate_w)])
    cols = np.concatenate([np.arange(s0, s0 + bw) for s0, bw in zip(starts, branch_w)]
                          + [np.arange(s0 + bw, s0 + bw + gw) for s0, bw, gw in zip(starts, branch_w, gate_w)])
    w_in_p = w_in[:, :, cols].astype(BF16)
    w_out_b = w_out.astype(BF16)
    widths = branch_w + (sum(gate_w),)

    pad_rows = (-(bsz + 1)) % 8
    cond = jnp.concatenate([c, c_ctx[None], jnp.zeros((pad_rows, d), F32)], axis=0)
    mod = _modulation(cond, mod_w, mod_b)
    mod_lat = mod[:, :bsz].reshape(depth, bsz, 3, d)
    mod_ctx = jnp.broadcast_to(mod[:, bsz].reshape(depth, 1, 3, d), (depth, bsz, 3, d))
    mods = jnp.stack([mod_lat, mod_ctx], axis=2)

    rope = _rope_tables(seq_len, ctx_len)
    xs = jnp.concatenate([x, ctx], axis=1)
    for l in range(depth):
        last = l == depth - 1
        hy, rw, gates, qw, kw, vw, qf, kf, vf = _inproj(xs, mods[l], norm_g[l], w_in_p[l], widths, n_lat_tiles,
                                                        rope, fa_q_norm[l], fa_k_norm[l], w_q, w_kv)

        hyc = _short_conv(hy, hy_conv[l], n_lat_tiles)
        filt = functools.partial(_hyena_two_sided_filters, fw1=hy_fw1[l], fb1=hy_fb1[l], freq=hy_freq[l],
                                 fw2=hy_fw2[l], fb2=hy_fb2[l], fw3=hy_fw3[l], width=w_hy)
        bias = hy_bias[l].reshape(2, 1, w_hy)
        spec = _filter_spectrum(filt(seq_len))
        nt_hy = w_hy // LANES
        y1 = _fftconv_gated(hyc, 0, hyc, nt_hy, spec, 0, bias, seq_len)
        a_lat = _fftconv_gated(y1, 0, hyc, 2 * nt_hy, spec, 1, bias, seq_len)
        a_ctx = a_lat if last else _ctx_hyena(hyc, seq_len // ctx_len, filt(ctx_len), bias, ctx_len, w_hy)

        rp3, yvq, rvk = _rwkv_par(rw, rw_conv[l], rw_w0[l], rw_w_up[l], rw_a0[l], rw_a_up[l],
                                  rw_k_k[l], rw_k_a[l], w_rw, n_lat_tiles)
        yf, yb = _rwkv_seq(rp3, yvq, seq_len // RW_CHUNK)
        b_mix = _rwkv_out(yf, yb, rvk, rw_r_k[l], rw_ln_g[l], rw_ln_b[l], w_rw)

        c_mix = _window_attention(qw, kw, vw, wa_sink[l], seq_len, ctx_len)
        d_mix = _dense_attention(qf, kf, vf, seq_len)

        xs = _outproj(a_lat, a_ctx, b_mix, c_mix, d_mix, gates, xs, mods[l], w_out_b[l], final_g,
                      n_lat_tiles, last)
    return xs
```

```python
import functools
import math

import jax
import jax.numpy as jnp
import numpy as np
from jax import lax
from jax.experimental import pallas as pl
from jax.experimental.pallas import tpu as pltpu

HEAD_DIM = 64
GRID_W = 64
WINDOW = 128
NORM_EPS = 1e-6
RW_GN_EPS = 64e-5
NEG_INF = -1e30
ROPE_THETA = 10000.0
HY_FAST_DECAY = 0.3
HY_SLOW_DECAY = 1.5
HY_DECAY_TARGET = 1e-2

ROW_TILE = 256
LANES = 128
SUBLANES = 8
FFT_N1 = 64
FFT_UNROLL = 8
FFT_PLANE_PAD = 8
RW_CHUNK = 64
VMEM_LIMIT = 60 * 1024 * 1024

F32 = jnp.float32
BF16 = jnp.bfloat16
HI = lax.Precision.HIGHEST


def _dot(a, b):
    return jnp.dot(a, b, preferred_element_type=F32, precision=HI)


def _dot_nt(a, b):
    return lax.dot_general(a, b, (((1,), (1,)), ((), ())), preferred_element_type=F32, precision=HI)


def _dot_tn(a, b):
    return lax.dot_general(a, b, (((0,), (0,)), ((), ())), preferred_element_type=F32, precision=HI)


def _cparams(*sem):
    return pltpu.CompilerParams(dimension_semantics=sem, vmem_limit_bytes=VMEM_LIMIT)


def _const_spec(shape):
    return pl.BlockSpec(shape, lambda *_: (0,) * len(shape), pipeline_mode=pl.Buffered(1))


def _silu(x):
    return x * (1.0 / (1.0 + jnp.exp(-x)))


def _mod_kernel(c_ref, w_ref, b_ref, o_ref):
    o_ref[0] = _dot(_silu(c_ref[...]), w_ref[0]) + b_ref[0]


def _modulation(cond, mod_w, mod_b):
    depth, d, d3 = mod_w.shape
    rows = cond.shape[0]
    return pl.pallas_call(
        _mod_kernel,
        out_shape=jax.ShapeDtypeStruct((depth, rows, d3), F32),
        grid=(depth,),
        in_specs=[pl.BlockSpec((rows, d), lambda l: (0, 0)),
                  pl.BlockSpec((1, d, d3), lambda l: (l, 0, 0)),
                  pl.BlockSpec((1, 1, d3), lambda l: (l, 0, 0))],
        out_specs=pl.BlockSpec((1, rows, d3), lambda l: (l, 0, 0)),
        compiler_params=_cparams("arbitrary"),
        name="modulation",
    )(cond, mod_w, mod_b.reshape(depth, 1, d3))


def _blockreal(m):
    return np.block([[m.real, -m.imag], [m.imag, m.real]])


@functools.lru_cache(maxsize=None)
def _fft_tables(seq_len):
    n = 2 * seq_len
    n1, n2 = FFT_N1, n // FFT_N1
    h1 = n1 // 2
    j2 = np.arange(n2)[:, None, None]
    k1 = np.arange(n1)[None, :, None]
    t1 = np.exp(-2j * np.pi * (j2 * k1 / n + k1 * np.arange(n1)[None, None, :] / n1))
    t1_data = np.stack([_blockreal(t1[j][:, :h1]) for j in range(n2)])
    t1_real = np.concatenate([t1.real, t1.imag], axis=1)
    f2 = np.exp(-2j * np.pi * np.outer(np.arange(n2), np.arange(n2)) / n2)
    f2_fwd = _blockreal(f2)
    f2_inv = _blockreal(np.conj(f2))
    t4 = np.exp(2j * np.pi * (np.arange(h1)[None, :, None] * np.arange(n1)[None, None, :] / n1
                              + j2 * np.arange(n1)[None, None, :] / n)) / n
    t4 = np.stack([_blockreal(t4[j]) for j in range(n2)])
    return tuple(_lhs3_table(t) for t in (t1_data, t1_real, f2_fwd, f2_inv, t4))


def _lhs3_table(m):
    hi = m.astype(np.float32).astype(BF16)
    lo = (m - hi.astype(np.float64)).astype(np.float32).astype(BF16)
    return np.concatenate([hi, lo, hi], axis=-1)


def _mm3(tbl3, x):
    return jnp.dot(tbl3, _rhs3(x), preferred_element_type=F32)


def _spectrum_kernel(k_ref, t1_ref, f2_ref, o_ref, a_ref, *, n1, n2):
    def stage1(j, carry):
        rows = k_ref[0, pl.ds(j, n1, stride=n2), :]
        a_ref[pl.ds(j, 2 * n1, stride=n2 + FFT_PLANE_PAD), :] = _mm3(t1_ref[j], rows)
        return carry
    lax.fori_loop(0, n2, stage1, 0, unroll=FFT_UNROLL)

    def stage2(i, carry):
        pitch = n2 + FFT_PLANE_PAD
        re = a_ref[pl.ds(pl.multiple_of(i * pitch, 8), n2), :]
        im = a_ref[pl.ds(pl.multiple_of((n1 + i) * pitch, 8), n2), :]
        o_ref[0, i] = _mm3(f2_ref[...], jnp.concatenate([re, im], axis=0))
        return carry
    lax.fori_loop(0, n1, stage2, 0, unroll=2)


def _filter_spectrum(kfilt):
    g, n, w = kfilt.shape
    n1, n2 = FFT_N1, n // FFT_N1
    _, t1_real, f2_fwd, _, _ = _fft_tables(n // 2)
    const = _const_spec
    return pl.pallas_call(
        functools.partial(_spectrum_kernel, n1=n1, n2=n2),
        out_shape=jax.ShapeDtypeStruct((g, n1, 2 * n2, w), F32),
        grid=(g, w // LANES),
        in_specs=[pl.BlockSpec((1, n, LANES), lambda gi, j: (gi, 0, j)),
                  const(t1_real.shape), const(f2_fwd.shape)],
        out_specs=pl.BlockSpec((1, n1, 2 * n2, LANES), lambda gi, j: (gi, 0, 0, j)),
        scratch_shapes=[pltpu.VMEM((2 * n1 * (n2 + FFT_PLANE_PAD), LANES), F32)],
        compiler_params=_cparams("arbitrary", "arbitrary"),
        name="hyena_filter_spectrum",
    )(kfilt, t1_real, f2_fwd)


def _fftconv_kernel(u_ref, m_ref, spec_ref, bias_ref, t1_ref, f2f_ref, f2i_ref, t4_ref, o_ref, a_ref,
                    *, n1, n2):
    h1 = n1 // 2

    def stage1(j, carry):
        za = u_ref[0, pl.ds(j, h1, stride=n2), :]
        zb = u_ref[1, pl.ds(j, h1, stride=n2), :]
        a_ref[pl.ds(j, 2 * n1, stride=n2 + FFT_PLANE_PAD), :] = _mm3(t1_ref[j], jnp.concatenate([za, zb], axis=0))
        return carry
    lax.fori_loop(0, n2, stage1, 0, unroll=FFT_UNROLL)

    def stage2(i, carry):
        pitch = n2 + FFT_PLANE_PAD
        re_rows = pl.ds(pl.multiple_of(i * pitch, 8), n2)
        im_rows = pl.ds(pl.multiple_of((n1 + i) * pitch, 8), n2)
        x = _mm3(f2f_ref[...], jnp.concatenate([a_ref[re_rows, :], a_ref[im_rows, :]], axis=0))
        xr, xi = x[:n2], x[n2:]
        kr, ki = spec_ref[0, i, :n2, :], spec_ref[0, i, n2:, :]
        y = jnp.concatenate([xr * kr - xi * ki, xr * ki + xi * kr], axis=0)
        b = _mm3(f2i_ref[...], y)
        a_ref[re_rows, :] = b[:n2]
        a_ref[im_rows, :] = b[n2:]
        return carry
    lax.fori_loop(0, n1, stage2, 0, unroll=2)

    bias = bias_ref[0]

    def stage4(j, carry):
        y = _mm3(t4_ref[j], a_ref[pl.ds(j, 2 * n1, stride=n2 + FFT_PLANE_PAD), :])
        rows = pl.ds(j, h1, stride=n2)
        for p in range(2):
            u = u_ref[p, rows, :]
            o_ref[p, rows, :] = m_ref[p, rows, :] * (y[p * h1:(p + 1) * h1] + bias * u)
        return carry
    lax.fori_loop(0, n2, stage4, 0, unroll=FFT_UNROLL)


def _fftconv_gated(u, u_col, mult, mult_col, spec, conv_idx, bias, seq_len):
    bsz = u.shape[0]
    w = spec.shape[-1]
    n = 2 * seq_len
    n1, n2 = FFT_N1, n // FFT_N1
    t1_data, _, f2_fwd, f2_inv, t4 = _fft_tables(seq_len)
    const = _const_spec
    return pl.pallas_call(
        functools.partial(_fftconv_kernel, n1=n1, n2=n2),
        out_shape=jax.ShapeDtypeStruct((bsz, seq_len, w), F32),
        grid=(w // LANES, bsz // 2),
        in_specs=[pl.BlockSpec((2, seq_len, LANES), lambda j, p: (p, 0, u_col + j)),
                  pl.BlockSpec((2, seq_len, LANES), lambda j, p: (p, 0, mult_col + j)),
                  pl.BlockSpec((1, n1, 2 * n2, LANES), lambda j, p: (conv_idx, 0, 0, j),
                               pipeline_mode=pl.Buffered(1)),
                  pl.BlockSpec((1, 1, LANES), lambda j, p: (conv_idx, 0, j)),
                  const(t1_data.shape), const(f2_fwd.shape), const(f2_inv.shape), const(t4.shape)],
        out_specs=pl.BlockSpec((2, seq_len, LANES), lambda j, p: (p, 0, j)),
        scratch_shapes=[pltpu.VMEM((2 * n1 * (n2 + FFT_PLANE_PAD), LANES), F32)],
        compiler_params=_cparams("arbitrary", "arbitrary"),
        name="hyena_fftconv",
    )(u, mult, spec, bias, t1_data, f2_fwd, f2_inv, t4)


@functools.lru_cache(maxsize=None)
def _small_fft_tables(seq_len):
    n = 2 * seq_len
    f = np.exp(-2j * np.pi * np.outer(np.arange(n), np.arange(n)) / n)
    fwd = _blockreal(f[:, :seq_len])
    real = np.concatenate([f.real, f.imag], axis=0)
    inv = _blockreal(np.conj(f)[:seq_len, :] / n)
    as32 = lambda a: jnp.asarray(a, dtype=F32)
    return as32(fwd), as32(real), as32(inv)


def _ctx_hyena_kernel(v_ref, x1_ref, x2_ref, k_ref, bias_ref, fwd_ref, real_ref, inv_ref, o_ref, *, seq_len):
    n = 2 * seq_len

    def conv(ua, ub, g):
        spec = _dot(real_ref[...], k_ref[g])
        x = _dot(fwd_ref[...], jnp.concatenate([ua, ub], axis=0))
        xr, xi, kr, ki = x[:n], x[n:], spec[:n], spec[n:]
        y = _dot(inv_ref[...], jnp.concatenate([xr * kr - xi * ki, xr * ki + xi * kr], axis=0))
        b = bias_ref[g]
        return y[:seq_len] + b * ua, y[seq_len:] + b * ub

    c1a, c1b = conv(v_ref[0], v_ref[1], 0)
    y1a, y1b = x1_ref[0] * c1a, x1_ref[1] * c1b
    c2a, c2b = conv(y1a, y1b, 1)
    o_ref[0] = x2_ref[0] * c2a
    o_ref[1] = x2_ref[1] * c2b


def _ctx_hyena(hyc, row_block, kfilt, bias, seq_len, w):
    bsz = hyc.shape[0]
    n = 2 * seq_len
    fwd, real, inv = _small_fft_tables(seq_len)
    nt = w // LANES
    col = lambda c0: pl.BlockSpec((2, seq_len, LANES), lambda j, p: (p, row_block, c0 + j))
    return pl.pallas_call(
        functools.partial(_ctx_hyena_kernel, seq_len=seq_len),
        out_shape=jax.ShapeDtypeStruct((bsz, seq_len, w), F32),
        grid=(nt, bsz // 2),
        in_specs=[col(0), col(nt), col(2 * nt),
                  pl.BlockSpec((2, n, LANES), lambda j, p: (0, 0, j)),
                  pl.BlockSpec((2, 1, LANES), lambda j, p: (0, 0, j)),
                  _const_spec((2 * n, 2 * seq_len)), _const_spec((2 * n, n)), _const_spec((2 * seq_len, 2 * n))],
        out_specs=pl.BlockSpec((2, seq_len, LANES), lambda j, p: (p, 0, j)),
        compiler_params=_cparams("arbitrary", "arbitrary"),
        name="hyena_ctx",
    )(hyc, hyc, hyc, kfilt, bias, fwd, real, inv)


def _rms(x, g):
    return x * lax.rsqrt(jnp.mean(x * x, axis=-1, keepdims=True) + NORM_EPS) * g


def _inproj_kernel(x_ref, xprev_ref, xnext_ref, mod_ref, g_ref, w_ref, cw_ref,
                   cos_ref, shi_ref, slo_ref, qg_ref, kg_ref, bdq_ref, bdk_ref,
                   hy_ref, rw_ref, gt_ref, qw_ref, kw_ref, vw_ref, qf_ref, kf_ref, vf_ref,
                   *, widths, wq, wk, n_lat_tiles, n_tiles):
    shift, scale = mod_ref[0, 0, 0:1, :], mod_ref[0, 0, 1:2, :]
    norm_mod = lambda x: (_rms(x, g_ref[...]) * (1.0 + scale) + shift).astype(BF16)
    z = jnp.dot(norm_mod(x_ref[0]), w_ref[...], preferred_element_type=F32)
    offs = np.cumsum((0,) + tuple(widths))
    part = lambda i: z[:, offs[i]:offs[i + 1]]
    n_conv = offs[2]
    halo = jnp.dot(norm_mod(jnp.concatenate([xprev_ref[0], xnext_ref[0]], axis=0)), w_ref[:, :n_conv],
                   preferred_element_type=F32)
    conv = _short_conv_tile(z[:, :n_conv], halo[:SUBLANES], halo[SUBLANES:], cw_ref[...], pl.program_id(1),
                            n_lat_tiles, n_tiles)
    hy_ref[0] = conv[:, :offs[1]]
    rw_ref[0] = conv[:, offs[1]:]
    gt_ref[0] = _silu(part(4))
    _qk_emit(part(2), part(3), cos_ref[...], shi_ref[...], slo_ref[...], qg_ref[...], kg_ref[...],
             bdq_ref[...], bdk_ref[...], qw_ref, kw_ref, vw_ref, qf_ref, kf_ref, vf_ref, wq, wk)


def _inproj(xs, mods, norm_g, w_perm, conv_w, widths, n_lat_tiles, rope, q_gain, k_gain, wq, wk):
    bsz, s, d = xs.shape
    nt = s // ROW_TILE
    nq, nk = wq // HEAD_DIM, wk // HEAD_DIM
    prev, nxt = _halo_specs(d, nt)
    tile = lambda wd: pl.BlockSpec((1, ROW_TILE, wd), lambda b, t: (b, t, 0))
    tab = pl.BlockSpec((ROW_TILE, HEAD_DIM), lambda b, t: (t, 0))
    hm = lambda n, wd=HEAD_DIM: pl.BlockSpec((1, n, ROW_TILE, wd), lambda b, t: (b, 0, t, 0))
    flat = lambda wd: jax.ShapeDtypeStruct((bsz, s, wd), F32)
    heads = lambda n, wd=HEAD_DIM: jax.ShapeDtypeStruct((bsz, n, s, wd), BF16)
    return pl.pallas_call(
        functools.partial(_inproj_kernel, widths=widths, wq=wq, wk=wk, n_lat_tiles=n_lat_tiles, n_tiles=nt),
        out_shape=[flat(widths[0]), flat(widths[1]), flat(widths[4]),
                   heads(nq), heads(nk), heads(nk, LANES), heads(nq), heads(nk), heads(nk, LANES)],
        grid=(bsz, nt),
        in_specs=[tile(d), prev, nxt,
                  pl.BlockSpec((1, 1, 3, d), lambda b, t: (b, t // n_lat_tiles, 0, 0)),
                  pl.BlockSpec((1, d), lambda b, t: (0, 0)),
                  _const_spec(w_perm.shape), pl.BlockSpec(conv_w.shape, lambda b, t: (0, 0)), tab, tab, tab,
                  pl.BlockSpec((1, wq), lambda b, t: (0, 0)), pl.BlockSpec((1, wk), lambda b, t: (0, 0)),
                  _const_spec((3 * wq, wq)), _const_spec((3 * wk, wk))],
        out_specs=[tile(widths[0]), tile(widths[1]), tile(widths[4]),
                   hm(nq), hm(nk), hm(nk, LANES), hm(nq), hm(nk), hm(nk, LANES)],
        compiler_params=_cparams("arbitrary", "arbitrary"),
        name="in_projection",
    )(xs, xs, xs, mods, norm_g.reshape(1, d), w_perm, conv_w, *rope,
      jnp.tile(q_gain, nq).reshape(1, wq), jnp.tile(k_gain, nk).reshape(1, wk),
      _head_block_diag(wq, 1.0 / HEAD_DIM), _head_block_diag(wk, 1.0 / HEAD_DIM))


def _halo_specs(width, n_tiles):
    per = ROW_TILE // SUBLANES
    prev = pl.BlockSpec((1, SUBLANES, width), lambda b, t: (b, jnp.maximum(t * per - 1, 0), 0))
    nxt = pl.BlockSpec((1, SUBLANES, width), lambda b, t: (b, jnp.minimum((t + 1) * per, n_tiles * per - 1), 0))
    return prev, nxt


def _short_conv_tile(z, prev8, next8, w, t, n_lat_tiles, n_tiles):
    first = jnp.logical_or(t == 0, t == n_lat_tiles)
    last = jnp.logical_or(t == n_lat_tiles - 1, t == n_tiles - 1)
    above = jnp.where(first, 0.0, prev8[SUBLANES - 1:SUBLANES, :])
    below = jnp.where(last, 0.0, next8[0:1, :])
    row = lax.broadcasted_iota(jnp.int32, z.shape, 0)
    zm1 = jnp.where(row == 0, above, pltpu.roll(z, 1, 0))
    zp1 = jnp.where(row == z.shape[0] - 1, below, pltpu.roll(z, z.shape[0] - 1, 0))
    return zm1 * w[0:1, :] + z * w[1:2, :] + zp1 * w[2:3, :]


def _outproj_kernel(a_lat_ref, a_ctx_ref, yf_ref, yb_ref, rvk_ref, rk_ref, lng_ref, lnb_ref, c_ref, d_ref,
                    gt_ref, x_ref, mod_ref, w_ref, fg_ref, o_ref, *, n_lat_tiles, final):
    is_ctx = pl.program_id(1) >= n_lat_tiles
    a = jnp.where(is_ctx, a_ctx_ref[0], a_lat_ref[0])
    b = _rwkv_readout_tile(yf_ref, yb_ref, rvk_ref[0], rk_ref[...], lng_ref[...], lnb_ref[...])
    mix = jnp.concatenate([a, b, c_ref[0], d_ref[0]], axis=-1) * gt_ref[0]
    y = jnp.dot(mix.astype(BF16), w_ref[...], preferred_element_type=F32)
    x = x_ref[0] + mod_ref[0, 0, 2:3, :] * y
    o_ref[0] = _rms(x, fg_ref[...]) if final else x


def _outproj(a_lat, a_ctx, yf, yb, rvk, r_k, ln_g, ln_b, cmix, dmix, gates, xs, mods, w_out, final_g,
             n_lat_tiles, final):
    bsz, s, d = xs.shape
    wb = a_lat.shape[-1]
    nh = yf.shape[1]
    nt = n_lat_tiles if final else s // ROW_TILE
    tile = lambda wd: pl.BlockSpec((1, ROW_TILE, wd), lambda b, t: (b, t, 0))
    hm = pl.BlockSpec((1, nh, ROW_TILE, HEAD_DIM), lambda b, t: (b, 0, t, 0))
    vec = pl.BlockSpec((1, wb), lambda b, t: (0, 0))
    return pl.pallas_call(
        functools.partial(_outproj_kernel, n_lat_tiles=n_lat_tiles, final=final),
        out_shape=jax.ShapeDtypeStruct((bsz, nt * ROW_TILE, d), F32),
        grid=(bsz, nt),
        in_specs=[pl.BlockSpec((1, ROW_TILE, wb), lambda b, t: (b, jnp.minimum(t, n_lat_tiles - 1), 0)),
                  pl.BlockSpec((1, ROW_TILE, wb), lambda b, t: (b, 0, 0)),
                  hm, hm, tile(3 * wb), vec, vec, vec,
                  tile(wb), tile(wb), tile(4 * wb), tile(d),
                  pl.BlockSpec((1, 1, 3, d), lambda b, t: (b, t // n_lat_tiles, 0, 0)),
                  _const_spec(w_out.shape),
                  pl.BlockSpec((1, d), lambda b, t: (0, 0))],
        out_specs=tile(d),
        compiler_params=_cparams("arbitrary", "arbitrary"),
        name="out_projection",
    )(a_lat, a_ctx, yf, yb, rvk, r_k.reshape(1, wb), ln_g.reshape(1, wb), ln_b.reshape(1, wb),
      cmix, dmix, gates, xs, mods, w_out, final_g.reshape(1, d))


def _rope_tables(seq_len, ctx_len):
    rows = seq_len // GRID_W
    row = jnp.repeat(jnp.arange(rows, dtype=F32), GRID_W)
    col = jnp.tile(jnp.arange(GRID_W, dtype=F32), rows)
    n_freq = HEAD_DIM // 4
    inv_freq = ROPE_THETA ** (-jnp.arange(n_freq, dtype=F32) / n_freq)
    ar, ac = row[:, None] * inv_freq, col[:, None] * inv_freq
    zero = jnp.zeros_like(ar)
    cos = jnp.concatenate([jnp.cos(ar), jnp.cos(ar), jnp.cos(ac), jnp.cos(ac)], axis=-1)
    sin_hi = jnp.concatenate([-jnp.sin(ar), zero, -jnp.sin(ac), zero], axis=-1)
    sin_lo = jnp.concatenate([zero, jnp.sin(ar), zero, jnp.sin(ac)], axis=-1)
    pad = lambda t, v: jnp.concatenate([t, jnp.full((ctx_len, HEAD_DIM), v, F32)], axis=0)
    return pad(cos, 1.0), pad(sin_hi, 0.0), pad(sin_lo, 0.0)


def _rope(x, cos, sin_hi, sin_lo):
    q = HEAD_DIM // 4
    w = x.shape[-1]
    return x * cos + pltpu.roll(x, w - q, 1) * sin_hi + pltpu.roll(x, q, 1) * sin_lo


def _head_mean_sq(x, bd3):
    return _mm_exact_rhs(x * x, bd3)


def _qk_emit(wa, fa, cos, shi, slo, q_gain, k_gain, bdq, bdk, qw_ref, kw_ref, vw_ref, qf_ref, kf_ref, vf_ref,
             wq, wk):
    nq, nk = wq // HEAD_DIM, wk // HEAD_DIM
    tab = lambda t, n: jnp.concatenate([t] * n, axis=-1)
    cq, hq, lq = tab(cos, nq), tab(shi, nq), tab(slo, nq)
    ck, hk, lk = tab(cos, nk), tab(shi, nk), tab(slo, nk)
    scale = HEAD_DIM ** -0.5

    def emit(ref, val, n):
        for h in range(n):
            ref[0, h] = val[:, h * HEAD_DIM:(h + 1) * HEAD_DIM].astype(ref.dtype)

    def emit_values(ref, v):
        lane = lax.broadcasted_iota(jnp.int32, (v.shape[0], LANES - HEAD_DIM), 1)
        ones_pad = jnp.where(lane == 0, 1.0, 0.0)
        for h in range(nk):
            ref[0, h] = jnp.concatenate([v[:, h * HEAD_DIM:(h + 1) * HEAD_DIM], ones_pad],
                                        axis=-1).astype(ref.dtype)

    emit(qw_ref, _rope(wa[:, :wq], cq, hq, lq) * scale, nq)
    emit(kw_ref, _rope(wa[:, wq:wq + wk], ck, hk, lk), nk)
    emit_values(vw_ref, wa[:, wq + wk:])
    q, k = fa[:, :wq], fa[:, wq:wq + wk]
    q = q * lax.rsqrt(_head_mean_sq(q, bdq) + NORM_EPS) * q_gain
    k = k * lax.rsqrt(_head_mean_sq(k, bdk) + NORM_EPS) * k_gain
    emit(qf_ref, _rope(q, cq, hq, lq) * scale, nq)
    emit(kf_ref, _rope(k, ck, hk, lk), nk)
    emit_values(vf_ref, fa[:, wq + wk:])


def _head_block_diag(width, value):
    h = np.arange(width) // HEAD_DIM
    bd = ((h[:, None] == h[None, :]) * value).astype(np.float32).astype(BF16)
    assert np.all(bd.astype(np.float32) == (h[:, None] == h[None, :]) * value)
    return np.concatenate([bd, bd, bd], axis=0)


def _window_attn_kernel(q_ref, k_ref, v_ref, sink_ref, o_ref, *, seq_len, ctx_len, tq):
    t = pl.program_id(2)
    n_lat = seq_len // tq
    g = q_ref.shape[1]
    sub = WINDOW
    band = 2 * WINDOW + sub
    kc, vc = k_ref[0, 0, seq_len:seq_len + ctx_len, :], v_ref[0, 0, seq_len:seq_len + ctx_len, :]
    nt_dot = lambda a, b: lax.dot_general(a, b, (((1,), (1,)), ((), ())), preferred_element_type=F32)
    for j in range(tq // sub):
        q = q_ref[0, :, j * sub:(j + 1) * sub, :].reshape(g * sub, HEAD_DIM)
        first = t * tq + j * sub
        start = pl.multiple_of(jnp.clip(first - WINDOW, 0, seq_len - band), WINDOW)
        kb, vb = k_ref[0, 0, pl.ds(start, band), :], v_ref[0, 0, pl.ds(start, band), :]
        s_ctx = nt_dot(q, kc)
        s_loc = nt_dot(q, kb)
        qpos = first + lax.broadcasted_iota(jnp.int32, (g, sub, band), 1).reshape(g * sub, band)
        kpos = start + lax.broadcasted_iota(jnp.int32, (g * sub, band), 1)
        valid = jnp.logical_and(jnp.abs(qpos - kpos) <= WINDOW, t < n_lat)
        s_loc = jnp.where(valid, s_loc, NEG_INF)
        sink = sink_ref[0, :, j * sub:(j + 1) * sub, :].reshape(g * sub, 1)
        m = jnp.maximum(jnp.maximum(jnp.max(s_ctx, axis=-1, keepdims=True),
                                    jnp.max(s_loc, axis=-1, keepdims=True)), sink)
        p_ctx, p_loc = jnp.exp(s_ctx - m), jnp.exp(s_loc - m)
        acc = (jnp.dot(p_ctx.astype(BF16), vc, preferred_element_type=F32)
               + jnp.dot(p_loc.astype(BF16), vb, preferred_element_type=F32))
        denom = acc[:, HEAD_DIM:HEAD_DIM + 1] + jnp.exp(sink - m)
        out = acc[:, :HEAD_DIM] / denom
        o_ref[0, j * sub:(j + 1) * sub, :] = jnp.concatenate(
            [out[h * sub:(h + 1) * sub] for h in range(g)], axis=-1)


def _window_attention(q, k, v, sink, seq_len, ctx_len):
    bsz, nq, s, _ = q.shape
    nkv = k.shape[1]
    g = nq // nkv
    tq = ROW_TILE
    sink_rows = jnp.broadcast_to(sink.astype(F32).reshape(nkv, g, 1, 1), (nkv, g, tq, 1))
    kv = lambda a: pl.BlockSpec((1, 1, s, a.shape[-1]), lambda b, h, t: (b, h, 0, 0))
    return pl.pallas_call(
        functools.partial(_window_attn_kernel, seq_len=seq_len, ctx_len=ctx_len, tq=tq),
        out_shape=jax.ShapeDtypeStruct((bsz, s, nq * HEAD_DIM), F32),
        grid=(bsz, nkv, s // tq),
        in_specs=[pl.BlockSpec((1, g, tq, HEAD_DIM), lambda b, h, t: (b, h, t, 0)), kv(k), kv(v),
                  pl.BlockSpec((1, g, tq, 1), lambda b, h, t: (h, 0, 0, 0))],
        out_specs=pl.BlockSpec((1, tq, g * HEAD_DIM), lambda b, h, t: (b, t, h)),
        compiler_params=_cparams("arbitrary", "arbitrary", "arbitrary"),
        name="window_attention",
    )(q, k, v, sink_rows)


DENSE_KEY_BLOCK = 1024


def _dense_attn_kernel(q_ref, k_ref, v_ref, o_ref, *, seq_len, tq, tk):
    t = pl.program_id(2)
    n_lat = seq_len // tq
    g = q_ref.shape[1]
    ctx_len = k_ref.shape[2] - seq_len
    q = q_ref[0].reshape(g * tq, HEAD_DIM)

    def step(carry, start, size):
        m, acc = carry
        s = lax.dot_general(q, k_ref[0, 0, start:start + size, :], (((1,), (1,)), ((), ())),
                            preferred_element_type=F32)
        m_new = jnp.maximum(m, jnp.max(s, axis=-1, keepdims=True))
        p = jnp.exp(s - m_new).astype(BF16)
        pv = jnp.dot(p, v_ref[0, 0, start:start + size, :], preferred_element_type=F32)
        return m_new, jnp.exp(m - m_new) * acc + pv

    def finish(carry):
        _, acc = carry
        out = acc[:, :HEAD_DIM] / acc[:, HEAD_DIM:HEAD_DIM + 1]
        o_ref[0] = jnp.concatenate([out[h * tq:(h + 1) * tq] for h in range(g)], axis=-1)

    init = (jnp.full((g * tq, 1), NEG_INF, F32), jnp.zeros((g * tq, v_ref.shape[-1]), F32))

    @pl.when(t < n_lat)
    def _():
        carry = step(init, seq_len, ctx_len)
        for j in range(seq_len // tk):
            carry = step(carry, j * tk, tk)
        finish(carry)

    @pl.when(t >= n_lat)
    def _():
        finish(step(init, seq_len, ctx_len))


def _dense_attention(q, k, v, seq_len):
    bsz, nq, s, _ = q.shape
    nkv = k.shape[1]
    g = nq // nkv
    tq = ROW_TILE
    tk = math.gcd(seq_len, DENSE_KEY_BLOCK)
    kv = lambda a: pl.BlockSpec((1, 1, s, a.shape[-1]), lambda b, h, t: (b, h, 0, 0))
    return pl.pallas_call(
        functools.partial(_dense_attn_kernel, seq_len=seq_len, tq=tq, tk=tk),
        out_shape=jax.ShapeDtypeStruct((bsz, s, nq * HEAD_DIM), F32),
        grid=(bsz, nkv, s // tq),
        in_specs=[pl.BlockSpec((1, g, tq, HEAD_DIM), lambda b, h, t: (b, h, t, 0)), kv(k), kv(v)],
        out_specs=pl.BlockSpec((1, tq, g * HEAD_DIM), lambda b, h, t: (b, t, h)),
        compiler_params=_cparams("arbitrary", "arbitrary", "arbitrary"),
        name="dense_attention",
    )(q, k, v)


def _softplus(x):
    return jnp.maximum(x, 0.0) + jnp.log(1.0 + jnp.exp(-jnp.abs(x)))


def _split(a):
    bits = lax.bitcast_convert_type(a, jnp.uint32) & jnp.uint32(0xFFFF0000)
    hi = lax.bitcast_convert_type(bits, F32)
    return hi, a - hi


def _rhs3(b):
    hi, lo = _split(b)
    return jnp.concatenate([hi, hi, lo], axis=-2).astype(BF16)


def _split3(a):
    hi, rest = _split(a)
    mid, lo = _split(rest)
    return hi, mid, lo


def _mm_exact_lhs(tbl3, x):
    return jnp.dot(tbl3, jnp.concatenate(_split3(x), axis=0).astype(BF16), preferred_element_type=F32)


def _mm_exact_rhs(x, tbl3):
    return jnp.dot(jnp.concatenate(_split3(x), axis=-1).astype(BF16), tbl3, preferred_element_type=F32)


def _mm_split(a, b):
    ah, al = _split(a)
    return jnp.dot(jnp.concatenate([ah, al, ah], axis=-1).astype(BF16), _rhs3(b), preferred_element_type=F32)


RW_DEPTH = 1


def _lhs4(a):
    return a.astype(BF16)


def _rhs4(b):
    return b.astype(BF16)


def _rhs4_nt(b):
    return b.astype(BF16)


def _bmm(lf, rf):
    return lax.dot_general(lf, rf, (((2,), (1,)), ((0,), (0,))), preferred_element_type=F32)


RW_SLOTS_PER_STEP = 16


def _rwkv_par_kernel(z_ref, w0_ref, wup_ref, a0_ref, aup_ref, kk_ref, ka_ref,
                     bd_ref, cum_ref,
                     rp3_ref, yvq_ref, rvk_ref,
                     lbig_ref, rbig_ref, at_ref, rt_ref, vv3_ref, bct_ref, kct_ref, wt_ref,
                     *, w, lora):
    tc = RW_CHUNK
    nc, nh = ROW_TILE // tc, w // HEAD_DIM
    nb = RW_SLOTS_PER_STEP
    z = z_ref[0]
    r, k, v = z[:, :w], z[:, w:2 * w], z[:, 2 * w:3 * w]
    w_low = jnp.tanh(z[:, 3 * w:3 * w + lora])
    a_low = z[:, 3 * w + lora:]
    kk = k * kk_ref[...]
    kk = kk / jnp.maximum(jnp.sqrt(_mm_exact_rhs(kk * kk, bd_ref[...])), 1e-12)
    ksum = jnp.zeros_like(k)
    for d in range(2):
        w_log = -_softplus(-(w0_ref[d:d + 1, :] + _mm_split(w_low, wup_ref[d]))) - 0.5
        lw = -jnp.exp(w_log)
        a = 1.0 / (1.0 + jnp.exp(-(a0_ref[d:d + 1, :] + _mm_split(a_low, aup_ref[d]))))
        kd = k * (1.0 + (a - 1.0) * ka_ref[...])
        ksum = ksum + kd
        sums = _mm_exact_lhs(cum_ref[d], lw)
        c, ctot = sums[:ROW_TILE], sums[ROW_TILE:]
        e_neg, e_rem = jnp.exp(-c), jnp.exp(ctot - c)
        b = kk * a
        at, rt = -kk * jnp.exp(c - lw), r * jnp.exp(c)
        w_tot = jnp.exp(ctot)
        bh, kh = b * e_neg, kd * e_neg
        bct, kct = (b * e_rem).T, (kd * e_rem).T
        for ci in range(nc):
            for h in range(nh):
                slot = (d * nc + ci) * nh + h
                rows, cols = slice(ci * tc, (ci + 1) * tc), slice(h * HEAD_DIM, (h + 1) * HEAD_DIM)
                lbig_ref[slot, :tc, :] = _lhs4(at[rows, cols])
                lbig_ref[slot, tc:, :] = _lhs4(rt[rows, cols])
                rbig_ref[slot, :tc, :] = _rhs4_nt(bh[rows, cols])
                rbig_ref[slot, tc:, :] = _rhs4_nt(kh[rows, cols])
                at_ref[slot] = at[rows, cols]
                rt_ref[slot] = rt[rows, cols]
                vv3_ref[slot] = _rhs4(v[rows, cols])
                bct_ref[slot] = _lhs4(bct[cols, rows])
                kct_ref[slot] = _lhs4(kct[cols, rows])
                wt_ref[slot] = w_tot[ci * tc:ci * tc + SUBLANES, cols]
    rvk_ref[0] = jnp.concatenate([r, v, ksum], axis=-1)

    row = lax.broadcasted_iota(jnp.int32, (tc, tc), 0)
    col = lax.broadcasted_iota(jnp.int32, (tc, tc), 1)
    eye = row == col

    def level_mask(s):
        return jnp.logical_and(row // (2 * s) == col // (2 * s), row // s != col // s)

    steps_per_dir = nc * nh // nb

    def chunk_group(g, carry):
        slots = pl.ds(pl.multiple_of(g * nb, nb), nb)
        d = g // steps_per_dir
        fwd = d == 0
        before = jnp.logical_or(jnp.logical_and(fwd, row > col),
                                jnp.logical_and(jnp.logical_not(fwd), row < col))
        upto = jnp.logical_or(before, eye)
        big = lax.dot_general(lbig_ref[slots], rbig_ref[slots], (((2,), (2,)), ((0,), (0,))),
                              preferred_element_type=F32)
        a_ab = jnp.where(before, big[:, :tc, :tc], 0.0)
        a_ak = jnp.where(before, big[:, :tc, tc:], 0.0)
        a_rb = jnp.where(upto, big[:, tc:, :tc], 0.0)
        a_rk = jnp.where(upto, big[:, tc:, tc:], 0.0)
        x = jnp.where(eye, 1.0, jnp.where(level_mask(1), a_ab, 0.0))
        s = 2
        while s < tc:
            half = _bmm(_lhs4(x), _rhs4(jnp.where(level_mask(s), a_ab, 0.0)))
            x = x + _bmm(_lhs4(half), _rhs4(x))
            s *= 2
        vv3 = vv3_ref[slots]
        akrk = _bmm(jnp.concatenate([_lhs4(a_ak), _lhs4(a_rk)], axis=1), vv3)
        xa3 = _rhs4(_bmm(_lhs4(x), _rhs4(jnp.concatenate([at_ref[slots], akrk[:, :tc]], axis=-1))))
        ra = _bmm(_lhs4(a_rb), xa3)
        pq = _bmm(bct_ref[slots], xa3)
        rp = rt_ref[slots] + ra[:, :, :HEAD_DIM]
        yv = ra[:, :, HEAD_DIM:] + akrk[:, tc:]
        p = jnp.where(eye[:HEAD_DIM, :HEAD_DIM], wt_ref[slots][:, 0:1, :], 0.0) + pq[:, :, :HEAD_DIM]
        q = pq[:, :, HEAD_DIM:] + _bmm(kct_ref[slots], vv3)
        rp3 = jnp.concatenate([_lhs4(rp), _lhs4(p)], axis=1)
        yvq = jnp.concatenate([yv, q], axis=1)
        for j in range(nb):
            ci = (g % steps_per_dir) * (nb // nh) + j // nh
            rp3_ref[0, d, j % nh, ci] = rp3[j]
            yvq_ref[0, d, j % nh, ci] = yvq[j]
        return carry
    lax.fori_loop(0, 2 * steps_per_dir, chunk_group, 0, unroll=True)


def _chunk_matrices():
    t = np.arange(ROW_TILE)
    same = (t[:, None] // RW_CHUNK) == (t[None, :] // RW_CHUNK)
    tabs = []
    for run in (same & (t[None, :] <= t[:, None]), same & (t[None, :] >= t[:, None])):
        m = np.concatenate([run, same], axis=0).astype(np.float32).astype(BF16)
        tabs.append(np.concatenate([m, m, m], axis=1))
    return np.stack(tabs)


def _rwkv_par(rw, w0, w_up, a0, a_up, k_k, k_a, w):
    bsz, s, width = rw.shape
    nt = s // ROW_TILE
    nh, nc = w // HEAD_DIM, ROW_TILE // RW_CHUNK
    lora = w_up.shape[1]
    full = lambda a: pl.BlockSpec(a.shape, lambda b, t: (0,) * a.ndim)
    cum = _chunk_matrices()
    bd = _head_block_diag(w, 1.0)
    vec = lambda a: a.reshape(1, w)
    n_chunks = s // RW_CHUNK
    nslots = 2 * nc * nh
    tc, hd = RW_CHUNK, HEAD_DIM
    consts = (w0, w_up, a0, a_up, vec(k_k), vec(k_a), bd, cum)
    return pl.pallas_call(
        functools.partial(_rwkv_par_kernel, w=w, lora=lora),
        out_shape=[jax.ShapeDtypeStruct((bsz, 2, nh, n_chunks, tc + hd, RW_DEPTH * hd), BF16),
                   jax.ShapeDtypeStruct((bsz, 2, nh, n_chunks, tc + hd, hd), F32),
                   jax.ShapeDtypeStruct((bsz, s, 3 * w), F32)],
        grid=(bsz, nt),
        in_specs=[pl.BlockSpec((1, ROW_TILE, width), lambda b, t: (b, t, 0))] + [full(a) for a in consts],
        out_specs=[pl.BlockSpec((1, 2, nh, nc, tc + hd, RW_DEPTH * hd), lambda b, t: (b, 0, 0, t, 0, 0)),
                   pl.BlockSpec((1, 2, nh, nc, tc + hd, hd), lambda b, t: (b, 0, 0, t, 0, 0)),
                   pl.BlockSpec((1, ROW_TILE, 3 * w), lambda b, t: (b, t, 0))],
        scratch_shapes=[pltpu.VMEM((nslots, 2 * tc, RW_DEPTH * hd), BF16),
                        pltpu.VMEM((nslots, 2 * tc, RW_DEPTH * hd), BF16),
                        pltpu.VMEM((nslots, tc, hd), F32),
                        pltpu.VMEM((nslots, tc, hd), F32),
                        pltpu.VMEM((nslots, RW_DEPTH * tc, hd), BF16),
                        pltpu.VMEM((nslots, hd, RW_DEPTH * tc), BF16),
                        pltpu.VMEM((nslots, hd, RW_DEPTH * tc), BF16),
                        pltpu.VMEM((nslots, SUBLANES, HEAD_DIM), F32)],
        compiler_params=_cparams("arbitrary", "arbitrary"),
        name="rwkv_chunk_prep",
    )(rw, *consts)


def _rwkv_seq_kernel(rp3f_ref, yvqf_ref, rp3b_ref, yvqb_ref, yf_ref, yb_ref, g_ref):
    @pl.when(pl.program_id(1) == 0)
    def _():
        g_ref[...] = jnp.zeros_like(g_ref)

    nh = g_ref.shape[1]
    tc = RW_CHUNK
    grp = rp3f_ref.shape[3]
    for step in range(grp):
        for d, (rp3, yvq, y) in enumerate(((rp3f_ref, yvqf_ref, yf_ref), (rp3b_ref, yvqb_ref, yb_ref))):
            ci = step if d == 0 else grp - 1 - step
            for h in range(nh):
                out = (jnp.dot(rp3[0, 0, h, ci], _rhs4(g_ref[d, h]), preferred_element_type=F32)
                       + yvq[0, 0, h, ci])
                y[0, h, ci * tc:(ci + 1) * tc, :] = out[:tc]
                g_ref[d, h] = out[tc:]


RW_SCAN_GROUP = 4


def _rwkv_seq(rp3, yvq, n_lat_chunks):
    bsz, _, nh, n_chunks = rp3.shape[:4]
    grp = RW_SCAN_GROUP
    assert n_lat_chunks % grp == 0 and n_chunks % grp == 0
    n_groups, n_lat, n_ctx = n_chunks // grp, n_lat_chunks // grp, (n_chunks - n_lat_chunks) // grp
    order = (lambda i: jnp.where(i < n_ctx, n_lat + i, i - n_ctx),
             lambda i: n_groups - 1 - i)
    blk = lambda d, a: pl.BlockSpec((1, 1, nh, grp) + a.shape[4:], lambda b, i: (b, d, 0, order[d](i), 0, 0))
    out = lambda d: pl.BlockSpec((1, nh, grp * RW_CHUNK, HEAD_DIM), lambda b, i: (b, 0, order[d](i), 0))
    shp = jax.ShapeDtypeStruct((bsz, nh, n_chunks * RW_CHUNK, HEAD_DIM), F32)
    return pl.pallas_call(
        _rwkv_seq_kernel,
        out_shape=[shp, shp],
        grid=(bsz, n_groups),
        in_specs=[blk(0, rp3), blk(0, yvq), blk(1, rp3), blk(1, yvq)],
        out_specs=[out(0), out(1)],
        scratch_shapes=[pltpu.VMEM((2, nh, HEAD_DIM, HEAD_DIM), F32)],
        compiler_params=_cparams("arbitrary", "arbitrary"),
        name="rwkv_state_scan",
    )(rp3, yvq, rp3, yvq)


def _rwkv_readout_tile(yf_ref, yb_ref, rvk, r_k, ln_g, ln_b):
    w = r_k.shape[-1]
    nh = w // HEAD_DIM
    r, v, ksum = rvk[:, :w], rvk[:, w:2 * w], rvk[:, 2 * w:]
    rkk = r * ksum * r_k
    normed, bonus = [], []
    for h in range(nh):
        cols = slice(h * HEAD_DIM, (h + 1) * HEAD_DIM)
        y = yf_ref[0, h] + yb_ref[0, h]
        mu = jnp.mean(y, axis=-1, keepdims=True)
        var = jnp.mean(jnp.square(y - mu), axis=-1, keepdims=True)
        normed.append((y - mu) * lax.rsqrt(var + RW_GN_EPS))
        bonus.append(jnp.sum(rkk[:, cols], axis=-1, keepdims=True) * v[:, cols])
    return jnp.concatenate(normed, axis=-1) * ln_g + ln_b + jnp.concatenate(bonus, axis=-1)


def _hyena_two_sided_filters(seq_len, fw1, fb1, freq, fw2, fb2, fw3, width):
    bands = (fw1.shape[0] - 1) // 2
    mm = functools.partial(jnp.matmul, precision=HI)
    n = jnp.arange(2 * seq_len)
    pos = jnp.where(n < seq_len, n, 2 * seq_len - n) % seq_len
    is_fwd = (n < seq_len)[:, None, None]
    live = (n != seq_len)[:, None, None]
    t = jnp.linspace(0.0, 1.0, seq_len, dtype=F32)[pos][:, None]
    wpos = (2.0 * math.pi / seq_len) * pos.astype(F32)[:, None]
    f = jnp.linspace(1e-4, bands - 1, bands, dtype=F32)[None, :]
    z = jnp.concatenate([t, jnp.cos(f * wpos), jnp.sin(f * wpos)], axis=-1)
    h = jnp.sin(freq * (mm(z, fw1) + fb1))
    h = jnp.sin(freq * (mm(h, fw2) + fb2))
    h = mm(h, fw3).reshape(2 * seq_len, 2, 2, width)
    max_decay = math.log(HY_DECAY_TARGET) / HY_FAST_DECAY
    min_decay = math.log(HY_DECAY_TARGET) / HY_SLOW_DECAY
    deltas = jnp.linspace(min_decay, max_decay, width, dtype=F32)
    h = jnp.where(is_fwd, h[:, :, 0], h[:, :, 1]) * jnp.exp(-t * jnp.abs(deltas))[:, None, :]
    norm = jnp.sum(jnp.abs(h), axis=0, keepdims=True)
    return jnp.moveaxis(jnp.where(live, h / norm, 0.0), 1, 0)


def kernel(x, c, ctx, c_ctx, mod_w, mod_b, norm_g, w_in, w_out, hy_conv, hy_fw1, hy_fb1, hy_freq, hy_fw2,
           hy_fb2, hy_fw3, hy_bias, rw_conv, rw_w0, rw_w_up, rw_a0, rw_a_up, rw_k_k, rw_k_a, rw_r_k,
           rw_ln_g, rw_ln_b, wa_sink, fa_q_norm, fa_k_norm, final_g):
    bsz, seq_len, d = x.shape
    ctx_len = ctx.shape[1]
    depth = w_in.shape[0]
    w_hy = hy_bias.shape[-1]
    w_rw = rw_w0.shape[-1]
    n_wa_heads = wa_sink.shape[-1]
    w_q = n_wa_heads * HEAD_DIM
    w_kv = w_q // 2
    lora = rw_w_up.shape[2] + rw_a_up.shape[2]
    branch_w = (3 * w_hy, 3 * w_rw + lora, w_q + 2 * w_kv, w_q + 2 * w_kv)
    gate_w = (w_hy, w_rw, w_q, w_q)
    assert seq_len % ROW_TILE == 0 and ctx_len % ROW_TILE == 0 and bsz % 2 == 0
    n_lat_tiles = seq_len // ROW_TILE

    starts = np.cumsum([0] + [bw + gw for bw, gw in zip(branch_w, gate_w)])
    cols = np.concatenate([np.arange(s0, s0 + bw) for s0, bw in zip(starts, branch_w)]
                          + [np.arange(s0 + bw, s0 + bw + gw) for s0, bw, gw in zip(starts, branch_w, gate_w)])
    w_in_p = w_in[:, :, cols].astype(BF16)
    w_out_b = w_out.astype(BF16)
    widths = branch_w + (sum(gate_w),)

    pad_rows = (-(bsz + 1)) % 8
    cond = jnp.concatenate([c, c_ctx[None], jnp.zeros((pad_rows, d), F32)], axis=0)
    mod = _modulation(cond, mod_w, mod_b)
    mod_lat = mod[:, :bsz].reshape(depth, bsz, 3, d)
    mod_ctx = jnp.broadcast_to(mod[:, bsz].reshape(depth, 1, 3, d), (depth, bsz, 3, d))
    mods = jnp.stack([mod_lat, mod_ctx], axis=2)

    rope = _rope_tables(seq_len, ctx_len)
    xs = jnp.concatenate([x, ctx], axis=1)
    for l in range(depth):
        last = l == depth - 1
        conv_w = jnp.concatenate([hy_conv[l], rw_conv[l]], axis=-1)
        hyc, rwc, gates, qw, kw, vw, qf, kf, vf = _inproj(
            xs, mods[l], norm_g[l], w_in_p[l], conv_w, widths, n_lat_tiles, rope,
            fa_q_norm[l], fa_k_norm[l], w_q, w_kv)

        filt = functools.partial(_hyena_two_sided_filters, fw1=hy_fw1[l], fb1=hy_fb1[l], freq=hy_freq[l],
                                 fw2=hy_fw2[l], fb2=hy_fb2[l], fw3=hy_fw3[l], width=w_hy)
        bias = hy_bias[l].reshape(2, 1, w_hy)
        spec = _filter_spectrum(filt(seq_len))
        nt_hy = w_hy // LANES
        y1 = _fftconv_gated(hyc, 0, hyc, nt_hy, spec, 0, bias, seq_len)
        a_lat = _fftconv_gated(y1, 0, hyc, 2 * nt_hy, spec, 1, bias, seq_len)
        a_ctx = a_lat if last else _ctx_hyena(hyc, seq_len // ctx_len, filt(ctx_len), bias, ctx_len, w_hy)

        rp3, yvq, rvk = _rwkv_par(rwc, rw_w0[l], rw_w_up[l], rw_a0[l], rw_a_up[l], rw_k_k[l], rw_k_a[l], w_rw)
        yf, yb = _rwkv_seq(rp3, yvq, seq_len // RW_CHUNK)

        c_mix = _window_attention(qw, kw, vw, wa_sink[l], seq_len, ctx_len)
        d_mix = _dense_attention(qf, kf, vf, seq_len)

        xs = _outproj(a_lat, a_ctx, yf, yb, rvk, rw_r_k[l], rw_ln_g[l], rw_ln_b[l], c_mix, d_mix, gates, xs,
                      mods[l], w_out_b[l], final_g, n_lat_tiles, last)
    return xs
```

```python
import functools
import math

import jax
import jax.numpy as jnp
import numpy as np
from jax import lax
from jax.experimental import pallas as pl
from jax.experimental.pallas import tpu as pltpu

HEAD_DIM = 64
GRID_W = 64
WINDOW = 128
NORM_EPS = 1e-6
RW_GN_EPS = 64e-5
NEG_INF = -1e30
ROPE_THETA = 10000.0
HY_FAST_DECAY = 0.3
HY_SLOW_DECAY = 1.5
HY_DECAY_TARGET = 1e-2

ROW_TILE = 256
LANES = 128
SUBLANES = 8
FFT_N1 = 64
FFT_UNROLL = 8
FFT_PLANE_PAD = 8
RW_CHUNK = 64
VMEM_LIMIT = 60 * 1024 * 1024

F32 = jnp.float32
BF16 = jnp.bfloat16
HI = lax.Precision.HIGHEST


def _dot(a, b):
    return jnp.dot(a, b, preferred_element_type=F32, precision=HI)


def _dot_nt(a, b):
    return lax.dot_general(a, b, (((1,), (1,)), ((), ())), preferred_element_type=F32, precision=HI)


def _dot_tn(a, b):
    return lax.dot_general(a, b, (((0,), (0,)), ((), ())), preferred_element_type=F32, precision=HI)


def _cparams(*sem):
    return pltpu.CompilerParams(dimension_semantics=sem, vmem_limit_bytes=VMEM_LIMIT)


def _const_spec(shape):
    return pl.BlockSpec(shape, lambda *_: (0,) * len(shape), pipeline_mode=pl.Buffered(1))


def _silu(x):
    return x * (1.0 / (1.0 + jnp.exp(-x)))


def _mod_kernel(c_ref, w_ref, b_ref, o_ref):
    o_ref[0] = _dot(_silu(c_ref[...]), w_ref[0]) + b_ref[0]


def _modulation(cond, mod_w, mod_b):
    depth, d, d3 = mod_w.shape
    rows = cond.shape[0]
    return pl.pallas_call(
        _mod_kernel,
        out_shape=jax.ShapeDtypeStruct((depth, rows, d3), F32),
        grid=(depth,),
        in_specs=[pl.BlockSpec((rows, d), lambda l: (0, 0)),
                  pl.BlockSpec((1, d, d3), lambda l: (l, 0, 0)),
                  pl.BlockSpec((1, 1, d3), lambda l: (l, 0, 0))],
        out_specs=pl.BlockSpec((1, rows, d3), lambda l: (l, 0, 0)),
        compiler_params=_cparams("arbitrary"),
        name="modulation",
    )(cond, mod_w, mod_b.reshape(depth, 1, d3))


def _blockreal(m):
    return np.block([[m.real, -m.imag], [m.imag, m.real]])


@functools.lru_cache(maxsize=None)
def _fft_tables(seq_len):
    n = 2 * seq_len
    n1, n2 = FFT_N1, n // FFT_N1
    h1 = n1 // 2
    j2 = np.arange(n2)[:, None, None]
    k1 = np.arange(n1)[None, :, None]
    t1 = np.exp(-2j * np.pi * (j2 * k1 / n + k1 * np.arange(n1)[None, None, :] / n1))
    t1_data = np.stack([_blockreal(t1[j][:, :h1]) for j in range(n2)])
    t1_real = np.concatenate([t1.real, t1.imag], axis=1)
    f2 = np.exp(-2j * np.pi * np.outer(np.arange(n2), np.arange(n2)) / n2)
    f2_fwd = _blockreal(f2)
    f2_inv = _blockreal(np.conj(f2))
    t4 = np.exp(2j * np.pi * (np.arange(h1)[None, :, None] * np.arange(n1)[None, None, :] / n1
                              + j2 * np.arange(n1)[None, None, :] / n)) / n
    t4 = np.stack([_blockreal(t4[j]) for j in range(n2)])
    return tuple(_lhs3_table(t) for t in (t1_data, t1_real, f2_fwd, f2_inv, t4))


def _lhs3_table(m):
    hi = m.astype(np.float32).astype(BF16)
    lo = (m - hi.astype(np.float64)).astype(np.float32).astype(BF16)
    return np.concatenate([hi, lo, hi], axis=-1)


def _mm3(tbl3, x):
    return jnp.dot(tbl3, _rhs3(x), preferred_element_type=F32)


def _spectrum_kernel(k_ref, t1_ref, f2_ref, o_ref, a_ref, *, n1, n2):
    def stage1(j, carry):
        rows = k_ref[0, pl.ds(j, n1, stride=n2), :]
        a_ref[pl.ds(j, 2 * n1, stride=n2 + FFT_PLANE_PAD), :] = _mm3(t1_ref[j], rows)
        return carry
    lax.fori_loop(0, n2, stage1, 0, unroll=FFT_UNROLL)

    def stage2(i, carry):
        pitch = n2 + FFT_PLANE_PAD
        re = a_ref[pl.ds(pl.multiple_of(i * pitch, 8), n2), :]
        im = a_ref[pl.ds(pl.multiple_of((n1 + i) * pitch, 8), n2), :]
        o_ref[0, i] = _mm3(f2_ref[...], jnp.concatenate([re, im], axis=0))
        return carry
    lax.fori_loop(0, n1, stage2, 0, unroll=2)


def _filter_spectrum(kfilt):
    g, n, w = kfilt.shape
    n1, n2 = FFT_N1, n // FFT_N1
    _, t1_real, f2_fwd, _, _ = _fft_tables(n // 2)
    const = _const_spec
    return pl.pallas_call(
        functools.partial(_spectrum_kernel, n1=n1, n2=n2),
        out_shape=jax.ShapeDtypeStruct((g, n1, 2 * n2, w), F32),
        grid=(g, w // LANES),
        in_specs=[pl.BlockSpec((1, n, LANES), lambda gi, j: (gi, 0, j)),
                  const(t1_real.shape), const(f2_fwd.shape)],
        out_specs=pl.BlockSpec((1, n1, 2 * n2, LANES), lambda gi, j: (gi, 0, 0, j)),
        scratch_shapes=[pltpu.VMEM((2 * n1 * (n2 + FFT_PLANE_PAD), LANES), F32)],
        compiler_params=_cparams("arbitrary", "arbitrary"),
        name="hyena_filter_spectrum",
    )(kfilt, t1_real, f2_fwd)


def _fftconv_kernel(u_ref, m_ref, spec_ref, bias_ref, t1_ref, f2f_ref, f2i_ref, t4_ref, o_ref, a_ref,
                    *, n1, n2):
    h1 = n1 // 2

    def stage1(j, carry):
        za = u_ref[0, pl.ds(j, h1, stride=n2), :]
        zb = u_ref[1, pl.ds(j, h1, stride=n2), :]
        a_ref[pl.ds(j, 2 * n1, stride=n2 + FFT_PLANE_PAD), :] = _mm3(t1_ref[j], jnp.concatenate([za, zb], axis=0))
        return carry
    lax.fori_loop(0, n2, stage1, 0, unroll=FFT_UNROLL)

    def stage2(i, carry):
        pitch = n2 + FFT_PLANE_PAD
        re_rows = pl.ds(pl.multiple_of(i * pitch, 8), n2)
        im_rows = pl.ds(pl.multiple_of((n1 + i) * pitch, 8), n2)
        x = _mm3(f2f_ref[...], jnp.concatenate([a_ref[re_rows, :], a_ref[im_rows, :]], axis=0))
        xr, xi = x[:n2], x[n2:]
        kr, ki = spec_ref[0, i, :n2, :], spec_ref[0, i, n2:, :]
        y = jnp.concatenate([xr * kr - xi * ki, xr * ki + xi * kr], axis=0)
        b = _mm3(f2i_ref[...], y)
        a_ref[re_rows, :] = b[:n2]
        a_ref[im_rows, :] = b[n2:]
        return carry
    lax.fori_loop(0, n1, stage2, 0, unroll=2)

    bias = bias_ref[0]

    def stage4(j, carry):
        y = _mm3(t4_ref[j], a_ref[pl.ds(j, 2 * n1, stride=n2 + FFT_PLANE_PAD), :])
        rows = pl.ds(j, h1, stride=n2)
        for p in range(2):
            u = u_ref[p, rows, :]
            o_ref[p, rows, :] = m_ref[p, rows, :] * (y[p * h1:(p + 1) * h1] + bias * u)
        return carry
    lax.fori_loop(0, n2, stage4, 0, unroll=FFT_UNROLL)


def _fftconv_gated(u, u_col, mult, mult_col, spec, conv_idx, bias, seq_len):
    bsz = u.shape[0]
    w = spec.shape[-1]
    n = 2 * seq_len
    n1, n2 = FFT_N1, n // FFT_N1
    t1_data, _, f2_fwd, f2_inv, t4 = _fft_tables(seq_len)
    const = _const_spec
    return pl.pallas_call(
        functools.partial(_fftconv_kernel, n1=n1, n2=n2),
        out_shape=jax.ShapeDtypeStruct((bsz, seq_len, w), F32),
        grid=(w // LANES, bsz // 2),
        in_specs=[pl.BlockSpec((2, seq_len, LANES), lambda j, p: (p, 0, u_col + j)),
                  pl.BlockSpec((2, seq_len, LANES), lambda j, p: (p, 0, mult_col + j)),
                  pl.BlockSpec((1, n1, 2 * n2, LANES), lambda j, p: (conv_idx, 0, 0, j),
                               pipeline_mode=pl.Buffered(1)),
                  pl.BlockSpec((1, 1, LANES), lambda j, p: (conv_idx, 0, j)),
                  const(t1_data.shape), const(f2_fwd.shape), const(f2_inv.shape), const(t4.shape)],
        out_specs=pl.BlockSpec((2, seq_len, LANES), lambda j, p: (p, 0, j)),
        scratch_shapes=[pltpu.VMEM((2 * n1 * (n2 + FFT_PLANE_PAD), LANES), F32)],
        compiler_params=_cparams("arbitrary", "arbitrary"),
        name="hyena_fftconv",
    )(u, mult, spec, bias, t1_data, f2_fwd, f2_inv, t4)


@functools.lru_cache(maxsize=None)
def _small_fft_tables(seq_len):
    n = 2 * seq_len
    f = np.exp(-2j * np.pi * np.outer(np.arange(n), np.arange(n)) / n)
    fwd = _blockreal(f[:, :seq_len])
    real = np.concatenate([f.real, f.imag], axis=0)
    inv = _blockreal(np.conj(f)[:seq_len, :] / n)
    as32 = lambda a: jnp.asarray(a, dtype=F32)
    return as32(fwd), as32(real), as32(inv)


def _ctx_hyena_kernel(v_ref, x1_ref, x2_ref, k_ref, bias_ref, fwd_ref, real_ref, inv_ref, o_ref, *, seq_len):
    n = 2 * seq_len

    def conv(ua, ub, g):
        spec = _dot(real_ref[...], k_ref[g])
        x = _dot(fwd_ref[...], jnp.concatenate([ua, ub], axis=0))
        xr, xi, kr, ki = x[:n], x[n:], spec[:n], spec[n:]
        y = _dot(inv_ref[...], jnp.concatenate([xr * kr - xi * ki, xr * ki + xi * kr], axis=0))
        b = bias_ref[g]
        return y[:seq_len] + b * ua, y[seq_len:] + b * ub

    c1a, c1b = conv(v_ref[0], v_ref[1], 0)
    y1a, y1b = x1_ref[0] * c1a, x1_ref[1] * c1b
    c2a, c2b = conv(y1a, y1b, 1)
    o_ref[0] = x2_ref[0] * c2a
    o_ref[1] = x2_ref[1] * c2b


def _ctx_hyena(hyc, row_block, kfilt, bias, seq_len, w):
    bsz = hyc.shape[0]
    n = 2 * seq_len
    fwd, real, inv = _small_fft_tables(seq_len)
    nt = w // LANES
    col = lambda c0: pl.BlockSpec((2, seq_len, LANES), lambda j, p: (p, row_block, c0 + j))
    return pl.pallas_call(
        functools.partial(_ctx_hyena_kernel, seq_len=seq_len),
        out_shape=jax.ShapeDtypeStruct((bsz, seq_len, w), F32),
        grid=(nt, bsz // 2),
        in_specs=[col(0), col(nt), col(2 * nt),
                  pl.BlockSpec((2, n, LANES), lambda j, p: (0, 0, j)),
                  pl.BlockSpec((2, 1, LANES), lambda j, p: (0, 0, j)),
                  _const_spec((2 * n, 2 * seq_len)), _const_spec((2 * n, n)), _const_spec((2 * seq_len, 2 * n))],
        out_specs=pl.BlockSpec((2, seq_len, LANES), lambda j, p: (p, 0, j)),
        compiler_params=_cparams("arbitrary", "arbitrary"),
        name="hyena_ctx",
    )(hyc, hyc, hyc, kfilt, bias, fwd, real, inv)


def _rms(x, g):
    return x * lax.rsqrt(jnp.mean(x * x, axis=-1, keepdims=True) + NORM_EPS) * g


def _inproj_kernel(x_ref, xprev_ref, xnext_ref, mod_ref, g_ref, w_ref, cw_ref,
                   cos_ref, shi_ref, slo_ref, qg_ref, kg_ref, bdq_ref, bdk_ref,
                   hy_ref, rw_ref, gt_ref, qw_ref, kw_ref, vw_ref, qf_ref, kf_ref, vf_ref,
                   *, widths, wq, wk, n_lat_tiles, n_tiles):
    shift, scale = mod_ref[0, 0, 0:1, :], mod_ref[0, 0, 1:2, :]
    norm_mod = lambda x: (_rms(x, g_ref[...]) * (1.0 + scale) + shift).astype(BF16)
    z = jnp.dot(norm_mod(x_ref[0]), w_ref[...], preferred_element_type=F32)
    offs = np.cumsum((0,) + tuple(widths))
    part = lambda i: z[:, offs[i]:offs[i + 1]]
    n_conv = offs[2]
    halo = jnp.dot(norm_mod(jnp.concatenate([xprev_ref[0], xnext_ref[0]], axis=0)), w_ref[:, :n_conv],
                   preferred_element_type=F32)
    conv = _short_conv_tile(z[:, :n_conv], halo[:SUBLANES], halo[SUBLANES:], cw_ref[...], pl.program_id(1),
                            n_lat_tiles, n_tiles)
    hy_ref[0] = conv[:, :offs[1]]
    rw_ref[0] = conv[:, offs[1]:]
    gt_ref[0] = _silu(part(4))
    _qk_emit(part(2), part(3), cos_ref[...], shi_ref[...], slo_ref[...], qg_ref[...], kg_ref[...],
             bdq_ref[...], bdk_ref[...], qw_ref, kw_ref, vw_ref, qf_ref, kf_ref, vf_ref, wq, wk)


def _inproj(xs, mods, norm_g, w_perm, conv_w, widths, n_lat_tiles, rope, q_gain, k_gain, wq, wk):
    bsz, s, d = xs.shape
    nt = s // ROW_TILE
    nq, nk = wq // HEAD_DIM, wk // HEAD_DIM
    prev, nxt = _halo_specs(d, nt)
    tile = lambda wd: pl.BlockSpec((1, ROW_TILE, wd), lambda b, t: (b, t, 0))
    tab = pl.BlockSpec((ROW_TILE, HEAD_DIM), lambda b, t: (t, 0))
    hm = lambda n, wd=HEAD_DIM: pl.BlockSpec((1, n, ROW_TILE, wd), lambda b, t: (b, 0, t, 0))
    flat = lambda wd: jax.ShapeDtypeStruct((bsz, s, wd), F32)
    heads = lambda n, wd=HEAD_DIM: jax.ShapeDtypeStruct((bsz, n, s, wd), BF16)
    return pl.pallas_call(
        functools.partial(_inproj_kernel, widths=widths, wq=wq, wk=wk, n_lat_tiles=n_lat_tiles, n_tiles=nt),
        out_shape=[flat(widths[0]), flat(widths[1]), flat(widths[4]),
                   heads(nq), heads(nk), heads(nk, LANES), heads(nq), heads(nk), heads(nk, LANES)],
        grid=(bsz, nt),
        in_specs=[tile(d), prev, nxt,
                  pl.BlockSpec((1, 1, 3, d), lambda b, t: (b, t // n_lat_tiles, 0, 0)),
                  pl.BlockSpec((1, d), lambda b, t: (0, 0)),
                  _const_spec(w_perm.shape), pl.BlockSpec(conv_w.shape, lambda b, t: (0, 0)), tab, tab, tab,
                  pl.BlockSpec((1, wq), lambda b, t: (0, 0)), pl.BlockSpec((1, wk), lambda b, t: (0, 0)),
                  _const_spec((3 * wq, wq)), _const_spec((3 * wk, wk))],
        out_specs=[tile(widths[0]), tile(widths[1]), tile(widths[4]),
                   hm(nq), hm(nk), hm(nk, LANES), hm(nq), hm(nk), hm(nk, LANES)],
        compiler_params=_cparams("arbitrary", "arbitrary"),
        name="in_projection",
    )(xs, xs, xs, mods, norm_g.reshape(1, d), w_perm, conv_w, *rope,
      jnp.tile(q_gain, nq).reshape(1, wq), jnp.tile(k_gain, nk).reshape(1, wk),
      _head_block_diag(wq, 1.0 / HEAD_DIM), _head_block_diag(wk, 1.0 / HEAD_DIM))


def _halo_specs(width, n_tiles):
    per = ROW_TILE // SUBLANES
    prev = pl.BlockSpec((1, SUBLANES, width), lambda b, t: (b, jnp.maximum(t * per - 1, 0), 0))
    nxt = pl.BlockSpec((1, SUBLANES, width), lambda b, t: (b, jnp.minimum((t + 1) * per, n_tiles * per - 1), 0))
    return prev, nxt


def _short_conv_tile(z, prev8, next8, w, t, n_lat_tiles, n_tiles):
    first = jnp.logical_or(t == 0, t == n_lat_tiles)
    last = jnp.logical_or(t == n_lat_tiles - 1, t == n_tiles - 1)
    above = jnp.where(first, 0.0, prev8[SUBLANES - 1:SUBLANES, :])
    below = jnp.where(last, 0.0, next8[0:1, :])
    row = lax.broadcasted_iota(jnp.int32, z.shape, 0)
    zm1 = jnp.where(row == 0, above, pltpu.roll(z, 1, 0))
    zp1 = jnp.where(row == z.shape[0] - 1, below, pltpu.roll(z, z.shape[0] - 1, 0))
    return zm1 * w[0:1, :] + z * w[1:2, :] + zp1 * w[2:3, :]


def _outproj_kernel(a_lat_ref, a_ctx_ref, yf_ref, yb_ref, rvk_ref, rk_ref, lng_ref, lnb_ref, hmean_ref, hsum_ref,
                    c_ref, d_ref, gt_ref, x_ref, mod_ref, w_ref, fg_ref, o_ref, *, n_lat_tiles, final):
    is_ctx = pl.program_id(1) >= n_lat_tiles
    a = jnp.where(is_ctx, a_ctx_ref[0], a_lat_ref[0])
    b = _rwkv_readout_tile(yf_ref[0] + yb_ref[0], rvk_ref[0], rk_ref[...], lng_ref[...], lnb_ref[...],
                           hmean_ref[...], hsum_ref[...])
    mix = jnp.concatenate([a, b, c_ref[0], d_ref[0]], axis=-1) * gt_ref[0]
    y = jnp.dot(mix.astype(BF16), w_ref[...], preferred_element_type=F32)
    x = x_ref[0] + mod_ref[0, 0, 2:3, :] * y
    o_ref[0] = _rms(x, fg_ref[...]) if final else x


def _outproj(a_lat, a_ctx, yf, yb, rvk, r_k, ln_g, ln_b, cmix, dmix, gates, xs, mods, w_out, final_g,
             n_lat_tiles, final):
    bsz, s, d = xs.shape
    wb = a_lat.shape[-1]
    nt = n_lat_tiles if final else s // ROW_TILE
    tile = lambda wd: pl.BlockSpec((1, ROW_TILE, wd), lambda b, t: (b, t, 0))
    vec = pl.BlockSpec((1, wb), lambda b, t: (0, 0))
    head_tab = _const_spec((3 * wb, wb))
    return pl.pallas_call(
        functools.partial(_outproj_kernel, n_lat_tiles=n_lat_tiles, final=final),
        out_shape=jax.ShapeDtypeStruct((bsz, nt * ROW_TILE, d), F32),
        grid=(bsz, nt),
        in_specs=[pl.BlockSpec((1, ROW_TILE, wb), lambda b, t: (b, jnp.minimum(t, n_lat_tiles - 1), 0)),
                  pl.BlockSpec((1, ROW_TILE, wb), lambda b, t: (b, 0, 0)),
                  tile(wb), tile(wb), tile(3 * wb), vec, vec, vec, head_tab, head_tab,
                  tile(wb), tile(wb), tile(4 * wb), tile(d),
                  pl.BlockSpec((1, 1, 3, d), lambda b, t: (b, t // n_lat_tiles, 0, 0)),
                  _const_spec(w_out.shape),
                  pl.BlockSpec((1, d), lambda b, t: (0, 0))],
        out_specs=tile(d),
        compiler_params=_cparams("arbitrary", "arbitrary"),
        name="out_projection",
    )(a_lat, a_ctx, yf, yb, rvk, r_k.reshape(1, wb), ln_g.reshape(1, wb), ln_b.reshape(1, wb),
      _head_block_diag(wb, 1.0 / HEAD_DIM), _head_block_diag(wb, 1.0), cmix, dmix, gates, xs, mods, w_out, final_g.reshape(1, d))


def _rope_tables(seq_len, ctx_len):
    rows = seq_len // GRID_W
    row = jnp.repeat(jnp.arange(rows, dtype=F32), GRID_W)
    col = jnp.tile(jnp.arange(GRID_W, dtype=F32), rows)
    n_freq = HEAD_DIM // 4
    inv_freq = ROPE_THETA ** (-jnp.arange(n_freq, dtype=F32) / n_freq)
    ar, ac = row[:, None] * inv_freq, col[:, None] * inv_freq
    zero = jnp.zeros_like(ar)
    cos = jnp.concatenate([jnp.cos(ar), jnp.cos(ar), jnp.cos(ac), jnp.cos(ac)], axis=-1)
    sin_hi = jnp.concatenate([-jnp.sin(ar), zero, -jnp.sin(ac), zero], axis=-1)
    sin_lo = jnp.concatenate([zero, jnp.sin(ar), zero, jnp.sin(ac)], axis=-1)
    pad = lambda t, v: jnp.concatenate([t, jnp.full((ctx_len, HEAD_DIM), v, F32)], axis=0)
    return pad(cos, 1.0), pad(sin_hi, 0.0), pad(sin_lo, 0.0)


def _rope(x, cos, sin_hi, sin_lo):
    q = HEAD_DIM // 4
    w = x.shape[-1]
    return x * cos + pltpu.roll(x, w - q, 1) * sin_hi + pltpu.roll(x, q, 1) * sin_lo


def _head_mean_sq(x, bd3):
    return _mm_exact_rhs(x * x, bd3)


def _qk_emit(wa, fa, cos, shi, slo, q_gain, k_gain, bdq, bdk, qw_ref, kw_ref, vw_ref, qf_ref, kf_ref, vf_ref,
             wq, wk):
    nq, nk = wq // HEAD_DIM, wk // HEAD_DIM
    tab = lambda t, n: jnp.concatenate([t] * n, axis=-1)
    cq, hq, lq = tab(cos, nq), tab(shi, nq), tab(slo, nq)
    ck, hk, lk = tab(cos, nk), tab(shi, nk), tab(slo, nk)
    scale = HEAD_DIM ** -0.5

    def emit(ref, val, n):
        for h in range(n):
            ref[0, h] = val[:, h * HEAD_DIM:(h + 1) * HEAD_DIM].astype(ref.dtype)

    def emit_values(ref, v):
        lane = lax.broadcasted_iota(jnp.int32, (v.shape[0], LANES - HEAD_DIM), 1)
        ones_pad = jnp.where(lane == 0, 1.0, 0.0)
        for h in range(nk):
            ref[0, h] = jnp.concatenate([v[:, h * HEAD_DIM:(h + 1) * HEAD_DIM], ones_pad],
                                        axis=-1).astype(ref.dtype)

    emit(qw_ref, _rope(wa[:, :wq], cq, hq, lq) * scale, nq)
    emit(kw_ref, _rope(wa[:, wq:wq + wk], ck, hk, lk), nk)
    emit_values(vw_ref, wa[:, wq + wk:])
    q, k = fa[:, :wq], fa[:, wq:wq + wk]
    q = q * lax.rsqrt(_head_mean_sq(q, bdq) + NORM_EPS) * q_gain
    k = k * lax.rsqrt(_head_mean_sq(k, bdk) + NORM_EPS) * k_gain
    emit(qf_ref, _rope(q, cq, hq, lq) * scale, nq)
    emit(kf_ref, _rope(k, ck, hk, lk), nk)
    emit_values(vf_ref, fa[:, wq + wk:])


def _head_block_diag(width, value):
    h = np.arange(width) // HEAD_DIM
    bd = ((h[:, None] == h[None, :]) * value).astype(np.float32).astype(BF16)
    assert np.all(bd.astype(np.float32) == (h[:, None] == h[None, :]) * value)
    return np.concatenate([bd, bd, bd], axis=0)


def _window_attn_kernel(q_ref, k_ref, v_ref, sink_ref, o_ref, *, seq_len, ctx_len, tq):
    t = pl.program_id(2)
    n_lat = seq_len // tq
    g = q_ref.shape[1]
    sub = WINDOW
    band = 2 * WINDOW + sub
    kc, vc = k_ref[0, 0, seq_len:seq_len + ctx_len, :], v_ref[0, 0, seq_len:seq_len + ctx_len, :]
    nt_dot = lambda a, b: lax.dot_general(a, b, (((1,), (1,)), ((), ())), preferred_element_type=F32)
    for j in range(tq // sub):
        q = q_ref[0, :, j * sub:(j + 1) * sub, :].reshape(g * sub, HEAD_DIM)
        first = t * tq + j * sub
        start = pl.multiple_of(jnp.clip(first - WINDOW, 0, seq_len - band), WINDOW)
        kb, vb = k_ref[0, 0, pl.ds(start, band), :], v_ref[0, 0, pl.ds(start, band), :]
        s_ctx = nt_dot(q, kc)
        s_loc = nt_dot(q, kb)
        qpos = first + lax.broadcasted_iota(jnp.int32, (g, sub, band), 1).reshape(g * sub, band)
        kpos = start + lax.broadcasted_iota(jnp.int32, (g * sub, band), 1)
        valid = jnp.logical_and(jnp.abs(qpos - kpos) <= WINDOW, t < n_lat)
        s_loc = jnp.where(valid, s_loc, NEG_INF)
        sink = sink_ref[0, :, j * sub:(j + 1) * sub, :].reshape(g * sub, 1)
        m = jnp.maximum(jnp.maximum(jnp.max(s_ctx, axis=-1, keepdims=True),
                                    jnp.max(s_loc, axis=-1, keepdims=True)), sink)
        p_ctx, p_loc = jnp.exp(s_ctx - m), jnp.exp(s_loc - m)
        acc = (jnp.dot(p_ctx.astype(BF16), vc, preferred_element_type=F32)
               + jnp.dot(p_loc.astype(BF16), vb, preferred_element_type=F32))
        denom = acc[:, HEAD_DIM:HEAD_DIM + 1] + jnp.exp(sink - m)
        out = acc[:, :HEAD_DIM] / denom
        o_ref[0, j * sub:(j + 1) * sub, :] = jnp.concatenate(
            [out[h * sub:(h + 1) * sub] for h in range(g)], axis=-1)


def _window_attention(q, k, v, sink, seq_len, ctx_len):
    bsz, nq, s, _ = q.shape
    nkv = k.shape[1]
    g = nq // nkv
    tq = ROW_TILE
    sink_rows = jnp.broadcast_to(sink.astype(F32).reshape(nkv, g, 1, 1), (nkv, g, tq, 1))
    kv = lambda a: pl.BlockSpec((1, 1, s, a.shape[-1]), lambda b, h, t: (b, h, 0, 0))
    return pl.pallas_call(
        functools.partial(_window_attn_kernel, seq_len=seq_len, ctx_len=ctx_len, tq=tq),
        out_shape=jax.ShapeDtypeStruct((bsz, s, nq * HEAD_DIM), F32),
        grid=(bsz, nkv, s // tq),
        in_specs=[pl.BlockSpec((1, g, tq, HEAD_DIM), lambda b, h, t: (b, h, t, 0)), kv(k), kv(v),
                  pl.BlockSpec((1, g, tq, 1), lambda b, h, t: (h, 0, 0, 0))],
        out_specs=pl.BlockSpec((1, tq, g * HEAD_DIM), lambda b, h, t: (b, t, h)),
        compiler_params=_cparams("arbitrary", "arbitrary", "arbitrary"),
        name="window_attention",
    )(q, k, v, sink_rows)


DENSE_KEY_BLOCK = 1024


def _dense_attn_kernel(q_ref, k_ref, v_ref, o_ref, *, seq_len, tq, tk):
    t = pl.program_id(2)
    n_lat = seq_len // tq
    g = q_ref.shape[1]
    ctx_len = k_ref.shape[2] - seq_len
    q = q_ref[0].reshape(g * tq, HEAD_DIM)

    def step(carry, start, size):
        m, acc = carry
        s = lax.dot_general(q, k_ref[0, 0, start:start + size, :], (((1,), (1,)), ((), ())),
                            preferred_element_type=F32)
        m_new = jnp.maximum(m, jnp.max(s, axis=-1, keepdims=True))
        p = jnp.exp(s - m_new).astype(BF16)
        pv = jnp.dot(p, v_ref[0, 0, start:start + size, :], preferred_element_type=F32)
        return m_new, jnp.exp(m - m_new) * acc + pv

    def finish(carry):
        _, acc = carry
        out = acc[:, :HEAD_DIM] / acc[:, HEAD_DIM:HEAD_DIM + 1]
        o_ref[0] = jnp.concatenate([out[h * tq:(h + 1) * tq] for h in range(g)], axis=-1)

    init = (jnp.full((g * tq, 1), NEG_INF, F32), jnp.zeros((g * tq, v_ref.shape[-1]), F32))

    @pl.when(t < n_lat)
    def _():
        carry = step(init, seq_len, ctx_len)
        for j in range(seq_len // tk):
            carry = step(carry, j * tk, tk)
        finish(carry)

    @pl.when(t >= n_lat)
    def _():
        finish(step(init, seq_len, ctx_len))


def _dense_attention(q, k, v, seq_len):
    bsz, nq, s, _ = q.shape
    nkv = k.shape[1]
    g = nq // nkv
    tq = ROW_TILE
    tk = math.gcd(seq_len, DENSE_KEY_BLOCK)
    kv = lambda a: pl.BlockSpec((1, 1, s, a.shape[-1]), lambda b, h, t: (b, h, 0, 0))
    return pl.pallas_call(
        functools.partial(_dense_attn_kernel, seq_len=seq_len, tq=tq, tk=tk),
        out_shape=jax.ShapeDtypeStruct((bsz, s, nq * HEAD_DIM), F32),
        grid=(bsz, nkv, s // tq),
        in_specs=[pl.BlockSpec((1, g, tq, HEAD_DIM), lambda b, h, t: (b, h, t, 0)), kv(k), kv(v)],
        out_specs=pl.BlockSpec((1, tq, g * HEAD_DIM), lambda b, h, t: (b, t, h)),
        compiler_params=_cparams("arbitrary", "arbitrary", "arbitrary"),
        name="dense_attention",
    )(q, k, v)


def _softplus(x):
    return jnp.maximum(x, 0.0) + jnp.log(1.0 + jnp.exp(-jnp.abs(x)))


def _split(a):
    bits = lax.bitcast_convert_type(a, jnp.uint32) & jnp.uint32(0xFFFF0000)
    hi = lax.bitcast_convert_type(bits, F32)
    return hi, a - hi


def _rhs3(b):
    hi, lo = _split(b)
    return jnp.concatenate([hi, hi, lo], axis=-2).astype(BF16)


def _split3(a):
    hi, rest = _split(a)
    mid, lo = _split(rest)
    return hi, mid, lo


def _mm_exact_lhs(tbl3, x):
    return jnp.dot(tbl3, jnp.concatenate(_split3(x), axis=0).astype(BF16), preferred_element_type=F32)


def _mm_exact_rhs(x, tbl3):
    return jnp.dot(jnp.concatenate(_split3(x), axis=-1).astype(BF16), tbl3, preferred_element_type=F32)


def _mm_split(a, b):
    ah, al = _split(a)
    return jnp.dot(jnp.concatenate([ah, al, ah], axis=-1).astype(BF16), _rhs3(b), preferred_element_type=F32)


RW_DEPTH = 1


def _lhs4(a):
    return a.astype(BF16)


def _rhs4(b):
    return b.astype(BF16)


def _rhs4_nt(b):
    return b.astype(BF16)


def _bmm(lf, rf):
    return lax.dot_general(lf, rf, (((2,), (1,)), ((0,), (0,))), preferred_element_type=F32)


RW_SLOTS_PER_STEP = 16


def _rwkv_par_kernel(z_ref, w0_ref, wup_ref, a0_ref, aup_ref, kk_ref, ka_ref,
                     bd_ref, cum_ref,
                     rp3_ref, yvq_ref, rvk_ref,
                     lbig_ref, rbig_ref, at_ref, rt_ref, vv3_ref, bct_ref, kct_ref, wt_ref,
                     *, w, lora):
    tc = RW_CHUNK
    nc, nh = ROW_TILE // tc, w // HEAD_DIM
    nb = RW_SLOTS_PER_STEP
    z = z_ref[0]
    r, k, v = z[:, :w], z[:, w:2 * w], z[:, 2 * w:3 * w]
    w_low = jnp.tanh(z[:, 3 * w:3 * w + lora])
    a_low = z[:, 3 * w + lora:]
    kk = k * kk_ref[...]
    kk = kk / jnp.maximum(jnp.sqrt(_mm_exact_rhs(kk * kk, bd_ref[...])), 1e-12)
    ksum = jnp.zeros_like(k)
    for d in range(2):
        w_log = -_softplus(-(w0_ref[d:d + 1, :] + _mm_split(w_low, wup_ref[d]))) - 0.5
        lw = -jnp.exp(w_log)
        a = 1.0 / (1.0 + jnp.exp(-(a0_ref[d:d + 1, :] + _mm_split(a_low, aup_ref[d]))))
        kd = k * (1.0 + (a - 1.0) * ka_ref[...])
        ksum = ksum + kd
        sums = _mm_exact_lhs(cum_ref[d], lw)
        c, ctot = sums[:ROW_TILE], sums[ROW_TILE:]
        e_neg, e_rem = jnp.exp(-c), jnp.exp(ctot - c)
        b = kk * a
        at, rt = -kk * jnp.exp(c - lw), r * jnp.exp(c)
        w_tot = jnp.exp(ctot)
        bh, kh = b * e_neg, kd * e_neg
        bct, kct = (b * e_rem).T, (kd * e_rem).T
        for ci in range(nc):
            for h in range(nh):
                slot = (d * nc + ci) * nh + h
                rows, cols = slice(ci * tc, (ci + 1) * tc), slice(h * HEAD_DIM, (h + 1) * HEAD_DIM)
                lbig_ref[slot, :tc, :] = _lhs4(at[rows, cols])
                lbig_ref[slot, tc:, :] = _lhs4(rt[rows, cols])
                rbig_ref[slot, :tc, :] = _rhs4_nt(bh[rows, cols])
                rbig_ref[slot, tc:, :] = _rhs4_nt(kh[rows, cols])
                at_ref[slot] = at[rows, cols]
                rt_ref[slot] = rt[rows, cols]
                vv3_ref[slot] = _rhs4(v[rows, cols])
                bct_ref[slot] = _lhs4(bct[cols, rows])
                kct_ref[slot] = _lhs4(kct[cols, rows])
                wt_ref[slot] = w_tot[ci * tc:ci * tc + SUBLANES, cols]
    rvk_ref[0] = jnp.concatenate([r, v, ksum], axis=-1)

    row = lax.broadcasted_iota(jnp.int32, (tc, tc), 0)
    col = lax.broadcasted_iota(jnp.int32, (tc, tc), 1)
    eye = row == col

    def level_mask(s):
        return jnp.logical_and(row // (2 * s) == col // (2 * s), row // s != col // s)

    steps_per_dir = nc * nh // nb

    def chunk_group(g, carry):
        slots = pl.ds(pl.multiple_of(g * nb, nb), nb)
        d = g // steps_per_dir
        fwd = d == 0
        before = jnp.logical_or(jnp.logical_and(fwd, row > col),
                                jnp.logical_and(jnp.logical_not(fwd), row < col))
        upto = jnp.logical_or(before, eye)
        big = lax.dot_general(lbig_ref[slots], rbig_ref[slots], (((2,), (2,)), ((0,), (0,))),
                              preferred_element_type=F32)
        a_ab = jnp.where(before, big[:, :tc, :tc], 0.0)
        a_ak = jnp.where(before, big[:, :tc, tc:], 0.0)
        a_rb = jnp.where(upto, big[:, tc:, :tc], 0.0)
        a_rk = jnp.where(upto, big[:, tc:, tc:], 0.0)
        x = jnp.where(eye, 1.0, jnp.where(level_mask(1), a_ab, 0.0))
        s = 2
        while s < tc:
            half = _bmm(_lhs4(x), _rhs4(jnp.where(level_mask(s), a_ab, 0.0)))
            x = x + _bmm(_lhs4(half), _rhs4(x))
            s *= 2
        vv3 = vv3_ref[slots]
        akrk = _bmm(jnp.concatenate([_lhs4(a_ak), _lhs4(a_rk)], axis=1), vv3)
        xa3 = _rhs4(_bmm(_lhs4(x), _rhs4(jnp.concatenate([at_ref[slots], akrk[:, :tc]], axis=-1))))
        ra = _bmm(_lhs4(a_rb), xa3)
        pq = _bmm(bct_ref[slots], xa3)
        rp = rt_ref[slots] + ra[:, :, :HEAD_DIM]
        yv = ra[:, :, HEAD_DIM:] + akrk[:, tc:]
        p = jnp.where(eye[:HEAD_DIM, :HEAD_DIM], wt_ref[slots][:, 0:1, :], 0.0) + pq[:, :, :HEAD_DIM]
        q = pq[:, :, HEAD_DIM:] + _bmm(kct_ref[slots], vv3)
        rp3 = jnp.concatenate([_lhs4(rp), _lhs4(p)], axis=1)
        yvq = jnp.concatenate([yv, q], axis=1)
        for j0 in range(0, nb, nh):
            ci = (g % steps_per_dir) * (nb // nh) + j0 // nh
            rp3_ref[0, d, ci] = jnp.concatenate([rp3[j0 + h] for h in range(nh)], axis=-1)
            yvq_ref[0, d, ci] = jnp.concatenate([yvq[j0 + h] for h in range(nh)], axis=-1)
        return carry
    lax.fori_loop(0, 2 * steps_per_dir, chunk_group, 0, unroll=True)


def _chunk_matrices():
    t = np.arange(ROW_TILE)
    same = (t[:, None] // RW_CHUNK) == (t[None, :] // RW_CHUNK)
    tabs = []
    for run in (same & (t[None, :] <= t[:, None]), same & (t[None, :] >= t[:, None])):
        m = np.concatenate([run, same], axis=0).astype(np.float32).astype(BF16)
        tabs.append(np.concatenate([m, m, m], axis=1))
    return np.stack(tabs)


def _rwkv_par(rw, w0, w_up, a0, a_up, k_k, k_a, w):
    bsz, s, width = rw.shape
    nt = s // ROW_TILE
    nh, nc = w // HEAD_DIM, ROW_TILE // RW_CHUNK
    lora = w_up.shape[1]
    full = lambda a: pl.BlockSpec(a.shape, lambda b, t: (0,) * a.ndim)
    cum = _chunk_matrices()
    bd = _head_block_diag(w, 1.0)
    vec = lambda a: a.reshape(1, w)
    n_chunks = s // RW_CHUNK
    nslots = 2 * nc * nh
    tc, hd = RW_CHUNK, HEAD_DIM
    consts = (w0, w_up, a0, a_up, vec(k_k), vec(k_a), bd, cum)
    return pl.pallas_call(
        functools.partial(_rwkv_par_kernel, w=w, lora=lora),
        out_shape=[jax.ShapeDtypeStruct((bsz, 2, n_chunks, tc + hd, RW_DEPTH * w), BF16),
                   jax.ShapeDtypeStruct((bsz, 2, n_chunks, tc + hd, w), F32),
                   jax.ShapeDtypeStruct((bsz, s, 3 * w), F32)],
        grid=(bsz, nt),
        in_specs=[pl.BlockSpec((1, ROW_TILE, width), lambda b, t: (b, t, 0))] + [full(a) for a in consts],
        out_specs=[pl.BlockSpec((1, 2, nc, tc + hd, RW_DEPTH * w), lambda b, t: (b, 0, t, 0, 0)),
                   pl.BlockSpec((1, 2, nc, tc + hd, w), lambda b, t: (b, 0, t, 0, 0)),
                   pl.BlockSpec((1, ROW_TILE, 3 * w), lambda b, t: (b, t, 0))],
        scratch_shapes=[pltpu.VMEM((nslots, 2 * tc, RW_DEPTH * hd), BF16),
                        pltpu.VMEM((nslots, 2 * tc, RW_DEPTH * hd), BF16),
                        pltpu.VMEM((nslots, tc, hd), F32),
                        pltpu.VMEM((nslots, tc, hd), F32),
                        pltpu.VMEM((nslots, RW_DEPTH * tc, hd), BF16),
                        pltpu.VMEM((nslots, hd, RW_DEPTH * tc), BF16),
                        pltpu.VMEM((nslots, hd, RW_DEPTH * tc), BF16),
                        pltpu.VMEM((nslots, SUBLANES, HEAD_DIM), F32)],
        compiler_params=_cparams("arbitrary", "arbitrary"),
        name="rwkv_chunk_prep",
    )(rw, *consts)


def _rwkv_seq_kernel(rp3f_ref, yvqf_ref, rp3b_ref, yvqb_ref, yf_ref, yb_ref, g_ref):
    @pl.when(pl.program_id(1) == 0)
    def _():
        g_ref[...] = jnp.zeros_like(g_ref)

    w = g_ref.shape[-1]
    tc = RW_CHUNK
    nh = w // HEAD_DIM
    grp = rp3f_ref.shape[2]
    row_head = lax.broadcasted_iota(jnp.int32, (w, w), 0) // HEAD_DIM
    col_head = lax.broadcasted_iota(jnp.int32, (w, w), 1) // HEAD_DIM
    on_diag = row_head == col_head
    for step in range(grp):
        for d, (rp3, yvq, y) in enumerate(((rp3f_ref, yvqf_ref, yf_ref), (rp3b_ref, yvqb_ref, yb_ref))):
            ci = step if d == 0 else grp - 1 - step
            out = jnp.dot(rp3[0, 0, ci], _rhs4(g_ref[d]), preferred_element_type=F32) + yvq[0, 0, ci]
            y[0, ci * tc:(ci + 1) * tc, :] = out[:tc]
            g_ref[d] = jnp.where(on_diag, jnp.concatenate([out[tc:]] * nh, axis=0), 0.0)


RW_SCAN_GROUP = 4


def _rwkv_seq(rp3, yvq, n_lat_chunks):
    bsz, _, n_chunks = rp3.shape[:3]
    w = yvq.shape[-1]
    grp = RW_SCAN_GROUP
    assert n_lat_chunks % grp == 0 and n_chunks % grp == 0
    n_groups, n_lat, n_ctx = n_chunks // grp, n_lat_chunks // grp, (n_chunks - n_lat_chunks) // grp
    order = (lambda i: jnp.where(i < n_ctx, n_lat + i, i - n_ctx),
             lambda i: n_groups - 1 - i)
    blk = lambda d, a: pl.BlockSpec((1, 1, grp) + a.shape[3:], lambda b, i: (b, d, order[d](i), 0, 0))
    out = lambda d: pl.BlockSpec((1, grp * RW_CHUNK, w), lambda b, i: (b, order[d](i), 0))
    shp = jax.ShapeDtypeStruct((bsz, n_chunks * RW_CHUNK, w), F32)
    return pl.pallas_call(
        _rwkv_seq_kernel,
        out_shape=[shp, shp],
        grid=(bsz, n_groups),
        in_specs=[blk(0, rp3), blk(0, yvq), blk(1, rp3), blk(1, yvq)],
        out_specs=[out(0), out(1)],
        scratch_shapes=[pltpu.VMEM((2, w, w), F32)],
        compiler_params=_cparams("arbitrary", "arbitrary"),
        name="rwkv_state_scan",
    )(rp3, yvq, rp3, yvq)


def _rwkv_readout_tile(y, rvk, r_k, ln_g, ln_b, head_mean, head_sum):
    w = r_k.shape[-1]
    r, v, ksum = rvk[:, :w], rvk[:, w:2 * w], rvk[:, 2 * w:]
    yc = y - _mm_exact_rhs(y, head_mean)
    var = _mm_exact_rhs(yc * yc, head_mean)
    bonus = _mm_exact_rhs(r * ksum * r_k, head_sum) * v
    return yc * lax.rsqrt(var + RW_GN_EPS) * ln_g + ln_b + bonus


def _hyena_two_sided_filters(seq_len, fw1, fb1, freq, fw2, fb2, fw3, width):
    bands = (fw1.shape[0] - 1) // 2
    mm = functools.partial(jnp.matmul, precision=HI)
    n = jnp.arange(2 * seq_len)
    pos = jnp.where(n < seq_len, n, 2 * seq_len - n) % seq_len
    is_fwd = (n < seq_len)[:, None, None]
    live = (n != seq_len)[:, None, None]
    t = jnp.linspace(0.0, 1.0, seq_len, dtype=F32)[pos][:, None]
    wpos = (2.0 * math.pi / seq_len) * pos.astype(F32)[:, None]
    f = jnp.linspace(1e-4, bands - 1, bands, dtype=F32)[None, :]
    z = jnp.concatenate([t, jnp.cos(f * wpos), jnp.sin(f * wpos)], axis=-1)
    h = jnp.sin(freq * (mm(z, fw1) + fb1))
    h = jnp.sin(freq * (mm(h, fw2) + fb2))
    h = mm(h, fw3).reshape(2 * seq_len, 2, 2, width)
    max_decay = math.log(HY_DECAY_TARGET) / HY_FAST_DECAY
    min_decay = math.log(HY_DECAY_TARGET) / HY_SLOW_DECAY
    deltas = jnp.linspace(min_decay, max_decay, width, dtype=F32)
    h = jnp.where(is_fwd, h[:, :, 0], h[:, :, 1]) * jnp.exp(-t * jnp.abs(deltas))[:, None, :]
    norm = jnp.sum(jnp.abs(h), axis=0, keepdims=True)
    return jnp.moveaxis(jnp.where(live, h / norm, 0.0), 1, 0)


def kernel(x, c, ctx, c_ctx, mod_w, mod_b, norm_g, w_in, w_out, hy_conv, hy_fw1, hy_fb1, hy_freq, hy_fw2,
           hy_fb2, hy_fw3, hy_bias, rw_conv, rw_w0, rw_w_up, rw_a0, rw_a_up, rw_k_k, rw_k_a, rw_r_k,
           rw_ln_g, rw_ln_b, wa_sink, fa_q_norm, fa_k_norm, final_g):
    bsz, seq_len, d = x.shape
    ctx_len = ctx.shape[1]
    depth = w_in.shape[0]
    w_hy = hy_bias.shape[-1]
    w_rw = rw_w0.shape[-1]
    n_wa_heads = wa_sink.shape[-1]
    w_q = n_wa_heads * HEAD_DIM
    w_kv = w_q // 2
    lora = rw_w_up.shape[2] + rw_a_up.shape[2]
    branch_w = (3 * w_hy, 3 * w_rw + lora, w_q + 2 * w_kv, w_q + 2 * w_kv)
    gate_w = (w_hy, w_rw, w_q, w_q)
    assert seq_len % ROW_TILE == 0 and ctx_len % ROW_TILE == 0 and bsz % 2 == 0
    n_lat_tiles = seq_len // ROW_TILE

    starts = np.cumsum([0] + [bw + gw for bw, gw in zip(branch_w, gate_w)])
    cols = np.concatenate([np.arange(s0, s0 + bw) for s0, bw in zip(starts, branch_w)]
                          + [np.arange(s0 + bw, s0 + bw + gw) for s0, bw, gw in zip(starts, branch_w, gate_w)])
    w_in_p = w_in[:, :, cols].astype(BF16)
    w_out_b = w_out.astype(BF16)
    widths = branch_w + (sum(gate_w),)

    pad_rows = (-(bsz + 1)) % 8
    cond = jnp.concatenate([c, c_ctx[None], jnp.zeros((pad_rows, d), F32)], axis=0)
    mod = _modulation(cond, mod_w, mod_b)
    mod_lat = mod[:, :bsz].reshape(depth, bsz, 3, d)
    mod_ctx = jnp.broadcast_to(mod[:, bsz].reshape(depth, 1, 3, d), (depth, bsz, 3, d))
    mods = jnp.stack([mod_lat, mod_ctx], axis=2)

    rope = _rope_tables(seq_len, ctx_len)
    xs = jnp.concatenate([x, ctx], axis=1)
    for l in range(depth):
        last = l == depth - 1
        conv_w = jnp.concatenate([hy_conv[l], rw_conv[l]], axis=-1)
        hyc, rwc, gates, qw, kw, vw, qf, kf, vf = _inproj(
            xs, mods[l], norm_g[l], w_in_p[l], conv_w, widths, n_lat_tiles, rope,
            fa_q_norm[l], fa_k_norm[l], w_q, w_kv)

        filt = functools.partial(_hyena_two_sided_filters, fw1=hy_fw1[l], fb1=hy_fb1[l], freq=hy_freq[l],
                                 fw2=hy_fw2[l], fb2=hy_fb2[l], fw3=hy_fw3[l], width=w_hy)
        bias = hy_bias[l].reshape(2, 1, w_hy)
        spec = _filter_spectrum(filt(seq_len))
        nt_hy = w_hy // LANES
        y1 = _fftconv_gated(hyc, 0, hyc, nt_hy, spec, 0, bias, seq_len)
        a_lat = _fftconv_gated(y1, 0, hyc, 2 * nt_hy, spec, 1, bias, seq_len)
        a_ctx = a_lat if last else _ctx_hyena(hyc, seq_len // ctx_len, filt(ctx_len), bias, ctx_len, w_hy)

        rp3, yvq, rvk = _rwkv_par(rwc, rw_w0[l], rw_w_up[l], rw_a0[l], rw_a_up[l], rw_k_k[l], rw_k_a[l], w_rw)
        yf, yb = _rwkv_seq(rp3, yvq, seq_len // RW_CHUNK)

        c_mix = _window_attention(qw, kw, vw, wa_sink[l], seq_len, ctx_len)
        d_mix = _dense_attention(qf, kf, vf, seq_len)

        xs = _outproj(a_lat, a_ctx, yf, yb, rvk, rw_r_k[l], rw_ln_g[l], rw_ln_b[l], c_mix, d_mix, gates, xs,
                      mods[l], w_out_b[l], final_g, n_lat_tiles, last)
    return xs
```

```python
import functools
import math

import jax
import jax.numpy as jnp
import numpy as np
from jax import lax
from jax.experimental import pallas as pl
from jax.experimental.pallas import tpu as pltpu

HEAD_DIM = 64
GRID_W = 64
WINDOW = 128
NORM_EPS = 1e-6
RW_GN_EPS = 64e-5
NEG_INF = -1e30
ROPE_THETA = 10000.0
HY_FAST_DECAY = 0.3
HY_SLOW_DECAY = 1.5
HY_DECAY_TARGET = 1e-2

ROW_TILE = 256
LANES = 128
SUBLANES = 8
FFT_N1 = 64
FFT_UNROLL = 8
FFT_PLANE_PAD = 8
RW_CHUNK = 64
VMEM_LIMIT = 60 * 1024 * 1024

F32 = jnp.float32
BF16 = jnp.bfloat16
HI = lax.Precision.HIGHEST


def _dot(a, b):
    return jnp.dot(a, b, preferred_element_type=F32, precision=HI)


def _dot_nt(a, b):
    return lax.dot_general(a, b, (((1,), (1,)), ((), ())), preferred_element_type=F32, precision=HI)


def _dot_tn(a, b):
    return lax.dot_general(a, b, (((0,), (0,)), ((), ())), preferred_element_type=F32, precision=HI)


def _cparams(*sem):
    return pltpu.CompilerParams(dimension_semantics=sem, vmem_limit_bytes=VMEM_LIMIT)


def _const_spec(shape):
    return pl.BlockSpec(shape, lambda *_: (0,) * len(shape), pipeline_mode=pl.Buffered(1))


def _silu(x):
    return x * (1.0 / (1.0 + jnp.exp(-x)))


def _mod_kernel(c_ref, w_ref, b_ref, o_ref):
    o_ref[0] = _dot(_silu(c_ref[...]), w_ref[0]) + b_ref[0]


def _modulation(cond, mod_w, mod_b):
    depth, d, d3 = mod_w.shape
    rows = cond.shape[0]
    return pl.pallas_call(
        _mod_kernel,
        out_shape=jax.ShapeDtypeStruct((depth, rows, d3), F32),
        grid=(depth,),
        in_specs=[pl.BlockSpec((rows, d), lambda l: (0, 0)),
                  pl.BlockSpec((1, d, d3), lambda l: (l, 0, 0)),
                  pl.BlockSpec((1, 1, d3), lambda l: (l, 0, 0))],
        out_specs=pl.BlockSpec((1, rows, d3), lambda l: (l, 0, 0)),
        compiler_params=_cparams("arbitrary"),
        name="modulation",
    )(cond, mod_w, mod_b.reshape(depth, 1, d3))


def _blockreal(m):
    return np.block([[m.real, -m.imag], [m.imag, m.real]])


@functools.lru_cache(maxsize=None)
def _fft_tables(seq_len):
    n = 2 * seq_len
    n1, n2 = FFT_N1, n // FFT_N1
    h1 = n1 // 2
    j2 = np.arange(n2)[:, None, None]
    k1 = np.arange(n1)[None, :, None]
    t1 = np.exp(-2j * np.pi * (j2 * k1 / n + k1 * np.arange(n1)[None, None, :] / n1))
    t1_data = np.stack([_blockreal(t1[j][:, :h1]) for j in range(n2)])
    t1_real = np.concatenate([t1.real, t1.imag], axis=1)
    f2 = np.exp(-2j * np.pi * np.outer(np.arange(n2), np.arange(n2)) / n2)
    f2_fwd = _blockreal(f2)
    f2_inv = _blockreal(np.conj(f2))
    t4 = np.exp(2j * np.pi * (np.arange(h1)[None, :, None] * np.arange(n1)[None, None, :] / n1
                              + j2 * np.arange(n1)[None, None, :] / n)) / n
    t4 = np.stack([_blockreal(t4[j]) for j in range(n2)])
    return tuple(_lhs3_table(t) for t in (t1_data, t1_real, f2_fwd, f2_inv, t4))


def _lhs3_table(m):
    hi = m.astype(np.float32).astype(BF16)
    lo = (m - hi.astype(np.float64)).astype(np.float32).astype(BF16)
    return np.concatenate([hi, lo, hi], axis=-1)


def _mm3(tbl3, x):
    return jnp.dot(tbl3, _rhs3(x), preferred_element_type=F32)


def _spectrum_kernel(k_ref, t1_ref, f2_ref, o_ref, a_ref, *, n1, n2):
    def stage1(j, carry):
        rows = k_ref[0, pl.ds(j, n1, stride=n2), :]
        a_ref[pl.ds(j, 2 * n1, stride=n2 + FFT_PLANE_PAD), :] = _mm3(t1_ref[j], rows)
        return carry
    lax.fori_loop(0, n2, stage1, 0, unroll=FFT_UNROLL)

    def stage2(i, carry):
        pitch = n2 + FFT_PLANE_PAD
        re = a_ref[pl.ds(pl.multiple_of(i * pitch, 8), n2), :]
        im = a_ref[pl.ds(pl.multiple_of((n1 + i) * pitch, 8), n2), :]
        o_ref[0, i] = _mm3(f2_ref[...], jnp.concatenate([re, im], axis=0))
        return carry
    lax.fori_loop(0, n1, stage2, 0, unroll=2)


def _filter_spectrum(kfilt):
    g, n, w = kfilt.shape
    n1, n2 = FFT_N1, n // FFT_N1
    _, t1_real, f2_fwd, _, _ = _fft_tables(n // 2)
    const = _const_spec
    return pl.pallas_call(
        functools.partial(_spectrum_kernel, n1=n1, n2=n2),
        out_shape=jax.ShapeDtypeStruct((g, n1, 2 * n2, w), F32),
        grid=(g, w // LANES),
        in_specs=[pl.BlockSpec((1, n, LANES), lambda gi, j: (gi, 0, j)),
                  const(t1_real.shape), const(f2_fwd.shape)],
        out_specs=pl.BlockSpec((1, n1, 2 * n2, LANES), lambda gi, j: (gi, 0, 0, j)),
        scratch_shapes=[pltpu.VMEM((2 * n1 * (n2 + FFT_PLANE_PAD), LANES), F32)],
        compiler_params=_cparams("arbitrary", "arbitrary"),
        name="hyena_filter_spectrum",
    )(kfilt, t1_real, f2_fwd)


def _fftconv_kernel(u_ref, m_ref, spec_ref, bias_ref, t1_ref, f2f_ref, f2i_ref, t4_ref, o_ref, a_ref,
                    *, n1, n2):
    h1 = n1 // 2

    def stage1(j, carry):
        za = u_ref[0, pl.ds(j, h1, stride=n2), :]
        zb = u_ref[1, pl.ds(j, h1, stride=n2), :]
        a_ref[pl.ds(j, 2 * n1, stride=n2 + FFT_PLANE_PAD), :] = _mm3(t1_ref[j], jnp.concatenate([za, zb], axis=0))
        return carry
    lax.fori_loop(0, n2, stage1, 0, unroll=FFT_UNROLL)

    def stage2(i, carry):
        pitch = n2 + FFT_PLANE_PAD
        re_rows = pl.ds(pl.multiple_of(i * pitch, 8), n2)
        im_rows = pl.ds(pl.multiple_of((n1 + i) * pitch, 8), n2)
        x = _mm3(f2f_ref[...], jnp.concatenate([a_ref[re_rows, :], a_ref[im_rows, :]], axis=0))
        xr, xi = x[:n2], x[n2:]
        kr, ki = spec_ref[0, i, :n2, :], spec_ref[0, i, n2:, :]
        y = jnp.concatenate([xr * kr - xi * ki, xr * ki + xi * kr], axis=0)
        b = _mm3(f2i_ref[...], y)
        a_ref[re_rows, :] = b[:n2]
        a_ref[im_rows, :] = b[n2:]
        return carry
    lax.fori_loop(0, n1, stage2, 0, unroll=2)

    bias = bias_ref[0]

    def stage4(j, carry):
        y = _mm3(t4_ref[j], a_ref[pl.ds(j, 2 * n1, stride=n2 + FFT_PLANE_PAD), :])
        rows = pl.ds(j, h1, stride=n2)
        for p in range(2):
            u = u_ref[p, rows, :]
            o_ref[p, rows, :] = m_ref[p, rows, :] * (y[p * h1:(p + 1) * h1] + bias * u)
        return carry
    lax.fori_loop(0, n2, stage4, 0, unroll=FFT_UNROLL)


def _fftconv_gated(u, u_col, mult, mult_col, spec, conv_idx, bias, seq_len):
    bsz = u.shape[0]
    w = spec.shape[-1]
    n = 2 * seq_len
    n1, n2 = FFT_N1, n // FFT_N1
    t1_data, _, f2_fwd, f2_inv, t4 = _fft_tables(seq_len)
    const = _const_spec
    return pl.pallas_call(
        functools.partial(_fftconv_kernel, n1=n1, n2=n2),
        out_shape=jax.ShapeDtypeStruct((bsz, seq_len, w), F32),
        grid=(w // LANES, bsz // 2),
        in_specs=[pl.BlockSpec((2, seq_len, LANES), lambda j, p: (p, 0, u_col + j)),
                  pl.BlockSpec((2, seq_len, LANES), lambda j, p: (p, 0, mult_col + j)),
                  pl.BlockSpec((1, n1, 2 * n2, LANES), lambda j, p: (conv_idx, 0, 0, j),
                               pipeline_mode=pl.Buffered(1)),
                  pl.BlockSpec((1, 1, LANES), lambda j, p: (conv_idx, 0, j)),
                  const(t1_data.shape), const(f2_fwd.shape), const(f2_inv.shape), const(t4.shape)],
        out_specs=pl.BlockSpec((2, seq_len, LANES), lambda j, p: (p, 0, j)),
        scratch_shapes=[pltpu.VMEM((2 * n1 * (n2 + FFT_PLANE_PAD), LANES), F32)],
        compiler_params=_cparams("arbitrary", "arbitrary"),
        name="hyena_fftconv",
    )(u, mult, spec, bias, t1_data, f2_fwd, f2_inv, t4)


@functools.lru_cache(maxsize=None)
def _small_fft_tables(seq_len):
    n = 2 * seq_len
    f = np.exp(-2j * np.pi * np.outer(np.arange(n), np.arange(n)) / n)
    fwd = _blockreal(f[:, :seq_len])
    real = np.concatenate([f.real, f.imag], axis=0)
    inv = _blockreal(np.conj(f)[:seq_len, :] / n)
    as32 = lambda a: jnp.asarray(a, dtype=F32)
    return as32(fwd), as32(real), as32(inv)


def _ctx_hyena_kernel(v_ref, x1_ref, x2_ref, k_ref, bias_ref, fwd_ref, real_ref, inv_ref, o_ref, *, seq_len):
    n = 2 * seq_len

    def conv(ua, ub, g):
        spec = _dot(real_ref[...], k_ref[g])
        x = _dot(fwd_ref[...], jnp.concatenate([ua, ub], axis=0))
        xr, xi, kr, ki = x[:n], x[n:], spec[:n], spec[n:]
        y = _dot(inv_ref[...], jnp.concatenate([xr * kr - xi * ki, xr * ki + xi * kr], axis=0))
        b = bias_ref[g]
        return y[:seq_len] + b * ua, y[seq_len:] + b * ub

    c1a, c1b = conv(v_ref[0], v_ref[1], 0)
    y1a, y1b = x1_ref[0] * c1a, x1_ref[1] * c1b
    c2a, c2b = conv(y1a, y1b, 1)
    o_ref[0] = x2_ref[0] * c2a
    o_ref[1] = x2_ref[1] * c2b


def _ctx_hyena(hyc, row_block, kfilt, bias, seq_len, w):
    bsz = hyc.shape[0]
    n = 2 * seq_len
    fwd, real, inv = _small_fft_tables(seq_len)
    nt = w // LANES
    col = lambda c0: pl.BlockSpec((2, seq_len, LANES), lambda j, p: (p, row_block, c0 + j))
    return pl.pallas_call(
        functools.partial(_ctx_hyena_kernel, seq_len=seq_len),
        out_shape=jax.ShapeDtypeStruct((bsz, seq_len, w), F32),
        grid=(nt, bsz // 2),
        in_specs=[col(0), col(nt), col(2 * nt),
                  pl.BlockSpec((2, n, LANES), lambda j, p: (0, 0, j)),
                  pl.BlockSpec((2, 1, LANES), lambda j, p: (0, 0, j)),
                  _const_spec((2 * n, 2 * seq_len)), _const_spec((2 * n, n)), _const_spec((2 * seq_len, 2 * n))],
        out_specs=pl.BlockSpec((2, seq_len, LANES), lambda j, p: (p, 0, j)),
        compiler_params=_cparams("arbitrary", "arbitrary"),
        name="hyena_ctx",
    )(hyc, hyc, hyc, kfilt, bias, fwd, real, inv)


def _rms(x, g):
    return x * lax.rsqrt(jnp.mean(x * x, axis=-1, keepdims=True) + NORM_EPS) * g


def _inproj_kernel(x_ref, xprev_ref, xnext_ref, mod_ref, g_ref, w_ref, cw_ref,
                   cos_ref, shi_ref, slo_ref, qg_ref, kg_ref, bdq_ref, bdk_ref,
                   hy_ref, rw_ref, gt_ref, qw_ref, kw_ref, vw_ref, qf_ref, kf_ref, vf_ref,
                   *, widths, wq, wk, n_lat_tiles, n_tiles):
    shift, scale = mod_ref[0, 0, 0:1, :], mod_ref[0, 0, 1:2, :]
    norm_mod = lambda x: (_rms(x, g_ref[...]) * (1.0 + scale) + shift).astype(BF16)
    z = jnp.dot(norm_mod(x_ref[0]), w_ref[...], preferred_element_type=F32)
    offs = np.cumsum((0,) + tuple(widths))
    part = lambda i: z[:, offs[i]:offs[i + 1]]
    n_conv = offs[2]
    halo = jnp.dot(norm_mod(jnp.concatenate([xprev_ref[0], xnext_ref[0]], axis=0)), w_ref[:, :n_conv],
                   preferred_element_type=F32)
    conv = _short_conv_tile(z[:, :n_conv], halo[:SUBLANES], halo[SUBLANES:], cw_ref[...], pl.program_id(1),
                            n_lat_tiles, n_tiles)
    hy_ref[0] = conv[:, :offs[1]]
    rw_ref[0] = conv[:, offs[1]:]
    gt_ref[0] = _silu(part(4))
    _qk_emit(part(2), part(3), cos_ref[...], shi_ref[...], slo_ref[...], qg_ref[...], kg_ref[...],
             bdq_ref[...], bdk_ref[...], qw_ref, kw_ref, vw_ref, qf_ref, kf_ref, vf_ref, wq, wk)


def _inproj(xs, mods, norm_g, w_perm, conv_w, widths, n_lat_tiles, rope, q_gain, k_gain, wq, wk):
    bsz, s, d = xs.shape
    nt = s // ROW_TILE
    nq, nk = wq // HEAD_DIM, wk // HEAD_DIM
    prev, nxt = _halo_specs(d, nt)
    tile = lambda wd: pl.BlockSpec((1, ROW_TILE, wd), lambda b, t: (b, t, 0))
    tab = pl.BlockSpec((ROW_TILE, HEAD_DIM), lambda b, t: (t, 0))
    hm = lambda n, wd=HEAD_DIM: pl.BlockSpec((1, n, ROW_TILE, wd), lambda b, t: (b, 0, t, 0))
    flat = lambda wd: jax.ShapeDtypeStruct((bsz, s, wd), F32)
    heads = lambda n, wd=HEAD_DIM: jax.ShapeDtypeStruct((bsz, n, s, wd), BF16)
    return pl.pallas_call(
        functools.partial(_inproj_kernel, widths=widths, wq=wq, wk=wk, n_lat_tiles=n_lat_tiles, n_tiles=nt),
        out_shape=[flat(widths[0]), flat(widths[1]), flat(widths[4]),
                   heads(nq), heads(nk), heads(nk, LANES), heads(nq), heads(nk), heads(nk, LANES)],
        grid=(bsz, nt),
        in_specs=[tile(d), prev, nxt,
                  pl.BlockSpec((1, 1, 3, d), lambda b, t: (b, t // n_lat_tiles, 0, 0)),
                  pl.BlockSpec((1, d), lambda b, t: (0, 0)),
                  _const_spec(w_perm.shape), pl.BlockSpec(conv_w.shape, lambda b, t: (0, 0)), tab, tab, tab,
                  pl.BlockSpec((1, wq), lambda b, t: (0, 0)), pl.BlockSpec((1, wk), lambda b, t: (0, 0)),
                  _const_spec((3 * wq, wq)), _const_spec((3 * wk, wk))],
        out_specs=[tile(widths[0]), tile(widths[1]), tile(widths[4]),
                   hm(nq), hm(nk), hm(nk, LANES), hm(nq), hm(nk), hm(nk, LANES)],
        compiler_params=_cparams("arbitrary", "arbitrary"),
        name="in_projection",
    )(xs, xs, xs, mods, norm_g.reshape(1, d), w_perm, conv_w, *rope,
      jnp.tile(q_gain, nq).reshape(1, wq), jnp.tile(k_gain, nk).reshape(1, wk),
      _head_block_diag(wq, 1.0 / HEAD_DIM), _head_block_diag(wk, 1.0 / HEAD_DIM))


def _halo_specs(width, n_tiles):
    per = ROW_TILE // SUBLANES
    prev = pl.BlockSpec((1, SUBLANES, width), lambda b, t: (b, jnp.maximum(t * per - 1, 0), 0))
    nxt = pl.BlockSpec((1, SUBLANES, width), lambda b, t: (b, jnp.minimum((t + 1) * per, n_tiles * per - 1), 0))
    return prev, nxt


def _short_conv_tile(z, prev8, next8, w, t, n_lat_tiles, n_tiles):
    first = jnp.logical_or(t == 0, t == n_lat_tiles)
    last = jnp.logical_or(t == n_lat_tiles - 1, t == n_tiles - 1)
    above = jnp.where(first, 0.0, prev8[SUBLANES - 1:SUBLANES, :])
    below = jnp.where(last, 0.0, next8[0:1, :])
    row = lax.broadcasted_iota(jnp.int32, z.shape, 0)
    zm1 = jnp.where(row == 0, above, pltpu.roll(z, 1, 0))
    zp1 = jnp.where(row == z.shape[0] - 1, below, pltpu.roll(z, z.shape[0] - 1, 0))
    return zm1 * w[0:1, :] + z * w[1:2, :] + zp1 * w[2:3, :]


def _outproj_kernel(a_lat_ref, a_ctx_ref, yf_ref, yb_ref, rvk_ref, rk_ref, lng_ref, lnb_ref, hmean_ref, hsum_ref,
                    c_ref, d_ref, gt_ref, x_ref, mod_ref, w_ref, fg_ref, o_ref, *, n_lat_tiles, final):
    is_ctx = pl.program_id(1) >= n_lat_tiles
    a = jnp.where(is_ctx, a_ctx_ref[0], a_lat_ref[0])
    b = _rwkv_readout_tile(yf_ref[0] + yb_ref[0], rvk_ref[0], rk_ref[...], lng_ref[...], lnb_ref[...],
                           hmean_ref[...], hsum_ref[...])
    mix = jnp.concatenate([a, b, c_ref[0], d_ref[0]], axis=-1) * gt_ref[0]
    y = jnp.dot(mix.astype(BF16), w_ref[...], preferred_element_type=F32)
    x = x_ref[0] + mod_ref[0, 0, 2:3, :] * y
    o_ref[0] = _rms(x, fg_ref[...]) if final else x


def _outproj(a_lat, a_ctx, yf, yb, rvk, r_k, ln_g, ln_b, cmix, dmix, gates, xs, mods, w_out, final_g,
             n_lat_tiles, final):
    bsz, s, d = xs.shape
    wb = a_lat.shape[-1]
    nt = n_lat_tiles if final else s // ROW_TILE
    tile = lambda wd: pl.BlockSpec((1, ROW_TILE, wd), lambda b, t: (b, t, 0))
    vec = pl.BlockSpec((1, wb), lambda b, t: (0, 0))
    head_tab = _const_spec((3 * wb, wb))
    return pl.pallas_call(
        functools.partial(_outproj_kernel, n_lat_tiles=n_lat_tiles, final=final),
        out_shape=jax.ShapeDtypeStruct((bsz, nt * ROW_TILE, d), F32),
        grid=(bsz, nt),
        in_specs=[pl.BlockSpec((1, ROW_TILE, wb), lambda b, t: (b, jnp.minimum(t, n_lat_tiles - 1), 0)),
                  pl.BlockSpec((1, ROW_TILE, wb), lambda b, t: (b, 0, 0)),
                  tile(wb), tile(wb), tile(3 * wb), vec, vec, vec, head_tab, head_tab,
                  tile(wb), tile(wb), tile(4 * wb), tile(d),
                  pl.BlockSpec((1, 1, 3, d), lambda b, t: (b, t // n_lat_tiles, 0, 0)),
                  _const_spec(w_out.shape),
                  pl.BlockSpec((1, d), lambda b, t: (0, 0))],
        out_specs=tile(d),
        compiler_params=_cparams("arbitrary", "arbitrary"),
        name="out_projection",
    )(a_lat, a_ctx, yf, yb, rvk, r_k.reshape(1, wb), ln_g.reshape(1, wb), ln_b.reshape(1, wb),
      _head_block_diag(wb, 1.0 / HEAD_DIM), _head_block_diag(wb, 1.0), cmix, dmix, gates, xs, mods, w_out, final_g.reshape(1, d))


def _rope_tables(seq_len, ctx_len):
    rows = seq_len // GRID_W
    row = jnp.repeat(jnp.arange(rows, dtype=F32), GRID_W)
    col = jnp.tile(jnp.arange(GRID_W, dtype=F32), rows)
    n_freq = HEAD_DIM // 4
    inv_freq = ROPE_THETA ** (-jnp.arange(n_freq, dtype=F32) / n_freq)
    ar, ac = row[:, None] * inv_freq, col[:, None] * inv_freq
    zero = jnp.zeros_like(ar)
    cos = jnp.concatenate([jnp.cos(ar), jnp.cos(ar), jnp.cos(ac), jnp.cos(ac)], axis=-1)
    sin_hi = jnp.concatenate([-jnp.sin(ar), zero, -jnp.sin(ac), zero], axis=-1)
    sin_lo = jnp.concatenate([zero, jnp.sin(ar), zero, jnp.sin(ac)], axis=-1)
    pad = lambda t, v: jnp.concatenate([t, jnp.full((ctx_len, HEAD_DIM), v, F32)], axis=0)
    return pad(cos, 1.0), pad(sin_hi, 0.0), pad(sin_lo, 0.0)


def _rope(x, cos, sin_hi, sin_lo):
    q = HEAD_DIM // 4
    w = x.shape[-1]
    return x * cos + pltpu.roll(x, w - q, 1) * sin_hi + pltpu.roll(x, q, 1) * sin_lo


def _head_mean_sq(x, bd3):
    return _mm_exact_rhs(x * x, bd3)


def _qk_emit(wa, fa, cos, shi, slo, q_gain, k_gain, bdq, bdk, qw_ref, kw_ref, vw_ref, qf_ref, kf_ref, vf_ref,
             wq, wk):
    nq, nk = wq // HEAD_DIM, wk // HEAD_DIM
    tab = lambda t, n: jnp.concatenate([t] * n, axis=-1)
    cq, hq, lq = tab(cos, nq), tab(shi, nq), tab(slo, nq)
    ck, hk, lk = tab(cos, nk), tab(shi, nk), tab(slo, nk)
    scale = HEAD_DIM ** -0.5

    def emit(ref, val, n):
        for h in range(n):
            ref[0, h] = val[:, h * HEAD_DIM:(h + 1) * HEAD_DIM].astype(ref.dtype)

    def emit_values(ref, v):
        lane = lax.broadcasted_iota(jnp.int32, (v.shape[0], LANES - HEAD_DIM), 1)
        ones_pad = jnp.where(lane == 0, 1.0, 0.0)
        for h in range(nk):
            ref[0, h] = jnp.concatenate([v[:, h * HEAD_DIM:(h + 1) * HEAD_DIM], ones_pad],
                                        axis=-1).astype(ref.dtype)

    emit(qw_ref, _rope(wa[:, :wq], cq, hq, lq) * scale, nq)
    emit(kw_ref, _rope(wa[:, wq:wq + wk], ck, hk, lk), nk)
    emit_values(vw_ref, wa[:, wq + wk:])
    q, k = fa[:, :wq], fa[:, wq:wq + wk]
    q = q * lax.rsqrt(_head_mean_sq(q, bdq) + NORM_EPS) * q_gain
    k = k * lax.rsqrt(_head_mean_sq(k, bdk) + NORM_EPS) * k_gain
    emit(qf_ref, _rope(q, cq, hq, lq) * scale, nq)
    emit(kf_ref, _rope(k, ck, hk, lk), nk)
    emit_values(vf_ref, fa[:, wq + wk:])


def _head_block_diag(width, value):
    h = np.arange(width) // HEAD_DIM
    bd = ((h[:, None] == h[None, :]) * value).astype(np.float32).astype(BF16)
    assert np.all(bd.astype(np.float32) == (h[:, None] == h[None, :]) * value)
    return np.concatenate([bd, bd, bd], axis=0)


def _window_attn_kernel(q_ref, k_ref, v_ref, sink_ref, o_ref, *, seq_len, ctx_len, tq):
    t = pl.program_id(2)
    n_lat = seq_len // tq
    g = q_ref.shape[1]
    sub = WINDOW
    band = 2 * WINDOW + sub
    kc, vc = k_ref[0, 0, seq_len:seq_len + ctx_len, :], v_ref[0, 0, seq_len:seq_len + ctx_len, :]
    nt_dot = lambda a, b: lax.dot_general(a, b, (((1,), (1,)), ((), ())), preferred_element_type=F32)
    for j in range(tq // sub):
        q = q_ref[0, :, j * sub:(j + 1) * sub, :].reshape(g * sub, HEAD_DIM)
        first = t * tq + j * sub
        start = pl.multiple_of(jnp.clip(first - WINDOW, 0, seq_len - band), WINDOW)
        kb, vb = k_ref[0, 0, pl.ds(start, band), :], v_ref[0, 0, pl.ds(start, band), :]
        s_ctx = nt_dot(q, kc)
        s_loc = nt_dot(q, kb)
        qpos = first + lax.broadcasted_iota(jnp.int32, (g, sub, band), 1).reshape(g * sub, band)
        kpos = start + lax.broadcasted_iota(jnp.int32, (g * sub, band), 1)
        valid = jnp.logical_and(jnp.abs(qpos - kpos) <= WINDOW, t < n_lat)
        s_loc = jnp.where(valid, s_loc, NEG_INF)
        sink = sink_ref[0, :, j * sub:(j + 1) * sub, :].reshape(g * sub, 1)
        m = jnp.maximum(jnp.maximum(jnp.max(s_ctx, axis=-1, keepdims=True),
                                    jnp.max(s_loc, axis=-1, keepdims=True)), sink)
        p_ctx, p_loc = jnp.exp(s_ctx - m), jnp.exp(s_loc - m)
        acc = (jnp.dot(p_ctx.astype(BF16), vc, preferred_element_type=F32)
               + jnp.dot(p_loc.astype(BF16), vb, preferred_element_type=F32))
        denom = acc[:, HEAD_DIM:HEAD_DIM + 1] + jnp.exp(sink - m)
        out = acc[:, :HEAD_DIM] / denom
        o_ref[0, j * sub:(j + 1) * sub, :] = jnp.concatenate(
            [out[h * sub:(h + 1) * sub] for h in range(g)], axis=-1)


def _window_attention(q, k, v, sink, seq_len, ctx_len):
    bsz, nq, s, _ = q.shape
    nkv = k.shape[1]
    g = nq // nkv
    tq = ROW_TILE
    sink_rows = jnp.broadcast_to(sink.astype(F32).reshape(nkv, g, 1, 1), (nkv, g, tq, 1))
    kv = lambda a: pl.BlockSpec((1, 1, s, a.shape[-1]), lambda b, h, t: (b, h, 0, 0))
    return pl.pallas_call(
        functools.partial(_window_attn_kernel, seq_len=seq_len, ctx_len=ctx_len, tq=tq),
        out_shape=jax.ShapeDtypeStruct((bsz, s, nq * HEAD_DIM), F32),
        grid=(bsz, nkv, s // tq),
        in_specs=[pl.BlockSpec((1, g, tq, HEAD_DIM), lambda b, h, t: (b, h, t, 0)), kv(k), kv(v),
                  pl.BlockSpec((1, g, tq, 1), lambda b, h, t: (h, 0, 0, 0))],
        out_specs=pl.BlockSpec((1, tq, g * HEAD_DIM), lambda b, h, t: (b, t, h)),
        compiler_params=_cparams("arbitrary", "arbitrary", "arbitrary"),
        name="window_attention",
    )(q, k, v, sink_rows)


DENSE_KEY_BLOCK = 1024


def _dense_attn_kernel(q_ref, k_ref, v_ref, o_ref, *, seq_len, tq, tk):
    t = pl.program_id(2)
    n_lat = seq_len // tq
    g = q_ref.shape[1]
    ctx_len = k_ref.shape[2] - seq_len
    q = q_ref[0].reshape(g * tq, HEAD_DIM)

    def step(carry, start, size):
        m, acc = carry
        s = lax.dot_general(q, k_ref[0, 0, start:start + size, :], (((1,), (1,)), ((), ())),
                            preferred_element_type=F32)
        m_new = jnp.maximum(m, jnp.max(s, axis=-1, keepdims=True))
        p = jnp.exp(s - m_new).astype(BF16)
        pv = jnp.dot(p, v_ref[0, 0, start:start + size, :], preferred_element_type=F32)
        return m_new, jnp.exp(m - m_new) * acc + pv

    def finish(carry):
        _, acc = carry
        out = acc[:, :HEAD_DIM] / acc[:, HEAD_DIM:HEAD_DIM + 1]
        o_ref[0] = jnp.concatenate([out[h * tq:(h + 1) * tq] for h in range(g)], axis=-1)

    init = (jnp.full((g * tq, 1), NEG_INF, F32), jnp.zeros((g * tq, v_ref.shape[-1]), F32))

    @pl.when(t < n_lat)
    def _():
        carry = step(init, seq_len, ctx_len)
        for j in range(seq_len // tk):
            carry = step(carry, j * tk, tk)
        finish(carry)

    @pl.when(t >= n_lat)
    def _():
        finish(step(init, seq_len, ctx_len))


def _dense_attention(q, k, v, seq_len):
    bsz, nq, s, _ = q.shape
    nkv = k.shape[1]
    g = nq // nkv
    tq = ROW_TILE
    tk = math.gcd(seq_len, DENSE_KEY_BLOCK)
    kv = lambda a: pl.BlockSpec((1, 1, s, a.shape[-1]), lambda b, h, t: (b, h, 0, 0))
    return pl.pallas_call(
        functools.partial(_dense_attn_kernel, seq_len=seq_len, tq=tq, tk=tk),
        out_shape=jax.ShapeDtypeStruct((bsz, s, nq * HEAD_DIM), F32),
        grid=(bsz, nkv, s // tq),
        in_specs=[pl.BlockSpec((1, g, tq, HEAD_DIM), lambda b, h, t: (b, h, t, 0)), kv(k), kv(v)],
        out_specs=pl.BlockSpec((1, tq, g * HEAD_DIM), lambda b, h, t: (b, t, h)),
        compiler_params=_cparams("arbitrary", "arbitrary", "arbitrary"),
        name="dense_attention",
    )(q, k, v)


def _softplus(x):
    return jnp.maximum(x, 0.0) + jnp.log(1.0 + jnp.exp(-jnp.abs(x)))


def _split(a):
    bits = lax.bitcast_convert_type(a, jnp.uint32) & jnp.uint32(0xFFFF0000)
    hi = lax.bitcast_convert_type(bits, F32)
    return hi, a - hi


def _rhs3(b):
    hi, lo = _split(b)
    return jnp.concatenate([hi, hi, lo], axis=-2).astype(BF16)


def _split3(a):
    hi, rest = _split(a)
    mid, lo = _split(rest)
    return hi, mid, lo


def _mm_exact_lhs(tbl3, x):
    return jnp.dot(tbl3, jnp.concatenate(_split3(x), axis=0).astype(BF16), preferred_element_type=F32)


def _mm_exact_rhs(x, tbl3):
    return jnp.dot(jnp.concatenate(_split3(x), axis=-1).astype(BF16), tbl3, preferred_element_type=F32)


def _mm_split(a, b):
    ah, al = _split(a)
    return jnp.dot(jnp.concatenate([ah, al, ah], axis=-1).astype(BF16), _rhs3(b), preferred_element_type=F32)


RW_DEPTH = 1


def _lhs4(a):
    return a.astype(BF16)


def _rhs4(b):
    return b.astype(BF16)


def _rhs4_nt(b):
    return b.astype(BF16)


def _bmm(lf, rf):
    return lax.dot_general(lf, rf, (((2,), (1,)), ((0,), (0,))), preferred_element_type=F32)


RW_SLOTS_PER_STEP = 16


def _rwkv_par_kernel(z_ref, w0_ref, wup_ref, a0_ref, aup_ref, kk_ref, ka_ref,
                     bd_ref, cum_ref,
                     rp3_ref, yvq_ref, rvk_ref,
                     lbig_ref, rbig_ref, at_ref, rt_ref, vv3_ref, bct_ref, kct_ref, wt_ref,
                     *, w, lora):
    tc = RW_CHUNK
    nc, nh = ROW_TILE // tc, w // HEAD_DIM
    nb = RW_SLOTS_PER_STEP
    z = z_ref[0]
    r, k, v = z[:, :w], z[:, w:2 * w], z[:, 2 * w:3 * w]
    w_low = jnp.tanh(z[:, 3 * w:3 * w + lora])
    a_low = z[:, 3 * w + lora:]
    kk = k * kk_ref[...]
    kk = kk / jnp.maximum(jnp.sqrt(_mm_exact_rhs(kk * kk, bd_ref[...])), 1e-12)
    ksum = jnp.zeros_like(k)
    for d in range(2):
        w_log = -_softplus(-(w0_ref[d:d + 1, :] + _mm_split(w_low, wup_ref[d]))) - 0.5
        lw = -jnp.exp(w_log)
        a = 1.0 / (1.0 + jnp.exp(-(a0_ref[d:d + 1, :] + _mm_split(a_low, aup_ref[d]))))
        kd = k * (1.0 + (a - 1.0) * ka_ref[...])
        ksum = ksum + kd
        sums = _mm_exact_lhs(cum_ref[d], lw)
        c, ctot = sums[:ROW_TILE], sums[ROW_TILE:]
        e_neg, e_rem = jnp.exp(-c), jnp.exp(ctot - c)
        b = kk * a
        at, rt = -kk * jnp.exp(c - lw), r * jnp.exp(c)
        w_tot = jnp.exp(ctot)
        bh, kh = b * e_neg, kd * e_neg
        bct, kct = (b * e_rem).T, (kd * e_rem).T
        for ci in range(nc):
            for h in range(nh):
                slot = (d * nc + ci) * nh + h
                rows, cols = slice(ci * tc, (ci + 1) * tc), slice(h * HEAD_DIM, (h + 1) * HEAD_DIM)
                lbig_ref[slot, :tc, :] = _lhs4(at[rows, cols])
                lbig_ref[slot, tc:, :] = _lhs4(rt[rows, cols])
                rbig_ref[slot, :tc, :] = _rhs4_nt(bh[rows, cols])
                rbig_ref[slot, tc:, :] = _rhs4_nt(kh[rows, cols])
                at_ref[slot] = at[rows, cols]
                rt_ref[slot] = rt[rows, cols]
                vv3_ref[slot] = _rhs4(v[rows, cols])
                bct_ref[slot] = _lhs4(bct[cols, rows])
                kct_ref[slot] = _lhs4(kct[cols, rows])
                wt_ref[slot] = w_tot[ci * tc:ci * tc + SUBLANES, cols]
    rvk_ref[0] = jnp.concatenate([r, v, ksum], axis=-1)

    row = lax.broadcasted_iota(jnp.int32, (tc, tc), 0)
    col = lax.broadcasted_iota(jnp.int32, (tc, tc), 1)
    eye = row == col

    def level_mask(s):
        return jnp.logical_and(row // (2 * s) == col // (2 * s), row // s != col // s)

    steps_per_dir = nc * nh // nb

    def chunk_group(g, carry):
        slots = pl.ds(pl.multiple_of(g * nb, nb), nb)
        d = g // steps_per_dir
        fwd = d == 0
        before = jnp.logical_or(jnp.logical_and(fwd, row > col),
                                jnp.logical_and(jnp.logical_not(fwd), row < col))
        upto = jnp.logical_or(before, eye)
        big = lax.dot_general(lbig_ref[slots], rbig_ref[slots], (((2,), (2,)), ((0,), (0,))),
                              preferred_element_type=F32)
        a_ab = jnp.where(before, big[:, :tc, :tc], 0.0)
        a_ak = jnp.where(before, big[:, :tc, tc:], 0.0)
        a_rb = jnp.where(upto, big[:, tc:, :tc], 0.0)
        a_rk = jnp.where(upto, big[:, tc:, tc:], 0.0)
        x = jnp.where(eye, 1.0, jnp.where(level_mask(1), a_ab, 0.0))
        s = 2
        while s < tc:
            half = _bmm(_lhs4(x), _rhs4(jnp.where(level_mask(s), a_ab, 0.0)))
            x = x + _bmm(_lhs4(half), _rhs4(x))
            s *= 2
        vv3 = vv3_ref[slots]
        akrk = _bmm(jnp.concatenate([_lhs4(a_ak), _lhs4(a_rk)], axis=1), vv3)
        xa3 = _rhs4(_bmm(_lhs4(x), _rhs4(jnp.concatenate([at_ref[slots], akrk[:, :tc]], axis=-1))))
        ra = _bmm(_lhs4(a_rb), xa3)
        pq = _bmm(bct_ref[slots], xa3)
        rp = rt_ref[slots] + ra[:, :, :HEAD_DIM]
        yv = ra[:, :, HEAD_DIM:] + akrk[:, tc:]
        p = jnp.where(eye[:HEAD_DIM, :HEAD_DIM], wt_ref[slots][:, 0:1, :], 0.0) + pq[:, :, :HEAD_DIM]
        q = pq[:, :, HEAD_DIM:] + _bmm(kct_ref[slots], vv3)
        rp3 = jnp.concatenate([_lhs4(rp), _lhs4(p)], axis=1)
        yvq = jnp.concatenate([yv, q], axis=1)
        for j0 in range(0, nb, nh):
            ci = (g % steps_per_dir) * (nb // nh) + j0 // nh
            rp3_ref[0, d, ci] = jnp.concatenate([rp3[j0 + h] for h in range(nh)], axis=-1)
            yvq_ref[0, d, ci] = jnp.concatenate([yvq[j0 + h] for h in range(nh)], axis=-1)
        return carry
    lax.fori_loop(0, 2 * steps_per_dir, chunk_group, 0, unroll=True)


def _chunk_matrices():
    t = np.arange(ROW_TILE)
    same = (t[:, None] // RW_CHUNK) == (t[None, :] // RW_CHUNK)
    tabs = []
    for run in (same & (t[None, :] <= t[:, None]), same & (t[None, :] >= t[:, None])):
        m = np.concatenate([run, same], axis=0).astype(np.float32).astype(BF16)
        tabs.append(np.concatenate([m, m, m], axis=1))
    return np.stack(tabs)


def _rwkv_par(rw, w0, w_up, a0, a_up, k_k, k_a, w):
    bsz, s, width = rw.shape
    nt = s // ROW_TILE
    nh, nc = w // HEAD_DIM, ROW_TILE // RW_CHUNK
    lora = w_up.shape[1]
    full = lambda a: pl.BlockSpec(a.shape, lambda b, t: (0,) * a.ndim)
    cum = _chunk_matrices()
    bd = _head_block_diag(w, 1.0)
    vec = lambda a: a.reshape(1, w)
    n_chunks = s // RW_CHUNK
    nslots = 2 * nc * nh
    tc, hd = RW_CHUNK, HEAD_DIM
    consts = (w0, w_up, a0, a_up, vec(k_k), vec(k_a), bd, cum)
    return pl.pallas_call(
        functools.partial(_rwkv_par_kernel, w=w, lora=lora),
        out_shape=[jax.ShapeDtypeStruct((bsz, 2, n_chunks, tc + hd, RW_DEPTH * w), BF16),
                   jax.ShapeDtypeStruct((bsz, 2, n_chunks, tc + hd, w), F32),
                   jax.ShapeDtypeStruct((bsz, s, 3 * w), F32)],
        grid=(bsz, nt),
        in_specs=[pl.BlockSpec((1, ROW_TILE, width), lambda b, t: (b, t, 0))] + [full(a) for a in consts],
        out_specs=[pl.BlockSpec((1, 2, nc, tc + hd, RW_DEPTH * w), lambda b, t: (b, 0, t, 0, 0)),
                   pl.BlockSpec((1, 2, nc, tc + hd, w), lambda b, t: (b, 0, t, 0, 0)),
                   pl.BlockSpec((1, ROW_TILE, 3 * w), lambda b, t: (b, t, 0))],
        scratch_shapes=[pltpu.VMEM((nslots, 2 * tc, RW_DEPTH * hd), BF16),
                        pltpu.VMEM((nslots, 2 * tc, RW_DEPTH * hd), BF16),
                        pltpu.VMEM((nslots, tc, hd), F32),
                        pltpu.VMEM((nslots, tc, hd), F32),
                        pltpu.VMEM((nslots, RW_DEPTH * tc, hd), BF16),
                        pltpu.VMEM((nslots, hd, RW_DEPTH * tc), BF16),
                        pltpu.VMEM((nslots, hd, RW_DEPTH * tc), BF16),
                        pltpu.VMEM((nslots, SUBLANES, HEAD_DIM), F32)],
        compiler_params=_cparams("arbitrary", "arbitrary"),
        name="rwkv_chunk_prep",
    )(rw, *consts)


def _rwkv_seq_kernel(rp3f_ref, yvqf_ref, rp3b_ref, yvqb_ref, yf_ref, yb_ref, g_ref):
    @pl.when(pl.program_id(1) == 0)
    def _():
        g_ref[...] = jnp.zeros_like(g_ref)

    w = g_ref.shape[-1]
    tc = RW_CHUNK
    nh = w // HEAD_DIM
    grp = rp3f_ref.shape[2]
    row_head = lax.broadcasted_iota(jnp.int32, (w, w), 0) // HEAD_DIM
    col_head = lax.broadcasted_iota(jnp.int32, (w, w), 1) // HEAD_DIM
    on_diag = row_head == col_head
    for step in range(grp):
        for d, (rp3, yvq, y) in enumerate(((rp3f_ref, yvqf_ref, yf_ref), (rp3b_ref, yvqb_ref, yb_ref))):
            ci = step if d == 0 else grp - 1 - step
            out = jnp.dot(rp3[0, 0, ci], _rhs4(g_ref[d]), preferred_element_type=F32) + yvq[0, 0, ci]
            y[0, ci * tc:(ci + 1) * tc, :] = out[:tc]
            g_ref[d] = jnp.where(on_diag, jnp.concatenate([out[tc:]] * nh, axis=0), 0.0)


RW_SCAN_GROUP = 4


def _rwkv_seq(rp3, yvq, n_lat_chunks):
    bsz, _, n_chunks = rp3.shape[:3]
    w = yvq.shape[-1]
    grp = RW_SCAN_GROUP
    assert n_lat_chunks % grp == 0 and n_chunks % grp == 0
    n_groups, n_lat, n_ctx = n_chunks // grp, n_lat_chunks // grp, (n_chunks - n_lat_chunks) // grp
    order = (lambda i: jnp.where(i < n_ctx, n_lat + i, i - n_ctx),
             lambda i: n_groups - 1 - i)
    blk = lambda d, a: pl.BlockSpec((1, 1, grp) + a.shape[3:], lambda b, i: (b, d, order[d](i), 0, 0))
    out = lambda d: pl.BlockSpec((1, grp * RW_CHUNK, w), lambda b, i: (b, order[d](i), 0))
    shp = jax.ShapeDtypeStruct((bsz, n_chunks * RW_CHUNK, w), F32)
    return pl.pallas_call(
        _rwkv_seq_kernel,
        out_shape=[shp, shp],
        grid=(bsz, n_groups),
        in_specs=[blk(0, rp3), blk(0, yvq), blk(1, rp3), blk(1, yvq)],
        out_specs=[out(0), out(1)],
        scratch_shapes=[pltpu.VMEM((2, w, w), F32)],
        compiler_params=_cparams("arbitrary", "arbitrary"),
        name="rwkv_state_scan",
    )(rp3, yvq, rp3, yvq)


def _rwkv_readout_tile(y, rvk, r_k, ln_g, ln_b, head_mean, head_sum):
    w = r_k.shape[-1]
    r, v, ksum = rvk[:, :w], rvk[:, w:2 * w], rvk[:, 2 * w:]
    yc = y - _mm_exact_rhs(y, head_mean)
    var = _mm_exact_rhs(yc * yc, head_mean)
    bonus = _mm_exact_rhs(r * ksum * r_k, head_sum) * v
    return yc * lax.rsqrt(var + RW_GN_EPS) * ln_g + ln_b + bonus


@functools.lru_cache(maxsize=None)
def _filter_position_features(seq_len, bands, width):
    n = np.arange(2 * seq_len)
    pos = np.where(n < seq_len, n, 2 * seq_len - n) % seq_len
    t = np.linspace(0.0, 1.0, seq_len)[pos][:, None]
    wpos = (2.0 * math.pi / seq_len) * pos[:, None]
    f = np.linspace(1e-4, bands - 1, bands)[None, :]
    z = np.concatenate([t, np.cos(f * wpos), np.sin(f * wpos)], axis=-1)
    return np.pad(z, ((0, 0), (0, width - z.shape[1]))).astype(np.float32)


def _hyena_filter_kernel(z_ref, fw1_ref, fb1_ref, freq_ref, fw2_ref, fb2_ref, fw3_ref, delta_ref, o_ref, *, seq_len):
    freq = freq_ref[...]
    rows = ROW_TILE
    c = o_ref.shape[-1]

    def taps(i, norm):
        blk = pl.ds(pl.multiple_of(i * rows, rows), rows)
        z = z_ref[blk, :]
        h = jnp.sin(freq * (_mm_split(z, fw1_ref[...]) + fb1_ref[...]))
        h = jnp.sin(freq * (_mm_split(h, fw2_ref[...]) + fb2_ref[...]))
        h = _mm_split(h, fw3_ref[0, 0])
        n = i * rows + lax.broadcasted_iota(jnp.int32, (rows, 1), 0)
        h = jnp.where(n < seq_len, h[:, :c], h[:, c:]) * jnp.exp(-z[:, 0:1] * jnp.abs(delta_ref[...]))
        o_ref[0, blk, :] = jnp.where(n != seq_len, h, 0.0)
        return norm + jnp.sum(jnp.abs(h), axis=0, keepdims=True)
    norm = lax.fori_loop(0, z_ref.shape[0] // rows, taps, jnp.zeros((1, c), F32))

    def normalise(i, carry):
        blk = pl.ds(pl.multiple_of(i * rows, rows), rows)
        o_ref[0, blk, :] = o_ref[0, blk, :] / norm
        return carry
    lax.fori_loop(0, z_ref.shape[0] // rows, normalise, 0)


def _hyena_two_sided_filters(seq_len, fw1, fb1, freq, fw2, fb2, fw3, width):
    ffn = fw1.shape[1]
    bands = (fw1.shape[0] - 1) // 2
    z = _filter_position_features(seq_len, bands, ffn)
    fw1p = jnp.pad(fw1, ((0, ffn - fw1.shape[0]), (0, 0)))
    nct = width // LANES
    fw3r = fw3.reshape(ffn, 2, 2, nct, LANES).transpose(1, 3, 0, 2, 4).reshape(2, nct, ffn, 2 * LANES)
    max_decay = math.log(HY_DECAY_TARGET) / HY_FAST_DECAY
    min_decay = math.log(HY_DECAY_TARGET) / HY_SLOW_DECAY
    deltas = jnp.linspace(min_decay, max_decay, width, dtype=F32).reshape(1, width)
    row = lambda a: a.reshape(1, ffn)
    full = lambda shape: pl.BlockSpec(shape, lambda o, j: (0,) * len(shape))
    return pl.pallas_call(
        functools.partial(_hyena_filter_kernel, seq_len=seq_len),
        out_shape=jax.ShapeDtypeStruct((2, 2 * seq_len, width), F32),
        grid=(2, nct),
        in_specs=[full(z.shape), full((ffn, ffn)), full((1, ffn)), full((1, ffn)), full((ffn, ffn)),
                  full((1, ffn)), pl.BlockSpec((1, 1, ffn, 2 * LANES), lambda o, j: (o, j, 0, 0)),
                  pl.BlockSpec((1, LANES), lambda o, j: (0, j))],
        out_specs=pl.BlockSpec((1, 2 * seq_len, LANES), lambda o, j: (o, 0, j)),
        compiler_params=_cparams("arbitrary", "arbitrary"),
        name="hyena_filter",
    )(z, fw1p, row(fb1), row(freq), fw2, row(fb2), fw3r, deltas)


def kernel(x, c, ctx, c_ctx, mod_w, mod_b, norm_g, w_in, w_out, hy_conv, hy_fw1, hy_fb1, hy_freq, hy_fw2,
           hy_fb2, hy_fw3, hy_bias, rw_conv, rw_w0, rw_w_up, rw_a0, rw_a_up, rw_k_k, rw_k_a, rw_r_k,
           rw_ln_g, rw_ln_b, wa_sink, fa_q_norm, fa_k_norm, final_g):
    bsz, seq_len, d = x.shape
    ctx_len = ctx.shape[1]
    depth = w_in.shape[0]
    w_hy = hy_bias.shape[-1]
    w_rw = rw_w0.shape[-1]
    n_wa_heads = wa_sink.shape[-1]
    w_q = n_wa_heads * HEAD_DIM
    w_kv = w_q // 2
    lora = rw_w_up.shape[2] + rw_a_up.shape[2]
    branch_w = (3 * w_hy, 3 * w_rw + lora, w_q + 2 * w_kv, w_q + 2 * w_kv)
    gate_w = (w_hy, w_rw, w_q, w_q)
    assert seq_len % ROW_TILE == 0 and ctx_len % ROW_TILE == 0 and bsz % 2 == 0
    n_lat_tiles = seq_len // ROW_TILE

    starts = np.cumsum([0] + [bw + gw for bw, gw in zip(branch_w, gate_w)])
    cols = np.concatenate([np.arange(s0, s0 + bw) for s0, bw in zip(starts, branch_w)]
                          + [np.arange(s0 + bw, s0 + bw + gw) for s0, bw, gw in zip(starts, branch_w, gate_w)])
    w_in_p = w_in[:, :, cols].astype(BF16)
    w_out_b = w_out.astype(BF16)
    widths = branch_w + (sum(gate_w),)

    pad_rows = (-(bsz + 1)) % 8
    cond = jnp.concatenate([c, c_ctx[None], jnp.zeros((pad_rows, d), F32)], axis=0)
    mod = _modulation(cond, mod_w, mod_b)
    mod_lat = mod[:, :bsz].reshape(depth, bsz, 3, d)
    mod_ctx = jnp.broadcast_to(mod[:, bsz].reshape(depth, 1, 3, d), (depth, bsz, 3, d))
    mods = jnp.stack([mod_lat, mod_ctx], axis=2)

    rope = _rope_tables(seq_len, ctx_len)
    xs = jnp.concatenate([x, ctx], axis=1)
    for l in range(depth):
        last = l == depth - 1
        conv_w = jnp.concatenate([hy_conv[l], rw_conv[l]], axis=-1)
        hyc, rwc, gates, qw, kw, vw, qf, kf, vf = _inproj(
            xs, mods[l], norm_g[l], w_in_p[l], conv_w, widths, n_lat_tiles, rope,
            fa_q_norm[l], fa_k_norm[l], w_q, w_kv)

        filt = functools.partial(_hyena_two_sided_filters, fw1=hy_fw1[l], fb1=hy_fb1[l], freq=hy_freq[l],
                                 fw2=hy_fw2[l], fb2=hy_fb2[l], fw3=hy_fw3[l], width=w_hy)
        bias = hy_bias[l].reshape(2, 1, w_hy)
        spec = _filter_spectrum(filt(seq_len))
        nt_hy = w_hy // LANES
        y1 = _fftconv_gated(hyc, 0, hyc, nt_hy, spec, 0, bias, seq_len)
        a_lat = _fftconv_gated(y1, 0, hyc, 2 * nt_hy, spec, 1, bias, seq_len)
        a_ctx = a_lat if last else _ctx_hyena(hyc, seq_len // ctx_len, filt(ctx_len), bias, ctx_len, w_hy)

        rp3, yvq, rvk = _rwkv_par(rwc, rw_w0[l], rw_w_up[l], rw_a0[l], rw_a_up[l], rw_k_k[l], rw_k_a[l], w_rw)
        yf, yb = _rwkv_seq(rp3, yvq, seq_len // RW_CHUNK)

        c_mix = _window_attention(qw, kw, vw, wa_sink[l], seq_len, ctx_len)
        d_mix = _dense_attention(qf, kf, vf, seq_len)

        xs = _outproj(a_lat, a_ctx, yf, yb, rvk, rw_r_k[l], rw_ln_g[l], rw_ln_b[l], c_mix, d_mix, gates, xs,
                      mods[l], w_out_b[l], final_g, n_lat_tiles, last)
    return xs
```

```python
import functools
import math

import jax
import jax.numpy as jnp
import numpy as np
from jax import lax
from jax.experimental import pallas as pl
from jax.experimental.pallas import tpu as pltpu

HEAD_DIM = 64
GRID_W = 64
WINDOW = 128
NORM_EPS = 1e-6
RW_GN_EPS = 64e-5
NEG_INF = -1e30
ROPE_THETA = 10000.0
HY_FAST_DECAY = 0.3
HY_SLOW_DECAY = 1.5
HY_DECAY_TARGET = 1e-2

ROW_TILE = 256
LANES = 128
SUBLANES = 8
FFT_N1 = 64
FFT_UNROLL = 8
FFT_PLANE_PAD = 8
RW_CHUNK = 64
VMEM_LIMIT = 60 * 1024 * 1024

F32 = jnp.float32
BF16 = jnp.bfloat16
HI = lax.Precision.HIGHEST


def _dot(a, b):
    return jnp.dot(a, b, preferred_element_type=F32, precision=HI)


def _dot_nt(a, b):
    return lax.dot_general(a, b, (((1,), (1,)), ((), ())), preferred_element_type=F32, precision=HI)


def _dot_tn(a, b):
    return lax.dot_general(a, b, (((0,), (0,)), ((), ())), preferred_element_type=F32, precision=HI)


def _cparams(*sem):
    return pltpu.CompilerParams(dimension_semantics=sem, vmem_limit_bytes=VMEM_LIMIT)


def _const_spec(shape):
    return pl.BlockSpec(shape, lambda *_: (0,) * len(shape), pipeline_mode=pl.Buffered(1))


def _silu(x):
    return x * (1.0 / (1.0 + jnp.exp(-x)))


def _mod_kernel(c_ref, w_ref, b_ref, o_ref):
    o_ref[0] = _dot(_silu(c_ref[...]), w_ref[0]) + b_ref[0]


def _modulation(cond, mod_w, mod_b):
    depth, d, d3 = mod_w.shape
    rows = cond.shape[0]
    return pl.pallas_call(
        _mod_kernel,
        out_shape=jax.ShapeDtypeStruct((depth, rows, d3), F32),
        grid=(depth,),
        in_specs=[pl.BlockSpec((rows, d), lambda l: (0, 0)),
                  pl.BlockSpec((1, d, d3), lambda l: (l, 0, 0)),
                  pl.BlockSpec((1, 1, d3), lambda l: (l, 0, 0))],
        out_specs=pl.BlockSpec((1, rows, d3), lambda l: (l, 0, 0)),
        compiler_params=_cparams("arbitrary"),
        name="modulation",
    )(cond, mod_w, mod_b.reshape(depth, 1, d3))


def _blockreal(m):
    return np.block([[m.real, -m.imag], [m.imag, m.real]])


@functools.lru_cache(maxsize=None)
def _fft_tables(seq_len):
    n = 2 * seq_len
    n1, n2 = FFT_N1, n // FFT_N1
    h1 = n1 // 2
    j2 = np.arange(n2)[:, None, None]
    k1 = np.arange(n1)[None, :, None]
    t1 = np.exp(-2j * np.pi * (j2 * k1 / n + k1 * np.arange(n1)[None, None, :] / n1))
    t1_data = np.stack([_blockreal(t1[j][:, :h1]) for j in range(n2)])
    t1_real = np.concatenate([t1.real, t1.imag], axis=1)
    f2 = np.exp(-2j * np.pi * np.outer(np.arange(n2), np.arange(n2)) / n2)
    f2_fwd = _blockreal(f2)
    f2_inv = _blockreal(np.conj(f2))
    t4 = np.exp(2j * np.pi * (np.arange(h1)[None, :, None] * np.arange(n1)[None, None, :] / n1
                              + j2 * np.arange(n1)[None, None, :] / n)) / n
    t4 = np.stack([_blockreal(t4[j]) for j in range(n2)])
    f2_parts = np.concatenate([f2.real, f2.imag], axis=0)
    return tuple(_lhs3_table(t) for t in (t1_data, t1_real, f2_fwd, f2_parts, t4))


def _lhs3_table(m):
    hi = m.astype(np.float32).astype(BF16)
    lo = (m - hi.astype(np.float64)).astype(np.float32).astype(BF16)
    return np.concatenate([hi, lo, hi], axis=-1)


def _mm3(tbl3, x):
    return jnp.dot(tbl3, _rhs3(x), preferred_element_type=F32)


def _spectrum_kernel(k_ref, t1_ref, f2_ref, o_ref, a_ref, *, n1, n2):
    def stage1(j, carry):
        rows = k_ref[0, pl.ds(j, n1, stride=n2), :]
        a_ref[pl.ds(j, 2 * n1, stride=n2 + FFT_PLANE_PAD), :] = _mm3(t1_ref[j], rows)
        return carry
    lax.fori_loop(0, n2, stage1, 0, unroll=FFT_UNROLL)

    def stage2(i, carry):
        pitch = n2 + FFT_PLANE_PAD
        re = a_ref[pl.ds(pl.multiple_of(i * pitch, 8), n2), :]
        im = a_ref[pl.ds(pl.multiple_of((n1 + i) * pitch, 8), n2), :]
        o_ref[0, i] = _mm3(f2_ref[...], jnp.concatenate([re, im], axis=0))
        return carry
    lax.fori_loop(0, n1, stage2, 0, unroll=2)


def _filter_spectrum(kfilt):
    g, n, w = kfilt.shape
    n1, n2 = FFT_N1, n // FFT_N1
    _, t1_real, f2_fwd, _, _ = _fft_tables(n // 2)
    const = _const_spec
    return pl.pallas_call(
        functools.partial(_spectrum_kernel, n1=n1, n2=n2),
        out_shape=jax.ShapeDtypeStruct((g, n1, 2 * n2, w), F32),
        grid=(g, w // LANES),
        in_specs=[pl.BlockSpec((1, n, LANES), lambda gi, j: (gi, 0, j)),
                  const(t1_real.shape), const(f2_fwd.shape)],
        out_specs=pl.BlockSpec((1, n1, 2 * n2, LANES), lambda gi, j: (gi, 0, 0, j)),
        scratch_shapes=[pltpu.VMEM((2 * n1 * (n2 + FFT_PLANE_PAD), LANES), F32)],
        compiler_params=_cparams("arbitrary", "arbitrary"),
        name="hyena_filter_spectrum",
    )(kfilt, t1_real, f2_fwd)


def _fftconv_kernel(u_ref, m_ref, spec_ref, bias_ref, t1_ref, f2p_ref, t4_ref, o_ref, ar_ref, ai_ref,
                    *, n1, n2):
    h1 = n1 // 2
    c = u_ref.shape[-1]
    pitch = n2 + FFT_PLANE_PAD

    def stage1(j, carry):
        za = u_ref[0, pl.ds(j, h1, stride=n2), :]
        zb = u_ref[1, pl.ds(j, h1, stride=n2), :]
        res = _mm3(t1_ref[j], jnp.concatenate([za, zb], axis=0))
        ar_ref[pl.ds(j, n1, stride=pitch), :] = res[:n1]
        ai_ref[pl.ds(j, n1, stride=pitch), :] = res[n1:]
        return carry
    lax.fori_loop(0, n2, stage1, 0, unroll=FFT_UNROLL)

    def stage2(i, carry):
        rows = pl.ds(pl.multiple_of(i * pitch, SUBLANES), n2)
        p = _mm3(f2p_ref[...], jnp.concatenate([ar_ref[rows, :], ai_ref[rows, :]], axis=-1))
        xr = p[:n2, :c] - p[n2:, c:]
        xi = p[:n2, c:] + p[n2:, :c]
        kr, ki = spec_ref[0, i, :n2, :], spec_ref[0, i, n2:, :]
        y = jnp.concatenate([xr * kr - xi * ki, xr * ki + xi * kr], axis=-1)
        q = _mm3(f2p_ref[...], y)
        ar_ref[rows, :] = q[:n2, :c] + q[n2:, c:]
        ai_ref[rows, :] = q[:n2, c:] - q[n2:, :c]
        return carry
    lax.fori_loop(0, n1, stage2, 0, unroll=8)

    bias = bias_ref[0]

    def stage4(j, carry):
        planes = pl.ds(j, n1, stride=pitch)
        y = _mm3(t4_ref[j], jnp.concatenate([ar_ref[planes, :], ai_ref[planes, :]], axis=0))
        rows = pl.ds(j, h1, stride=n2)
        for p in range(2):
            u = u_ref[p, rows, :]
            o_ref[p, rows, :] = m_ref[p, rows, :] * (y[p * h1:(p + 1) * h1] + bias * u)
        return carry
    lax.fori_loop(0, n2, stage4, 0, unroll=FFT_UNROLL)


def _fftconv_gated(u, u_col, mult, mult_col, spec, conv_idx, bias, seq_len):
    bsz = u.shape[0]
    w = spec.shape[-1]
    n = 2 * seq_len
    n1, n2 = FFT_N1, n // FFT_N1
    t1_data, _, _, f2_parts, t4 = _fft_tables(seq_len)
    const = _const_spec
    return pl.pallas_call(
        functools.partial(_fftconv_kernel, n1=n1, n2=n2),
        out_shape=jax.ShapeDtypeStruct((bsz, seq_len, w), F32),
        grid=(w // LANES, bsz // 2),
        in_specs=[pl.BlockSpec((2, seq_len, LANES), lambda j, p: (p, 0, u_col + j)),
                  pl.BlockSpec((2, seq_len, LANES), lambda j, p: (p, 0, mult_col + j)),
                  pl.BlockSpec((1, n1, 2 * n2, LANES), lambda j, p: (conv_idx, 0, 0, j),
                               pipeline_mode=pl.Buffered(1)),
                  pl.BlockSpec((1, 1, LANES), lambda j, p: (conv_idx, 0, j)),
                  const(t1_data.shape), const(f2_parts.shape), const(t4.shape)],
        out_specs=pl.BlockSpec((2, seq_len, LANES), lambda j, p: (p, 0, j)),
        scratch_shapes=[pltpu.VMEM((n1 * (n2 + FFT_PLANE_PAD), LANES), F32)] * 2,
        compiler_params=_cparams("arbitrary", "arbitrary"),
        name="hyena_fftconv",
    )(u, mult, spec, bias, t1_data, f2_parts, t4)


@functools.lru_cache(maxsize=None)
def _small_fft_tables(seq_len):
    n = 2 * seq_len
    f = np.exp(-2j * np.pi * np.outer(np.arange(n), np.arange(n)) / n)
    fwd = _blockreal(f[:, :seq_len])
    real = np.concatenate([f.real, f.imag], axis=0)
    inv = _blockreal(np.conj(f)[:seq_len, :] / n)
    as32 = lambda a: jnp.asarray(a, dtype=F32)
    return as32(fwd), as32(real), as32(inv)


def _ctx_hyena_kernel(v_ref, x1_ref, x2_ref, k_ref, bias_ref, fwd_ref, real_ref, inv_ref, o_ref, *, seq_len):
    n = 2 * seq_len

    def conv(ua, ub, g):
        spec = _dot(real_ref[...], k_ref[g])
        x = _dot(fwd_ref[...], jnp.concatenate([ua, ub], axis=0))
        xr, xi, kr, ki = x[:n], x[n:], spec[:n], spec[n:]
        y = _dot(inv_ref[...], jnp.concatenate([xr * kr - xi * ki, xr * ki + xi * kr], axis=0))
        b = bias_ref[g]
        return y[:seq_len] + b * ua, y[seq_len:] + b * ub

    c1a, c1b = conv(v_ref[0], v_ref[1], 0)
    y1a, y1b = x1_ref[0] * c1a, x1_ref[1] * c1b
    c2a, c2b = conv(y1a, y1b, 1)
    o_ref[0] = x2_ref[0] * c2a
    o_ref[1] = x2_ref[1] * c2b


def _ctx_hyena(hyc, row_block, kfilt, bias, seq_len, w):
    bsz = hyc.shape[0]
    n = 2 * seq_len
    fwd, real, inv = _small_fft_tables(seq_len)
    nt = w // LANES
    col = lambda c0: pl.BlockSpec((2, seq_len, LANES), lambda j, p: (p, row_block, c0 + j))
    return pl.pallas_call(
        functools.partial(_ctx_hyena_kernel, seq_len=seq_len),
        out_shape=jax.ShapeDtypeStruct((bsz, seq_len, w), F32),
        grid=(nt, bsz // 2),
        in_specs=[col(0), col(nt), col(2 * nt),
                  pl.BlockSpec((2, n, LANES), lambda j, p: (0, 0, j)),
                  pl.BlockSpec((2, 1, LANES), lambda j, p: (0, 0, j)),
                  _const_spec((2 * n, 2 * seq_len)), _const_spec((2 * n, n)), _const_spec((2 * seq_len, 2 * n))],
        out_specs=pl.BlockSpec((2, seq_len, LANES), lambda j, p: (p, 0, j)),
        compiler_params=_cparams("arbitrary", "arbitrary"),
        name="hyena_ctx",
    )(hyc, hyc, hyc, kfilt, bias, fwd, real, inv)


def _rms(x, g):
    return x * lax.rsqrt(jnp.mean(x * x, axis=-1, keepdims=True) + NORM_EPS) * g


def _inproj_kernel(x_ref, xprev_ref, xnext_ref, mod_ref, g_ref, w_ref, cw_ref,
                   cos_ref, shi_ref, slo_ref, qg_ref, kg_ref, bdq_ref, bdk_ref,
                   hy_ref, rw_ref, gt_ref, qw_ref, kw_ref, vw_ref, qf_ref, kf_ref, vf_ref,
                   *, widths, wq, wk, n_lat_tiles, n_tiles):
    shift, scale = mod_ref[0, 0, 0:1, :], mod_ref[0, 0, 1:2, :]
    norm_mod = lambda x: (_rms(x, g_ref[...]) * (1.0 + scale) + shift).astype(BF16)
    z = jnp.dot(norm_mod(x_ref[0]), w_ref[...], preferred_element_type=F32)
    offs = np.cumsum((0,) + tuple(widths))
    part = lambda i: z[:, offs[i]:offs[i + 1]]
    n_conv = offs[2]
    halo = jnp.dot(norm_mod(jnp.concatenate([xprev_ref[0], xnext_ref[0]], axis=0)), w_ref[:, :n_conv],
                   preferred_element_type=F32)
    conv = _short_conv_tile(z[:, :n_conv], halo[:SUBLANES], halo[SUBLANES:], cw_ref[...], pl.program_id(1),
                            n_lat_tiles, n_tiles)
    hy_ref[0] = conv[:, :offs[1]]
    rw_ref[0] = conv[:, offs[1]:]
    gt_ref[0] = _silu(part(4))
    _qk_emit(part(2), part(3), cos_ref[...], shi_ref[...], slo_ref[...], qg_ref[...], kg_ref[...],
             bdq_ref[...], bdk_ref[...], qw_ref, kw_ref, vw_ref, qf_ref, kf_ref, vf_ref, wq, wk)


def _inproj(xs, mods, norm_g, w_perm, conv_w, widths, n_lat_tiles, rope, q_gain, k_gain, wq, wk):
    bsz, s, d = xs.shape
    nt = s // ROW_TILE
    nq, nk = wq // HEAD_DIM, wk // HEAD_DIM
    prev, nxt = _halo_specs(d, nt)
    tile = lambda wd: pl.BlockSpec((1, ROW_TILE, wd), lambda b, t: (b, t, 0))
    tab = pl.BlockSpec((ROW_TILE, HEAD_DIM), lambda b, t: (t, 0))
    hm = lambda n, wd=HEAD_DIM: pl.BlockSpec((1, n, ROW_TILE, wd), lambda b, t: (b, 0, t, 0))
    flat = lambda wd: jax.ShapeDtypeStruct((bsz, s, wd), F32)
    heads = lambda n, wd=HEAD_DIM: jax.ShapeDtypeStruct((bsz, n, s, wd), BF16)
    return pl.pallas_call(
        functools.partial(_inproj_kernel, widths=widths, wq=wq, wk=wk, n_lat_tiles=n_lat_tiles, n_tiles=nt),
        out_shape=[flat(widths[0]), flat(widths[1]), flat(widths[4]),
                   heads(nq), heads(nk), heads(nk, LANES), heads(nq), heads(nk), heads(nk, LANES)],
        grid=(bsz, nt),
        in_specs=[tile(d), prev, nxt,
                  pl.BlockSpec((1, 1, 3, d), lambda b, t: (b, t // n_lat_tiles, 0, 0)),
                  pl.BlockSpec((1, d), lambda b, t: (0, 0)),
                  _const_spec(w_perm.shape), pl.BlockSpec(conv_w.shape, lambda b, t: (0, 0)), tab, tab, tab,
                  pl.BlockSpec((1, wq), lambda b, t: (0, 0)), pl.BlockSpec((1, wk), lambda b, t: (0, 0)),
                  _const_spec((3 * wq, wq)), _const_spec((3 * wk, wk))],
        out_specs=[tile(widths[0]), tile(widths[1]), tile(widths[4]),
                   hm(nq), hm(nk), hm(nk, LANES), hm(nq), hm(nk), hm(nk, LANES)],
        compiler_params=_cparams("arbitrary", "arbitrary"),
        name="in_projection",
    )(xs, xs, xs, mods, norm_g.reshape(1, d), w_perm, conv_w, *rope,
      jnp.tile(q_gain, nq).reshape(1, wq), jnp.tile(k_gain, nk).reshape(1, wk),
      _head_block_diag(wq, 1.0 / HEAD_DIM), _head_block_diag(wk, 1.0 / HEAD_DIM))


def _halo_specs(width, n_tiles):
    per = ROW_TILE // SUBLANES
    prev = pl.BlockSpec((1, SUBLANES, width), lambda b, t: (b, jnp.maximum(t * per - 1, 0), 0))
    nxt = pl.BlockSpec((1, SUBLANES, width), lambda b, t: (b, jnp.minimum((t + 1) * per, n_tiles * per - 1), 0))
    return prev, nxt


def _short_conv_tile(z, prev8, next8, w, t, n_lat_tiles, n_tiles):
    first = jnp.logical_or(t == 0, t == n_lat_tiles)
    last = jnp.logical_or(t == n_lat_tiles - 1, t == n_tiles - 1)
    above = jnp.where(first, 0.0, prev8[SUBLANES - 1:SUBLANES, :])
    below = jnp.where(last, 0.0, next8[0:1, :])
    row = lax.broadcasted_iota(jnp.int32, z.shape, 0)
    zm1 = jnp.where(row == 0, above, pltpu.roll(z, 1, 0))
    zp1 = jnp.where(row == z.shape[0] - 1, below, pltpu.roll(z, z.shape[0] - 1, 0))
    return zm1 * w[0:1, :] + z * w[1:2, :] + zp1 * w[2:3, :]


def _outproj_kernel(a_lat_ref, a_ctx_ref, yf_ref, yb_ref, rvk_ref, rk_ref, lng_ref, lnb_ref, hmean_ref, hsum_ref,
                    c_ref, d_ref, gt_ref, x_ref, mod_ref, w_ref, fg_ref, o_ref, *, n_lat_tiles, final):
    is_ctx = pl.program_id(1) >= n_lat_tiles
    a = jnp.where(is_ctx, a_ctx_ref[0], a_lat_ref[0])
    b = _rwkv_readout_tile(yf_ref[0] + yb_ref[0], rvk_ref[0], rk_ref[...], lng_ref[...], lnb_ref[...],
                           hmean_ref[...], hsum_ref[...])
    mix = jnp.concatenate([a, b, c_ref[0], d_ref[0]], axis=-1) * gt_ref[0]
    y = jnp.dot(mix.astype(BF16), w_ref[...], preferred_element_type=F32)
    x = x_ref[0] + mod_ref[0, 0, 2:3, :] * y
    o_ref[0] = _rms(x, fg_ref[...]) if final else x


def _outproj(a_lat, a_ctx, yf, yb, rvk, r_k, ln_g, ln_b, cmix, dmix, gates, xs, mods, w_out, final_g,
             n_lat_tiles, final):
    bsz, s, d = xs.shape
    wb = a_lat.shape[-1]
    nt = n_lat_tiles if final else s // ROW_TILE
    tile = lambda wd: pl.BlockSpec((1, ROW_TILE, wd), lambda b, t: (b, t, 0))
    vec = pl.BlockSpec((1, wb), lambda b, t: (0, 0))
    head_tab = _const_spec((3 * wb, wb))
    return pl.pallas_call(
        functools.partial(_outproj_kernel, n_lat_tiles=n_lat_tiles, final=final),
        out_shape=jax.ShapeDtypeStruct((bsz, nt * ROW_TILE, d), F32),
        grid=(bsz, nt),
        in_specs=[pl.BlockSpec((1, ROW_TILE, wb), lambda b, t: (b, jnp.minimum(t, n_lat_tiles - 1), 0)),
                  pl.BlockSpec((1, ROW_TILE, wb), lambda b, t: (b, 0, 0)),
                  tile(wb), tile(wb), tile(3 * wb), vec, vec, vec, head_tab, head_tab,
                  tile(wb), tile(wb), tile(4 * wb), tile(d),
                  pl.BlockSpec((1, 1, 3, d), lambda b, t: (b, t // n_lat_tiles, 0, 0)),
                  _const_spec(w_out.shape),
                  pl.BlockSpec((1, d), lambda b, t: (0, 0))],
        out_specs=tile(d),
        compiler_params=_cparams("arbitrary", "arbitrary"),
        name="out_projection",
    )(a_lat, a_ctx, yf, yb, rvk, r_k.reshape(1, wb), ln_g.reshape(1, wb), ln_b.reshape(1, wb),
      _head_block_diag(wb, 1.0 / HEAD_DIM), _head_block_diag(wb, 1.0), cmix, dmix, gates, xs, mods, w_out, final_g.reshape(1, d))


def _rope_tables(seq_len, ctx_len):
    rows = seq_len // GRID_W
    row = jnp.repeat(jnp.arange(rows, dtype=F32), GRID_W)
    col = jnp.tile(jnp.arange(GRID_W, dtype=F32), rows)
    n_freq = HEAD_DIM // 4
    inv_freq = ROPE_THETA ** (-jnp.arange(n_freq, dtype=F32) / n_freq)
    ar, ac = row[:, None] * inv_freq, col[:, None] * inv_freq
    zero = jnp.zeros_like(ar)
    cos = jnp.concatenate([jnp.cos(ar), jnp.cos(ar), jnp.cos(ac), jnp.cos(ac)], axis=-1)
    sin_hi = jnp.concatenate([-jnp.sin(ar), zero, -jnp.sin(ac), zero], axis=-1)
    sin_lo = jnp.concatenate([zero, jnp.sin(ar), zero, jnp.sin(ac)], axis=-1)
    pad = lambda t, v: jnp.concatenate([t, jnp.full((ctx_len, HEAD_DIM), v, F32)], axis=0)
    return pad(cos, 1.0), pad(sin_hi, 0.0), pad(sin_lo, 0.0)


def _rope(x, cos, sin_hi, sin_lo):
    q = HEAD_DIM // 4
    w = x.shape[-1]
    return x * cos + pltpu.roll(x, w - q, 1) * sin_hi + pltpu.roll(x, q, 1) * sin_lo


def _head_mean_sq(x, bd3):
    return _mm_exact_rhs(x * x, bd3)


def _qk_emit(wa, fa, cos, shi, slo, q_gain, k_gain, bdq, bdk, qw_ref, kw_ref, vw_ref, qf_ref, kf_ref, vf_ref,
             wq, wk):
    nq, nk = wq // HEAD_DIM, wk // HEAD_DIM
    tab = lambda t, n: jnp.concatenate([t] * n, axis=-1)
    cq, hq, lq = tab(cos, nq), tab(shi, nq), tab(slo, nq)
    ck, hk, lk = tab(cos, nk), tab(shi, nk), tab(slo, nk)
    scale = HEAD_DIM ** -0.5

    def emit(ref, val, n):
        for h in range(n):
            ref[0, h] = val[:, h * HEAD_DIM:(h + 1) * HEAD_DIM].astype(ref.dtype)

    def emit_values(ref, v):
        lane = lax.broadcasted_iota(jnp.int32, (v.shape[0], LANES - HEAD_DIM), 1)
        ones_pad = jnp.where(lane == 0, 1.0, 0.0)
        for h in range(nk):
            ref[0, h] = jnp.concatenate([v[:, h * HEAD_DIM:(h + 1) * HEAD_DIM], ones_pad],
                                        axis=-1).astype(ref.dtype)

    emit(qw_ref, _rope(wa[:, :wq], cq, hq, lq) * scale, nq)
    emit(kw_ref, _rope(wa[:, wq:wq + wk], ck, hk, lk), nk)
    emit_values(vw_ref, wa[:, wq + wk:])
    q, k = fa[:, :wq], fa[:, wq:wq + wk]
    q = q * lax.rsqrt(_head_mean_sq(q, bdq) + NORM_EPS) * q_gain
    k = k * lax.rsqrt(_head_mean_sq(k, bdk) + NORM_EPS) * k_gain
    emit(qf_ref, _rope(q, cq, hq, lq) * scale, nq)
    emit(kf_ref, _rope(k, ck, hk, lk), nk)
    emit_values(vf_ref, fa[:, wq + wk:])


def _head_block_diag(width, value):
    h = np.arange(width) // HEAD_DIM
    bd = ((h[:, None] == h[None, :]) * value).astype(np.float32).astype(BF16)
    assert np.all(bd.astype(np.float32) == (h[:, None] == h[None, :]) * value)
    return np.concatenate([bd, bd, bd], axis=0)


def _window_attn_kernel(q_ref, k_ref, v_ref, sink_ref, o_ref, *, seq_len, ctx_len, tq):
    t = pl.program_id(2)
    n_lat = seq_len // tq
    g = q_ref.shape[1]
    sub = WINDOW
    band = 2 * WINDOW + sub
    kc, vc = k_ref[0, 0, seq_len:seq_len + ctx_len, :], v_ref[0, 0, seq_len:seq_len + ctx_len, :]
    nt_dot = lambda a, b: lax.dot_general(a, b, (((1,), (1,)), ((), ())), preferred_element_type=F32)
    for j in range(tq // sub):
        q = q_ref[0, :, j * sub:(j + 1) * sub, :].reshape(g * sub, HEAD_DIM)
        first = t * tq + j * sub
        start = pl.multiple_of(jnp.clip(first - WINDOW, 0, seq_len - band), WINDOW)
        kb, vb = k_ref[0, 0, pl.ds(start, band), :], v_ref[0, 0, pl.ds(start, band), :]
        s_ctx = nt_dot(q, kc)
        s_loc = nt_dot(q, kb)
        qpos = first + lax.broadcasted_iota(jnp.int32, (g, sub, band), 1).reshape(g * sub, band)
        kpos = start + lax.broadcasted_iota(jnp.int32, (g * sub, band), 1)
        valid = jnp.logical_and(jnp.abs(qpos - kpos) <= WINDOW, t < n_lat)
        s_loc = jnp.where(valid, s_loc, NEG_INF)
        sink = sink_ref[0, :, j * sub:(j + 1) * sub, :].reshape(g * sub, 1)
        m = jnp.maximum(jnp.maximum(jnp.max(s_ctx, axis=-1, keepdims=True),
                                    jnp.max(s_loc, axis=-1, keepdims=True)), sink)
        p_ctx, p_loc = jnp.exp(s_ctx - m), jnp.exp(s_loc - m)
        acc = (jnp.dot(p_ctx.astype(BF16), vc, preferred_element_type=F32)
               + jnp.dot(p_loc.astype(BF16), vb, preferred_element_type=F32))
        denom = acc[:, HEAD_DIM:HEAD_DIM + 1] + jnp.exp(sink - m)
        out = acc[:, :HEAD_DIM] / denom
        o_ref[0, j * sub:(j + 1) * sub, :] = jnp.concatenate(
            [out[h * sub:(h + 1) * sub] for h in range(g)], axis=-1)


def _window_attention(q, k, v, sink, seq_len, ctx_len):
    bsz, nq, s, _ = q.shape
    nkv = k.shape[1]
    g = nq // nkv
    tq = ROW_TILE
    sink_rows = jnp.broadcast_to(sink.astype(F32).reshape(nkv, g, 1, 1), (nkv, g, tq, 1))
    kv = lambda a: pl.BlockSpec((1, 1, s, a.shape[-1]), lambda b, h, t: (b, h, 0, 0))
    return pl.pallas_call(
        functools.partial(_window_attn_kernel, seq_len=seq_len, ctx_len=ctx_len, tq=tq),
        out_shape=jax.ShapeDtypeStruct((bsz, s, nq * HEAD_DIM), F32),
        grid=(bsz, nkv, s // tq),
        in_specs=[pl.BlockSpec((1, g, tq, HEAD_DIM), lambda b, h, t: (b, h, t, 0)), kv(k), kv(v),
                  pl.BlockSpec((1, g, tq, 1), lambda b, h, t: (h, 0, 0, 0))],
        out_specs=pl.BlockSpec((1, tq, g * HEAD_DIM), lambda b, h, t: (b, t, h)),
        compiler_params=_cparams("arbitrary", "arbitrary", "arbitrary"),
        name="window_attention",
    )(q, k, v, sink_rows)


DENSE_KEY_BLOCK = 1024


def _dense_attn_kernel(q_ref, k_ref, v_ref, o_ref, *, seq_len, tq, tk):
    t = pl.program_id(2)
    n_lat = seq_len // tq
    g = q_ref.shape[1]
    ctx_len = k_ref.shape[2] - seq_len
    q = q_ref[0].reshape(g * tq, HEAD_DIM)

    def step(carry, start, size):
        m, acc = carry
        s = lax.dot_general(q, k_ref[0, 0, start:start + size, :], (((1,), (1,)), ((), ())),
                            preferred_element_type=F32)
        m_new = jnp.maximum(m, jnp.max(s, axis=-1, keepdims=True))
        p = jnp.exp(s - m_new).astype(BF16)
        pv = jnp.dot(p, v_ref[0, 0, start:start + size, :], preferred_element_type=F32)
        return m_new, jnp.exp(m - m_new) * acc + pv

    def finish(carry):
        _, acc = carry
        out = acc[:, :HEAD_DIM] / acc[:, HEAD_DIM:HEAD_DIM + 1]
        o_ref[0] = jnp.concatenate([out[h * tq:(h + 1) * tq] for h in range(g)], axis=-1)

    init = (jnp.full((g * tq, 1), NEG_INF, F32), jnp.zeros((g * tq, v_ref.shape[-1]), F32))

    @pl.when(t < n_lat)
    def _():
        carry = step(init, seq_len, ctx_len)
        for j in range(seq_len // tk):
            carry = step(carry, j * tk, tk)
        finish(carry)

    @pl.when(t >= n_lat)
    def _():
        finish(step(init, seq_len, ctx_len))


def _dense_attention(q, k, v, seq_len):
    bsz, nq, s, _ = q.shape
    nkv = k.shape[1]
    g = nq // nkv
    tq = ROW_TILE
    tk = math.gcd(seq_len, DENSE_KEY_BLOCK)
    kv = lambda a: pl.BlockSpec((1, 1, s, a.shape[-1]), lambda b, h, t: (b, h, 0, 0))
    return pl.pallas_call(
        functools.partial(_dense_attn_kernel, seq_len=seq_len, tq=tq, tk=tk),
        out_shape=jax.ShapeDtypeStruct((bsz, s, nq * HEAD_DIM), F32),
        grid=(bsz, nkv, s // tq),
        in_specs=[pl.BlockSpec((1, g, tq, HEAD_DIM), lambda b, h, t: (b, h, t, 0)), kv(k), kv(v)],
        out_specs=pl.BlockSpec((1, tq, g * HEAD_DIM), lambda b, h, t: (b, t, h)),
        compiler_params=_cparams("arbitrary", "arbitrary", "arbitrary"),
        name="dense_attention",
    )(q, k, v)


def _softplus(x):
    return jnp.maximum(x, 0.0) + jnp.log(1.0 + jnp.exp(-jnp.abs(x)))


def _split(a):
    bits = lax.bitcast_convert_type(a, jnp.uint32) & jnp.uint32(0xFFFF0000)
    hi = lax.bitcast_convert_type(bits, F32)
    return hi, a - hi


def _rhs3(b):
    hi, lo = _split(b)
    return jnp.concatenate([hi, hi, lo], axis=-2).astype(BF16)


def _split3(a):
    hi, rest = _split(a)
    mid, lo = _split(rest)
    return hi, mid, lo


def _mm_exact_lhs(tbl3, x):
    return jnp.dot(tbl3, jnp.concatenate(_split3(x), axis=0).astype(BF16), preferred_element_type=F32)


def _mm_exact_rhs(x, tbl3):
    return jnp.dot(jnp.concatenate(_split3(x), axis=-1).astype(BF16), tbl3, preferred_element_type=F32)


def _mm_split(a, b):
    ah, al = _split(a)
    return jnp.dot(jnp.concatenate([ah, al, ah], axis=-1).astype(BF16), _rhs3(b), preferred_element_type=F32)


RW_DEPTH = 1


def _lhs4(a):
    return a.astype(BF16)


def _rhs4(b):
    return b.astype(BF16)


def _rhs4_nt(b):
    return b.astype(BF16)


def _bmm(lf, rf):
    return lax.dot_general(lf, rf, (((2,), (1,)), ((0,), (0,))), preferred_element_type=F32)


RW_SLOTS_PER_STEP = 16


def _rwkv_par_kernel(z_ref, w0_ref, wup_ref, a0_ref, aup_ref, kk_ref, ka_ref,
                     bd_ref, cum_ref,
                     rp3_ref, yvq_ref, rvk_ref,
                     lbig_ref, rbig_ref, at_ref, rt_ref, vv3_ref, bct_ref, kct_ref, wt_ref,
                     *, w, lora):
    tc = RW_CHUNK
    nc, nh = ROW_TILE // tc, w // HEAD_DIM
    nb = RW_SLOTS_PER_STEP
    z = z_ref[0]
    r, k, v = z[:, :w], z[:, w:2 * w], z[:, 2 * w:3 * w]
    w_low = jnp.tanh(z[:, 3 * w:3 * w + lora])
    a_low = z[:, 3 * w + lora:]
    kk = k * kk_ref[...]
    kk = kk / jnp.maximum(jnp.sqrt(_mm_exact_rhs(kk * kk, bd_ref[...])), 1e-12)
    ksum = jnp.zeros_like(k)
    for d in range(2):
        w_log = -_softplus(-(w0_ref[d:d + 1, :] + _mm_split(w_low, wup_ref[d]))) - 0.5
        lw = -jnp.exp(w_log)
        a = 1.0 / (1.0 + jnp.exp(-(a0_ref[d:d + 1, :] + _mm_split(a_low, aup_ref[d]))))
        kd = k * (1.0 + (a - 1.0) * ka_ref[...])
        ksum = ksum + kd
        sums = _mm_exact_lhs(cum_ref[d], lw)
        c, ctot = sums[:ROW_TILE], sums[ROW_TILE:]
        e_neg, e_rem = jnp.exp(-c), jnp.exp(ctot - c)
        b = kk * a
        at, rt = -kk * jnp.exp(c - lw), r * jnp.exp(c)
        w_tot = jnp.exp(ctot)
        bh, kh = b * e_neg, kd * e_neg
        bct, kct = (b * e_rem).T, (kd * e_rem).T
        for ci in range(nc):
            for h in range(nh):
                slot = (d * nc + ci) * nh + h
                rows, cols = slice(ci * tc, (ci + 1) * tc), slice(h * HEAD_DIM, (h + 1) * HEAD_DIM)
                lbig_ref[slot, :tc, :] = _lhs4(at[rows, cols])
                lbig_ref[slot, tc:, :] = _lhs4(rt[rows, cols])
                rbig_ref[slot, :tc, :] = _rhs4_nt(bh[rows, cols])
                rbig_ref[slot, tc:, :] = _rhs4_nt(kh[rows, cols])
                at_ref[slot] = at[rows, cols]
                rt_ref[slot] = rt[rows, cols]
                vv3_ref[slot] = _rhs4(v[rows, cols])
                bct_ref[slot] = _lhs4(bct[cols, rows])
                kct_ref[slot] = _lhs4(kct[cols, rows])
                wt_ref[slot] = w_tot[ci * tc:ci * tc + SUBLANES, cols]
    rvk_ref[0] = jnp.concatenate([r, v, ksum], axis=-1)

    row = lax.broadcasted_iota(jnp.int32, (tc, tc), 0)
    col = lax.broadcasted_iota(jnp.int32, (tc, tc), 1)
    eye = row == col

    def level_mask(s):
        return jnp.logical_and(row // (2 * s) == col // (2 * s), row // s != col // s)

    steps_per_dir = nc * nh // nb

    def chunk_group(g, carry):
        slots = pl.ds(pl.multiple_of(g * nb, nb), nb)
        d = g // steps_per_dir
        fwd = d == 0
        before = jnp.logical_or(jnp.logical_and(fwd, row > col),
                                jnp.logical_and(jnp.logical_not(fwd), row < col))
        upto = jnp.logical_or(before, eye)
        big = lax.dot_general(lbig_ref[slots], rbig_ref[slots], (((2,), (2,)), ((0,), (0,))),
                              preferred_element_type=F32)
        a_ab = jnp.where(before, big[:, :tc, :tc], 0.0)
        a_ak = jnp.where(before, big[:, :tc, tc:], 0.0)
        a_rb = jnp.where(upto, big[:, tc:, :tc], 0.0)
        a_rk = jnp.where(upto, big[:, tc:, tc:], 0.0)
        x = jnp.where(eye, 1.0, jnp.where(level_mask(1), a_ab, 0.0))
        s = 2
        while s < tc:
            half = _bmm(_lhs4(x), _rhs4(jnp.where(level_mask(s), a_ab, 0.0)))
            x = x + _bmm(_lhs4(half), _rhs4(x))
            s *= 2
        vv3 = vv3_ref[slots]
        akrk = _bmm(jnp.concatenate([_lhs4(a_ak), _lhs4(a_rk)], axis=1), vv3)
        xa3 = _rhs4(_bmm(_lhs4(x), _rhs4(jnp.concatenate([at_ref[slots], akrk[:, :tc]], axis=-1))))
        ra = _bmm(_lhs4(a_rb), xa3)
        pq = _bmm(bct_ref[slots], xa3)
        rp = rt_ref[slots] + ra[:, :, :HEAD_DIM]
        yv = ra[:, :, HEAD_DIM:] + akrk[:, tc:]
        p = jnp.where(eye[:HEAD_DIM, :HEAD_DIM], wt_ref[slots][:, 0:1, :], 0.0) + pq[:, :, :HEAD_DIM]
        q = pq[:, :, HEAD_DIM:] + _bmm(kct_ref[slots], vv3)
        rp3 = jnp.concatenate([_lhs4(rp), _lhs4(p)], axis=1)
        yvq = jnp.concatenate([yv, q], axis=1)
        for j0 in range(0, nb, nh):
            ci = (g % steps_per_dir) * (nb // nh) + j0 // nh
            rp3_ref[0, d, ci] = jnp.concatenate([rp3[j0 + h] for h in range(nh)], axis=-1)
            yvq_ref[0, d, ci] = jnp.concatenate([yvq[j0 + h] for h in range(nh)], axis=-1)
        return carry
    lax.fori_loop(0, 2 * steps_per_dir, chunk_group, 0, unroll=True)


def _chunk_matrices():
    t = np.arange(ROW_TILE)
    same = (t[:, None] // RW_CHUNK) == (t[None, :] // RW_CHUNK)
    tabs = []
    for run in (same & (t[None, :] <= t[:, None]), same & (t[None, :] >= t[:, None])):
        m = np.concatenate([run, same], axis=0).astype(np.float32).astype(BF16)
        tabs.append(np.concatenate([m, m, m], axis=1))
    return np.stack(tabs)


def _rwkv_par(rw, w0, w_up, a0, a_up, k_k, k_a, w):
    bsz, s, width = rw.shape
    nt = s // ROW_TILE
    nh, nc = w // HEAD_DIM, ROW_TILE // RW_CHUNK
    lora = w_up.shape[1]
    full = lambda a: pl.BlockSpec(a.shape, lambda b, t: (0,) * a.ndim)
    cum = _chunk_matrices()
    bd = _head_block_diag(w, 1.0)
    vec = lambda a: a.reshape(1, w)
    n_chunks = s // RW_CHUNK
    nslots = 2 * nc * nh
    tc, hd = RW_CHUNK, HEAD_DIM
    consts = (w0, w_up, a0, a_up, vec(k_k), vec(k_a), bd, cum)
    return pl.pallas_call(
        functools.partial(_rwkv_par_kernel, w=w, lora=lora),
        out_shape=[jax.ShapeDtypeStruct((bsz, 2, n_chunks, tc + hd, RW_DEPTH * w), BF16),
                   jax.ShapeDtypeStruct((bsz, 2, n_chunks, tc + hd, w), F32),
                   jax.ShapeDtypeStruct((bsz, s, 3 * w), F32)],
        grid=(bsz, nt),
        in_specs=[pl.BlockSpec((1, ROW_TILE, width), lambda b, t: (b, t, 0))] + [full(a) for a in consts],
        out_specs=[pl.BlockSpec((1, 2, nc, tc + hd, RW_DEPTH * w), lambda b, t: (b, 0, t, 0, 0)),
                   pl.BlockSpec((1, 2, nc, tc + hd, w), lambda b, t: (b, 0, t, 0, 0)),
                   pl.BlockSpec((1, ROW_TILE, 3 * w), lambda b, t: (b, t, 0))],
        scratch_shapes=[pltpu.VMEM((nslots, 2 * tc, RW_DEPTH * hd), BF16),
                        pltpu.VMEM((nslots, 2 * tc, RW_DEPTH * hd), BF16),
                        pltpu.VMEM((nslots, tc, hd), F32),
                        pltpu.VMEM((nslots, tc, hd), F32),
                        pltpu.VMEM((nslots, RW_DEPTH * tc, hd), BF16),
                        pltpu.VMEM((nslots, hd, RW_DEPTH * tc), BF16),
                        pltpu.VMEM((nslots, hd, RW_DEPTH * tc), BF16),
                        pltpu.VMEM((nslots, SUBLANES, HEAD_DIM), F32)],
        compiler_params=_cparams("arbitrary", "arbitrary"),
        name="rwkv_chunk_prep",
    )(rw, *consts)


def _rwkv_seq_kernel(rp3f_ref, yvqf_ref, rp3b_ref, yvqb_ref, yf_ref, yb_ref, g_ref):
    @pl.when(pl.program_id(1) == 0)
    def _():
        g_ref[...] = jnp.zeros_like(g_ref)

    w = g_ref.shape[-1]
    tc = RW_CHUNK
    nh = w // HEAD_DIM
    grp = rp3f_ref.shape[2]
    row_head = lax.broadcasted_iota(jnp.int32, (w, w), 0) // HEAD_DIM
    col_head = lax.broadcasted_iota(jnp.int32, (w, w), 1) // HEAD_DIM
    on_diag = row_head == col_head
    for step in range(grp):
        for d, (rp3, yvq, y) in enumerate(((rp3f_ref, yvqf_ref, yf_ref), (rp3b_ref, yvqb_ref, yb_ref))):
            ci = step if d == 0 else grp - 1 - step
            out = jnp.dot(rp3[0, 0, ci], _rhs4(g_ref[d]), preferred_element_type=F32) + yvq[0, 0, ci]
            y[0, ci * tc:(ci + 1) * tc, :] = out[:tc]
            g_ref[d] = jnp.where(on_diag, jnp.concatenate([out[tc:]] * nh, axis=0), 0.0)


RW_SCAN_GROUP = 4


def _rwkv_seq(rp3, yvq, n_lat_chunks):
    bsz, _, n_chunks = rp3.shape[:3]
    w = yvq.shape[-1]
    grp = RW_SCAN_GROUP
    assert n_lat_chunks % grp == 0 and n_chunks % grp == 0
    n_groups, n_lat, n_ctx = n_chunks // grp, n_lat_chunks // grp, (n_chunks - n_lat_chunks) // grp
    order = (lambda i: jnp.where(i < n_ctx, n_lat + i, i - n_ctx),
             lambda i: n_groups - 1 - i)
    blk = lambda d, a: pl.BlockSpec((1, 1, grp) + a.shape[3:], lambda b, i: (b, d, order[d](i), 0, 0))
    out = lambda d: pl.BlockSpec((1, grp * RW_CHUNK, w), lambda b, i: (b, order[d](i), 0))
    shp = jax.ShapeDtypeStruct((bsz, n_chunks * RW_CHUNK, w), F32)
    return pl.pallas_call(
        _rwkv_seq_kernel,
        out_shape=[shp, shp],
        grid=(bsz, n_groups),
        in_specs=[blk(0, rp3), blk(0, yvq), blk(1, rp3), blk(1, yvq)],
        out_specs=[out(0), out(1)],
        scratch_shapes=[pltpu.VMEM((2, w, w), F32)],
        compiler_params=_cparams("arbitrary", "arbitrary"),
        name="rwkv_state_scan",
    )(rp3, yvq, rp3, yvq)


def _rwkv_readout_tile(y, rvk, r_k, ln_g, ln_b, head_mean, head_sum):
    w = r_k.shape[-1]
    r, v, ksum = rvk[:, :w], rvk[:, w:2 * w], rvk[:, 2 * w:]
    yc = y - _mm_exact_rhs(y, head_mean)
    var = _mm_exact_rhs(yc * yc, head_mean)
    bonus = _mm_exact_rhs(r * ksum * r_k, head_sum) * v
    return yc * lax.rsqrt(var + RW_GN_EPS) * ln_g + ln_b + bonus


@functools.lru_cache(maxsize=None)
def _filter_position_features(seq_len, bands, width):
    n = np.arange(2 * seq_len)
    pos = np.where(n < seq_len, n, 2 * seq_len - n) % seq_len
    t = np.linspace(0.0, 1.0, seq_len)[pos][:, None]
    wpos = (2.0 * math.pi / seq_len) * pos[:, None]
    f = np.linspace(1e-4, bands - 1, bands)[None, :]
    z = np.concatenate([t, np.cos(f * wpos), np.sin(f * wpos)], axis=-1)
    return np.pad(z, ((0, 0), (0, width - z.shape[1]))).astype(np.float32)


def _hyena_filter_kernel(z_ref, fw1_ref, fb1_ref, freq_ref, fw2_ref, fb2_ref, fw3_ref, delta_ref, o_ref, *, seq_len):
    freq = freq_ref[...]
    rows = ROW_TILE
    n_orders, _, c = o_ref.shape

    def taps(i, norms):
        blk = pl.ds(pl.multiple_of(i * rows, rows), rows)
        z = z_ref[blk, :]
        h = jnp.sin(freq * (_mm_split(z, fw1_ref[...]) + fb1_ref[...]))
        h = jnp.sin(freq * (_mm_split(h, fw2_ref[...]) + fb2_ref[...]))
        h = _mm_split(h, fw3_ref[...])
        n = i * rows + lax.broadcasted_iota(jnp.int32, (rows, 1), 0)
        decay = jnp.exp(-z[:, 0:1] * jnp.abs(delta_ref[...]))
        out = []
        for o in range(n_orders):
            ho = jnp.where(n < seq_len, h[:, 2 * o * c:(2 * o + 1) * c], h[:, (2 * o + 1) * c:(2 * o + 2) * c])
            ho = ho * decay
            o_ref[o, blk, :] = jnp.where(n != seq_len, ho, 0.0)
            out.append(norms[o] + jnp.sum(jnp.abs(ho), axis=0, keepdims=True))
        return tuple(out)
    zero = jnp.zeros((1, c), F32)
    norms = lax.fori_loop(0, z_ref.shape[0] // rows, taps, (zero,) * n_orders)

    def normalise(i, carry):
        blk = pl.ds(pl.multiple_of(i * rows, rows), rows)
        for o in range(n_orders):
            o_ref[o, blk, :] = o_ref[o, blk, :] / norms[o]
        return carry
    lax.fori_loop(0, z_ref.shape[0] // rows, normalise, 0)


def _hyena_two_sided_filters(seq_len, fw1, fb1, freq, fw2, fb2, fw3, width):
    ffn = fw1.shape[1]
    bands = (fw1.shape[0] - 1) // 2
    z = _filter_position_features(seq_len, bands, ffn)
    fw1p = jnp.pad(fw1, ((0, ffn - fw1.shape[0]), (0, 0)))
    max_decay = math.log(HY_DECAY_TARGET) / HY_FAST_DECAY
    min_decay = math.log(HY_DECAY_TARGET) / HY_SLOW_DECAY
    deltas = jnp.linspace(min_decay, max_decay, width, dtype=F32).reshape(1, width)
    row = lambda a: a.reshape(1, ffn)
    full = lambda shape: pl.BlockSpec(shape, lambda i: (0,) * len(shape))
    return pl.pallas_call(
        functools.partial(_hyena_filter_kernel, seq_len=seq_len),
        out_shape=jax.ShapeDtypeStruct((2, 2 * seq_len, width), F32),
        grid=(1,),
        in_specs=[full(z.shape), full((ffn, ffn)), full((1, ffn)), full((1, ffn)), full((ffn, ffn)),
                  full((1, ffn)), full(fw3.shape), full((1, width))],
        out_specs=full((2, 2 * seq_len, width)),
        compiler_params=_cparams("arbitrary"),
        name="hyena_filter",
    )(z, fw1p, row(fb1), row(freq), fw2, row(fb2), fw3, deltas)


def kernel(x, c, ctx, c_ctx, mod_w, mod_b, norm_g, w_in, w_out, hy_conv, hy_fw1, hy_fb1, hy_freq, hy_fw2,
           hy_fb2, hy_fw3, hy_bias, rw_conv, rw_w0, rw_w_up, rw_a0, rw_a_up, rw_k_k, rw_k_a, rw_r_k,
           rw_ln_g, rw_ln_b, wa_sink, fa_q_norm, fa_k_norm, final_g):
    bsz, seq_len, d = x.shape
    ctx_len = ctx.shape[1]
    depth = w_in.shape[0]
    w_hy = hy_bias.shape[-1]
    w_rw = rw_w0.shape[-1]
    n_wa_heads = wa_sink.shape[-1]
    w_q = n_wa_heads * HEAD_DIM
    w_kv = w_q // 2
    lora = rw_w_up.shape[2] + rw_a_up.shape[2]
    branch_w = (3 * w_hy, 3 * w_rw + lora, w_q + 2 * w_kv, w_q + 2 * w_kv)
    gate_w = (w_hy, w_rw, w_q, w_q)
    assert seq_len % ROW_TILE == 0 and ctx_len % ROW_TILE == 0 and bsz % 2 == 0
    n_lat_tiles = seq_len // ROW_TILE

    starts = np.cumsum([0] + [bw + gw for bw, gw in zip(branch_w, gate_w)])
    cols = np.concatenate([np.arange(s0, s0 + bw) for s0, bw in zip(starts, branch_w)]
                          + [np.arange(s0 + bw, s0 + bw + gw) for s0, bw, gw in zip(starts, branch_w, gate_w)])
    w_in_p = w_in[:, :, cols].astype(BF16)
    w_out_b = w_out.astype(BF16)
    widths = branch_w + (sum(gate_w),)

    pad_rows = (-(bsz + 1)) % 8
    cond = jnp.concatenate([c, c_ctx[None], jnp.zeros((pad_rows, d), F32)], axis=0)
    mod = _modulation(cond, mod_w, mod_b)
    mod_lat = mod[:, :bsz].reshape(depth, bsz, 3, d)
    mod_ctx = jnp.broadcast_to(mod[:, bsz].reshape(depth, 1, 3, d), (depth, bsz, 3, d))
    mods = jnp.stack([mod_lat, mod_ctx], axis=2)

    rope = _rope_tables(seq_len, ctx_len)
    xs = jnp.concatenate([x, ctx], axis=1)
    for l in range(depth):
        last = l == depth - 1
        conv_w = jnp.concatenate([hy_conv[l], rw_conv[l]], axis=-1)
        hyc, rwc, gates, qw, kw, vw, qf, kf, vf = _inproj(
            xs, mods[l], norm_g[l], w_in_p[l], conv_w, widths, n_lat_tiles, rope,
            fa_q_norm[l], fa_k_norm[l], w_q, w_kv)

        filt = functools.partial(_hyena_two_sided_filters, fw1=hy_fw1[l], fb1=hy_fb1[l], freq=hy_freq[l],
                                 fw2=hy_fw2[l], fb2=hy_fb2[l], fw3=hy_fw3[l], width=w_hy)
        bias = hy_bias[l].reshape(2, 1, w_hy)
        spec = _filter_spectrum(filt(seq_len))
        nt_hy = w_hy // LANES
        y1 = _fftconv_gated(hyc, 0, hyc, nt_hy, spec, 0, bias, seq_len)
        a_lat = _fftconv_gated(y1, 0, hyc, 2 * nt_hy, spec, 1, bias, seq_len)
        a_ctx = a_lat if last else _ctx_hyena(hyc, seq_len // ctx_len, filt(ctx_len), bias, ctx_len, w_hy)

        rp3, yvq, rvk = _rwkv_par(rwc, rw_w0[l], rw_w_up[l], rw_a0[l], rw_a_up[l], rw_k_k[l], rw_k_a[l], w_rw)
        yf, yb = _rwkv_seq(rp3, yvq, seq_len // RW_CHUNK)

        c_mix = _window_attention(qw, kw, vw, wa_sink[l], seq_len, ctx_len)
        d_mix = _dense_attention(qf, kf, vf, seq_len)

        xs = _outproj(a_lat, a_ctx, yf, yb, rvk, rw_r_k[l], rw_ln_g[l], rw_ln_b[l], c_mix, d_mix, gates, xs,
                      mods[l], w_out_b[l], final_g, n_lat_tiles, last)
    return xs
```

```python
import functools
import math

import jax
import jax.numpy as jnp
import numpy as np
from jax import lax
from jax.experimental import pallas as pl
from jax.experimental.pallas import tpu as pltpu

HEAD_DIM = 64
GRID_W = 64
WINDOW = 128
NORM_EPS = 1e-6
RW_GN_EPS = 64e-5
NEG_INF = -1e30
ROPE_THETA = 10000.0
HY_FAST_DECAY = 0.3
HY_SLOW_DECAY = 1.5
HY_DECAY_TARGET = 1e-2

ROW_TILE = 256
LANES = 128
SUBLANES = 8
FFT_N1 = 64
FFT_UNROLL = 8
FFT_PLANE_PAD = 8
RW_CHUNK = 64
VMEM_LIMIT = 60 * 1024 * 1024

F32 = jnp.float32
BF16 = jnp.bfloat16
HI = lax.Precision.HIGHEST


def _dot(a, b):
    return jnp.dot(a, b, preferred_element_type=F32, precision=HI)


def _dot_nt(a, b):
    return lax.dot_general(a, b, (((1,), (1,)), ((), ())), preferred_element_type=F32, precision=HI)


def _dot_tn(a, b):
    return lax.dot_general(a, b, (((0,), (0,)), ((), ())), preferred_element_type=F32, precision=HI)


def _cparams(*sem):
    return pltpu.CompilerParams(dimension_semantics=sem, vmem_limit_bytes=VMEM_LIMIT)


def _const_spec(shape):
    return pl.BlockSpec(shape, lambda *_: (0,) * len(shape), pipeline_mode=pl.Buffered(1))


def _silu(x):
    return x * (1.0 / (1.0 + jnp.exp(-x)))


def _mod_kernel(c_ref, w_ref, b_ref, o_ref):
    o_ref[0] = _dot(_silu(c_ref[...]), w_ref[0]) + b_ref[0]


def _modulation(cond, mod_w, mod_b):
    depth, d, d3 = mod_w.shape
    rows = cond.shape[0]
    return pl.pallas_call(
        _mod_kernel,
        out_shape=jax.ShapeDtypeStruct((depth, rows, d3), F32),
        grid=(depth,),
        in_specs=[pl.BlockSpec((rows, d), lambda l: (0, 0)),
                  pl.BlockSpec((1, d, d3), lambda l: (l, 0, 0)),
                  pl.BlockSpec((1, 1, d3), lambda l: (l, 0, 0))],
        out_specs=pl.BlockSpec((1, rows, d3), lambda l: (l, 0, 0)),
        compiler_params=_cparams("arbitrary"),
        name="modulation",
    )(cond, mod_w, mod_b.reshape(depth, 1, d3))


def _blockreal(m):
    return np.block([[m.real, -m.imag], [m.imag, m.real]])


@functools.lru_cache(maxsize=None)
def _fft_tables(seq_len):
    n = 2 * seq_len
    n1, n2 = FFT_N1, n // FFT_N1
    h1 = n1 // 2
    j2 = np.arange(n2)[:, None, None]
    k1 = np.arange(n1)[None, :, None]
    t1 = np.exp(-2j * np.pi * (j2 * k1 / n + k1 * np.arange(n1)[None, None, :] / n1))
    t1_data = np.stack([_blockreal(t1[j][:, :h1]) for j in range(n2)])
    t1_real = np.concatenate([t1.real, t1.imag], axis=1)
    f2 = np.exp(-2j * np.pi * np.outer(np.arange(n2), np.arange(n2)) / n2)
    f2_fwd = _blockreal(f2)
    f2_inv = _blockreal(np.conj(f2))
    t4 = np.exp(2j * np.pi * (np.arange(h1)[None, :, None] * np.arange(n1)[None, None, :] / n1
                              + j2 * np.arange(n1)[None, None, :] / n)) / n
    t4 = np.stack([_blockreal(t4[j]) for j in range(n2)])
    f2_parts = np.concatenate([f2.real, f2.imag], axis=0)
    return tuple(_lhs3_table(t) for t in (t1_data, t1_real, f2_fwd, f2_parts, t4))


def _lhs3_table(m):
    hi = m.astype(np.float32).astype(BF16)
    lo = (m - hi.astype(np.float64)).astype(np.float32).astype(BF16)
    return np.concatenate([hi, lo, hi], axis=-1)


def _mm3(tbl3, x):
    return jnp.dot(tbl3, _rhs3(x), preferred_element_type=F32)


def _spectrum_kernel(k_ref, t1_ref, f2_ref, o_ref, a_ref, *, n1, n2):
    def stage1(j, carry):
        rows = k_ref[0, pl.ds(j, n1, stride=n2), :]
        a_ref[pl.ds(j, 2 * n1, stride=n2 + FFT_PLANE_PAD), :] = _mm3(t1_ref[j], rows)
        return carry
    lax.fori_loop(0, n2, stage1, 0, unroll=FFT_UNROLL)

    def stage2(i, carry):
        pitch = n2 + FFT_PLANE_PAD
        re = a_ref[pl.ds(pl.multiple_of(i * pitch, 8), n2), :]
        im = a_ref[pl.ds(pl.multiple_of((n1 + i) * pitch, 8), n2), :]
        o_ref[0, i] = _mm3(f2_ref[...], jnp.concatenate([re, im], axis=0))
        return carry
    lax.fori_loop(0, n1, stage2, 0, unroll=2)


def _filter_spectrum(kfilt):
    g, n, w = kfilt.shape
    n1, n2 = FFT_N1, n // FFT_N1
    _, t1_real, f2_fwd, _, _ = _fft_tables(n // 2)
    const = _const_spec
    return pl.pallas_call(
        functools.partial(_spectrum_kernel, n1=n1, n2=n2),
        out_shape=jax.ShapeDtypeStruct((g, n1, 2 * n2, w), F32),
        grid=(g, w // LANES),
        in_specs=[pl.BlockSpec((1, n, LANES), lambda gi, j: (gi, 0, j)),
                  const(t1_real.shape), const(f2_fwd.shape)],
        out_specs=pl.BlockSpec((1, n1, 2 * n2, LANES), lambda gi, j: (gi, 0, 0, j)),
        scratch_shapes=[pltpu.VMEM((2 * n1 * (n2 + FFT_PLANE_PAD), LANES), F32)],
        compiler_params=_cparams("arbitrary", "arbitrary"),
        name="hyena_filter_spectrum",
    )(kfilt, t1_real, f2_fwd)


def _fftconv_kernel(u_ref, m_ref, spec_ref, bias_ref, t1_ref, f2p_ref, t4_ref, o_ref, ar_ref, ai_ref,
                    *, n1, n2):
    h1 = n1 // 2
    c = u_ref.shape[-1]
    pitch = n2 + FFT_PLANE_PAD

    def stage1(j, carry):
        za = u_ref[0, pl.ds(j, h1, stride=n2), :]
        zb = u_ref[1, pl.ds(j, h1, stride=n2), :]
        res = _mm3(t1_ref[j], jnp.concatenate([za, zb], axis=0))
        ar_ref[pl.ds(j, n1, stride=pitch), :] = res[:n1]
        ai_ref[pl.ds(j, n1, stride=pitch), :] = res[n1:]
        return carry
    lax.fori_loop(0, n2, stage1, 0, unroll=FFT_UNROLL)

    def stage2(i, carry):
        rows = pl.ds(pl.multiple_of(i * pitch, SUBLANES), n2)
        p = _mm3(f2p_ref[...], jnp.concatenate([ar_ref[rows, :], ai_ref[rows, :]], axis=-1))
        xr = p[:n2, :c] - p[n2:, c:]
        xi = p[:n2, c:] + p[n2:, :c]
        kr, ki = spec_ref[0, i, :n2, :], spec_ref[0, i, n2:, :]
        y = jnp.concatenate([xr * kr - xi * ki, xr * ki + xi * kr], axis=-1)
        q = _mm3(f2p_ref[...], y)
        ar_ref[rows, :] = q[:n2, :c] + q[n2:, c:]
        ai_ref[rows, :] = q[:n2, c:] - q[n2:, :c]
        return carry
    lax.fori_loop(0, n1, stage2, 0, unroll=8)

    bias = bias_ref[0]

    def stage4(j, carry):
        planes = pl.ds(j, n1, stride=pitch)
        y = _mm3(t4_ref[j], jnp.concatenate([ar_ref[planes, :], ai_ref[planes, :]], axis=0))
        rows = pl.ds(j, h1, stride=n2)
        for p in range(2):
            u = u_ref[p, rows, :]
            o_ref[p, rows, :] = m_ref[p, rows, :] * (y[p * h1:(p + 1) * h1] + bias * u)
        return carry
    lax.fori_loop(0, n2, stage4, 0, unroll=FFT_UNROLL)


def _fftconv_gated(u, u_col, mult, mult_col, spec, conv_idx, bias, seq_len):
    bsz = u.shape[0]
    w = spec.shape[-1]
    n = 2 * seq_len
    n1, n2 = FFT_N1, n // FFT_N1
    t1_data, _, _, f2_parts, t4 = _fft_tables(seq_len)
    const = _const_spec
    return pl.pallas_call(
        functools.partial(_fftconv_kernel, n1=n1, n2=n2),
        out_shape=jax.ShapeDtypeStruct((bsz, seq_len, w), F32),
        grid=(w // LANES, bsz // 2),
        in_specs=[pl.BlockSpec((2, seq_len, LANES), lambda j, p: (p, 0, u_col + j)),
                  pl.BlockSpec((2, seq_len, LANES), lambda j, p: (p, 0, mult_col + j)),
                  pl.BlockSpec((1, n1, 2 * n2, LANES), lambda j, p: (conv_idx, 0, 0, j),
                               pipeline_mode=pl.Buffered(1)),
                  pl.BlockSpec((1, 1, LANES), lambda j, p: (conv_idx, 0, j)),
                  const(t1_data.shape), const(f2_parts.shape), const(t4.shape)],
        out_specs=pl.BlockSpec((2, seq_len, LANES), lambda j, p: (p, 0, j)),
        scratch_shapes=[pltpu.VMEM((n1 * (n2 + FFT_PLANE_PAD), LANES), F32)] * 2,
        compiler_params=_cparams("arbitrary", "arbitrary"),
        name="hyena_fftconv",
    )(u, mult, spec, bias, t1_data, f2_parts, t4)


@functools.lru_cache(maxsize=None)
def _small_fft_tables(seq_len):
    n = 2 * seq_len
    f = np.exp(-2j * np.pi * np.outer(np.arange(n), np.arange(n)) / n)
    fwd = _blockreal(f[:, :seq_len])
    real = np.concatenate([f.real, f.imag], axis=0)
    inv = _blockreal(np.conj(f)[:seq_len, :] / n)
    as32 = lambda a: jnp.asarray(a, dtype=F32)
    return as32(fwd), as32(real), as32(inv)


def _ctx_hyena_kernel(v_ref, x1_ref, x2_ref, k_ref, bias_ref, fwd_ref, real_ref, inv_ref, o_ref, *, seq_len):
    n = 2 * seq_len

    def conv(ua, ub, g):
        spec = _dot(real_ref[...], k_ref[g])
        x = _dot(fwd_ref[...], jnp.concatenate([ua, ub], axis=0))
        xr, xi, kr, ki = x[:n], x[n:], spec[:n], spec[n:]
        y = _dot(inv_ref[...], jnp.concatenate([xr * kr - xi * ki, xr * ki + xi * kr], axis=0))
        b = bias_ref[g]
        return y[:seq_len] + b * ua, y[seq_len:] + b * ub

    c1a, c1b = conv(v_ref[0], v_ref[1], 0)
    y1a, y1b = x1_ref[0] * c1a, x1_ref[1] * c1b
    c2a, c2b = conv(y1a, y1b, 1)
    o_ref[0] = x2_ref[0] * c2a
    o_ref[1] = x2_ref[1] * c2b


def _ctx_hyena(hyc, row_block, kfilt, bias, seq_len, w):
    bsz = hyc.shape[0]
    n = 2 * seq_len
    fwd, real, inv = _small_fft_tables(seq_len)
    nt = w // LANES
    col = lambda c0: pl.BlockSpec((2, seq_len, LANES), lambda j, p: (p, row_block, c0 + j))
    return pl.pallas_call(
        functools.partial(_ctx_hyena_kernel, seq_len=seq_len),
        out_shape=jax.ShapeDtypeStruct((bsz, seq_len, w), F32),
        grid=(nt, bsz // 2),
        in_specs=[col(0), col(nt), col(2 * nt),
                  pl.BlockSpec((2, n, LANES), lambda j, p: (0, 0, j)),
                  pl.BlockSpec((2, 1, LANES), lambda j, p: (0, 0, j)),
                  _const_spec((2 * n, 2 * seq_len)), _const_spec((2 * n, n)), _const_spec((2 * seq_len, 2 * n))],
        out_specs=pl.BlockSpec((2, seq_len, LANES), lambda j, p: (p, 0, j)),
        compiler_params=_cparams("arbitrary", "arbitrary"),
        name="hyena_ctx",
    )(hyc, hyc, hyc, kfilt, bias, fwd, real, inv)


def _rms(x, g):
    return x * lax.rsqrt(jnp.mean(x * x, axis=-1, keepdims=True) + NORM_EPS) * g


def _inproj_kernel(x_ref, xprev_ref, xnext_ref, mod_ref, g_ref, w_ref, cw_ref,
                   cos_ref, shi_ref, slo_ref, qg_ref, kg_ref, bdq_ref, bdk_ref,
                   hy_ref, rw_ref, gt_ref, qw_ref, kw_ref, vw_ref, qf_ref, kf_ref, vf_ref,
                   *, widths, wq, wk, n_lat_tiles, n_tiles):
    shift, scale = mod_ref[0, 0, 0:1, :], mod_ref[0, 0, 1:2, :]
    norm_mod = lambda x: (_rms(x, g_ref[...]) * (1.0 + scale) + shift).astype(BF16)
    z = jnp.dot(norm_mod(x_ref[0]), w_ref[...], preferred_element_type=F32)
    offs = np.cumsum((0,) + tuple(widths))
    part = lambda i: z[:, offs[i]:offs[i + 1]]
    n_conv = offs[2]
    halo = jnp.dot(norm_mod(jnp.concatenate([xprev_ref[0], xnext_ref[0]], axis=0)), w_ref[:, :n_conv],
                   preferred_element_type=F32)
    conv = _short_conv_tile(z[:, :n_conv], halo[:SUBLANES], halo[SUBLANES:], cw_ref[...], pl.program_id(1),
                            n_lat_tiles, n_tiles)
    hy_ref[0] = conv[:, :offs[1]]
    rw_ref[0] = conv[:, offs[1]:]
    gt_ref[0] = _silu(part(4))
    _qk_emit(part(2), part(3), cos_ref[...], shi_ref[...], slo_ref[...], qg_ref[...], kg_ref[...],
             bdq_ref[...], bdk_ref[...], qw_ref, kw_ref, vw_ref, qf_ref, kf_ref, vf_ref, wq, wk)


def _inproj(xs, mods, norm_g, w_perm, conv_w, widths, n_lat_tiles, rope, q_gain, k_gain, wq, wk):
    bsz, s, d = xs.shape
    nt = s // ROW_TILE
    nq, nk = wq // HEAD_DIM, wk // HEAD_DIM
    prev, nxt = _halo_specs(d, nt)
    tile = lambda wd: pl.BlockSpec((1, ROW_TILE, wd), lambda b, t: (b, t, 0))
    tab = pl.BlockSpec((ROW_TILE, HEAD_DIM), lambda b, t: (t, 0))
    hm = lambda n, wd=HEAD_DIM: pl.BlockSpec((1, n, ROW_TILE, wd), lambda b, t: (b, 0, t, 0))
    flat = lambda wd: jax.ShapeDtypeStruct((bsz, s, wd), F32)
    heads = lambda n, wd=HEAD_DIM: jax.ShapeDtypeStruct((bsz, n, s, wd), BF16)
    return pl.pallas_call(
        functools.partial(_inproj_kernel, widths=widths, wq=wq, wk=wk, n_lat_tiles=n_lat_tiles, n_tiles=nt),
        out_shape=[flat(widths[0]), flat(widths[1]), flat(widths[4]),
                   heads(nq), heads(nk), heads(nk, LANES), heads(nq), heads(nk), heads(nk, LANES)],
        grid=(bsz, nt),
        in_specs=[tile(d), prev, nxt,
                  pl.BlockSpec((1, 1, 3, d), lambda b, t: (b, t // n_lat_tiles, 0, 0)),
                  pl.BlockSpec((1, d), lambda b, t: (0, 0)),
                  _const_spec(w_perm.shape), pl.BlockSpec(conv_w.shape, lambda b, t: (0, 0)), tab, tab, tab,
                  pl.BlockSpec((1, wq), lambda b, t: (0, 0)), pl.BlockSpec((1, wk), lambda b, t: (0, 0)),
                  _const_spec((3 * wq, wq)), _const_spec((3 * wk, wk))],
        out_specs=[tile(widths[0]), tile(widths[1]), tile(widths[4]),
                   hm(nq), hm(nk), hm(nk, LANES), hm(nq), hm(nk), hm(nk, LANES)],
        compiler_params=_cparams("arbitrary", "arbitrary"),
        name="in_projection",
    )(xs, xs, xs, mods, norm_g.reshape(1, d), w_perm, conv_w, *rope,
      jnp.tile(q_gain, nq).reshape(1, wq), jnp.tile(k_gain, nk).reshape(1, wk),
      _head_block_diag(wq, 1.0 / HEAD_DIM), _head_block_diag(wk, 1.0 / HEAD_DIM))


def _halo_specs(width, n_tiles):
    per = ROW_TILE // SUBLANES
    prev = pl.BlockSpec((1, SUBLANES, width), lambda b, t: (b, jnp.maximum(t * per - 1, 0), 0))
    nxt = pl.BlockSpec((1, SUBLANES, width), lambda b, t: (b, jnp.minimum((t + 1) * per, n_tiles * per - 1), 0))
    return prev, nxt


def _short_conv_tile(z, prev8, next8, w, t, n_lat_tiles, n_tiles):
    first = jnp.logical_or(t == 0, t == n_lat_tiles)
    last = jnp.logical_or(t == n_lat_tiles - 1, t == n_tiles - 1)
    above = jnp.where(first, 0.0, prev8[SUBLANES - 1:SUBLANES, :])
    below = jnp.where(last, 0.0, next8[0:1, :])
    row = lax.broadcasted_iota(jnp.int32, z.shape, 0)
    zm1 = jnp.where(row == 0, above, pltpu.roll(z, 1, 0))
    zp1 = jnp.where(row == z.shape[0] - 1, below, pltpu.roll(z, z.shape[0] - 1, 0))
    return zm1 * w[0:1, :] + z * w[1:2, :] + zp1 * w[2:3, :]


def _outproj_kernel(a_lat_ref, a_ctx_ref, yf_ref, yb_ref, rvk_ref, rk_ref, lng_ref, lnb_ref, hmean_ref, hsum_ref,
                    c_ref, d_ref, gt_ref, x_ref, mod_ref, w_ref, fg_ref, o_ref, *, n_lat_tiles, final):
    is_ctx = pl.program_id(1) >= n_lat_tiles
    a = jnp.where(is_ctx, a_ctx_ref[0], a_lat_ref[0])
    b = _rwkv_readout_tile(yf_ref[0] + yb_ref[0], rvk_ref[0], rk_ref[...], lng_ref[...], lnb_ref[...],
                           hmean_ref[...], hsum_ref[...])
    mix = jnp.concatenate([a, b, c_ref[0], d_ref[0]], axis=-1) * gt_ref[0]
    y = jnp.dot(mix.astype(BF16), w_ref[...], preferred_element_type=F32)
    x = x_ref[0] + mod_ref[0, 0, 2:3, :] * y
    o_ref[0] = _rms(x, fg_ref[...]) if final else x


def _outproj(a_lat, a_ctx, yf, yb, rvk, r_k, ln_g, ln_b, cmix, dmix, gates, xs, mods, w_out, final_g,
             n_lat_tiles, final):
    bsz, s, d = xs.shape
    wb = a_lat.shape[-1]
    nt = n_lat_tiles if final else s // ROW_TILE
    tile = lambda wd: pl.BlockSpec((1, ROW_TILE, wd), lambda b, t: (b, t, 0))
    vec = pl.BlockSpec((1, wb), lambda b, t: (0, 0))
    head_tab = _const_spec((3 * wb, wb))
    return pl.pallas_call(
        functools.partial(_outproj_kernel, n_lat_tiles=n_lat_tiles, final=final),
        out_shape=jax.ShapeDtypeStruct((bsz, nt * ROW_TILE, d), F32),
        grid=(bsz, nt),
        in_specs=[pl.BlockSpec((1, ROW_TILE, wb), lambda b, t: (b, jnp.minimum(t, n_lat_tiles - 1), 0)),
                  pl.BlockSpec((1, ROW_TILE, wb), lambda b, t: (b, 0, 0)),
                  tile(wb), tile(wb), tile(3 * wb), vec, vec, vec, head_tab, head_tab,
                  tile(wb), tile(wb), tile(4 * wb), tile(d),
                  pl.BlockSpec((1, 1, 3, d), lambda b, t: (b, t // n_lat_tiles, 0, 0)),
                  _const_spec(w_out.shape),
                  pl.BlockSpec((1, d), lambda b, t: (0, 0))],
        out_specs=tile(d),
        compiler_params=_cparams("arbitrary", "arbitrary"),
        name="out_projection",
    )(a_lat, a_ctx, yf, yb, rvk, r_k.reshape(1, wb), ln_g.reshape(1, wb), ln_b.reshape(1, wb),
      _head_block_diag(wb, 1.0 / HEAD_DIM), _head_block_diag(wb, 1.0), cmix, dmix, gates, xs, mods, w_out, final_g.reshape(1, d))


def _rope_tables(seq_len, ctx_len):
    rows = seq_len // GRID_W
    row = jnp.repeat(jnp.arange(rows, dtype=F32), GRID_W)
    col = jnp.tile(jnp.arange(GRID_W, dtype=F32), rows)
    n_freq = HEAD_DIM // 4
    inv_freq = ROPE_THETA ** (-jnp.arange(n_freq, dtype=F32) / n_freq)
    ar, ac = row[:, None] * inv_freq, col[:, None] * inv_freq
    zero = jnp.zeros_like(ar)
    cos = jnp.concatenate([jnp.cos(ar), jnp.cos(ar), jnp.cos(ac), jnp.cos(ac)], axis=-1)
    sin_hi = jnp.concatenate([-jnp.sin(ar), zero, -jnp.sin(ac), zero], axis=-1)
    sin_lo = jnp.concatenate([zero, jnp.sin(ar), zero, jnp.sin(ac)], axis=-1)
    pad = lambda t, v: jnp.concatenate([t, jnp.full((ctx_len, HEAD_DIM), v, F32)], axis=0)
    return pad(cos, 1.0), pad(sin_hi, 0.0), pad(sin_lo, 0.0)


def _rope(x, cos, sin_hi, sin_lo):
    q = HEAD_DIM // 4
    w = x.shape[-1]
    return x * cos + pltpu.roll(x, w - q, 1) * sin_hi + pltpu.roll(x, q, 1) * sin_lo


def _head_mean_sq(x, bd3):
    return _mm_exact_rhs(x * x, bd3)


def _qk_emit(wa, fa, cos, shi, slo, q_gain, k_gain, bdq, bdk, qw_ref, kw_ref, vw_ref, qf_ref, kf_ref, vf_ref,
             wq, wk):
    nq, nk = wq // HEAD_DIM, wk // HEAD_DIM
    tab = lambda t, n: jnp.concatenate([t] * n, axis=-1)
    cq, hq, lq = tab(cos, nq), tab(shi, nq), tab(slo, nq)
    ck, hk, lk = tab(cos, nk), tab(shi, nk), tab(slo, nk)
    scale = HEAD_DIM ** -0.5

    def emit(ref, val, n):
        for h in range(n):
            ref[0, h] = val[:, h * HEAD_DIM:(h + 1) * HEAD_DIM].astype(ref.dtype)

    def emit_values(ref, v):
        lane = lax.broadcasted_iota(jnp.int32, (v.shape[0], LANES - HEAD_DIM), 1)
        ones_pad = jnp.where(lane == 0, 1.0, 0.0)
        for h in range(nk):
            ref[0, h] = jnp.concatenate([v[:, h * HEAD_DIM:(h + 1) * HEAD_DIM], ones_pad],
                                        axis=-1).astype(ref.dtype)

    emit(qw_ref, _rope(wa[:, :wq], cq, hq, lq) * scale, nq)
    emit(kw_ref, _rope(wa[:, wq:wq + wk], ck, hk, lk), nk)
    emit_values(vw_ref, wa[:, wq + wk:])
    q, k = fa[:, :wq], fa[:, wq:wq + wk]
    q = q * lax.rsqrt(_head_mean_sq(q, bdq) + NORM_EPS) * q_gain
    k = k * lax.rsqrt(_head_mean_sq(k, bdk) + NORM_EPS) * k_gain
    emit(qf_ref, _rope(q, cq, hq, lq) * scale, nq)
    emit(kf_ref, _rope(k, ck, hk, lk), nk)
    emit_values(vf_ref, fa[:, wq + wk:])


def _head_block_diag(width, value):
    h = np.arange(width) // HEAD_DIM
    bd = ((h[:, None] == h[None, :]) * value).astype(np.float32).astype(BF16)
    assert np.all(bd.astype(np.float32) == (h[:, None] == h[None, :]) * value)
    return np.concatenate([bd, bd, bd], axis=0)


def _window_attn_kernel(q_ref, k_ref, v_ref, sink_ref, o_ref, *, seq_len, ctx_len, tq):
    t = pl.program_id(2)
    n_lat = seq_len // tq
    g = q_ref.shape[1]
    sub = WINDOW
    band = 2 * WINDOW + sub
    kc, vc = k_ref[0, 0, seq_len:seq_len + ctx_len, :], v_ref[0, 0, seq_len:seq_len + ctx_len, :]
    nt_dot = lambda a, b: lax.dot_general(a, b, (((1,), (1,)), ((), ())), preferred_element_type=F32)
    for j in range(tq // sub):
        q = q_ref[0, :, j * sub:(j + 1) * sub, :].reshape(g * sub, HEAD_DIM)
        first = t * tq + j * sub
        start = pl.multiple_of(jnp.clip(first - WINDOW, 0, seq_len - band), WINDOW)
        kb, vb = k_ref[0, 0, pl.ds(start, band), :], v_ref[0, 0, pl.ds(start, band), :]
        s_ctx = nt_dot(q, kc)
        s_loc = nt_dot(q, kb)
        qpos = first + lax.broadcasted_iota(jnp.int32, (g, sub, band), 1).reshape(g * sub, band)
        kpos = start + lax.broadcasted_iota(jnp.int32, (g * sub, band), 1)
        valid = jnp.logical_and(jnp.abs(qpos - kpos) <= WINDOW, t < n_lat)
        s_loc = jnp.where(valid, s_loc, NEG_INF)
        sink = sink_ref[0, :, j * sub:(j + 1) * sub, :].reshape(g * sub, 1)
        m = jnp.maximum(jnp.maximum(jnp.max(s_ctx, axis=-1, keepdims=True),
                                    jnp.max(s_loc, axis=-1, keepdims=True)), sink)
        p_ctx, p_loc = jnp.exp(s_ctx - m), jnp.exp(s_loc - m)
        acc = (jnp.dot(p_ctx.astype(BF16), vc, preferred_element_type=F32)
               + jnp.dot(p_loc.astype(BF16), vb, preferred_element_type=F32))
        denom = acc[:, HEAD_DIM:HEAD_DIM + 1] + jnp.exp(sink - m)
        out = acc[:, :HEAD_DIM] / denom
        o_ref[0, j * sub:(j + 1) * sub, :] = jnp.concatenate(
            [out[h * sub:(h + 1) * sub] for h in range(g)], axis=-1)


def _window_attention(q, k, v, sink, seq_len, ctx_len):
    bsz, nq, s, _ = q.shape
    nkv = k.shape[1]
    g = nq // nkv
    tq = ROW_TILE
    sink_rows = jnp.broadcast_to(sink.astype(F32).reshape(nkv, g, 1, 1), (nkv, g, tq, 1))
    kv = lambda a: pl.BlockSpec((1, 1, s, a.shape[-1]), lambda b, h, t: (b, h, 0, 0))
    return pl.pallas_call(
        functools.partial(_window_attn_kernel, seq_len=seq_len, ctx_len=ctx_len, tq=tq),
        out_shape=jax.ShapeDtypeStruct((bsz, s, nq * HEAD_DIM), F32),
        grid=(bsz, nkv, s // tq),
        in_specs=[pl.BlockSpec((1, g, tq, HEAD_DIM), lambda b, h, t: (b, h, t, 0)), kv(k), kv(v),
                  pl.BlockSpec((1, g, tq, 1), lambda b, h, t: (h, 0, 0, 0))],
        out_specs=pl.BlockSpec((1, tq, g * HEAD_DIM), lambda b, h, t: (b, t, h)),
        compiler_params=_cparams("arbitrary", "arbitrary", "arbitrary"),
        name="window_attention",
    )(q, k, v, sink_rows)


DENSE_KEY_BLOCK = 1024


def _dense_attn_kernel(q_ref, k_ref, v_ref, o_ref, *, seq_len, tq, tk):
    t = pl.program_id(2)
    n_lat = seq_len // tq
    g = q_ref.shape[1]
    ctx_len = k_ref.shape[2] - seq_len
    q = q_ref[0].reshape(g * tq, HEAD_DIM)

    def step(carry, start, size):
        m, acc = carry
        s = lax.dot_general(q, k_ref[0, 0, start:start + size, :], (((1,), (1,)), ((), ())),
                            preferred_element_type=F32)
        m_new = jnp.maximum(m, jnp.max(s, axis=-1, keepdims=True))
        p = jnp.exp(s - m_new).astype(BF16)
        pv = jnp.dot(p, v_ref[0, 0, start:start + size, :], preferred_element_type=F32)
        return m_new, jnp.exp(m - m_new) * acc + pv

    def finish(carry):
        _, acc = carry
        out = acc[:, :HEAD_DIM] / acc[:, HEAD_DIM:HEAD_DIM + 1]
        o_ref[0] = jnp.concatenate([out[h * tq:(h + 1) * tq] for h in range(g)], axis=-1)

    init = (jnp.full((g * tq, 1), NEG_INF, F32), jnp.zeros((g * tq, v_ref.shape[-1]), F32))

    @pl.when(t < n_lat)
    def _():
        carry = step(init, seq_len, ctx_len)
        for j in range(seq_len // tk):
            carry = step(carry, j * tk, tk)
        finish(carry)

    @pl.when(t >= n_lat)
    def _():
        finish(step(init, seq_len, ctx_len))


def _dense_attention(q, k, v, seq_len):
    bsz, nq, s, _ = q.shape
    nkv = k.shape[1]
    g = nq // nkv
    tq = ROW_TILE
    tk = math.gcd(seq_len, DENSE_KEY_BLOCK)
    kv = lambda a: pl.BlockSpec((1, 1, s, a.shape[-1]), lambda b, h, t: (b, h, 0, 0))
    return pl.pallas_call(
        functools.partial(_dense_attn_kernel, seq_len=seq_len, tq=tq, tk=tk),
        out_shape=jax.ShapeDtypeStruct((bsz, s, nq * HEAD_DIM), F32),
        grid=(bsz, nkv, s // tq),
        in_specs=[pl.BlockSpec((1, g, tq, HEAD_DIM), lambda b, h, t: (b, h, t, 0)), kv(k), kv(v)],
        out_specs=pl.BlockSpec((1, tq, g * HEAD_DIM), lambda b, h, t: (b, t, h)),
        compiler_params=_cparams("arbitrary", "arbitrary", "arbitrary"),
        name="dense_attention",
    )(q, k, v)


def _softplus(x):
    return jnp.maximum(x, 0.0) + jnp.log(1.0 + jnp.exp(-jnp.abs(x)))


def _split(a):
    bits = lax.bitcast_convert_type(a, jnp.uint32) & jnp.uint32(0xFFFF0000)
    hi = lax.bitcast_convert_type(bits, F32)
    return hi, a - hi


def _rhs3(b):
    hi, lo = _split(b)
    return jnp.concatenate([hi, hi, lo], axis=-2).astype(BF16)


def _split3(a):
    hi, rest = _split(a)
    mid, lo = _split(rest)
    return hi, mid, lo


def _mm_exact_lhs(tbl3, x):
    return jnp.dot(tbl3, jnp.concatenate(_split3(x), axis=0).astype(BF16), preferred_element_type=F32)


def _mm_exact_rhs(x, tbl3):
    return jnp.dot(jnp.concatenate(_split3(x), axis=-1).astype(BF16), tbl3, preferred_element_type=F32)


def _mm_split(a, b):
    ah, al = _split(a)
    return jnp.dot(jnp.concatenate([ah, al, ah], axis=-1).astype(BF16), _rhs3(b), preferred_element_type=F32)


RW_DEPTH = 1


def _lhs4(a):
    return a.astype(BF16)


def _rhs4(b):
    return b.astype(BF16)


def _rhs4_nt(b):
    return b.astype(BF16)


def _bmm(lf, rf):
    return lax.dot_general(lf, rf, (((2,), (1,)), ((0,), (0,))), preferred_element_type=F32)


RW_SLOTS_PER_STEP = 16


def _rwkv_par_kernel(z_ref, w0_ref, wup_ref, a0_ref, aup_ref, kk_ref, ka_ref, bd_ref, cum_ref,
                     rp3_ref, yvq_ref, rvk_ref, *, w, lora):
    tc = RW_CHUNK
    nc, nh = ROW_TILE // tc, w // HEAD_DIM
    assert tc == HEAD_DIM
    z = z_ref[0]
    r, k, v = z[:, :w], z[:, w:2 * w], z[:, 2 * w:3 * w]
    w_low = jnp.tanh(z[:, 3 * w:3 * w + lora])
    a_low = z[:, 3 * w + lora:]
    kk = k * kk_ref[...]
    kk = kk / jnp.maximum(jnp.sqrt(_mm_exact_rhs(kk * kk, bd_ref[...])), 1e-12)
    ksum = jnp.zeros_like(k)

    tpos = lax.broadcasted_iota(jnp.int32, (tc, w), 0)
    spos = lax.broadcasted_iota(jnp.int32, (tc, w), 1) % tc
    eye = tpos == spos
    on_diag = (lax.broadcasted_iota(jnp.int32, (w, w), 0) // tc
               == lax.broadcasted_iota(jnp.int32, (w, w), 1) // tc)
    packed = lambda a: a.reshape(nc, tc, w)

    def level_mask(s):
        return jnp.logical_and(tpos // (2 * s) == spos // (2 * s), tpos // s != spos // s)

    def blockdiag(m):
        tiled = jnp.concatenate([m.astype(BF16)] * nh, axis=1)
        return jnp.where(on_diag, tiled, jnp.zeros_like(tiled))

    def own_blocks(full):
        kept = jnp.where(on_diag, full, 0.0)
        return sum(kept[:, h * tc:(h + 1) * tc, :] for h in range(nh))

    per_dir = []
    for d in range(2):
        w_log = -_softplus(-(w0_ref[d:d + 1, :] + _mm_split(w_low, wup_ref[d]))) - 0.5
        lw = -jnp.exp(w_log)
        a = 1.0 / (1.0 + jnp.exp(-(a0_ref[d:d + 1, :] + _mm_split(a_low, aup_ref[d]))))
        kd = k * (1.0 + (a - 1.0) * ka_ref[...])
        ksum = ksum + kd
        sums = _mm_exact_lhs(cum_ref[d], lw)
        c, ctot = sums[:ROW_TILE], sums[ROW_TILE:]
        e_neg, e_rem = jnp.exp(-c), jnp.exp(ctot - c)
        b = kk * a
        before = tpos > spos if d == 0 else tpos < spos
        per_dir.append((packed(-kk * jnp.exp(c - lw)), packed(r * jnp.exp(c)), packed(b * e_neg),
                        packed(kd * e_neg), packed(b * e_rem), packed(kd * e_rem), packed(jnp.exp(ctot)),
                        jnp.broadcast_to(before, (nc, tc, w))))
    rvk_ref[0] = jnp.concatenate([r, v, ksum], axis=-1)

    at_p, rt_p, bh_p, kh_p, bc_p, kc_p, wt_p, before = (jnp.concatenate(parts, axis=0) for parts in zip(*per_dir))
    v_p = jnp.concatenate([packed(v)] * 2, axis=0)
    upto = jnp.logical_or(before, eye)
    big = lax.dot_general(jnp.concatenate([at_p, rt_p], axis=1).astype(BF16),
                          jnp.concatenate([blockdiag(bh_p), blockdiag(kh_p)], axis=1),
                          (((2,), (2,)), ((0,), (0,))), preferred_element_type=F32)
    a_ab = jnp.where(before, big[:, :tc, :w], 0.0)
    a_ak = jnp.where(before, big[:, :tc, w:], 0.0)
    a_rb = jnp.where(upto, big[:, tc:, :w], 0.0)
    a_rk = jnp.where(upto, big[:, tc:, w:], 0.0)
    x = jnp.where(eye, 1.0, jnp.where(level_mask(1), a_ab, 0.0))
    s = 2
    while s < tc:
        half = _bmm(_lhs4(x), blockdiag(jnp.where(level_mask(s), a_ab, 0.0)))
        x = x + _bmm(_lhs4(half), blockdiag(x))
        s *= 2
    akrk = _bmm(_lhs4(jnp.concatenate([a_ak, a_rk], axis=1)), blockdiag(v_p))
    xa = _bmm(_lhs4(x), jnp.concatenate([blockdiag(at_p), blockdiag(akrk[:, :tc])], axis=2))
    ra = _bmm(_lhs4(a_rb), jnp.concatenate([blockdiag(xa[:, :, :w]), blockdiag(xa[:, :, w:])], axis=2))
    rp = rt_p + ra[:, :, :w]
    yv = ra[:, :, w:] + akrk[:, tc:]
    lhs_t = jnp.concatenate([bc_p, kc_p], axis=1).astype(BF16)
    rhs_t = jnp.concatenate([xa, jnp.concatenate([jnp.zeros_like(v_p), v_p], axis=2)], axis=1).astype(BF16)
    full = lax.dot_general(lhs_t, rhs_t, (((1,), (1,)), ((0,), (0,))), preferred_element_type=F32)
    p = own_blocks(full[:, :, :w]) + jnp.where(eye, wt_p[:, 0:1, :], 0.0)
    q = own_blocks(full[:, :, w:])
    rp3_ref[0] = jnp.concatenate([rp, p], axis=1).astype(BF16).reshape(2, nc, tc + HEAD_DIM, w)
    yvq_ref[0] = jnp.concatenate([yv, q], axis=1).reshape(2, nc, tc + HEAD_DIM, w)


def _chunk_matrices():
    t = np.arange(ROW_TILE)
    same = (t[:, None] // RW_CHUNK) == (t[None, :] // RW_CHUNK)
    tabs = []
    for run in (same & (t[None, :] <= t[:, None]), same & (t[None, :] >= t[:, None])):
        m = np.concatenate([run, same], axis=0).astype(np.float32).astype(BF16)
        tabs.append(np.concatenate([m, m, m], axis=1))
    return np.stack(tabs)


def _rwkv_par(rw, w0, w_up, a0, a_up, k_k, k_a, w):
    bsz, s, width = rw.shape
    nt = s // ROW_TILE
    nh, nc = w // HEAD_DIM, ROW_TILE // RW_CHUNK
    lora = w_up.shape[1]
    full = lambda a: pl.BlockSpec(a.shape, lambda b, t: (0,) * a.ndim)
    cum = _chunk_matrices()
    bd = _head_block_diag(w, 1.0)
    vec = lambda a: a.reshape(1, w)
    n_chunks = s // RW_CHUNK
    tc, hd = RW_CHUNK, HEAD_DIM
    consts = (w0, w_up, a0, a_up, vec(k_k), vec(k_a), bd, cum)
    return pl.pallas_call(
        functools.partial(_rwkv_par_kernel, w=w, lora=lora),
        out_shape=[jax.ShapeDtypeStruct((bsz, 2, n_chunks, tc + hd, RW_DEPTH * w), BF16),
                   jax.ShapeDtypeStruct((bsz, 2, n_chunks, tc + hd, w), F32),
                   jax.ShapeDtypeStruct((bsz, s, 3 * w), F32)],
        grid=(bsz, nt),
        in_specs=[pl.BlockSpec((1, ROW_TILE, width), lambda b, t: (b, t, 0))] + [full(a) for a in consts],
        out_specs=[pl.BlockSpec((1, 2, nc, tc + hd, RW_DEPTH * w), lambda b, t: (b, 0, t, 0, 0)),
                   pl.BlockSpec((1, 2, nc, tc + hd, w), lambda b, t: (b, 0, t, 0, 0)),
                   pl.BlockSpec((1, ROW_TILE, 3 * w), lambda b, t: (b, t, 0))],
        compiler_params=_cparams("arbitrary", "arbitrary"),
        name="rwkv_chunk_prep",
    )(rw, *consts)


def _rwkv_seq_kernel(rp3f_ref, yvqf_ref, rp3b_ref, yvqb_ref, yf_ref, yb_ref, g_ref):
    @pl.when(pl.program_id(1) == 0)
    def _():
        g_ref[...] = jnp.zeros_like(g_ref)

    w = g_ref.shape[-1]
    tc = RW_CHUNK
    nh = w // HEAD_DIM
    grp = rp3f_ref.shape[2]
    row_head = lax.broadcasted_iota(jnp.int32, (w, w), 0) // HEAD_DIM
    col_head = lax.broadcasted_iota(jnp.int32, (w, w), 1) // HEAD_DIM
    on_diag = row_head == col_head
    for step in range(grp):
        for d, (rp3, yvq, y) in enumerate(((rp3f_ref, yvqf_ref, yf_ref), (rp3b_ref, yvqb_ref, yb_ref))):
            ci = step if d == 0 else grp - 1 - step
            out = jnp.dot(rp3[0, 0, ci], _rhs4(g_ref[d]), preferred_element_type=F32) + yvq[0, 0, ci]
            y[0, ci * tc:(ci + 1) * tc, :] = out[:tc]
            g_ref[d] = jnp.where(on_diag, jnp.concatenate([out[tc:]] * nh, axis=0), 0.0)


RW_SCAN_GROUP = 4


def _rwkv_seq(rp3, yvq, n_lat_chunks):
    bsz, _, n_chunks = rp3.shape[:3]
    w = yvq.shape[-1]
    grp = RW_SCAN_GROUP
    assert n_lat_chunks % grp == 0 and n_chunks % grp == 0
    n_groups, n_lat, n_ctx = n_chunks // grp, n_lat_chunks // grp, (n_chunks - n_lat_chunks) // grp
    order = (lambda i: jnp.where(i < n_ctx, n_lat + i, i - n_ctx),
             lambda i: n_groups - 1 - i)
    blk = lambda d, a: pl.BlockSpec((1, 1, grp) + a.shape[3:], lambda b, i: (b, d, order[d](i), 0, 0))
    out = lambda d: pl.BlockSpec((1, grp * RW_CHUNK, w), lambda b, i: (b, order[d](i), 0))
    shp = jax.ShapeDtypeStruct((bsz, n_chunks * RW_CHUNK, w), F32)
    return pl.pallas_call(
        _rwkv_seq_kernel,
        out_shape=[shp, shp],
        grid=(bsz, n_groups),
        in_specs=[blk(0, rp3), blk(0, yvq), blk(1, rp3), blk(1, yvq)],
        out_specs=[out(0), out(1)],
        scratch_shapes=[pltpu.VMEM((2, w, w), F32)],
        compiler_params=_cparams("arbitrary", "arbitrary"),
        name="rwkv_state_scan",
    )(rp3, yvq, rp3, yvq)


def _rwkv_readout_tile(y, rvk, r_k, ln_g, ln_b, head_mean, head_sum):
    w = r_k.shape[-1]
    r, v, ksum = rvk[:, :w], rvk[:, w:2 * w], rvk[:, 2 * w:]
    yc = y - _mm_exact_rhs(y, head_mean)
    var = _mm_exact_rhs(yc * yc, head_mean)
    bonus = _mm_exact_rhs(r * ksum * r_k, head_sum) * v
    return yc * lax.rsqrt(var + RW_GN_EPS) * ln_g + ln_b + bonus


@functools.lru_cache(maxsize=None)
def _filter_position_features(seq_len, bands, width):
    n = np.arange(2 * seq_len)
    pos = np.where(n < seq_len, n, 2 * seq_len - n) % seq_len
    t = np.linspace(0.0, 1.0, seq_len)[pos][:, None]
    wpos = (2.0 * math.pi / seq_len) * pos[:, None]
    f = np.linspace(1e-4, bands - 1, bands)[None, :]
    z = np.concatenate([t, np.cos(f * wpos), np.sin(f * wpos)], axis=-1)
    return np.pad(z, ((0, 0), (0, width - z.shape[1]))).astype(np.float32)


def _hyena_filter_kernel(z_ref, fw1_ref, fb1_ref, freq_ref, fw2_ref, fb2_ref, fw3_ref, delta_ref, o_ref, *, seq_len):
    freq = freq_ref[...]
    rows = ROW_TILE
    n_orders, _, c = o_ref.shape

    def taps(i, norms):
        blk = pl.ds(pl.multiple_of(i * rows, rows), rows)
        z = z_ref[blk, :]
        h = jnp.sin(freq * (_mm_split(z, fw1_ref[...]) + fb1_ref[...]))
        h = jnp.sin(freq * (_mm_split(h, fw2_ref[...]) + fb2_ref[...]))
        h = _mm_split(h, fw3_ref[...])
        n = i * rows + lax.broadcasted_iota(jnp.int32, (rows, 1), 0)
        decay = jnp.exp(-z[:, 0:1] * jnp.abs(delta_ref[...]))
        out = []
        for o in range(n_orders):
            ho = jnp.where(n < seq_len, h[:, 2 * o * c:(2 * o + 1) * c], h[:, (2 * o + 1) * c:(2 * o + 2) * c])
            ho = ho * decay
            o_ref[o, blk, :] = jnp.where(n != seq_len, ho, 0.0)
            out.append(norms[o] + jnp.sum(jnp.abs(ho), axis=0, keepdims=True))
        return tuple(out)
    zero = jnp.zeros((1, c), F32)
    norms = lax.fori_loop(0, z_ref.shape[0] // rows, taps, (zero,) * n_orders)

    def normalise(i, carry):
        blk = pl.ds(pl.multiple_of(i * rows, rows), rows)
        for o in range(n_orders):
            o_ref[o, blk, :] = o_ref[o, blk, :] / norms[o]
        return carry
    lax.fori_loop(0, z_ref.shape[0] // rows, normalise, 0)


def _hyena_two_sided_filters(seq_len, fw1, fb1, freq, fw2, fb2, fw3, width):
    ffn = fw1.shape[1]
    bands = (fw1.shape[0] - 1) // 2
    z = _filter_position_features(seq_len, bands, ffn)
    fw1p = jnp.pad(fw1, ((0, ffn - fw1.shape[0]), (0, 0)))
    max_decay = math.log(HY_DECAY_TARGET) / HY_FAST_DECAY
    min_decay = math.log(HY_DECAY_TARGET) / HY_SLOW_DECAY
    deltas = jnp.linspace(min_decay, max_decay, width, dtype=F32).reshape(1, width)
    row = lambda a: a.reshape(1, ffn)
    full = lambda shape: pl.BlockSpec(shape, lambda i: (0,) * len(shape))
    return pl.pallas_call(
        functools.partial(_hyena_filter_kernel, seq_len=seq_len),
        out_shape=jax.ShapeDtypeStruct((2, 2 * seq_len, width), F32),
        grid=(1,),
        in_specs=[full(z.shape), full((ffn, ffn)), full((1, ffn)), full((1, ffn)), full((ffn, ffn)),
                  full((1, ffn)), full(fw3.shape), full((1, width))],
        out_specs=full((2, 2 * seq_len, width)),
        compiler_params=_cparams("arbitrary"),
        name="hyena_filter",
    )(z, fw1p, row(fb1), row(freq), fw2, row(fb2), fw3, deltas)


def kernel(x, c, ctx, c_ctx, mod_w, mod_b, norm_g, w_in, w_out, hy_conv, hy_fw1, hy_fb1, hy_freq, hy_fw2,
           hy_fb2, hy_fw3, hy_bias, rw_conv, rw_w0, rw_w_up, rw_a0, rw_a_up, rw_k_k, rw_k_a, rw_r_k,
           rw_ln_g, rw_ln_b, wa_sink, fa_q_norm, fa_k_norm, final_g):
    bsz, seq_len, d = x.shape
    ctx_len = ctx.shape[1]
    depth = w_in.shape[0]
    w_hy = hy_bias.shape[-1]
    w_rw = rw_w0.shape[-1]
    n_wa_heads = wa_sink.shape[-1]
    w_q = n_wa_heads * HEAD_DIM
    w_kv = w_q // 2
    lora = rw_w_up.shape[2] + rw_a_up.shape[2]
    branch_w = (3 * w_hy, 3 * w_rw + lora, w_q + 2 * w_kv, w_q + 2 * w_kv)
    gate_w = (w_hy, w_rw, w_q, w_q)
    assert seq_len % ROW_TILE == 0 and ctx_len % ROW_TILE == 0 and bsz % 2 == 0
    n_lat_tiles = seq_len // ROW_TILE

    starts = np.cumsum([0] + [bw + gw for bw, gw in zip(branch_w, gate_w)])
    cols = np.concatenate([np.arange(s0, s0 + bw) for s0, bw in zip(starts, branch_w)]
                          + [np.arange(s0 + bw, s0 + bw + gw) for s0, bw, gw in zip(starts, branch_w, gate_w)])
    w_in_p = w_in[:, :, cols].astype(BF16)
    w_out_b = w_out.astype(BF16)
    widths = branch_w + (sum(gate_w),)

    pad_rows = (-(bsz + 1)) % 8
    cond = jnp.concatenate([c, c_ctx[None], jnp.zeros((pad_rows, d), F32)], axis=0)
    mod = _modulation(cond, mod_w, mod_b)
    mod_lat = mod[:, :bsz].reshape(depth, bsz, 3, d)
    mod_ctx = jnp.broadcast_to(mod[:, bsz].reshape(depth, 1, 3, d), (depth, bsz, 3, d))
    mods = jnp.stack([mod_lat, mod_ctx], axis=2)

    rope = _rope_tables(seq_len, ctx_len)
    xs = jnp.concatenate([x, ctx], axis=1)
    for l in range(depth):
        last = l == depth - 1
        conv_w = jnp.concatenate([hy_conv[l], rw_conv[l]], axis=-1)
        hyc, rwc, gates, qw, kw, vw, qf, kf, vf = _inproj(
            xs, mods[l], norm_g[l], w_in_p[l], conv_w, widths, n_lat_tiles, rope,
            fa_q_norm[l], fa_k_norm[l], w_q, w_kv)

        filt = functools.partial(_hyena_two_sided_filters, fw1=hy_fw1[l], fb1=hy_fb1[l], freq=hy_freq[l],
                                 fw2=hy_fw2[l], fb2=hy_fb2[l], fw3=hy_fw3[l], width=w_hy)
        bias = hy_bias[l].reshape(2, 1, w_hy)
        spec = _filter_spectrum(filt(seq_len))
        nt_hy = w_hy // LANES
        y1 = _fftconv_gated(hyc, 0, hyc, nt_hy, spec, 0, bias, seq_len)
        a_lat = _fftconv_gated(y1, 0, hyc, 2 * nt_hy, spec, 1, bias, seq_len)
        a_ctx = a_lat if last else _ctx_hyena(hyc, seq_len // ctx_len, filt(ctx_len), bias, ctx_len, w_hy)

        rp3, yvq, rvk = _rwkv_par(rwc, rw_w0[l], rw_w_up[l], rw_a0[l], rw_a_up[l], rw_k_k[l], rw_k_a[l], w_rw)
        yf, yb = _rwkv_seq(rp3, yvq, seq_len // RW_CHUNK)

        c_mix = _window_attention(qw, kw, vw, wa_sink[l], seq_len, ctx_len)
        d_mix = _dense_attention(qf, kf, vf, seq_len)

        xs = _outproj(a_lat, a_ctx, yf, yb, rvk, rw_r_k[l], rw_ln_g[l], rw_ln_b[l], c_mix, d_mix, gates, xs,
                      mods[l], w_out_b[l], final_g, n_lat_tiles, last)
    return xs
```

```python
import functools
import math

import jax
import jax.numpy as jnp
import numpy as np
from jax import lax
from jax.experimental import pallas as pl
from jax.experimental.pallas import tpu as pltpu

HEAD_DIM = 64
GRID_W = 64
WINDOW = 128
NORM_EPS = 1e-6
RW_GN_EPS = 64e-5
NEG_INF = -1e30
ROPE_THETA = 10000.0
HY_FAST_DECAY = 0.3
HY_SLOW_DECAY = 1.5
HY_DECAY_TARGET = 1e-2

ROW_TILE = 256
LANES = 128
SUBLANES = 8
FFT_N1 = 64
FFT_UNROLL = 8
FFT_PLANE_PAD = 8
RW_CHUNK = 64
VMEM_LIMIT = 60 * 1024 * 1024

F32 = jnp.float32
BF16 = jnp.bfloat16
HI = lax.Precision.HIGHEST


def _dot(a, b):
    return jnp.dot(a, b, preferred_element_type=F32, precision=HI)


def _cparams(*sem):
    return pltpu.CompilerParams(dimension_semantics=sem, vmem_limit_bytes=VMEM_LIMIT)


def _const_spec(shape):
    return pl.BlockSpec(shape, lambda *_: (0,) * len(shape), pipeline_mode=pl.Buffered(1))


def _silu(x):
    return x * (1.0 / (1.0 + jnp.exp(-x)))


def _mod_kernel(c_ref, w_ref, b_ref, o_ref):
    o_ref[0] = _dot(_silu(c_ref[...]), w_ref[0]) + b_ref[0]


def _modulation(cond, mod_w, mod_b):
    depth, d, d3 = mod_w.shape
    rows = cond.shape[0]
    return pl.pallas_call(
        _mod_kernel,
        out_shape=jax.ShapeDtypeStruct((depth, rows, d3), F32),
        grid=(depth,),
        in_specs=[pl.BlockSpec((rows, d), lambda l: (0, 0)),
                  pl.BlockSpec((1, d, d3), lambda l: (l, 0, 0)),
                  pl.BlockSpec((1, 1, d3), lambda l: (l, 0, 0))],
        out_specs=pl.BlockSpec((1, rows, d3), lambda l: (l, 0, 0)),
        compiler_params=_cparams("arbitrary"),
        name="modulation",
    )(cond, mod_w, mod_b.reshape(depth, 1, d3))


def _blockreal(m):
    return np.block([[m.real, -m.imag], [m.imag, m.real]])


@functools.lru_cache(maxsize=None)
def _fft_tables(seq_len):
    n = 2 * seq_len
    n1, n2 = FFT_N1, n // FFT_N1
    h1 = n1 // 2
    j2 = np.arange(n2)[:, None, None]
    k1 = np.arange(n1)[None, :, None]
    t1 = np.exp(-2j * np.pi * (j2 * k1 / n + k1 * np.arange(n1)[None, None, :] / n1))
    t1_data = np.stack([_blockreal(t1[j][:, :h1]) for j in range(n2)])
    t1_real = np.concatenate([t1.real, t1.imag], axis=1)
    f2 = np.exp(-2j * np.pi * np.outer(np.arange(n2), np.arange(n2)) / n2)
    f2_fwd = _blockreal(f2)
    f2_inv = _blockreal(np.conj(f2))
    t4 = np.exp(2j * np.pi * (np.arange(h1)[None, :, None] * np.arange(n1)[None, None, :] / n1
                              + j2 * np.arange(n1)[None, None, :] / n)) / n
    t4 = np.stack([_blockreal(t4[j]) for j in range(n2)])
    f2_parts = np.concatenate([f2.real, f2.imag], axis=0)
    return tuple(_lhs3_table(t) for t in (t1_data, t1_real, f2_fwd, f2_parts, t4))


def _lhs3_table(m):
    hi = m.astype(np.float32).astype(BF16)
    lo = (m - hi.astype(np.float64)).astype(np.float32).astype(BF16)
    return np.concatenate([hi, lo, hi], axis=-1)


def _mm3(tbl3, x):
    return jnp.dot(tbl3, _rhs3(x), preferred_element_type=F32)


def _spectrum_kernel(k_ref, t1_ref, f2_ref, o_ref, a_ref, *, n1, n2):
    def stage1(j, carry):
        rows = k_ref[0, pl.ds(j, n1, stride=n2), :]
        a_ref[pl.ds(j, 2 * n1, stride=n2 + FFT_PLANE_PAD), :] = _mm3(t1_ref[j], rows)
        return carry
    lax.fori_loop(0, n2, stage1, 0, unroll=FFT_UNROLL)

    def stage2(i, carry):
        pitch = n2 + FFT_PLANE_PAD
        re = a_ref[pl.ds(pl.multiple_of(i * pitch, 8), n2), :]
        im = a_ref[pl.ds(pl.multiple_of((n1 + i) * pitch, 8), n2), :]
        o_ref[0, i] = _mm3(f2_ref[...], jnp.concatenate([re, im], axis=0))
        return carry
    lax.fori_loop(0, n1, stage2, 0, unroll=2)


def _filter_spectrum(kfilt):
    g, n, w = kfilt.shape
    n1, n2 = FFT_N1, n // FFT_N1
    _, t1_real, f2_fwd, _, _ = _fft_tables(n // 2)
    const = _const_spec
    return pl.pallas_call(
        functools.partial(_spectrum_kernel, n1=n1, n2=n2),
        out_shape=jax.ShapeDtypeStruct((g, n1, 2 * n2, w), F32),
        grid=(g, w // LANES),
        in_specs=[pl.BlockSpec((1, n, LANES), lambda gi, j: (gi, 0, j)),
                  const(t1_real.shape), const(f2_fwd.shape)],
        out_specs=pl.BlockSpec((1, n1, 2 * n2, LANES), lambda gi, j: (gi, 0, 0, j)),
        scratch_shapes=[pltpu.VMEM((2 * n1 * (n2 + FFT_PLANE_PAD), LANES), F32)],
        compiler_params=_cparams("arbitrary", "arbitrary"),
        name="hyena_filter_spectrum",
    )(kfilt, t1_real, f2_fwd)


def _fftconv_kernel(u_ref, m_ref, spec_ref, bias_ref, t1_ref, f2p_ref, t4_ref, o_ref, ar_ref, ai_ref,
                    *, n1, n2):
    h1 = n1 // 2
    c = u_ref.shape[-1]
    pitch = n2 + FFT_PLANE_PAD

    def stage1(j, carry):
        za = u_ref[0, pl.ds(j, h1, stride=n2), :]
        zb = u_ref[1, pl.ds(j, h1, stride=n2), :]
        res = _mm3(t1_ref[j], jnp.concatenate([za, zb], axis=0))
        ar_ref[pl.ds(j, n1, stride=pitch), :] = res[:n1]
        ai_ref[pl.ds(j, n1, stride=pitch), :] = res[n1:]
        return carry
    lax.fori_loop(0, n2, stage1, 0, unroll=FFT_UNROLL)

    def stage2(i, carry):
        rows = pl.ds(pl.multiple_of(i * pitch, SUBLANES), n2)
        p = _mm3(f2p_ref[...], jnp.concatenate([ar_ref[rows, :], ai_ref[rows, :]], axis=-1))
        xr = p[:n2, :c] - p[n2:, c:]
        xi = p[:n2, c:] + p[n2:, :c]
        kr, ki = spec_ref[0, i, :n2, :], spec_ref[0, i, n2:, :]
        y = jnp.concatenate([xr * kr - xi * ki, xr * ki + xi * kr], axis=-1)
        q = _mm3(f2p_ref[...], y)
        ar_ref[rows, :] = q[:n2, :c] + q[n2:, c:]
        ai_ref[rows, :] = q[:n2, c:] - q[n2:, :c]
        return carry
    lax.fori_loop(0, n1, stage2, 0, unroll=8)

    bias = bias_ref[0]

    def stage4(j, carry):
        planes = pl.ds(j, n1, stride=pitch)
        y = _mm3(t4_ref[j], jnp.concatenate([ar_ref[planes, :], ai_ref[planes, :]], axis=0))
        rows = pl.ds(j, h1, stride=n2)
        for p in range(2):
            u = u_ref[p, rows, :]
            o_ref[p, rows, :] = m_ref[p, rows, :] * (y[p * h1:(p + 1) * h1] + bias * u)
        return carry
    lax.fori_loop(0, n2, stage4, 0, unroll=FFT_UNROLL)


def _fftconv_gated(u, u_col, mult, mult_col, spec, conv_idx, bias, seq_len):
    bsz = u.shape[0]
    w = spec.shape[-1]
    n = 2 * seq_len
    n1, n2 = FFT_N1, n // FFT_N1
    t1_data, _, _, f2_parts, t4 = _fft_tables(seq_len)
    const = _const_spec
    return pl.pallas_call(
        functools.partial(_fftconv_kernel, n1=n1, n2=n2),
        out_shape=jax.ShapeDtypeStruct((bsz, seq_len, w), F32),
        grid=(w // LANES, bsz // 2),
        in_specs=[pl.BlockSpec((2, seq_len, LANES), lambda j, p: (p, 0, u_col + j)),
                  pl.BlockSpec((2, seq_len, LANES), lambda j, p: (p, 0, mult_col + j)),
                  pl.BlockSpec((1, n1, 2 * n2, LANES), lambda j, p: (conv_idx, 0, 0, j),
                               pipeline_mode=pl.Buffered(1)),
                  pl.BlockSpec((1, 1, LANES), lambda j, p: (conv_idx, 0, j)),
                  const(t1_data.shape), const(f2_parts.shape), const(t4.shape)],
        out_specs=pl.BlockSpec((2, seq_len, LANES), lambda j, p: (p, 0, j)),
        scratch_shapes=[pltpu.VMEM((n1 * (n2 + FFT_PLANE_PAD), LANES), F32)] * 2,
        compiler_params=_cparams("arbitrary", "arbitrary"),
        name="hyena_fftconv",
    )(u, mult, spec, bias, t1_data, f2_parts, t4)


@functools.lru_cache(maxsize=None)
def _small_fft_tables(seq_len):
    n = 2 * seq_len
    f = np.exp(-2j * np.pi * np.outer(np.arange(n), np.arange(n)) / n)
    fwd = _blockreal(f[:, :seq_len])
    real = np.concatenate([f.real, f.imag], axis=0)
    inv = _blockreal(np.conj(f)[:seq_len, :] / n)
    return _lhs3_table(fwd), _lhs3_table(real), _lhs3_table(inv)


def _ctx_hyena_kernel(v_ref, x1_ref, x2_ref, k_ref, bias_ref, fwd_ref, real_ref, inv_ref, o_ref, *, seq_len):
    n = 2 * seq_len

    def conv(ua, ub, g):
        spec = _mm3(real_ref[...], k_ref[g])
        x = _mm3(fwd_ref[...], jnp.concatenate([ua, ub], axis=0))
        xr, xi, kr, ki = x[:n], x[n:], spec[:n], spec[n:]
        y = _mm3(inv_ref[...], jnp.concatenate([xr * kr - xi * ki, xr * ki + xi * kr], axis=0))
        b = bias_ref[g]
        return y[:seq_len] + b * ua, y[seq_len:] + b * ub

    c1a, c1b = conv(v_ref[0], v_ref[1], 0)
    y1a, y1b = x1_ref[0] * c1a, x1_ref[1] * c1b
    c2a, c2b = conv(y1a, y1b, 1)
    o_ref[0] = x2_ref[0] * c2a
    o_ref[1] = x2_ref[1] * c2b


def _ctx_hyena(hyc, row_block, kfilt, bias, seq_len, w):
    bsz = hyc.shape[0]
    n = 2 * seq_len
    fwd, real, inv = _small_fft_tables(seq_len)
    nt = w // LANES
    col = lambda c0: pl.BlockSpec((2, seq_len, LANES), lambda j, p: (p, row_block, c0 + j))
    return pl.pallas_call(
        functools.partial(_ctx_hyena_kernel, seq_len=seq_len),
        out_shape=jax.ShapeDtypeStruct((bsz, seq_len, w), F32),
        grid=(nt, bsz // 2),
        in_specs=[col(0), col(nt), col(2 * nt),
                  pl.BlockSpec((2, n, LANES), lambda j, p: (0, 0, j)),
                  pl.BlockSpec((2, 1, LANES), lambda j, p: (0, 0, j)),
                  _const_spec(fwd.shape), _const_spec(real.shape), _const_spec(inv.shape)],
        out_specs=pl.BlockSpec((2, seq_len, LANES), lambda j, p: (p, 0, j)),
        compiler_params=_cparams("arbitrary", "arbitrary"),
        name="hyena_ctx",
    )(hyc, hyc, hyc, kfilt, bias, fwd, real, inv)


def _rms(x, g):
    return x * lax.rsqrt(jnp.mean(x * x, axis=-1, keepdims=True) + NORM_EPS) * g


def _inproj_kernel(x_ref, xprev_ref, xnext_ref, mod_ref, g_ref, w_ref, hycw_ref, rwcw_ref,
                   cos_ref, shi_ref, slo_ref, qg_ref, kg_ref, bdq_ref, bdk_ref,
                   hy_ref, rw_ref, gt_ref, qw_ref, kw_ref, vw_ref, qf_ref, kf_ref, vf_ref,
                   *, splits, wq, wk, n_lat_tiles, n_tiles):
    shift, scale = mod_ref[0, 0, 0:1, :], mod_ref[0, 0, 1:2, :]
    norm_mod = lambda x: (_rms(x, g_ref[...]) * (1.0 + scale) + shift).astype(BF16)
    z = jnp.dot(norm_mod(x_ref[0]), w_ref[...], preferred_element_type=F32)
    offs = np.cumsum((0,) + tuple(splits))
    seg = lambda i: z[:, offs[i]:offs[i + 1]]
    halo_h = norm_mod(jnp.concatenate([xprev_ref[0], xnext_ref[0]], axis=0))
    for i, cw_ref, o_ref in ((0, hycw_ref, hy_ref), (2, rwcw_ref, rw_ref)):
        halo = jnp.dot(halo_h, w_ref[:, offs[i]:offs[i + 1]], preferred_element_type=F32)
        o_ref[0] = _short_conv_tile(seg(i), halo[:SUBLANES], halo[SUBLANES:], cw_ref[...], pl.program_id(1),
                                    n_lat_tiles, n_tiles)
    gt_ref[0] = _silu(jnp.concatenate([seg(1), seg(3), seg(5), seg(7)], axis=-1))
    _qk_emit(seg(4), seg(6), cos_ref[...], shi_ref[...], slo_ref[...], qg_ref[...], kg_ref[...],
             bdq_ref[...], bdk_ref[...], qw_ref, kw_ref, vw_ref, qf_ref, kf_ref, vf_ref, wq, wk)


def _inproj(xs, mods, norm_g, w_in, hy_conv, rw_conv, splits, n_lat_tiles, rope, q_gain, k_gain, wq, wk):
    widths = (splits[0], splits[2], None, None, sum(splits[1::2]))
    bsz, s, d = xs.shape
    nt = s // ROW_TILE
    nq, nk = wq // HEAD_DIM, wk // HEAD_DIM
    prev, nxt = _halo_specs(d, nt)
    tile = lambda wd: pl.BlockSpec((1, ROW_TILE, wd), lambda b, t: (b, t, 0))
    tab = pl.BlockSpec((ROW_TILE, HEAD_DIM), lambda b, t: (t, 0))
    hm = lambda n, wd=HEAD_DIM: pl.BlockSpec((1, n, ROW_TILE, wd), lambda b, t: (b, 0, t, 0))
    flat = lambda wd: jax.ShapeDtypeStruct((bsz, s, wd), F32)
    heads = lambda n, wd=HEAD_DIM: jax.ShapeDtypeStruct((bsz, n, s, wd), BF16)
    return pl.pallas_call(
        functools.partial(_inproj_kernel, splits=splits, wq=wq, wk=wk, n_lat_tiles=n_lat_tiles, n_tiles=nt),
        out_shape=[flat(widths[0]), flat(widths[1]), flat(widths[4]),
                   heads(nq), heads(nk), heads(nk, LANES), heads(nq), heads(nk), heads(nk, LANES)],
        grid=(bsz, nt),
        in_specs=[tile(d), prev, nxt,
                  pl.BlockSpec((1, 1, 3, d), lambda b, t: (b, t // n_lat_tiles, 0, 0)),
                  pl.BlockSpec((1, d), lambda b, t: (0, 0)),
                  _const_spec(w_in.shape), pl.BlockSpec(hy_conv.shape, lambda b, t: (0, 0)),
                  pl.BlockSpec(rw_conv.shape, lambda b, t: (0, 0)), tab, tab, tab,
                  pl.BlockSpec((1, wq), lambda b, t: (0, 0)), pl.BlockSpec((1, wk), lambda b, t: (0, 0)),
                  _const_spec((3 * wq, wq)), _const_spec((3 * wk, wk))],
        out_specs=[tile(widths[0]), tile(widths[1]), tile(widths[4]),
                   hm(nq), hm(nk), hm(nk, LANES), hm(nq), hm(nk), hm(nk, LANES)],
        compiler_params=_cparams("arbitrary", "arbitrary"),
        name="in_projection",
    )(xs, xs, xs, mods, norm_g.reshape(1, d), w_in, hy_conv, rw_conv, *rope,
      jnp.tile(q_gain, nq).reshape(1, wq), jnp.tile(k_gain, nk).reshape(1, wk),
      _head_block_diag(wq, 1.0 / HEAD_DIM), _head_block_diag(wk, 1.0 / HEAD_DIM))


def _halo_specs(width, n_tiles):
    per = ROW_TILE // SUBLANES
    prev = pl.BlockSpec((1, SUBLANES, width), lambda b, t: (b, jnp.maximum(t * per - 1, 0), 0))
    nxt = pl.BlockSpec((1, SUBLANES, width), lambda b, t: (b, jnp.minimum((t + 1) * per, n_tiles * per - 1), 0))
    return prev, nxt


def _short_conv_tile(z, prev8, next8, w, t, n_lat_tiles, n_tiles):
    first = jnp.logical_or(t == 0, t == n_lat_tiles)
    last = jnp.logical_or(t == n_lat_tiles - 1, t == n_tiles - 1)
    above = jnp.where(first, 0.0, prev8[SUBLANES - 1:SUBLANES, :])
    below = jnp.where(last, 0.0, next8[0:1, :])
    row = lax.broadcasted_iota(jnp.int32, z.shape, 0)
    zm1 = jnp.where(row == 0, above, pltpu.roll(z, 1, 0))
    zp1 = jnp.where(row == z.shape[0] - 1, below, pltpu.roll(z, z.shape[0] - 1, 0))
    return zm1 * w[0:1, :] + z * w[1:2, :] + zp1 * w[2:3, :]


def _outproj_kernel(a_lat_ref, a_ctx_ref, yf_ref, yb_ref, rvk_ref, rk_ref, lng_ref, lnb_ref, hmean_ref, hsum_ref,
                    c_ref, d_ref, gt_ref, x_ref, mod_ref, w_ref, fg_ref, o_ref, *, n_lat_tiles, final):
    is_ctx = pl.program_id(1) >= n_lat_tiles
    a = jnp.where(is_ctx, a_ctx_ref[0], a_lat_ref[0])
    b = _rwkv_readout_tile(yf_ref[0] + yb_ref[0], rvk_ref[0], rk_ref[...], lng_ref[...], lnb_ref[...],
                           hmean_ref[...], hsum_ref[...])
    mix = jnp.concatenate([a, b, c_ref[0], d_ref[0]], axis=-1) * gt_ref[0]
    y = jnp.dot(mix.astype(BF16), w_ref[...], preferred_element_type=F32)
    x = x_ref[0] + mod_ref[0, 0, 2:3, :] * y
    o_ref[0] = _rms(x, fg_ref[...]) if final else x


def _outproj(a_lat, a_ctx, yf, yb, rvk, r_k, ln_g, ln_b, cmix, dmix, gates, xs, mods, w_out, final_g,
             n_lat_tiles, final):
    bsz, s, d = xs.shape
    wb = a_lat.shape[-1]
    nt = n_lat_tiles if final else s // ROW_TILE
    tile = lambda wd: pl.BlockSpec((1, ROW_TILE, wd), lambda b, t: (b, t, 0))
    vec = pl.BlockSpec((1, wb), lambda b, t: (0, 0))
    head_tab = _const_spec((3 * wb, wb))
    return pl.pallas_call(
        functools.partial(_outproj_kernel, n_lat_tiles=n_lat_tiles, final=final),
        out_shape=jax.ShapeDtypeStruct((bsz, nt * ROW_TILE, d), F32),
        grid=(bsz, nt),
        in_specs=[pl.BlockSpec((1, ROW_TILE, wb), lambda b, t: (b, jnp.minimum(t, n_lat_tiles - 1), 0)),
                  pl.BlockSpec((1, ROW_TILE, wb), lambda b, t: (b, 0, 0)),
                  tile(wb), tile(wb), tile(3 * wb), vec, vec, vec, head_tab, head_tab,
                  tile(wb), tile(wb), tile(4 * wb), tile(d),
                  pl.BlockSpec((1, 1, 3, d), lambda b, t: (b, t // n_lat_tiles, 0, 0)),
                  _const_spec(w_out.shape),
                  pl.BlockSpec((1, d), lambda b, t: (0, 0))],
        out_specs=tile(d),
        compiler_params=_cparams("arbitrary", "arbitrary"),
        name="out_projection",
    )(a_lat, a_ctx, yf, yb, rvk, r_k.reshape(1, wb), ln_g.reshape(1, wb), ln_b.reshape(1, wb),
      _head_block_diag(wb, 1.0 / HEAD_DIM), _head_block_diag(wb, 1.0), cmix, dmix, gates, xs, mods, w_out, final_g.reshape(1, d))


def _rope_tables(seq_len, ctx_len):
    rows = seq_len // GRID_W
    row = jnp.repeat(jnp.arange(rows, dtype=F32), GRID_W)
    col = jnp.tile(jnp.arange(GRID_W, dtype=F32), rows)
    n_freq = HEAD_DIM // 4
    inv_freq = ROPE_THETA ** (-jnp.arange(n_freq, dtype=F32) / n_freq)
    ar, ac = row[:, None] * inv_freq, col[:, None] * inv_freq
    zero = jnp.zeros_like(ar)
    cos = jnp.concatenate([jnp.cos(ar), jnp.cos(ar), jnp.cos(ac), jnp.cos(ac)], axis=-1)
    sin_hi = jnp.concatenate([-jnp.sin(ar), zero, -jnp.sin(ac), zero], axis=-1)
    sin_lo = jnp.concatenate([zero, jnp.sin(ar), zero, jnp.sin(ac)], axis=-1)
    pad = lambda t, v: jnp.concatenate([t, jnp.full((ctx_len, HEAD_DIM), v, F32)], axis=0)
    return pad(cos, 1.0), pad(sin_hi, 0.0), pad(sin_lo, 0.0)


def _rope(x, cos, sin_hi, sin_lo):
    q = HEAD_DIM // 4
    w = x.shape[-1]
    return x * cos + pltpu.roll(x, w - q, 1) * sin_hi + pltpu.roll(x, q, 1) * sin_lo


def _head_mean_sq(x, bd3):
    return _mm_exact_rhs(x * x, bd3)


def _qk_emit(wa, fa, cos, shi, slo, q_gain, k_gain, bdq, bdk, qw_ref, kw_ref, vw_ref, qf_ref, kf_ref, vf_ref,
             wq, wk):
    nq, nk = wq // HEAD_DIM, wk // HEAD_DIM
    tab = lambda t, n: jnp.concatenate([t] * n, axis=-1)
    cq, hq, lq = tab(cos, nq), tab(shi, nq), tab(slo, nq)
    ck, hk, lk = tab(cos, nk), tab(shi, nk), tab(slo, nk)
    scale = HEAD_DIM ** -0.5

    def emit(ref, val, n):
        for h in range(n):
            ref[0, h] = val[:, h * HEAD_DIM:(h + 1) * HEAD_DIM].astype(ref.dtype)

    def emit_values(ref, v):
        lane = lax.broadcasted_iota(jnp.int32, (v.shape[0], LANES - HEAD_DIM), 1)
        ones_pad = jnp.where(lane == 0, 1.0, 0.0)
        for h in range(nk):
            ref[0, h] = jnp.concatenate([v[:, h * HEAD_DIM:(h + 1) * HEAD_DIM], ones_pad],
                                        axis=-1).astype(ref.dtype)

    emit(qw_ref, _rope(wa[:, :wq], cq, hq, lq) * scale, nq)
    emit(kw_ref, _rope(wa[:, wq:wq + wk], ck, hk, lk), nk)
    emit_values(vw_ref, wa[:, wq + wk:])
    q, k = fa[:, :wq], fa[:, wq:wq + wk]
    q = q * lax.rsqrt(_head_mean_sq(q, bdq) + NORM_EPS) * q_gain
    k = k * lax.rsqrt(_head_mean_sq(k, bdk) + NORM_EPS) * k_gain
    emit(qf_ref, _rope(q, cq, hq, lq) * scale, nq)
    emit(kf_ref, _rope(k, ck, hk, lk), nk)
    emit_values(vf_ref, fa[:, wq + wk:])


def _head_block_diag(width, value):
    h = np.arange(width) // HEAD_DIM
    bd = ((h[:, None] == h[None, :]) * value).astype(np.float32).astype(BF16)
    assert np.all(bd.astype(np.float32) == (h[:, None] == h[None, :]) * value)
    return np.concatenate([bd, bd, bd], axis=0)


def _window_attn_kernel(q_ref, k_ref, v_ref, sink_ref, o_ref, *, seq_len, ctx_len, tq):
    t = pl.program_id(2)
    n_lat = seq_len // tq
    g = q_ref.shape[1]
    sub = WINDOW
    band = 2 * WINDOW + sub
    kc, vc = k_ref[0, 0, seq_len:seq_len + ctx_len, :], v_ref[0, 0, seq_len:seq_len + ctx_len, :]
    nt_dot = lambda a, b: lax.dot_general(a, b, (((1,), (1,)), ((), ())), preferred_element_type=F32)
    for j in range(tq // sub):
        q = q_ref[0, :, j * sub:(j + 1) * sub, :].reshape(g * sub, HEAD_DIM)
        first = t * tq + j * sub
        start = pl.multiple_of(jnp.clip(first - WINDOW, 0, seq_len - band), WINDOW)
        kb, vb = k_ref[0, 0, pl.ds(start, band), :], v_ref[0, 0, pl.ds(start, band), :]
        s_ctx = nt_dot(q, kc)
        s_loc = nt_dot(q, kb)
        qpos = first + lax.broadcasted_iota(jnp.int32, (g, sub, band), 1).reshape(g * sub, band)
        kpos = start + lax.broadcasted_iota(jnp.int32, (g * sub, band), 1)
        valid = jnp.logical_and(jnp.abs(qpos - kpos) <= WINDOW, t < n_lat)
        s_loc = jnp.where(valid, s_loc, NEG_INF)
        sink = sink_ref[0, :, j * sub:(j + 1) * sub, :].reshape(g * sub, 1)
        m = jnp.maximum(jnp.maximum(jnp.max(s_ctx, axis=-1, keepdims=True),
                                    jnp.max(s_loc, axis=-1, keepdims=True)), sink)
        p_ctx, p_loc = jnp.exp(s_ctx - m), jnp.exp(s_loc - m)
        acc = (jnp.dot(p_ctx.astype(BF16), vc, preferred_element_type=F32)
               + jnp.dot(p_loc.astype(BF16), vb, preferred_element_type=F32))
        denom = acc[:, HEAD_DIM:HEAD_DIM + 1] + jnp.exp(sink - m)
        out = acc[:, :HEAD_DIM] / denom
        o_ref[0, j * sub:(j + 1) * sub, :] = jnp.concatenate(
            [out[h * sub:(h + 1) * sub] for h in range(g)], axis=-1)


def _window_attention(q, k, v, sink, seq_len, ctx_len):
    bsz, nq, s, _ = q.shape
    nkv = k.shape[1]
    g = nq // nkv
    tq = ROW_TILE
    sink_rows = jnp.broadcast_to(sink.astype(F32).reshape(nkv, g, 1, 1), (nkv, g, tq, 1))
    kv = lambda a: pl.BlockSpec((1, 1, s, a.shape[-1]), lambda b, h, t: (b, h, 0, 0))
    return pl.pallas_call(
        functools.partial(_window_attn_kernel, seq_len=seq_len, ctx_len=ctx_len, tq=tq),
        out_shape=jax.ShapeDtypeStruct((bsz, s, nq * HEAD_DIM), F32),
        grid=(bsz, nkv, s // tq),
        in_specs=[pl.BlockSpec((1, g, tq, HEAD_DIM), lambda b, h, t: (b, h, t, 0)), kv(k), kv(v),
                  pl.BlockSpec((1, g, tq, 1), lambda b, h, t: (h, 0, 0, 0))],
        out_specs=pl.BlockSpec((1, tq, g * HEAD_DIM), lambda b, h, t: (b, t, h)),
        compiler_params=_cparams("arbitrary", "arbitrary", "arbitrary"),
        name="window_attention",
    )(q, k, v, sink_rows)


DENSE_KEY_BLOCK = 1024


def _dense_attn_kernel(q_ref, k_ref, v_ref, o_ref, *, seq_len, tq, tk):
    t = pl.program_id(2)
    n_lat = seq_len // tq
    g = q_ref.shape[1]
    ctx_len = k_ref.shape[2] - seq_len
    q = q_ref[0].reshape(g * tq, HEAD_DIM)

    def step(carry, start, size):
        m, acc = carry
        s = lax.dot_general(q, k_ref[0, 0, start:start + size, :], (((1,), (1,)), ((), ())),
                            preferred_element_type=F32)
        m_new = jnp.maximum(m, jnp.max(s, axis=-1, keepdims=True))
        p = jnp.exp(s - m_new).astype(BF16)
        pv = jnp.dot(p, v_ref[0, 0, start:start + size, :], preferred_element_type=F32)
        return m_new, jnp.exp(m - m_new) * acc + pv

    def finish(carry):
        _, acc = carry
        out = acc[:, :HEAD_DIM] / acc[:, HEAD_DIM:HEAD_DIM + 1]
        o_ref[0] = jnp.concatenate([out[h * tq:(h + 1) * tq] for h in range(g)], axis=-1)

    init = (jnp.full((g * tq, 1), NEG_INF, F32), jnp.zeros((g * tq, v_ref.shape[-1]), F32))

    @pl.when(t < n_lat)
    def _():
        carry = step(init, seq_len, ctx_len)
        for j in range(seq_len // tk):
            carry = step(carry, j * tk, tk)
        finish(carry)

    @pl.when(t >= n_lat)
    def _():
        finish(step(init, seq_len, ctx_len))


def _dense_attention(q, k, v, seq_len):
    bsz, nq, s, _ = q.shape
    nkv = k.shape[1]
    g = nq // nkv
    tq = ROW_TILE
    tk = math.gcd(seq_len, DENSE_KEY_BLOCK)
    kv = lambda a: pl.BlockSpec((1, 1, s, a.shape[-1]), lambda b, h, t: (b, h, 0, 0))
    return pl.pallas_call(
        functools.partial(_dense_attn_kernel, seq_len=seq_len, tq=tq, tk=tk),
        out_shape=jax.ShapeDtypeStruct((bsz, s, nq * HEAD_DIM), F32),
        grid=(bsz, nkv, s // tq),
        in_specs=[pl.BlockSpec((1, g, tq, HEAD_DIM), lambda b, h, t: (b, h, t, 0)), kv(k), kv(v)],
        out_specs=pl.BlockSpec((1, tq, g * HEAD_DIM), lambda b, h, t: (b, t, h)),
        compiler_params=_cparams("arbitrary", "arbitrary", "arbitrary"),
        name="dense_attention",
    )(q, k, v)


def _softplus(x):
    return jnp.maximum(x, 0.0) + jnp.log(1.0 + jnp.exp(-jnp.abs(x)))


def _split(a):
    bits = lax.bitcast_convert_type(a, jnp.uint32) & jnp.uint32(0xFFFF0000)
    hi = lax.bitcast_convert_type(bits, F32)
    return hi, a - hi


def _rhs3(b):
    hi, lo = _split(b)
    return jnp.concatenate([hi, hi, lo], axis=-2).astype(BF16)


def _split3(a):
    hi, rest = _split(a)
    mid, lo = _split(rest)
    return hi, mid, lo


def _mm_exact_lhs(tbl3, x):
    return jnp.dot(tbl3, jnp.concatenate(_split3(x), axis=0).astype(BF16), preferred_element_type=F32)


def _mm_exact_rhs(x, tbl3):
    return jnp.dot(jnp.concatenate(_split3(x), axis=-1).astype(BF16), tbl3, preferred_element_type=F32)


def _mm_split(a, b):
    ah, al = _split(a)
    return jnp.dot(jnp.concatenate([ah, al, ah], axis=-1).astype(BF16), _rhs3(b), preferred_element_type=F32)


def _bmm(a, bf):
    return lax.dot_general(a.astype(BF16), bf, (((2,), (1,)), ((0,), (0,))), preferred_element_type=F32)


def _rwkv_par_kernel(z_ref, w0_ref, wup_ref, a0_ref, aup_ref, kk_ref, ka_ref, bd_ref, cum_ref,
                     rp3_ref, yvq_ref, rvk_ref, *, w, lora):
    tc = RW_CHUNK
    nc, nh = ROW_TILE // tc, w // HEAD_DIM
    assert tc == HEAD_DIM
    z = z_ref[0]
    r, k, v = z[:, :w], z[:, w:2 * w], z[:, 2 * w:3 * w]
    w_low = jnp.tanh(z[:, 3 * w:3 * w + lora])
    a_low = z[:, 3 * w + lora:]
    kk = k * kk_ref[...]
    kk = kk / jnp.maximum(jnp.sqrt(_mm_exact_rhs(kk * kk, bd_ref[...])), 1e-12)
    ksum = jnp.zeros_like(k)

    tpos = lax.broadcasted_iota(jnp.int32, (tc, w), 0)
    spos = lax.broadcasted_iota(jnp.int32, (tc, w), 1) % tc
    eye = tpos == spos
    on_diag = (lax.broadcasted_iota(jnp.int32, (w, w), 0) // tc
               == lax.broadcasted_iota(jnp.int32, (w, w), 1) // tc)
    packed = lambda a: a.reshape(nc, tc, w)

    def level_mask(s):
        return jnp.logical_and(tpos // (2 * s) == spos // (2 * s), tpos // s != spos // s)

    def blockdiag(m):
        tiled = jnp.concatenate([m.astype(BF16)] * nh, axis=1)
        return jnp.where(on_diag, tiled, jnp.zeros_like(tiled))

    def own_blocks(full):
        kept = jnp.where(on_diag, full, 0.0)
        return sum(kept[:, h * tc:(h + 1) * tc, :] for h in range(nh))

    per_dir = []
    for d in range(2):
        w_log = -_softplus(-(w0_ref[d:d + 1, :] + _mm_split(w_low, wup_ref[d]))) - 0.5
        lw = -jnp.exp(w_log)
        a = 1.0 / (1.0 + jnp.exp(-(a0_ref[d:d + 1, :] + _mm_split(a_low, aup_ref[d]))))
        kd = k * (1.0 + (a - 1.0) * ka_ref[...])
        ksum = ksum + kd
        sums = _mm_exact_lhs(cum_ref[d], lw)
        c, ctot = sums[:ROW_TILE], sums[ROW_TILE:]
        e_neg, e_rem = jnp.exp(-c), jnp.exp(ctot - c)
        b = kk * a
        before = tpos > spos if d == 0 else tpos < spos
        per_dir.append((packed(-kk * jnp.exp(c - lw)), packed(r * jnp.exp(c)), packed(b * e_neg),
                        packed(kd * e_neg), packed(b * e_rem), packed(kd * e_rem), packed(jnp.exp(ctot)),
                        jnp.broadcast_to(before, (nc, tc, w))))
    rvk_ref[0] = jnp.concatenate([r, v, ksum], axis=-1)

    at_p, rt_p, bh_p, kh_p, bc_p, kc_p, wt_p, before = (jnp.concatenate(parts, axis=0) for parts in zip(*per_dir))
    v_p = jnp.concatenate([packed(v)] * 2, axis=0)
    upto = jnp.logical_or(before, eye)
    big = lax.dot_general(jnp.concatenate([at_p, rt_p], axis=1).astype(BF16),
                          jnp.concatenate([blockdiag(bh_p), blockdiag(kh_p)], axis=1),
                          (((2,), (2,)), ((0,), (0,))), preferred_element_type=F32)
    a_ab = jnp.where(before, big[:, :tc, :w], 0.0)
    a_ak = jnp.where(before, big[:, :tc, w:], 0.0)
    a_rb = jnp.where(upto, big[:, tc:, :w], 0.0)
    a_rk = jnp.where(upto, big[:, tc:, w:], 0.0)
    x = jnp.where(eye, 1.0, jnp.where(level_mask(1), a_ab, 0.0))
    s = 2
    while s < tc:
        half = _bmm(x, blockdiag(jnp.where(level_mask(s), a_ab, 0.0)))
        x = x + _bmm(half, blockdiag(x))
        s *= 2
    akrk = _bmm(jnp.concatenate([a_ak, a_rk], axis=1), blockdiag(v_p))
    xa = _bmm(x, jnp.concatenate([blockdiag(at_p), blockdiag(akrk[:, :tc])], axis=2))
    ra = _bmm(a_rb, jnp.concatenate([blockdiag(xa[:, :, :w]), blockdiag(xa[:, :, w:])], axis=2))
    rp = rt_p + ra[:, :, :w]
    yv = ra[:, :, w:] + akrk[:, tc:]
    lhs_t = jnp.concatenate([bc_p, kc_p], axis=1).astype(BF16)
    rhs_t = jnp.concatenate([xa, jnp.concatenate([jnp.zeros_like(v_p), v_p], axis=2)], axis=1).astype(BF16)
    full = lax.dot_general(lhs_t, rhs_t, (((1,), (1,)), ((0,), (0,))), preferred_element_type=F32)
    p = own_blocks(full[:, :, :w]) + jnp.where(eye, wt_p[:, 0:1, :], 0.0)
    q = own_blocks(full[:, :, w:])
    rp3_ref[0] = jnp.concatenate([rp, p], axis=1).astype(BF16).reshape(2, nc, tc + HEAD_DIM, w)
    yvq_ref[0] = jnp.concatenate([yv, q], axis=1).reshape(2, nc, tc + HEAD_DIM, w)


def _chunk_matrices():
    t = np.arange(ROW_TILE)
    same = (t[:, None] // RW_CHUNK) == (t[None, :] // RW_CHUNK)
    tabs = []
    for run in (same & (t[None, :] <= t[:, None]), same & (t[None, :] >= t[:, None])):
        m = np.concatenate([run, same], axis=0).astype(np.float32).astype(BF16)
        tabs.append(np.concatenate([m, m, m], axis=1))
    return np.stack(tabs)


def _rwkv_par(rw, w0, w_up, a0, a_up, k_k, k_a, w):
    bsz, s, width = rw.shape
    nt = s // ROW_TILE
    nh, nc = w // HEAD_DIM, ROW_TILE // RW_CHUNK
    lora = w_up.shape[1]
    full = lambda a: pl.BlockSpec(a.shape, lambda b, t: (0,) * a.ndim)
    cum = _chunk_matrices()
    bd = _head_block_diag(w, 1.0)
    vec = lambda a: a.reshape(1, w)
    n_chunks = s // RW_CHUNK
    tc, hd = RW_CHUNK, HEAD_DIM
    consts = (w0, w_up, a0, a_up, vec(k_k), vec(k_a), bd, cum)
    return pl.pallas_call(
        functools.partial(_rwkv_par_kernel, w=w, lora=lora),
        out_shape=[jax.ShapeDtypeStruct((bsz, 2, n_chunks, tc + hd, w), BF16),
                   jax.ShapeDtypeStruct((bsz, 2, n_chunks, tc + hd, w), F32),
                   jax.ShapeDtypeStruct((bsz, s, 3 * w), F32)],
        grid=(bsz, nt),
        in_specs=[pl.BlockSpec((1, ROW_TILE, width), lambda b, t: (b, t, 0))] + [full(a) for a in consts],
        out_specs=[pl.BlockSpec((1, 2, nc, tc + hd, w), lambda b, t: (b, 0, t, 0, 0)),
                   pl.BlockSpec((1, 2, nc, tc + hd, w), lambda b, t: (b, 0, t, 0, 0)),
                   pl.BlockSpec((1, ROW_TILE, 3 * w), lambda b, t: (b, t, 0))],
        compiler_params=_cparams("arbitrary", "arbitrary"),
        name="rwkv_chunk_prep",
    )(rw, *consts)


def _rwkv_seq_kernel(rp3f_ref, yvqf_ref, rp3b_ref, yvqb_ref, yf_ref, yb_ref, g_ref):
    @pl.when(pl.program_id(1) == 0)
    def _():
        g_ref[...] = jnp.zeros_like(g_ref)

    w = g_ref.shape[-1]
    tc = RW_CHUNK
    nh = w // HEAD_DIM
    grp = rp3f_ref.shape[2]
    row_head = lax.broadcasted_iota(jnp.int32, (w, w), 0) // HEAD_DIM
    col_head = lax.broadcasted_iota(jnp.int32, (w, w), 1) // HEAD_DIM
    on_diag = row_head == col_head
    for step in range(grp):
        for d, (rp3, yvq, y) in enumerate(((rp3f_ref, yvqf_ref, yf_ref), (rp3b_ref, yvqb_ref, yb_ref))):
            ci = step if d == 0 else grp - 1 - step
            out = jnp.dot(rp3[0, 0, ci], g_ref[d].astype(BF16), preferred_element_type=F32) + yvq[0, 0, ci]
            y[0, ci * tc:(ci + 1) * tc, :] = out[:tc]
            g_ref[d] = jnp.where(on_diag, jnp.concatenate([out[tc:]] * nh, axis=0), 0.0)


RW_SCAN_GROUP = 4


def _rwkv_seq(rp3, yvq, n_lat_chunks):
    bsz, _, n_chunks = rp3.shape[:3]
    w = yvq.shape[-1]
    grp = RW_SCAN_GROUP
    assert n_lat_chunks % grp == 0 and n_chunks % grp == 0
    n_groups, n_lat, n_ctx = n_chunks // grp, n_lat_chunks // grp, (n_chunks - n_lat_chunks) // grp
    order = (lambda i: jnp.where(i < n_ctx, n_lat + i, i - n_ctx),
             lambda i: n_groups - 1 - i)
    blk = lambda d, a: pl.BlockSpec((1, 1, grp) + a.shape[3:], lambda b, i: (b, d, order[d](i), 0, 0))
    out = lambda d: pl.BlockSpec((1, grp * RW_CHUNK, w), lambda b, i: (b, order[d](i), 0))
    shp = jax.ShapeDtypeStruct((bsz, n_chunks * RW_CHUNK, w), F32)
    return pl.pallas_call(
        _rwkv_seq_kernel,
        out_shape=[shp, shp],
        grid=(bsz, n_groups),
        in_specs=[blk(0, rp3), blk(0, yvq), blk(1, rp3), blk(1, yvq)],
        out_specs=[out(0), out(1)],
        scratch_shapes=[pltpu.VMEM((2, w, w), F32)],
        compiler_params=_cparams("arbitrary", "arbitrary"),
        name="rwkv_state_scan",
    )(rp3, yvq, rp3, yvq)


def _rwkv_readout_tile(y, rvk, r_k, ln_g, ln_b, head_mean, head_sum):
    w = r_k.shape[-1]
    r, v, ksum = rvk[:, :w], rvk[:, w:2 * w], rvk[:, 2 * w:]
    yc = y - _mm_exact_rhs(y, head_mean)
    var = _mm_exact_rhs(yc * yc, head_mean)
    bonus = _mm_exact_rhs(r * ksum * r_k, head_sum) * v
    return yc * lax.rsqrt(var + RW_GN_EPS) * ln_g + ln_b + bonus


@functools.lru_cache(maxsize=None)
def _filter_position_features(seq_len, bands, width):
    n = np.arange(2 * seq_len)
    pos = np.where(n < seq_len, n, 2 * seq_len - n) % seq_len
    t = np.linspace(0.0, 1.0, seq_len)[pos][:, None]
    wpos = (2.0 * math.pi / seq_len) * pos[:, None]
    f = np.linspace(1e-4, bands - 1, bands)[None, :]
    z = np.concatenate([t, np.cos(f * wpos), np.sin(f * wpos)], axis=-1)
    return np.pad(z, ((0, 0), (0, width - z.shape[1]))).astype(np.float32)


def _hyena_filter_kernel(z_ref, fw1_ref, fb1_ref, freq_ref, fw2_ref, fb2_ref, fw3_ref, delta_ref, o_ref, *, seq_len):
    freq = freq_ref[...]
    rows = ROW_TILE
    n_orders, _, c = o_ref.shape

    def taps(i, norms):
        blk = pl.ds(pl.multiple_of(i * rows, rows), rows)
        z = z_ref[blk, :]
        h = jnp.sin(freq * (_mm_split(z, fw1_ref[...]) + fb1_ref[...]))
        h = jnp.sin(freq * (_mm_split(h, fw2_ref[...]) + fb2_ref[...]))
        h = _mm_split(h, fw3_ref[...])
        n = i * rows + lax.broadcasted_iota(jnp.int32, (rows, 1), 0)
        decay = jnp.exp(-z[:, 0:1] * jnp.abs(delta_ref[...]))
        out = []
        for o in range(n_orders):
            ho = jnp.where(n < seq_len, h[:, 2 * o * c:(2 * o + 1) * c], h[:, (2 * o + 1) * c:(2 * o + 2) * c])
            ho = ho * decay
            o_ref[o, blk, :] = jnp.where(n != seq_len, ho, 0.0)
            out.append(norms[o] + jnp.sum(jnp.abs(ho), axis=0, keepdims=True))
        return tuple(out)
    zero = jnp.zeros((1, c), F32)
    norms = lax.fori_loop(0, z_ref.shape[0] // rows, taps, (zero,) * n_orders)

    def normalise(i, carry):
        blk = pl.ds(pl.multiple_of(i * rows, rows), rows)
        for o in range(n_orders):
            o_ref[o, blk, :] = o_ref[o, blk, :] / norms[o]
        return carry
    lax.fori_loop(0, z_ref.shape[0] // rows, normalise, 0)


def _hyena_two_sided_filters(seq_len, fw1, fb1, freq, fw2, fb2, fw3, width):
    ffn = fw1.shape[1]
    bands = (fw1.shape[0] - 1) // 2
    z = _filter_position_features(seq_len, bands, ffn)
    fw1p = jnp.pad(fw1, ((0, ffn - fw1.shape[0]), (0, 0)))
    max_decay = math.log(HY_DECAY_TARGET) / HY_FAST_DECAY
    min_decay = math.log(HY_DECAY_TARGET) / HY_SLOW_DECAY
    deltas = jnp.linspace(min_decay, max_decay, width, dtype=F32).reshape(1, width)
    row = lambda a: a.reshape(1, ffn)
    full = lambda shape: pl.BlockSpec(shape, lambda i: (0,) * len(shape))
    return pl.pallas_call(
        functools.partial(_hyena_filter_kernel, seq_len=seq_len),
        out_shape=jax.ShapeDtypeStruct((2, 2 * seq_len, width), F32),
        grid=(1,),
        in_specs=[full(z.shape), full((ffn, ffn)), full((1, ffn)), full((1, ffn)), full((ffn, ffn)),
                  full((1, ffn)), full(fw3.shape), full((1, width))],
        out_specs=full((2, 2 * seq_len, width)),
        compiler_params=_cparams("arbitrary"),
        name="hyena_filter",
    )(z, fw1p, row(fb1), row(freq), fw2, row(fb2), fw3, deltas)


def kernel(x, c, ctx, c_ctx, mod_w, mod_b, norm_g, w_in, w_out, hy_conv, hy_fw1, hy_fb1, hy_freq, hy_fw2,
           hy_fb2, hy_fw3, hy_bias, rw_conv, rw_w0, rw_w_up, rw_a0, rw_a_up, rw_k_k, rw_k_a, rw_r_k,
           rw_ln_g, rw_ln_b, wa_sink, fa_q_norm, fa_k_norm, final_g):
    bsz, seq_len, d = x.shape
    ctx_len = ctx.shape[1]
    depth = w_in.shape[0]
    w_hy = hy_bias.shape[-1]
    w_rw = rw_w0.shape[-1]
    n_wa_heads = wa_sink.shape[-1]
    w_q = n_wa_heads * HEAD_DIM
    w_kv = w_q // 2
    lora = rw_w_up.shape[2] + rw_a_up.shape[2]
    branch_w = (3 * w_hy, 3 * w_rw + lora, w_q + 2 * w_kv, w_q + 2 * w_kv)
    gate_w = (w_hy, w_rw, w_q, w_q)
    assert seq_len % ROW_TILE == 0 and ctx_len % ROW_TILE == 0 and bsz % 2 == 0
    n_lat_tiles = seq_len // ROW_TILE

    splits = tuple(wd for pair in zip(branch_w, gate_w) for wd in pair)
    assert all(wd % LANES == 0 for wd in splits)
    w_in_b = w_in.astype(BF16)
    w_out_b = w_out.astype(BF16)

    pad_rows = (-(bsz + 1)) % SUBLANES
    cond = jnp.concatenate([c, c_ctx[None], jnp.zeros((pad_rows, d), F32)], axis=0)
    mod = _modulation(cond, mod_w, mod_b)
    mod_lat = mod[:, :bsz].reshape(depth, bsz, 3, d)
    mod_ctx = jnp.broadcast_to(mod[:, bsz].reshape(depth, 1, 3, d), (depth, bsz, 3, d))
    mods = jnp.stack([mod_lat, mod_ctx], axis=2)

    rope = _rope_tables(seq_len, ctx_len)
    xs = jnp.concatenate([x, ctx], axis=1)
    for l in range(depth):
        last = l == depth - 1
        hyc, rwc, gates, qw, kw, vw, qf, kf, vf = _inproj(
            xs, mods[l], norm_g[l], w_in_b[l], hy_conv[l], rw_conv[l], splits, n_lat_tiles, rope,
            fa_q_norm[l], fa_k_norm[l], w_q, w_kv)

        filt = functools.partial(_hyena_two_sided_filters, fw1=hy_fw1[l], fb1=hy_fb1[l], freq=hy_freq[l],
                                 fw2=hy_fw2[l], fb2=hy_fb2[l], fw3=hy_fw3[l], width=w_hy)
        bias = hy_bias[l].reshape(2, 1, w_hy)
        spec = _filter_spectrum(filt(seq_len))
        nt_hy = w_hy // LANES
        y1 = _fftconv_gated(hyc, 0, hyc, nt_hy, spec, 0, bias, seq_len)
        a_lat = _fftconv_gated(y1, 0, hyc, 2 * nt_hy, spec, 1, bias, seq_len)
        a_ctx = a_lat if last else _ctx_hyena(hyc, seq_len // ctx_len, filt(ctx_len), bias, ctx_len, w_hy)

        rp3, yvq, rvk = _rwkv_par(rwc, rw_w0[l], rw_w_up[l], rw_a0[l], rw_a_up[l], rw_k_k[l], rw_k_a[l], w_rw)
        yf, yb = _rwkv_seq(rp3, yvq, seq_len // RW_CHUNK)

        c_mix = _window_attention(qw, kw, vw, wa_sink[l], seq_len, ctx_len)
        d_mix = _dense_attention(qf, kf, vf, seq_len)

        xs = _outproj(a_lat, a_ctx, yf, yb, rvk, rw_r_k[l], rw_ln_g[l], rw_ln_b[l], c_mix, d_mix, gates, xs,
                      mods[l], w_out_b[l], final_g, n_lat_tiles, last)
    return xs
```

```python
import functools
import math

import jax
import jax.numpy as jnp
import numpy as np
from jax import lax
from jax.experimental import pallas as pl
from jax.experimental.pallas import tpu as pltpu

HEAD_DIM = 64
GRID_W = 64
WINDOW = 128
NORM_EPS = 1e-6
RW_GN_EPS = 64e-5
NEG_INF = -1e30
ROPE_THETA = 10000.0
HY_FAST_DECAY = 0.3
HY_SLOW_DECAY = 1.5
HY_DECAY_TARGET = 1e-2

ROW_TILE = 256
LANES = 128
SUBLANES = 8
FFT_N1 = 64
FFT_UNROLL = 16
FFT_PLANE_PAD = 8
RW_CHUNK = 64
VMEM_LIMIT = 60 * 1024 * 1024

F32 = jnp.float32
BF16 = jnp.bfloat16
HI = lax.Precision.HIGHEST


def _dot(a, b):
    return jnp.dot(a, b, preferred_element_type=F32, precision=HI)


def _cparams(*sem):
    return pltpu.CompilerParams(dimension_semantics=sem, vmem_limit_bytes=VMEM_LIMIT)


def _const_spec(shape):
    return pl.BlockSpec(shape, lambda *_: (0,) * len(shape), pipeline_mode=pl.Buffered(1))


def _silu(x):
    return x * (1.0 / (1.0 + jnp.exp(-x)))


def _mod_kernel(c_ref, w_ref, b_ref, o_ref):
    o_ref[0] = _dot(_silu(c_ref[...]), w_ref[0]) + b_ref[0]


def _modulation(cond, mod_w, mod_b):
    depth, d, d3 = mod_w.shape
    rows = cond.shape[0]
    return pl.pallas_call(
        _mod_kernel,
        out_shape=jax.ShapeDtypeStruct((depth, rows, d3), F32),
        grid=(depth,),
        in_specs=[pl.BlockSpec((rows, d), lambda l: (0, 0)),
                  pl.BlockSpec((1, d, d3), lambda l: (l, 0, 0)),
                  pl.BlockSpec((1, 1, d3), lambda l: (l, 0, 0))],
        out_specs=pl.BlockSpec((1, rows, d3), lambda l: (l, 0, 0)),
        compiler_params=_cparams("arbitrary"),
        name="modulation",
    )(cond, mod_w, mod_b.reshape(depth, 1, d3))


def _blockreal(m):
    return np.block([[m.real, -m.imag], [m.imag, m.real]])


@functools.lru_cache(maxsize=None)
def _fft_tables(seq_len):
    n = 2 * seq_len
    n1, n2 = FFT_N1, n // FFT_N1
    h1 = n1 // 2
    j2 = np.arange(n2)[:, None, None]
    k1 = np.arange(n1)[None, :, None]
    t1 = np.exp(-2j * np.pi * (j2 * k1 / n + k1 * np.arange(n1)[None, None, :] / n1))
    t1_data = np.stack([_blockreal(t1[j][:, :h1]) for j in range(n2)])
    t1_real = np.concatenate([t1.real, t1.imag], axis=1)
    f2 = np.exp(-2j * np.pi * np.outer(np.arange(n2), np.arange(n2)) / n2)
    f2_fwd = _blockreal(f2)
    f2_inv = _blockreal(np.conj(f2))
    t4 = np.exp(2j * np.pi * (np.arange(h1)[None, :, None] * np.arange(n1)[None, None, :] / n1
                              + j2 * np.arange(n1)[None, None, :] / n)) / n
    t4 = np.stack([_blockreal(t4[j]) for j in range(n2)])
    f2_parts = np.concatenate([f2.real, f2.imag], axis=0)
    return tuple(_lhs3_table(t) for t in (t1_data, t1_real, f2_fwd, f2_parts, t4))


def _lhs3_table(m):
    hi = m.astype(np.float32).astype(BF16)
    lo = (m - hi.astype(np.float64)).astype(np.float32).astype(BF16)
    return np.concatenate([hi, lo, hi], axis=-1)


def _mm3(tbl3, x):
    return jnp.dot(tbl3, _rhs3(x), preferred_element_type=F32)


def _spectrum_kernel(k_ref, t1_ref, f2_ref, o_ref, a_ref, *, n1, n2):
    def stage1(j, carry):
        rows = k_ref[0, pl.ds(j, n1, stride=n2), :]
        a_ref[pl.ds(j, 2 * n1, stride=n2 + FFT_PLANE_PAD), :] = _mm3(t1_ref[j], rows)
        return carry
    lax.fori_loop(0, n2, stage1, 0, unroll=FFT_UNROLL)

    def stage2(i, carry):
        pitch = n2 + FFT_PLANE_PAD
        re = a_ref[pl.ds(pl.multiple_of(i * pitch, 8), n2), :]
        im = a_ref[pl.ds(pl.multiple_of((n1 + i) * pitch, 8), n2), :]
        o_ref[0, i] = _mm3(f2_ref[...], jnp.concatenate([re, im], axis=0))
        return carry
    lax.fori_loop(0, n1, stage2, 0, unroll=2)


def _filter_spectrum(kfilt):
    g, n, w = kfilt.shape
    n1, n2 = FFT_N1, n // FFT_N1
    _, t1_real, f2_fwd, _, _ = _fft_tables(n // 2)
    const = _const_spec
    return pl.pallas_call(
        functools.partial(_spectrum_kernel, n1=n1, n2=n2),
        out_shape=jax.ShapeDtypeStruct((g, n1, 2 * n2, w), F32),
        grid=(g, w // LANES),
        in_specs=[pl.BlockSpec((1, n, LANES), lambda gi, j: (gi, 0, j)),
                  const(t1_real.shape), const(f2_fwd.shape)],
        out_specs=pl.BlockSpec((1, n1, 2 * n2, LANES), lambda gi, j: (gi, 0, 0, j)),
        scratch_shapes=[pltpu.VMEM((2 * n1 * (n2 + FFT_PLANE_PAD), LANES), F32)],
        compiler_params=_cparams("arbitrary", "arbitrary"),
        name="hyena_filter_spectrum",
    )(kfilt, t1_real, f2_fwd)


def _fftconv_kernel(u_ref, m_ref, spec_ref, bias_ref, t1_ref, f2p_ref, t4_ref, o_ref, ar_ref, ai_ref,
                    *, n1, n2):
    h1 = n1 // 2
    c = u_ref.shape[-1]
    pitch = n2 + FFT_PLANE_PAD

    def stage1(j, carry):
        za = u_ref[0, pl.ds(j, h1, stride=n2), :]
        zb = u_ref[1, pl.ds(j, h1, stride=n2), :]
        res = _mm3(t1_ref[j], jnp.concatenate([za, zb], axis=0))
        ar_ref[pl.ds(j, n1, stride=pitch), :] = res[:n1]
        ai_ref[pl.ds(j, n1, stride=pitch), :] = res[n1:]
        return carry
    lax.fori_loop(0, n2, stage1, 0, unroll=FFT_UNROLL)

    def stage2(i, carry):
        rows = pl.ds(pl.multiple_of(i * pitch, SUBLANES), n2)
        p = _mm3(f2p_ref[...], jnp.concatenate([ar_ref[rows, :], ai_ref[rows, :]], axis=-1))
        xr = p[:n2, :c] - p[n2:, c:]
        xi = p[:n2, c:] + p[n2:, :c]
        kr, ki = spec_ref[0, i, :n2, :], spec_ref[0, i, n2:, :]
        y = jnp.concatenate([xr * kr - xi * ki, xr * ki + xi * kr], axis=-1)
        q = _mm3(f2p_ref[...], y)
        ar_ref[rows, :] = q[:n2, :c] + q[n2:, c:]
        ai_ref[rows, :] = q[:n2, c:] - q[n2:, :c]
        return carry
    lax.fori_loop(0, n1, stage2, 0, unroll=8)

    bias = bias_ref[0]

    def stage4(j, carry):
        planes = pl.ds(j, n1, stride=pitch)
        y = _mm3(t4_ref[j], jnp.concatenate([ar_ref[planes, :], ai_ref[planes, :]], axis=0))
        rows = pl.ds(j, h1, stride=n2)
        for p in range(2):
            u = u_ref[p, rows, :]
            o_ref[p, rows, :] = m_ref[p, rows, :] * (y[p * h1:(p + 1) * h1] + bias * u)
        return carry
    lax.fori_loop(0, n2, stage4, 0, unroll=FFT_UNROLL)


def _fftconv_gated(u, u_col, mult, mult_col, spec, conv_idx, bias, seq_len):
    bsz = u.shape[0]
    w = spec.shape[-1]
    n = 2 * seq_len
    n1, n2 = FFT_N1, n // FFT_N1
    t1_data, _, _, f2_parts, t4 = _fft_tables(seq_len)
    const = _const_spec
    return pl.pallas_call(
        functools.partial(_fftconv_kernel, n1=n1, n2=n2),
        out_shape=jax.ShapeDtypeStruct((bsz, seq_len, w), F32),
        grid=(w // LANES, bsz // 2),
        in_specs=[pl.BlockSpec((2, seq_len, LANES), lambda j, p: (p, 0, u_col + j)),
                  pl.BlockSpec((2, seq_len, LANES), lambda j, p: (p, 0, mult_col + j)),
                  pl.BlockSpec((1, n1, 2 * n2, LANES), lambda j, p: (conv_idx, 0, 0, j),
                               pipeline_mode=pl.Buffered(1)),
                  pl.BlockSpec((1, 1, LANES), lambda j, p: (conv_idx, 0, j)),
                  const(t1_data.shape), const(f2_parts.shape), const(t4.shape)],
        out_specs=pl.BlockSpec((2, seq_len, LANES), lambda j, p: (p, 0, j)),
        scratch_shapes=[pltpu.VMEM((n1 * (n2 + FFT_PLANE_PAD), LANES), F32)] * 2,
        compiler_params=_cparams("arbitrary", "arbitrary"),
        name="hyena_fftconv",
    )(u, mult, spec, bias, t1_data, f2_parts, t4)


@functools.lru_cache(maxsize=None)
def _small_fft_tables(seq_len):
    n = 2 * seq_len
    f = np.exp(-2j * np.pi * np.outer(np.arange(n), np.arange(n)) / n)
    fwd = _blockreal(f[:, :seq_len])
    real = np.concatenate([f.real, f.imag], axis=0)
    inv = _blockreal(np.conj(f)[:seq_len, :] / n)
    return _lhs3_table(fwd), _lhs3_table(real), _lhs3_table(inv)


def _ctx_hyena_kernel(v_ref, x1_ref, x2_ref, k_ref, bias_ref, fwd_ref, real_ref, inv_ref, o_ref, *, seq_len):
    n = 2 * seq_len

    def conv(ua, ub, g):
        spec = _mm3(real_ref[...], k_ref[g])
        x = _mm3(fwd_ref[...], jnp.concatenate([ua, ub], axis=0))
        xr, xi, kr, ki = x[:n], x[n:], spec[:n], spec[n:]
        y = _mm3(inv_ref[...], jnp.concatenate([xr * kr - xi * ki, xr * ki + xi * kr], axis=0))
        b = bias_ref[g]
        return y[:seq_len] + b * ua, y[seq_len:] + b * ub

    c1a, c1b = conv(v_ref[0], v_ref[1], 0)
    y1a, y1b = x1_ref[0] * c1a, x1_ref[1] * c1b
    c2a, c2b = conv(y1a, y1b, 1)
    o_ref[0] = x2_ref[0] * c2a
    o_ref[1] = x2_ref[1] * c2b


def _ctx_hyena(hyc, row_block, kfilt, bias, seq_len, w):
    bsz = hyc.shape[0]
    n = 2 * seq_len
    fwd, real, inv = _small_fft_tables(seq_len)
    nt = w // LANES
    col = lambda c0: pl.BlockSpec((2, seq_len, LANES), lambda j, p: (p, row_block, c0 + j))
    return pl.pallas_call(
        functools.partial(_ctx_hyena_kernel, seq_len=seq_len),
        out_shape=jax.ShapeDtypeStruct((bsz, seq_len, w), F32),
        grid=(nt, bsz // 2),
        in_specs=[col(0), col(nt), col(2 * nt),
                  pl.BlockSpec((2, n, LANES), lambda j, p: (0, 0, j)),
                  pl.BlockSpec((2, 1, LANES), lambda j, p: (0, 0, j)),
                  _const_spec(fwd.shape), _const_spec(real.shape), _const_spec(inv.shape)],
        out_specs=pl.BlockSpec((2, seq_len, LANES), lambda j, p: (p, 0, j)),
        compiler_params=_cparams("arbitrary", "arbitrary"),
        name="hyena_ctx",
    )(hyc, hyc, hyc, kfilt, bias, fwd, real, inv)


def _rms(x, g):
    return x * lax.rsqrt(jnp.mean(x * x, axis=-1, keepdims=True) + NORM_EPS) * g


def _inproj_kernel(x_ref, xprev_ref, xnext_ref, mod_ref, g_ref, w_ref, hycw_ref, rwcw_ref,
                   cos_ref, shi_ref, slo_ref, qg_ref, kg_ref, bdq_ref, bdk_ref,
                   hy_ref, rw_ref, gt_ref, qw_ref, kw_ref, vw_ref, qf_ref, kf_ref, vf_ref,
                   *, splits, wq, wk, n_lat_tiles, n_tiles):
    shift, scale = mod_ref[0, 0, 0:1, :], mod_ref[0, 0, 1:2, :]
    norm_mod = lambda x: (_rms(x, g_ref[...]) * (1.0 + scale) + shift).astype(BF16)
    z = jnp.dot(norm_mod(x_ref[0]), w_ref[...], preferred_element_type=F32)
    offs = np.cumsum((0,) + tuple(splits))
    seg = lambda i: z[:, offs[i]:offs[i + 1]]
    halo_h = norm_mod(jnp.concatenate([xprev_ref[0], xnext_ref[0]], axis=0))
    for i, cw_ref, o_ref in ((0, hycw_ref, hy_ref), (2, rwcw_ref, rw_ref)):
        halo = jnp.dot(halo_h, w_ref[:, offs[i]:offs[i + 1]], preferred_element_type=F32)
        o_ref[0] = _short_conv_tile(seg(i), halo[:SUBLANES], halo[SUBLANES:], cw_ref[...], pl.program_id(1),
                                    n_lat_tiles, n_tiles)
    gt_ref[0] = _silu(jnp.concatenate([seg(1), seg(3), seg(5), seg(7)], axis=-1))
    _qk_emit(seg(4), seg(6), cos_ref[...], shi_ref[...], slo_ref[...], qg_ref[...], kg_ref[...],
             bdq_ref[...], bdk_ref[...], qw_ref, kw_ref, vw_ref, qf_ref, kf_ref, vf_ref, wq, wk)


def _inproj(xs, mods, norm_g, w_in, hy_conv, rw_conv, splits, n_lat_tiles, rope, q_gain, k_gain, wq, wk):
    widths = (splits[0], splits[2], None, None, sum(splits[1::2]))
    bsz, s, d = xs.shape
    nt = s // ROW_TILE
    nq, nk = wq // HEAD_DIM, wk // HEAD_DIM
    prev, nxt = _halo_specs(d, nt)
    tile = lambda wd: pl.BlockSpec((1, ROW_TILE, wd), lambda b, t: (b, t, 0))
    tab = pl.BlockSpec((ROW_TILE, HEAD_DIM), lambda b, t: (t, 0))
    hm = lambda n, wd=HEAD_DIM: pl.BlockSpec((1, n, ROW_TILE, wd), lambda b, t: (b, 0, t, 0))
    flat = lambda wd: jax.ShapeDtypeStruct((bsz, s, wd), F32)
    heads = lambda n, wd=HEAD_DIM: jax.ShapeDtypeStruct((bsz, n, s, wd), BF16)
    return pl.pallas_call(
        functools.partial(_inproj_kernel, splits=splits, wq=wq, wk=wk, n_lat_tiles=n_lat_tiles, n_tiles=nt),
        out_shape=[flat(widths[0]), flat(widths[1]), flat(widths[4]),
                   heads(nq), heads(nk), heads(nk, LANES), heads(nq), heads(nk), heads(nk, LANES)],
        grid=(bsz, nt),
        in_specs=[tile(d), prev, nxt,
                  pl.BlockSpec((1, 1, 3, d), lambda b, t: (b, t // n_lat_tiles, 0, 0)),
                  pl.BlockSpec((1, d), lambda b, t: (0, 0)),
                  _const_spec(w_in.shape), pl.BlockSpec(hy_conv.shape, lambda b, t: (0, 0)),
                  pl.BlockSpec(rw_conv.shape, lambda b, t: (0, 0)), tab, tab, tab,
                  pl.BlockSpec((1, wq), lambda b, t: (0, 0)), pl.BlockSpec((1, wk), lambda b, t: (0, 0)),
                  _const_spec((3 * wq, wq)), _const_spec((3 * wk, wk))],
        out_specs=[tile(widths[0]), tile(widths[1]), tile(widths[4]),
                   hm(nq), hm(nk), hm(nk, LANES), hm(nq), hm(nk), hm(nk, LANES)],
        compiler_params=_cparams("arbitrary", "arbitrary"),
        name="in_projection",
    )(xs, xs, xs, mods, norm_g.reshape(1, d), w_in, hy_conv, rw_conv, *rope,
      jnp.tile(q_gain, nq).reshape(1, wq), jnp.tile(k_gain, nk).reshape(1, wk),
      _head_block_diag(wq, 1.0 / HEAD_DIM), _head_block_diag(wk, 1.0 / HEAD_DIM))


def _halo_specs(width, n_tiles):
    per = ROW_TILE // SUBLANES
    prev = pl.BlockSpec((1, SUBLANES, width), lambda b, t: (b, jnp.maximum(t * per - 1, 0), 0))
    nxt = pl.BlockSpec((1, SUBLANES, width), lambda b, t: (b, jnp.minimum((t + 1) * per, n_tiles * per - 1), 0))
    return prev, nxt


def _short_conv_tile(z, prev8, next8, w, t, n_lat_tiles, n_tiles):
    first = jnp.logical_or(t == 0, t == n_lat_tiles)
    last = jnp.logical_or(t == n_lat_tiles - 1, t == n_tiles - 1)
    above = jnp.where(first, 0.0, prev8[SUBLANES - 1:SUBLANES, :])
    below = jnp.where(last, 0.0, next8[0:1, :])
    row = lax.broadcasted_iota(jnp.int32, z.shape, 0)
    zm1 = jnp.where(row == 0, above, pltpu.roll(z, 1, 0))
    zp1 = jnp.where(row == z.shape[0] - 1, below, pltpu.roll(z, z.shape[0] - 1, 0))
    return zm1 * w[0:1, :] + z * w[1:2, :] + zp1 * w[2:3, :]


def _outproj_kernel(a_lat_ref, a_ctx_ref, yf_ref, yb_ref, rvk_ref, rk_ref, lng_ref, lnb_ref, hmean_ref, hsum_ref,
                    c_ref, d_ref, gt_ref, x_ref, mod_ref, w_ref, fg_ref, o_ref, *, n_lat_tiles, final):
    is_ctx = pl.program_id(1) >= n_lat_tiles
    a = jnp.where(is_ctx, a_ctx_ref[0], a_lat_ref[0])
    b = _rwkv_readout_tile(yf_ref[0] + yb_ref[0], rvk_ref[0], rk_ref[...], lng_ref[...], lnb_ref[...],
                           hmean_ref[...], hsum_ref[...])
    mix = jnp.concatenate([a, b, c_ref[0], d_ref[0]], axis=-1) * gt_ref[0]
    y = jnp.dot(mix.astype(BF16), w_ref[...], preferred_element_type=F32)
    x = x_ref[0] + mod_ref[0, 0, 2:3, :] * y
    o_ref[0] = _rms(x, fg_ref[...]) if final else x


def _outproj(a_lat, a_ctx, yf, yb, rvk, r_k, ln_g, ln_b, cmix, dmix, gates, xs, mods, w_out, final_g,
             n_lat_tiles, final):
    bsz, s, d = xs.shape
    wb = a_lat.shape[-1]
    nt = n_lat_tiles if final else s // ROW_TILE
    tile = lambda wd: pl.BlockSpec((1, ROW_TILE, wd), lambda b, t: (b, t, 0))
    vec = pl.BlockSpec((1, wb), lambda b, t: (0, 0))
    head_tab = _const_spec((3 * wb, wb))
    return pl.pallas_call(
        functools.partial(_outproj_kernel, n_lat_tiles=n_lat_tiles, final=final),
        out_shape=jax.ShapeDtypeStruct((bsz, nt * ROW_TILE, d), F32),
        grid=(bsz, nt),
        in_specs=[pl.BlockSpec((1, ROW_TILE, wb), lambda b, t: (b, jnp.minimum(t, n_lat_tiles - 1), 0)),
                  pl.BlockSpec((1, ROW_TILE, wb), lambda b, t: (b, 0, 0)),
                  tile(wb), tile(wb), tile(3 * wb), vec, vec, vec, head_tab, head_tab,
                  tile(wb), tile(wb), tile(4 * wb), tile(d),
                  pl.BlockSpec((1, 1, 3, d), lambda b, t: (b, t // n_lat_tiles, 0, 0)),
                  _const_spec(w_out.shape),
                  pl.BlockSpec((1, d), lambda b, t: (0, 0))],
        out_specs=tile(d),
        compiler_params=_cparams("arbitrary", "arbitrary"),
        name="out_projection",
    )(a_lat, a_ctx, yf, yb, rvk, r_k.reshape(1, wb), ln_g.reshape(1, wb), ln_b.reshape(1, wb),
      _head_block_diag(wb, 1.0 / HEAD_DIM), _head_block_diag(wb, 1.0), cmix, dmix, gates, xs, mods, w_out, final_g.reshape(1, d))


def _rope_tables(seq_len, ctx_len):
    rows = seq_len // GRID_W
    row = jnp.repeat(jnp.arange(rows, dtype=F32), GRID_W)
    col = jnp.tile(jnp.arange(GRID_W, dtype=F32), rows)
    n_freq = HEAD_DIM // 4
    inv_freq = ROPE_THETA ** (-jnp.arange(n_freq, dtype=F32) / n_freq)
    ar, ac = row[:, None] * inv_freq, col[:, None] * inv_freq
    zero = jnp.zeros_like(ar)
    cos = jnp.concatenate([jnp.cos(ar), jnp.cos(ar), jnp.cos(ac), jnp.cos(ac)], axis=-1)
    sin_hi = jnp.concatenate([-jnp.sin(ar), zero, -jnp.sin(ac), zero], axis=-1)
    sin_lo = jnp.concatenate([zero, jnp.sin(ar), zero, jnp.sin(ac)], axis=-1)
    pad = lambda t, v: jnp.concatenate([t, jnp.full((ctx_len, HEAD_DIM), v, F32)], axis=0)
    return pad(cos, 1.0), pad(sin_hi, 0.0), pad(sin_lo, 0.0)


def _rope(x, cos, sin_hi, sin_lo):
    q = HEAD_DIM // 4
    w = x.shape[-1]
    return x * cos + pltpu.roll(x, w - q, 1) * sin_hi + pltpu.roll(x, q, 1) * sin_lo


def _head_mean_sq(x, bd3):
    return _mm_exact_rhs(x * x, bd3)


def _qk_emit(wa, fa, cos, shi, slo, q_gain, k_gain, bdq, bdk, qw_ref, kw_ref, vw_ref, qf_ref, kf_ref, vf_ref,
             wq, wk):
    nq, nk = wq // HEAD_DIM, wk // HEAD_DIM
    tab = lambda t, n: jnp.concatenate([t] * n, axis=-1)
    cq, hq, lq = tab(cos, nq), tab(shi, nq), tab(slo, nq)
    ck, hk, lk = tab(cos, nk), tab(shi, nk), tab(slo, nk)
    scale = HEAD_DIM ** -0.5

    def emit(ref, val, n):
        for h in range(n):
            ref[0, h] = val[:, h * HEAD_DIM:(h + 1) * HEAD_DIM].astype(ref.dtype)

    def emit_values(ref, v):
        lane = lax.broadcasted_iota(jnp.int32, (v.shape[0], LANES - HEAD_DIM), 1)
        ones_pad = jnp.where(lane == 0, 1.0, 0.0)
        for h in range(nk):
            ref[0, h] = jnp.concatenate([v[:, h * HEAD_DIM:(h + 1) * HEAD_DIM], ones_pad],
                                        axis=-1).astype(ref.dtype)

    emit(qw_ref, _rope(wa[:, :wq], cq, hq, lq) * scale, nq)
    emit(kw_ref, _rope(wa[:, wq:wq + wk], ck, hk, lk), nk)
    emit_values(vw_ref, wa[:, wq + wk:])
    q, k = fa[:, :wq], fa[:, wq:wq + wk]
    q = q * lax.rsqrt(_head_mean_sq(q, bdq) + NORM_EPS) * q_gain
    k = k * lax.rsqrt(_head_mean_sq(k, bdk) + NORM_EPS) * k_gain
    emit(qf_ref, _rope(q, cq, hq, lq) * scale, nq)
    emit(kf_ref, _rope(k, ck, hk, lk), nk)
    emit_values(vf_ref, fa[:, wq + wk:])


def _head_block_diag(width, value):
    h = np.arange(width) // HEAD_DIM
    bd = ((h[:, None] == h[None, :]) * value).astype(np.float32).astype(BF16)
    assert np.all(bd.astype(np.float32) == (h[:, None] == h[None, :]) * value)
    return np.concatenate([bd, bd, bd], axis=0)


def _window_attn_kernel(q_ref, k_ref, v_ref, sink_ref, o_ref, *, seq_len, ctx_len, tq):
    t = pl.program_id(2)
    n_lat = seq_len // tq
    g = q_ref.shape[1]
    sub = WINDOW
    band = 2 * WINDOW + sub
    kc, vc = k_ref[0, 0, seq_len:seq_len + ctx_len, :], v_ref[0, 0, seq_len:seq_len + ctx_len, :]
    nt_dot = lambda a, b: lax.dot_general(a, b, (((1,), (1,)), ((), ())), preferred_element_type=F32)
    for j in range(tq // sub):
        q = q_ref[0, :, j * sub:(j + 1) * sub, :].reshape(g * sub, HEAD_DIM)
        first = t * tq + j * sub
        start = pl.multiple_of(jnp.clip(first - WINDOW, 0, seq_len - band), WINDOW)
        kb, vb = k_ref[0, 0, pl.ds(start, band), :], v_ref[0, 0, pl.ds(start, band), :]
        s_ctx = nt_dot(q, kc)
        s_loc = nt_dot(q, kb)
        qpos = first + lax.broadcasted_iota(jnp.int32, (g, sub, band), 1).reshape(g * sub, band)
        kpos = start + lax.broadcasted_iota(jnp.int32, (g * sub, band), 1)
        valid = jnp.logical_and(jnp.abs(qpos - kpos) <= WINDOW, t < n_lat)
        s_loc = jnp.where(valid, s_loc, NEG_INF)
        sink = sink_ref[0, :, j * sub:(j + 1) * sub, :].reshape(g * sub, 1)
        m = jnp.maximum(jnp.maximum(jnp.max(s_ctx, axis=-1, keepdims=True),
                                    jnp.max(s_loc, axis=-1, keepdims=True)), sink)
        p_ctx, p_loc = jnp.exp(s_ctx - m), jnp.exp(s_loc - m)
        acc = (jnp.dot(p_ctx.astype(BF16), vc, preferred_element_type=F32)
               + jnp.dot(p_loc.astype(BF16), vb, preferred_element_type=F32))
        denom = acc[:, HEAD_DIM:HEAD_DIM + 1] + jnp.exp(sink - m)
        out = acc[:, :HEAD_DIM] / denom
        o_ref[0, j * sub:(j + 1) * sub, :] = jnp.concatenate(
            [out[h * sub:(h + 1) * sub] for h in range(g)], axis=-1)


def _window_attention(q, k, v, sink, seq_len, ctx_len):
    bsz, nq, s, _ = q.shape
    nkv = k.shape[1]
    g = nq // nkv
    tq = ROW_TILE
    sink_rows = jnp.broadcast_to(sink.astype(F32).reshape(nkv, g, 1, 1), (nkv, g, tq, 1))
    kv = lambda a: pl.BlockSpec((1, 1, s, a.shape[-1]), lambda b, h, t: (b, h, 0, 0))
    return pl.pallas_call(
        functools.partial(_window_attn_kernel, seq_len=seq_len, ctx_len=ctx_len, tq=tq),
        out_shape=jax.ShapeDtypeStruct((bsz, s, nq * HEAD_DIM), F32),
        grid=(bsz, nkv, s // tq),
        in_specs=[pl.BlockSpec((1, g, tq, HEAD_DIM), lambda b, h, t: (b, h, t, 0)), kv(k), kv(v),
                  pl.BlockSpec((1, g, tq, 1), lambda b, h, t: (h, 0, 0, 0))],
        out_specs=pl.BlockSpec((1, tq, g * HEAD_DIM), lambda b, h, t: (b, t, h)),
        compiler_params=_cparams("arbitrary", "arbitrary", "arbitrary"),
        name="window_attention",
    )(q, k, v, sink_rows)


DENSE_KEY_BLOCK = 2048


def _dense_attn_kernel(q_ref, k_ref, v_ref, o_ref, *, seq_len, tq, tk):
    t = pl.program_id(2)
    n_lat = seq_len // tq
    g = q_ref.shape[1]
    ctx_len = k_ref.shape[2] - seq_len
    q = q_ref[0].reshape(g * tq, HEAD_DIM)

    def step(carry, start, size):
        m, acc = carry
        s = lax.dot_general(q, k_ref[0, 0, start:start + size, :], (((1,), (1,)), ((), ())),
                            preferred_element_type=F32)
        m_new = jnp.maximum(m, jnp.max(s, axis=-1, keepdims=True))
        p = jnp.exp(s - m_new).astype(BF16)
        pv = jnp.dot(p, v_ref[0, 0, start:start + size, :], preferred_element_type=F32)
        return m_new, jnp.exp(m - m_new) * acc + pv

    def finish(carry):
        _, acc = carry
        out = acc[:, :HEAD_DIM] / acc[:, HEAD_DIM:HEAD_DIM + 1]
        o_ref[0] = jnp.concatenate([out[h * tq:(h + 1) * tq] for h in range(g)], axis=-1)

    init = (jnp.full((g * tq, 1), NEG_INF, F32), jnp.zeros((g * tq, v_ref.shape[-1]), F32))

    @pl.when(t < n_lat)
    def _():
        carry = step(init, seq_len, ctx_len)
        for j in range(seq_len // tk):
            carry = step(carry, j * tk, tk)
        finish(carry)

    @pl.when(t >= n_lat)
    def _():
        finish(step(init, seq_len, ctx_len))


def _dense_attention(q, k, v, seq_len):
    bsz, nq, s, _ = q.shape
    nkv = k.shape[1]
    g = nq // nkv
    tq = ROW_TILE
    tk = math.gcd(seq_len, DENSE_KEY_BLOCK)
    kv = lambda a: pl.BlockSpec((1, 1, s, a.shape[-1]), lambda b, h, t: (b, h, 0, 0))
    return pl.pallas_call(
        functools.partial(_dense_attn_kernel, seq_len=seq_len, tq=tq, tk=tk),
        out_shape=jax.ShapeDtypeStruct((bsz, s, nq * HEAD_DIM), F32),
        grid=(bsz, nkv, s // tq),
        in_specs=[pl.BlockSpec((1, g, tq, HEAD_DIM), lambda b, h, t: (b, h, t, 0)), kv(k), kv(v)],
        out_specs=pl.BlockSpec((1, tq, g * HEAD_DIM), lambda b, h, t: (b, t, h)),
        compiler_params=_cparams("arbitrary", "arbitrary", "arbitrary"),
        name="dense_attention",
    )(q, k, v)


def _softplus(x):
    return jnp.maximum(x, 0.0) + jnp.log(1.0 + jnp.exp(-jnp.abs(x)))


def _split(a):
    bits = lax.bitcast_convert_type(a, jnp.uint32) & jnp.uint32(0xFFFF0000)
    hi = lax.bitcast_convert_type(bits, F32)
    return hi, a - hi


def _rhs3(b):
    hi, lo = _split(b)
    return jnp.concatenate([hi, hi, lo], axis=-2).astype(BF16)


def _split3(a):
    hi, rest = _split(a)
    mid, lo = _split(rest)
    return hi, mid, lo


def _mm_exact_lhs(tbl3, x):
    return jnp.dot(tbl3, jnp.concatenate(_split3(x), axis=0).astype(BF16), preferred_element_type=F32)


def _mm_exact_rhs(x, tbl3):
    return jnp.dot(jnp.concatenate(_split3(x), axis=-1).astype(BF16), tbl3, preferred_element_type=F32)


def _mm_split(a, b):
    ah, al = _split(a)
    return jnp.dot(jnp.concatenate([ah, al, ah], axis=-1).astype(BF16), _rhs3(b), preferred_element_type=F32)


def _bmm(a, bf):
    return lax.dot_general(a.astype(BF16), bf, (((2,), (1,)), ((0,), (0,))), preferred_element_type=F32)


def _rwkv_par_kernel(z_ref, w0_ref, wup_ref, a0_ref, aup_ref, kk_ref, ka_ref, bd_ref, cum_ref,
                     rp3_ref, yvq_ref, rvk_ref, *, w, lora):
    tc = RW_CHUNK
    nc, nh = ROW_TILE // tc, w // HEAD_DIM
    assert tc == HEAD_DIM
    z = z_ref[0]
    r, k, v = z[:, :w], z[:, w:2 * w], z[:, 2 * w:3 * w]
    w_low = jnp.tanh(z[:, 3 * w:3 * w + lora])
    a_low = z[:, 3 * w + lora:]
    kk = k * kk_ref[...]
    kk = kk / jnp.maximum(jnp.sqrt(_mm_exact_rhs(kk * kk, bd_ref[...])), 1e-12)
    ksum = jnp.zeros_like(k)

    tpos = lax.broadcasted_iota(jnp.int32, (tc, w), 0)
    spos = lax.broadcasted_iota(jnp.int32, (tc, w), 1) % tc
    eye = tpos == spos
    packed = lambda a: a.reshape(nc, tc, w)

    def level_mask(s):
        return jnp.logical_and(tpos // (2 * s) == spos // (2 * s), tpos // s != spos // s)

    n_lane_tiles = w // LANES
    lane_head = lax.broadcasted_iota(jnp.int32, (tc, LANES), 1) // tc

    def head_lanes(h, tile):
        return jnp.where(lane_head == h % (LANES // tc), tile, jnp.zeros_like(tile))

    def blockdiag(m):
        mb = m.astype(BF16)
        zeros = jnp.zeros(mb.shape[:2] + (LANES,), BF16)
        rows = []
        for h in range(nh):
            j = h * tc // LANES
            kept = head_lanes(h, mb[:, :, j * LANES:(j + 1) * LANES])
            rows.append(jnp.concatenate([kept if jj == j else zeros for jj in range(n_lane_tiles)], axis=2))
        return jnp.concatenate(rows, axis=1)

    def own_blocks(full):
        tiles = []
        for j in range(n_lane_tiles):
            heads = [h for h in range(nh) if h * tc // LANES == j]
            tiles.append(sum(head_lanes(h, full[:, h * tc:(h + 1) * tc, j * LANES:(j + 1) * LANES]) for h in heads))
        return jnp.concatenate(tiles, axis=2)

    per_dir = []
    for d in range(2):
        w_log = -_softplus(-(w0_ref[d:d + 1, :] + _mm_split(w_low, wup_ref[d]))) - 0.5
        lw = -jnp.exp(w_log)
        a = 1.0 / (1.0 + jnp.exp(-(a0_ref[d:d + 1, :] + _mm_split(a_low, aup_ref[d]))))
        kd = k * (1.0 + (a - 1.0) * ka_ref[...])
        ksum = ksum + kd
        sums = _mm_exact_lhs(cum_ref[d], lw)
        c, ctot = sums[:ROW_TILE], sums[ROW_TILE:]
        e_neg, e_rem = jnp.exp(-c), jnp.exp(ctot - c)
        b = kk * a
        before = tpos > spos if d == 0 else tpos < spos
        per_dir.append((packed(-kk * jnp.exp(c - lw)), packed(r * jnp.exp(c)), packed(b * e_neg),
                        packed(kd * e_neg), packed(b * e_rem), packed(kd * e_rem), packed(jnp.exp(ctot)),
                        jnp.broadcast_to(before, (nc, tc, w))))
    rvk_ref[0] = jnp.concatenate([r, v, ksum], axis=-1)

    at_p, rt_p, bh_p, kh_p, bc_p, kc_p, wt_p, before = (jnp.concatenate(parts, axis=0) for parts in zip(*per_dir))
    v_p = jnp.concatenate([packed(v)] * 2, axis=0)
    upto = jnp.logical_or(before, eye)
    big = lax.dot_general(jnp.concatenate([at_p, rt_p], axis=1).astype(BF16),
                          jnp.concatenate([blockdiag(bh_p), blockdiag(kh_p)], axis=1),
                          (((2,), (2,)), ((0,), (0,))), preferred_element_type=F32)
    a_ab = jnp.where(before, big[:, :tc, :w], 0.0)
    a_ak = jnp.where(before, big[:, :tc, w:], 0.0)
    a_rb = jnp.where(upto, big[:, tc:, :w], 0.0)
    a_rk = jnp.where(upto, big[:, tc:, w:], 0.0)
    x = jnp.where(eye, 1.0, jnp.where(level_mask(1), a_ab, 0.0))
    a_ab16 = a_ab.astype(BF16)
    s = 2
    while s < tc:
        half = _bmm(x, blockdiag(jnp.where(level_mask(s), a_ab16, jnp.zeros_like(a_ab16))))
        x = x + _bmm(half, blockdiag(x))
        s *= 2
    akrk = _bmm(jnp.concatenate([a_ak, a_rk], axis=1), blockdiag(v_p))
    xa = _bmm(x, jnp.concatenate([blockdiag(at_p), blockdiag(akrk[:, :tc])], axis=2))
    ra = _bmm(a_rb, jnp.concatenate([blockdiag(xa[:, :, :w]), blockdiag(xa[:, :, w:])], axis=2))
    rp = rt_p + ra[:, :, :w]
    yv = ra[:, :, w:] + akrk[:, tc:]
    lhs_t = jnp.concatenate([bc_p, kc_p], axis=1).astype(BF16)
    rhs_t = jnp.concatenate([xa, jnp.concatenate([jnp.zeros_like(v_p), v_p], axis=2)], axis=1).astype(BF16)
    full = lax.dot_general(lhs_t, rhs_t, (((1,), (1,)), ((0,), (0,))), preferred_element_type=F32)
    p = own_blocks(full[:, :, :w]) + jnp.where(eye, wt_p[:, 0:1, :], 0.0)
    q = own_blocks(full[:, :, w:])
    rp3_ref[0] = jnp.concatenate([rp, p], axis=1).astype(BF16).reshape(2, nc, tc + HEAD_DIM, w)
    yvq_ref[0] = jnp.concatenate([yv, q], axis=1).reshape(2, nc, tc + HEAD_DIM, w)


def _chunk_matrices():
    t = np.arange(ROW_TILE)
    same = (t[:, None] // RW_CHUNK) == (t[None, :] // RW_CHUNK)
    tabs = []
    for run in (same & (t[None, :] <= t[:, None]), same & (t[None, :] >= t[:, None])):
        m = np.concatenate([run, same], axis=0).astype(np.float32).astype(BF16)
        tabs.append(np.concatenate([m, m, m], axis=1))
    return np.stack(tabs)


def _rwkv_par(rw, w0, w_up, a0, a_up, k_k, k_a, w):
    bsz, s, width = rw.shape
    nt = s // ROW_TILE
    nh, nc = w // HEAD_DIM, ROW_TILE // RW_CHUNK
    lora = w_up.shape[1]
    full = lambda a: pl.BlockSpec(a.shape, lambda b, t: (0,) * a.ndim)
    cum = _chunk_matrices()
    bd = _head_block_diag(w, 1.0)
    vec = lambda a: a.reshape(1, w)
    n_chunks = s // RW_CHUNK
    tc, hd = RW_CHUNK, HEAD_DIM
    consts = (w0, w_up, a0, a_up, vec(k_k), vec(k_a), bd, cum)
    return pl.pallas_call(
        functools.partial(_rwkv_par_kernel, w=w, lora=lora),
        out_shape=[jax.ShapeDtypeStruct((bsz, 2, n_chunks, tc + hd, w), BF16),
                   jax.ShapeDtypeStruct((bsz, 2, n_chunks, tc + hd, w), F32),
                   jax.ShapeDtypeStruct((bsz, s, 3 * w), F32)],
        grid=(bsz, nt),
        in_specs=[pl.BlockSpec((1, ROW_TILE, width), lambda b, t: (b, t, 0))] + [full(a) for a in consts],
        out_specs=[pl.BlockSpec((1, 2, nc, tc + hd, w), lambda b, t: (b, 0, t, 0, 0)),
                   pl.BlockSpec((1, 2, nc, tc + hd, w), lambda b, t: (b, 0, t, 0, 0)),
                   pl.BlockSpec((1, ROW_TILE, 3 * w), lambda b, t: (b, t, 0))],
        compiler_params=_cparams("arbitrary", "arbitrary"),
        name="rwkv_chunk_prep",
    )(rw, *consts)


def _rwkv_seq_kernel(rp3f_ref, yvqf_ref, rp3b_ref, yvqb_ref, yf_ref, yb_ref, g_ref):
    @pl.when(pl.program_id(1) == 0)
    def _():
        g_ref[...] = jnp.zeros_like(g_ref)

    w = g_ref.shape[-1]
    tc = RW_CHUNK
    nh = w // HEAD_DIM
    grp = rp3f_ref.shape[2]
    row_head = lax.broadcasted_iota(jnp.int32, (w, w), 0) // HEAD_DIM
    col_head = lax.broadcasted_iota(jnp.int32, (w, w), 1) // HEAD_DIM
    on_diag = row_head == col_head
    for step in range(grp):
        for d, (rp3, yvq, y) in enumerate(((rp3f_ref, yvqf_ref, yf_ref), (rp3b_ref, yvqb_ref, yb_ref))):
            ci = step if d == 0 else grp - 1 - step
            out = jnp.dot(rp3[0, 0, ci], g_ref[d].astype(BF16), preferred_element_type=F32) + yvq[0, 0, ci]
            y[0, ci * tc:(ci + 1) * tc, :] = out[:tc]
            g_ref[d] = jnp.where(on_diag, jnp.concatenate([out[tc:]] * nh, axis=0), 0.0)


RW_SCAN_GROUP = 4


def _rwkv_seq(rp3, yvq, n_lat_chunks):
    bsz, _, n_chunks = rp3.shape[:3]
    w = yvq.shape[-1]
    grp = RW_SCAN_GROUP
    assert n_lat_chunks % grp == 0 and n_chunks % grp == 0
    n_groups, n_lat, n_ctx = n_chunks // grp, n_lat_chunks // grp, (n_chunks - n_lat_chunks) // grp
    order = (lambda i: jnp.where(i < n_ctx, n_lat + i, i - n_ctx),
             lambda i: n_groups - 1 - i)
    blk = lambda d, a: pl.BlockSpec((1, 1, grp) + a.shape[3:], lambda b, i: (b, d, order[d](i), 0, 0))
    out = lambda d: pl.BlockSpec((1, grp * RW_CHUNK, w), lambda b, i: (b, order[d](i), 0))
    shp = jax.ShapeDtypeStruct((bsz, n_chunks * RW_CHUNK, w), F32)
    return pl.pallas_call(
        _rwkv_seq_kernel,
        out_shape=[shp, shp],
        grid=(bsz, n_groups),
        in_specs=[blk(0, rp3), blk(0, yvq), blk(1, rp3), blk(1, yvq)],
        out_specs=[out(0), out(1)],
        scratch_shapes=[pltpu.VMEM((2, w, w), F32)],
        compiler_params=_cparams("arbitrary", "arbitrary"),
        name="rwkv_state_scan",
    )(rp3, yvq, rp3, yvq)


def _rwkv_readout_tile(y, rvk, r_k, ln_g, ln_b, head_mean, head_sum):
    w = r_k.shape[-1]
    r, v, ksum = rvk[:, :w], rvk[:, w:2 * w], rvk[:, 2 * w:]
    yc = y - _mm_exact_rhs(y, head_mean)
    var = _mm_exact_rhs(yc * yc, head_mean)
    bonus = _mm_exact_rhs(r * ksum * r_k, head_sum) * v
    return yc * lax.rsqrt(var + RW_GN_EPS) * ln_g + ln_b + bonus


@functools.lru_cache(maxsize=None)
def _filter_position_features(seq_len, bands, width):
    n = np.arange(2 * seq_len)
    pos = np.where(n < seq_len, n, 2 * seq_len - n) % seq_len
    t = np.linspace(0.0, 1.0, seq_len)[pos][:, None]
    wpos = (2.0 * math.pi / seq_len) * pos[:, None]
    f = np.linspace(1e-4, bands - 1, bands)[None, :]
    z = np.concatenate([t, np.cos(f * wpos), np.sin(f * wpos)], axis=-1)
    return np.pad(z, ((0, 0), (0, width - z.shape[1]))).astype(np.float32)


def _hyena_filter_kernel(z_ref, fw1_ref, fb1_ref, freq_ref, fw2_ref, fb2_ref, fw3_ref, delta_ref, o_ref, *, seq_len):
    freq = freq_ref[...]
    rows = ROW_TILE
    n_orders, _, c = o_ref.shape

    def taps(i, norms):
        blk = pl.ds(pl.multiple_of(i * rows, rows), rows)
        z = z_ref[blk, :]
        h = jnp.sin(freq * (_mm_split(z, fw1_ref[...]) + fb1_ref[...]))
        h = jnp.sin(freq * (_mm_split(h, fw2_ref[...]) + fb2_ref[...]))
        h = _mm_split(h, fw3_ref[...])
        n = i * rows + lax.broadcasted_iota(jnp.int32, (rows, 1), 0)
        decay = jnp.exp(-z[:, 0:1] * jnp.abs(delta_ref[...]))
        out = []
        for o in range(n_orders):
            ho = jnp.where(n < seq_len, h[:, 2 * o * c:(2 * o + 1) * c], h[:, (2 * o + 1) * c:(2 * o + 2) * c])
            ho = ho * decay
            o_ref[o, blk, :] = jnp.where(n != seq_len, ho, 0.0)
            out.append(norms[o] + jnp.sum(jnp.abs(ho), axis=0, keepdims=True))
        return tuple(out)
    zero = jnp.zeros((1, c), F32)
    norms = lax.fori_loop(0, z_ref.shape[0] // rows, taps, (zero,) * n_orders)

    def normalise(i, carry):
        blk = pl.ds(pl.multiple_of(i * rows, rows), rows)
        for o in range(n_orders):
            o_ref[o, blk, :] = o_ref[o, blk, :] / norms[o]
        return carry
    lax.fori_loop(0, z_ref.shape[0] // rows, normalise, 0)


def _hyena_two_sided_filters(seq_len, fw1, fb1, freq, fw2, fb2, fw3, width):
    ffn = fw1.shape[1]
    bands = (fw1.shape[0] - 1) // 2
    z = _filter_position_features(seq_len, bands, ffn)
    fw1p = jnp.pad(fw1, ((0, ffn - fw1.shape[0]), (0, 0)))
    max_decay = math.log(HY_DECAY_TARGET) / HY_FAST_DECAY
    min_decay = math.log(HY_DECAY_TARGET) / HY_SLOW_DECAY
    deltas = jnp.linspace(min_decay, max_decay, width, dtype=F32).reshape(1, width)
    row = lambda a: a.reshape(1, ffn)
    full = lambda shape: pl.BlockSpec(shape, lambda i: (0,) * len(shape))
    return pl.pallas_call(
        functools.partial(_hyena_filter_kernel, seq_len=seq_len),
        out_shape=jax.ShapeDtypeStruct((2, 2 * seq_len, width), F32),
        grid=(1,),
        in_specs=[full(z.shape), full((ffn, ffn)), full((1, ffn)), full((1, ffn)), full((ffn, ffn)),
                  full((1, ffn)), full(fw3.shape), full((1, width))],
        out_specs=full((2, 2 * seq_len, width)),
        compiler_params=_cparams("arbitrary"),
        name="hyena_filter",
    )(z, fw1p, row(fb1), row(freq), fw2, row(fb2), fw3, deltas)


def kernel(x, c, ctx, c_ctx, mod_w, mod_b, norm_g, w_in, w_out, hy_conv, hy_fw1, hy_fb1, hy_freq, hy_fw2,
           hy_fb2, hy_fw3, hy_bias, rw_conv, rw_w0, rw_w_up, rw_a0, rw_a_up, rw_k_k, rw_k_a, rw_r_k,
           rw_ln_g, rw_ln_b, wa_sink, fa_q_norm, fa_k_norm, final_g):
    bsz, seq_len, d = x.shape
    ctx_len = ctx.shape[1]
    depth = w_in.shape[0]
    w_hy = hy_bias.shape[-1]
    w_rw = rw_w0.shape[-1]
    n_wa_heads = wa_sink.shape[-1]
    w_q = n_wa_heads * HEAD_DIM
    w_kv = w_q // 2
    lora = rw_w_up.shape[2] + rw_a_up.shape[2]
    branch_w = (3 * w_hy, 3 * w_rw + lora, w_q + 2 * w_kv, w_q + 2 * w_kv)
    gate_w = (w_hy, w_rw, w_q, w_q)
    assert seq_len % ROW_TILE == 0 and ctx_len % ROW_TILE == 0 and bsz % 2 == 0
    n_lat_tiles = seq_len // ROW_TILE

    splits = tuple(wd for pair in zip(branch_w, gate_w) for wd in pair)
    assert all(wd % LANES == 0 for wd in splits)
    w_in_b = w_in.astype(BF16)
    w_out_b = w_out.astype(BF16)

    pad_rows = (-(bsz + 1)) % SUBLANES
    cond = jnp.concatenate([c, c_ctx[None], jnp.zeros((pad_rows, d), F32)], axis=0)
    mod = _modulation(cond, mod_w, mod_b)
    mod_lat = mod[:, :bsz].reshape(depth, bsz, 3, d)
    mod_ctx = jnp.broadcast_to(mod[:, bsz].reshape(depth, 1, 3, d), (depth, bsz, 3, d))
    mods = jnp.stack([mod_lat, mod_ctx], axis=2)

    rope = _rope_tables(seq_len, ctx_len)
    xs = jnp.concatenate([x, ctx], axis=1)
    for l in range(depth):
        last = l == depth - 1
        hyc, rwc, gates, qw, kw, vw, qf, kf, vf = _inproj(
            xs, mods[l], norm_g[l], w_in_b[l], hy_conv[l], rw_conv[l], splits, n_lat_tiles, rope,
            fa_q_norm[l], fa_k_norm[l], w_q, w_kv)

        filt = functools.partial(_hyena_two_sided_filters, fw1=hy_fw1[l], fb1=hy_fb1[l], freq=hy_freq[l],
                                 fw2=hy_fw2[l], fb2=hy_fb2[l], fw3=hy_fw3[l], width=w_hy)
        bias = hy_bias[l].reshape(2, 1, w_hy)
        spec = _filter_spectrum(filt(seq_len))
        nt_hy = w_hy // LANES
        y1 = _fftconv_gated(hyc, 0, hyc, nt_hy, spec, 0, bias, seq_len)
        a_lat = _fftconv_gated(y1, 0, hyc, 2 * nt_hy, spec, 1, bias, seq_len)
        a_ctx = a_lat if last else _ctx_hyena(hyc, seq_len // ctx_len, filt(ctx_len), bias, ctx_len, w_hy)

        rp3, yvq, rvk = _rwkv_par(rwc, rw_w0[l], rw_w_up[l], rw_a0[l], rw_a_up[l], rw_k_k[l], rw_k_a[l], w_rw)
        yf, yb = _rwkv_seq(rp3, yvq, seq_len // RW_CHUNK)

        c_mix = _window_attention(qw, kw, vw, wa_sink[l], seq_len, ctx_len)
        d_mix = _dense_attention(qf, kf, vf, seq_len)

        xs = _outproj(a_lat, a_ctx, yf, yb, rvk, rw_r_k[l], rw_ln_g[l], rw_ln_b[l], c_mix, d_mix, gates, xs,
                      mods[l], w_out_b[l], final_g, n_lat_tiles, last)
    return xs
```

```python
import functools
import math

import jax
import jax.numpy as jnp
import numpy as np
from jax import lax
from jax.experimental import pallas as pl
from jax.experimental.pallas import tpu as pltpu

HEAD_DIM = 64
GRID_W = 64
WINDOW = 128
NORM_EPS = 1e-6
RW_GN_EPS = 64e-5
NEG_INF = -1e30
ROPE_THETA = 10000.0
HY_FAST_DECAY = 0.3
HY_SLOW_DECAY = 1.5
HY_DECAY_TARGET = 1e-2

ROW_TILE = 256
LANES = 128
SUBLANES = 8
FFT_N1 = 64
FFT_UNROLL = 16
FFT_PLANE_PAD = 8
RW_CHUNK = 64
VMEM_LIMIT = 60 * 1024 * 1024

F32 = jnp.float32
BF16 = jnp.bfloat16
HI = lax.Precision.HIGHEST


def _dot(a, b):
    return jnp.dot(a, b, preferred_element_type=F32, precision=HI)


def _cparams(*sem):
    return pltpu.CompilerParams(dimension_semantics=sem, vmem_limit_bytes=VMEM_LIMIT)


def _const_spec(shape):
    return pl.BlockSpec(shape, lambda *_: (0,) * len(shape), pipeline_mode=pl.Buffered(1))


def _silu(x):
    return x * (1.0 / (1.0 + jnp.exp(-x)))


def _pad_groups(x, group):
    gap = jnp.zeros((FFT_PLANE_PAD, x.shape[1]), x.dtype)
    pieces = []
    for g in range(x.shape[0] // group):
        pieces += [x[g * group:(g + 1) * group], gap]
    return jnp.concatenate(pieces, axis=0)


def _unpad_groups(x, group):
    pitch = group + FFT_PLANE_PAD
    return jnp.concatenate([x[g * pitch:g * pitch + group] for g in range(x.shape[0] // pitch)], axis=0)


def _padded_rows(rows, group):
    return rows // group * (group + FFT_PLANE_PAD)


def _mod_kernel(c_ref, w_ref, b_ref, o_ref):
    o_ref[0] = _dot(_silu(c_ref[...]), w_ref[0]) + b_ref[0]


def _modulation(cond, mod_w, mod_b):
    depth, d, d3 = mod_w.shape
    rows = cond.shape[0]
    return pl.pallas_call(
        _mod_kernel,
        out_shape=jax.ShapeDtypeStruct((depth, rows, d3), F32),
        grid=(depth,),
        in_specs=[pl.BlockSpec((rows, d), lambda l: (0, 0)),
                  pl.BlockSpec((1, d, d3), lambda l: (l, 0, 0)),
                  pl.BlockSpec((1, 1, d3), lambda l: (l, 0, 0))],
        out_specs=pl.BlockSpec((1, rows, d3), lambda l: (l, 0, 0)),
        compiler_params=_cparams("arbitrary"),
        name="modulation",
    )(cond, mod_w, mod_b.reshape(depth, 1, d3))


def _blockreal(m):
    return np.block([[m.real, -m.imag], [m.imag, m.real]])


@functools.lru_cache(maxsize=None)
def _fft_tables(seq_len):
    n = 2 * seq_len
    n1, n2 = FFT_N1, n // FFT_N1
    h1 = n1 // 2
    j2 = np.arange(n2)[:, None, None]
    k1 = np.arange(n1)[None, :, None]
    t1 = np.exp(-2j * np.pi * (j2 * k1 / n + k1 * np.arange(n1)[None, None, :] / n1))
    t1_data = np.stack([_blockreal(t1[j][:, :h1]) for j in range(n2)])
    t1_real = np.concatenate([t1.real, t1.imag], axis=1)
    f2 = np.exp(-2j * np.pi * np.outer(np.arange(n2), np.arange(n2)) / n2)
    f2_fwd = _blockreal(f2)
    f2_inv = _blockreal(np.conj(f2))
    t4 = np.exp(2j * np.pi * (np.arange(h1)[None, :, None] * np.arange(n1)[None, None, :] / n1
                              + j2 * np.arange(n1)[None, None, :] / n)) / n
    t4 = np.stack([_blockreal(t4[j]) for j in range(n2)])
    f2_parts = np.concatenate([f2.real, f2.imag], axis=0)
    return tuple(_lhs3_table(t) for t in (t1_data, t1_real, f2_fwd, f2_parts, t4))


def _lhs3_table(m):
    hi = m.astype(np.float32).astype(BF16)
    lo = (m - hi.astype(np.float64)).astype(np.float32).astype(BF16)
    return np.concatenate([hi, lo, hi], axis=-1)


def _mm3(tbl3, x):
    return jnp.dot(tbl3, _rhs3(x), preferred_element_type=F32)


def _spectrum_kernel(k_ref, t1_ref, f2_ref, o_ref, a_ref, *, n1, n2):
    def stage1(j, carry):
        rows = k_ref[0, pl.ds(j, n1, stride=n2), :]
        a_ref[pl.ds(j, 2 * n1, stride=n2 + FFT_PLANE_PAD), :] = _mm3(t1_ref[j], rows)
        return carry
    lax.fori_loop(0, n2, stage1, 0, unroll=FFT_UNROLL)

    def stage2(i, carry):
        pitch = n2 + FFT_PLANE_PAD
        re = a_ref[pl.ds(pl.multiple_of(i * pitch, 8), n2), :]
        im = a_ref[pl.ds(pl.multiple_of((n1 + i) * pitch, 8), n2), :]
        o_ref[0, i] = _mm3(f2_ref[...], jnp.concatenate([re, im], axis=0))
        return carry
    lax.fori_loop(0, n1, stage2, 0, unroll=8)


def _filter_spectrum(kfilt):
    g, n, w = kfilt.shape
    n1, n2 = FFT_N1, n // FFT_N1
    _, t1_real, f2_fwd, _, _ = _fft_tables(n // 2)
    const = _const_spec
    return pl.pallas_call(
        functools.partial(_spectrum_kernel, n1=n1, n2=n2),
        out_shape=jax.ShapeDtypeStruct((g, n1, 2 * n2, w), F32),
        grid=(g, w // LANES),
        in_specs=[pl.BlockSpec((1, n, LANES), lambda gi, j: (gi, 0, j)),
                  const(t1_real.shape), const(f2_fwd.shape)],
        out_specs=pl.BlockSpec((1, n1, 2 * n2, LANES), lambda gi, j: (gi, 0, 0, j)),
        scratch_shapes=[pltpu.VMEM((2 * n1 * (n2 + FFT_PLANE_PAD), LANES), F32)],
        compiler_params=_cparams("arbitrary", "arbitrary"),
        name="hyena_filter_spectrum",
    )(kfilt, t1_real, f2_fwd)


def _fftconv_kernel(u_ref, m_ref, spec_ref, bias_ref, t1_ref, f2p_ref, t4_ref, o_ref, ar_ref, ai_ref,
                    *, n1, n2):
    h1 = n1 // 2
    c = u_ref.shape[-1]
    pitch = n2 + FFT_PLANE_PAD

    def stage1(j, carry):
        za = u_ref[0, pl.ds(j, h1, stride=pitch), :]
        zb = u_ref[1, pl.ds(j, h1, stride=pitch), :]
        res = _mm3(t1_ref[j], jnp.concatenate([za, zb], axis=0))
        ar_ref[pl.ds(j, n1, stride=pitch), :] = res[:n1]
        ai_ref[pl.ds(j, n1, stride=pitch), :] = res[n1:]
        return carry
    lax.fori_loop(0, n2, stage1, 0, unroll=FFT_UNROLL)

    def stage2(i, carry):
        rows = pl.ds(pl.multiple_of(i * pitch, SUBLANES), n2)
        p = _mm3(f2p_ref[...], jnp.concatenate([ar_ref[rows, :], ai_ref[rows, :]], axis=-1))
        xr = p[:n2, :c] - p[n2:, c:]
        xi = p[:n2, c:] + p[n2:, :c]
        kr, ki = spec_ref[0, i, :n2, :], spec_ref[0, i, n2:, :]
        y = jnp.concatenate([xr * kr - xi * ki, xr * ki + xi * kr], axis=-1)
        q = _mm3(f2p_ref[...], y)
        ar_ref[rows, :] = q[:n2, :c] + q[n2:, c:]
        ai_ref[rows, :] = q[:n2, c:] - q[n2:, :c]
        return carry
    lax.fori_loop(0, n1, stage2, 0, unroll=16)

    bias = bias_ref[0]

    def stage4(j, carry):
        planes = pl.ds(j, n1, stride=pitch)
        y = _mm3(t4_ref[j], jnp.concatenate([ar_ref[planes, :], ai_ref[planes, :]], axis=0))
        rows = pl.ds(j, h1, stride=pitch)
        for p in range(2):
            u = u_ref[p, rows, :]
            o_ref[p, rows, :] = m_ref[p, rows, :] * (y[p * h1:(p + 1) * h1] + bias * u)
        return carry
    lax.fori_loop(0, n2, stage4, 0, unroll=FFT_UNROLL)
    for r in range(n2, pitch):
        for p in range(2):
            o_ref[p, pl.ds(r, h1, stride=pitch), :] = jnp.zeros((h1, c), F32)


def _fftconv_gated(u, u_col, mult, mult_col, spec, conv_idx, bias, seq_len):
    bsz = u.shape[0]
    w = spec.shape[-1]
    n = 2 * seq_len
    n1, n2 = FFT_N1, n // FFT_N1
    rows = (n1 // 2) * (n2 + FFT_PLANE_PAD)
    t1_data, _, _, f2_parts, t4 = _fft_tables(seq_len)
    const = _const_spec
    return pl.pallas_call(
        functools.partial(_fftconv_kernel, n1=n1, n2=n2),
        out_shape=jax.ShapeDtypeStruct((bsz, rows, w), F32),
        grid=(w // LANES, bsz // 2),
        in_specs=[pl.BlockSpec((2, rows, LANES), lambda j, p: (p, 0, u_col + j)),
                  pl.BlockSpec((2, rows, LANES), lambda j, p: (p, 0, mult_col + j)),
                  pl.BlockSpec((1, n1, 2 * n2, LANES), lambda j, p: (conv_idx, 0, 0, j),
                               pipeline_mode=pl.Buffered(1)),
                  pl.BlockSpec((1, 1, LANES), lambda j, p: (conv_idx, 0, j)),
                  const(t1_data.shape), const(f2_parts.shape), const(t4.shape)],
        out_specs=pl.BlockSpec((2, rows, LANES), lambda j, p: (p, 0, j)),
        scratch_shapes=[pltpu.VMEM((n1 * (n2 + FFT_PLANE_PAD), LANES), F32)] * 2,
        compiler_params=_cparams("arbitrary", "arbitrary"),
        name="hyena_fftconv",
    )(u, mult, spec, bias, t1_data, f2_parts, t4)


@functools.lru_cache(maxsize=None)
def _small_fft_tables(seq_len):
    n = 2 * seq_len
    f = np.exp(-2j * np.pi * np.outer(np.arange(n), np.arange(n)) / n)
    fwd = _blockreal(f[:, :seq_len])
    real = np.concatenate([f.real, f.imag], axis=0)
    inv = _blockreal(np.conj(f)[:seq_len, :] / n)
    return _lhs3_table(fwd), _lhs3_table(real), _lhs3_table(inv)


def _ctx_hyena_kernel(v_ref, x1_ref, x2_ref, k_ref, bias_ref, fwd_ref, real_ref, inv_ref, o_ref,
                      *, seq_len, group):
    n = 2 * seq_len
    tokens = lambda ref, p: _unpad_groups(ref[p], group)

    def conv(ua, ub, g):
        spec = _mm3(real_ref[...], k_ref[g])
        x = _mm3(fwd_ref[...], jnp.concatenate([ua, ub], axis=0))
        xr, xi, kr, ki = x[:n], x[n:], spec[:n], spec[n:]
        y = _mm3(inv_ref[...], jnp.concatenate([xr * kr - xi * ki, xr * ki + xi * kr], axis=0))
        b = bias_ref[g]
        return y[:seq_len] + b * ua, y[seq_len:] + b * ub

    c1a, c1b = conv(tokens(v_ref, 0), tokens(v_ref, 1), 0)
    y1a, y1b = tokens(x1_ref, 0) * c1a, tokens(x1_ref, 1) * c1b
    c2a, c2b = conv(y1a, y1b, 1)
    o_ref[0] = tokens(x2_ref, 0) * c2a
    o_ref[1] = tokens(x2_ref, 1) * c2b


def _ctx_hyena(hyc, row_block, kfilt, bias, seq_len, w, group):
    bsz = hyc.shape[0]
    n = 2 * seq_len
    fwd, real, inv = _small_fft_tables(seq_len)
    nt = w // LANES
    col = lambda c0: pl.BlockSpec((2, _padded_rows(seq_len, group), LANES), lambda j, p: (p, row_block, c0 + j))
    return pl.pallas_call(
        functools.partial(_ctx_hyena_kernel, seq_len=seq_len, group=group),
        out_shape=jax.ShapeDtypeStruct((bsz, seq_len, w), F32),
        grid=(nt, bsz // 2),
        in_specs=[col(0), col(nt), col(2 * nt),
                  pl.BlockSpec((2, n, LANES), lambda j, p: (0, 0, j)),
                  pl.BlockSpec((2, 1, LANES), lambda j, p: (0, 0, j)),
                  _const_spec(fwd.shape), _const_spec(real.shape), _const_spec(inv.shape)],
        out_specs=pl.BlockSpec((2, seq_len, LANES), lambda j, p: (p, 0, j)),
        compiler_params=_cparams("arbitrary", "arbitrary"),
        name="hyena_ctx",
    )(hyc, hyc, hyc, kfilt, bias, fwd, real, inv)


def _rms(x, g):
    return x * lax.rsqrt(jnp.mean(x * x, axis=-1, keepdims=True) + NORM_EPS) * g


def _inproj_kernel(x_ref, xprev_ref, xnext_ref, mod_ref, g_ref, w_ref, hycw_ref, rwcw_ref,
                   cos_ref, shi_ref, slo_ref, qg_ref, kg_ref, bdq_ref, bdk_ref,
                   hy_ref, rw_ref, gt_ref, qw_ref, kw_ref, vw_ref, qf_ref, kf_ref, vf_ref,
                   *, splits, wq, wk, n_lat_tiles, n_tiles, hy_group):
    shift, scale = mod_ref[0, 0, 0:1, :], mod_ref[0, 0, 1:2, :]
    norm_mod = lambda x: (_rms(x, g_ref[...]) * (1.0 + scale) + shift).astype(BF16)
    z = jnp.dot(norm_mod(x_ref[0]), w_ref[...], preferred_element_type=F32)
    offs = np.cumsum((0,) + tuple(splits))
    seg = lambda i: z[:, offs[i]:offs[i + 1]]
    halo_h = norm_mod(jnp.concatenate([xprev_ref[0], xnext_ref[0]], axis=0))
    for i, cw_ref, o_ref in ((0, hycw_ref, hy_ref), (2, rwcw_ref, rw_ref)):
        halo = jnp.dot(halo_h, w_ref[:, offs[i]:offs[i + 1]], preferred_element_type=F32)
        conv = _short_conv_tile(seg(i), halo[:SUBLANES], halo[SUBLANES:], cw_ref[...], pl.program_id(1),
                                n_lat_tiles, n_tiles)
        o_ref[0] = _pad_groups(conv, hy_group) if o_ref is hy_ref else conv
    gt_ref[0] = _silu(jnp.concatenate([seg(1), seg(3), seg(5), seg(7)], axis=-1))
    _qk_emit(seg(4), seg(6), cos_ref[...], shi_ref[...], slo_ref[...], qg_ref[...], kg_ref[...],
             bdq_ref[...], bdk_ref[...], qw_ref, kw_ref, vw_ref, qf_ref, kf_ref, vf_ref, wq, wk)


def _inproj(xs, mods, norm_g, w_in, hy_conv, rw_conv, splits, n_lat_tiles, rope, q_gain, k_gain, wq, wk,
            hy_group):
    widths = (splits[0], splits[2], None, None, sum(splits[1::2]))
    assert ROW_TILE % hy_group == 0
    hy_tile = _padded_rows(ROW_TILE, hy_group)
    bsz, s, d = xs.shape
    nt = s // ROW_TILE
    nq, nk = wq // HEAD_DIM, wk // HEAD_DIM
    prev, nxt = _halo_specs(d, nt)
    tile = lambda wd: pl.BlockSpec((1, ROW_TILE, wd), lambda b, t: (b, t, 0))
    tab = pl.BlockSpec((ROW_TILE, HEAD_DIM), lambda b, t: (t, 0))
    hm = lambda n, wd=HEAD_DIM: pl.BlockSpec((1, n, ROW_TILE, wd), lambda b, t: (b, 0, t, 0))
    flat = lambda wd: jax.ShapeDtypeStruct((bsz, s, wd), F32)
    heads = lambda n, wd=HEAD_DIM: jax.ShapeDtypeStruct((bsz, n, s, wd), BF16)
    return pl.pallas_call(
        functools.partial(_inproj_kernel, splits=splits, wq=wq, wk=wk, n_lat_tiles=n_lat_tiles, n_tiles=nt,
                          hy_group=hy_group),
        out_shape=[jax.ShapeDtypeStruct((bsz, nt * hy_tile, widths[0]), F32), flat(widths[1]), flat(widths[4]),
                   heads(nq), heads(nk), heads(nk, LANES), heads(nq), heads(nk), heads(nk, LANES)],
        grid=(bsz, nt),
        in_specs=[tile(d), prev, nxt,
                  pl.BlockSpec((1, 1, 3, d), lambda b, t: (b, t // n_lat_tiles, 0, 0)),
                  pl.BlockSpec((1, d), lambda b, t: (0, 0)),
                  _const_spec(w_in.shape), pl.BlockSpec(hy_conv.shape, lambda b, t: (0, 0)),
                  pl.BlockSpec(rw_conv.shape, lambda b, t: (0, 0)), tab, tab, tab,
                  pl.BlockSpec((1, wq), lambda b, t: (0, 0)), pl.BlockSpec((1, wk), lambda b, t: (0, 0)),
                  _const_spec((3 * wq, wq)), _const_spec((3 * wk, wk))],
        out_specs=[pl.BlockSpec((1, hy_tile, widths[0]), lambda b, t: (b, t, 0)), tile(widths[1]), tile(widths[4]),
                   hm(nq), hm(nk), hm(nk, LANES), hm(nq), hm(nk), hm(nk, LANES)],
        compiler_params=_cparams("arbitrary", "arbitrary"),
        name="in_projection",
    )(xs, xs, xs, mods, norm_g.reshape(1, d), w_in, hy_conv, rw_conv, *rope,
      jnp.tile(q_gain, nq).reshape(1, wq), jnp.tile(k_gain, nk).reshape(1, wk),
      _head_block_diag(wq, 1.0 / HEAD_DIM), _head_block_diag(wk, 1.0 / HEAD_DIM))


def _halo_specs(width, n_tiles):
    per = ROW_TILE // SUBLANES
    prev = pl.BlockSpec((1, SUBLANES, width), lambda b, t: (b, jnp.maximum(t * per - 1, 0), 0))
    nxt = pl.BlockSpec((1, SUBLANES, width), lambda b, t: (b, jnp.minimum((t + 1) * per, n_tiles * per - 1), 0))
    return prev, nxt


def _short_conv_tile(z, prev8, next8, w, t, n_lat_tiles, n_tiles):
    first = jnp.logical_or(t == 0, t == n_lat_tiles)
    last = jnp.logical_or(t == n_lat_tiles - 1, t == n_tiles - 1)
    above = jnp.where(first, 0.0, prev8[SUBLANES - 1:SUBLANES, :])
    below = jnp.where(last, 0.0, next8[0:1, :])
    row = lax.broadcasted_iota(jnp.int32, z.shape, 0)
    zm1 = jnp.where(row == 0, above, pltpu.roll(z, 1, 0))
    zp1 = jnp.where(row == z.shape[0] - 1, below, pltpu.roll(z, z.shape[0] - 1, 0))
    return zm1 * w[0:1, :] + z * w[1:2, :] + zp1 * w[2:3, :]


def _outproj_kernel(a_lat_ref, a_ctx_ref, yf_ref, yb_ref, rvk_ref, rk_ref, lng_ref, lnb_ref, hmean_ref, hsum_ref,
                    c_ref, d_ref, gt_ref, x_ref, mod_ref, w_ref, fg_ref, o_ref, *, n_lat_tiles, final, hy_group):
    is_ctx = pl.program_id(1) >= n_lat_tiles
    a = jnp.where(is_ctx, a_ctx_ref[0], _unpad_groups(a_lat_ref[0], hy_group))
    b = _rwkv_readout_tile(yf_ref[0] + yb_ref[0], rvk_ref[0], rk_ref[...], lng_ref[...], lnb_ref[...],
                           hmean_ref[...], hsum_ref[...])
    mix = jnp.concatenate([a, b, c_ref[0], d_ref[0]], axis=-1) * gt_ref[0]
    y = jnp.dot(mix.astype(BF16), w_ref[...], preferred_element_type=F32)
    x = x_ref[0] + mod_ref[0, 0, 2:3, :] * y
    o_ref[0] = _rms(x, fg_ref[...]) if final else x


def _outproj(a_lat, a_ctx, yf, yb, rvk, r_k, ln_g, ln_b, cmix, dmix, gates, xs, mods, w_out, final_g,
             n_lat_tiles, final, hy_group):
    bsz, s, d = xs.shape
    wb = a_lat.shape[-1]
    nt = n_lat_tiles if final else s // ROW_TILE
    tile = lambda wd: pl.BlockSpec((1, ROW_TILE, wd), lambda b, t: (b, t, 0))
    vec = pl.BlockSpec((1, wb), lambda b, t: (0, 0))
    head_tab = _const_spec((3 * wb, wb))
    return pl.pallas_call(
        functools.partial(_outproj_kernel, n_lat_tiles=n_lat_tiles, final=final, hy_group=hy_group),
        out_shape=jax.ShapeDtypeStruct((bsz, nt * ROW_TILE, d), F32),
        grid=(bsz, nt),
        in_specs=[pl.BlockSpec((1, _padded_rows(ROW_TILE, hy_group), wb),
                               lambda b, t: (b, jnp.minimum(t, n_lat_tiles - 1), 0)),
                  pl.BlockSpec((1, ROW_TILE, wb), lambda b, t: (b, 0, 0)),
                  tile(wb), tile(wb), tile(3 * wb), vec, vec, vec, head_tab, head_tab,
                  tile(wb), tile(wb), tile(4 * wb), tile(d),
                  pl.BlockSpec((1, 1, 3, d), lambda b, t: (b, t // n_lat_tiles, 0, 0)),
                  _const_spec(w_out.shape),
                  pl.BlockSpec((1, d), lambda b, t: (0, 0))],
        out_specs=tile(d),
        compiler_params=_cparams("arbitrary", "arbitrary"),
        name="out_projection",
    )(a_lat, a_ctx, yf, yb, rvk, r_k.reshape(1, wb), ln_g.reshape(1, wb), ln_b.reshape(1, wb),
      _head_block_diag(wb, 1.0 / HEAD_DIM), _head_block_diag(wb, 1.0), cmix, dmix, gates, xs, mods, w_out, final_g.reshape(1, d))


def _rope_tables(seq_len, ctx_len):
    rows = seq_len // GRID_W
    row = jnp.repeat(jnp.arange(rows, dtype=F32), GRID_W)
    col = jnp.tile(jnp.arange(GRID_W, dtype=F32), rows)
    n_freq = HEAD_DIM // 4
    inv_freq = ROPE_THETA ** (-jnp.arange(n_freq, dtype=F32) / n_freq)
    ar, ac = row[:, None] * inv_freq, col[:, None] * inv_freq
    zero = jnp.zeros_like(ar)
    cos = jnp.concatenate([jnp.cos(ar), jnp.cos(ar), jnp.cos(ac), jnp.cos(ac)], axis=-1)
    sin_hi = jnp.concatenate([-jnp.sin(ar), zero, -jnp.sin(ac), zero], axis=-1)
    sin_lo = jnp.concatenate([zero, jnp.sin(ar), zero, jnp.sin(ac)], axis=-1)
    pad = lambda t, v: jnp.concatenate([t, jnp.full((ctx_len, HEAD_DIM), v, F32)], axis=0)
    return pad(cos, 1.0), pad(sin_hi, 0.0), pad(sin_lo, 0.0)


def _rope(x, cos, sin_hi, sin_lo):
    q = HEAD_DIM // 4
    w = x.shape[-1]
    return x * cos + pltpu.roll(x, w - q, 1) * sin_hi + pltpu.roll(x, q, 1) * sin_lo


def _head_mean_sq(x, bd3):
    return _mm_exact_rhs(x * x, bd3)


def _qk_emit(wa, fa, cos, shi, slo, q_gain, k_gain, bdq, bdk, qw_ref, kw_ref, vw_ref, qf_ref, kf_ref, vf_ref,
             wq, wk):
    nq, nk = wq // HEAD_DIM, wk // HEAD_DIM
    tab = lambda t, n: jnp.concatenate([t] * n, axis=-1)
    cq, hq, lq = tab(cos, nq), tab(shi, nq), tab(slo, nq)
    ck, hk, lk = tab(cos, nk), tab(shi, nk), tab(slo, nk)
    scale = HEAD_DIM ** -0.5

    def emit(ref, val, n):
        for h in range(n):
            ref[0, h] = val[:, h * HEAD_DIM:(h + 1) * HEAD_DIM].astype(ref.dtype)

    def emit_values(ref, v):
        lane = lax.broadcasted_iota(jnp.int32, (v.shape[0], LANES - HEAD_DIM), 1)
        ones_pad = jnp.where(lane == 0, 1.0, 0.0)
        for h in range(nk):
            ref[0, h] = jnp.concatenate([v[:, h * HEAD_DIM:(h + 1) * HEAD_DIM], ones_pad],
                                        axis=-1).astype(ref.dtype)

    emit(qw_ref, _rope(wa[:, :wq], cq, hq, lq) * scale, nq)
    emit(kw_ref, _rope(wa[:, wq:wq + wk], ck, hk, lk), nk)
    emit_values(vw_ref, wa[:, wq + wk:])
    q, k = fa[:, :wq], fa[:, wq:wq + wk]
    q = q * lax.rsqrt(_head_mean_sq(q, bdq) + NORM_EPS) * q_gain
    k = k * lax.rsqrt(_head_mean_sq(k, bdk) + NORM_EPS) * k_gain
    emit(qf_ref, _rope(q, cq, hq, lq) * scale, nq)
    emit(kf_ref, _rope(k, ck, hk, lk), nk)
    emit_values(vf_ref, fa[:, wq + wk:])


def _head_block_diag(width, value):
    h = np.arange(width) // HEAD_DIM
    bd = ((h[:, None] == h[None, :]) * value).astype(np.float32).astype(BF16)
    assert np.all(bd.astype(np.float32) == (h[:, None] == h[None, :]) * value)
    return np.concatenate([bd, bd, bd], axis=0)


def _window_attn_kernel(q_ref, k_ref, v_ref, sink_ref, o_ref, *, seq_len, ctx_len, tq):
    t = pl.program_id(2)
    n_lat = seq_len // tq
    g = q_ref.shape[1]
    sub = WINDOW_SUB_ROWS
    band = 2 * WINDOW + sub
    kc, vc = k_ref[0, 0, seq_len:seq_len + ctx_len, :], v_ref[0, 0, seq_len:seq_len + ctx_len, :]
    nt_dot = lambda a, b: lax.dot_general(a, b, (((1,), (1,)), ((), ())), preferred_element_type=F32)
    for j in range(tq // sub):
        q = q_ref[0, :, j * sub:(j + 1) * sub, :].reshape(g * sub, HEAD_DIM)
        first = t * tq + j * sub
        start = pl.multiple_of(jnp.clip(first - WINDOW, 0, seq_len - band), sub)
        kb, vb = k_ref[0, 0, pl.ds(start, band), :], v_ref[0, 0, pl.ds(start, band), :]
        s_ctx = nt_dot(q, kc)
        s_loc = nt_dot(q, kb)
        qpos = first + lax.broadcasted_iota(jnp.int32, (g, sub, band), 1).reshape(g * sub, band)
        kpos = start + lax.broadcasted_iota(jnp.int32, (g * sub, band), 1)
        valid = jnp.logical_and(jnp.abs(qpos - kpos) <= WINDOW, t < n_lat)
        s_loc = jnp.where(valid, s_loc, NEG_INF)
        sink = sink_ref[0, :, j * sub:(j + 1) * sub, :].reshape(g * sub, 1)
        m = jnp.maximum(jnp.maximum(jnp.max(s_ctx, axis=-1, keepdims=True),
                                    jnp.max(s_loc, axis=-1, keepdims=True)), sink)
        p_ctx, p_loc = jnp.exp(s_ctx - m), jnp.exp(s_loc - m)
        acc = (jnp.dot(p_ctx.astype(BF16), vc, preferred_element_type=F32)
               + jnp.dot(p_loc.astype(BF16), vb, preferred_element_type=F32))
        denom = acc[:, HEAD_DIM:HEAD_DIM + 1] + jnp.exp(sink - m)
        out = acc[:, :HEAD_DIM] / denom
        o_ref[0, j * sub:(j + 1) * sub, :] = jnp.concatenate(
            [out[h * sub:(h + 1) * sub] for h in range(g)], axis=-1)


def _window_attention(q, k, v, sink, seq_len, ctx_len):
    bsz, nq, s, _ = q.shape
    nkv = k.shape[1]
    g = nq // nkv
    tq = ROW_TILE
    sink_rows = jnp.broadcast_to(sink.astype(F32).reshape(nkv, g, 1, 1), (nkv, g, tq, 1))
    kv = lambda a: pl.BlockSpec((1, 1, s, a.shape[-1]), lambda b, h, t: (b, h, 0, 0))
    return pl.pallas_call(
        functools.partial(_window_attn_kernel, seq_len=seq_len, ctx_len=ctx_len, tq=tq),
        out_shape=jax.ShapeDtypeStruct((bsz, s, nq * HEAD_DIM), F32),
        grid=(bsz, nkv, s // tq),
        in_specs=[pl.BlockSpec((1, g, tq, HEAD_DIM), lambda b, h, t: (b, h, t, 0)), kv(k), kv(v),
                  pl.BlockSpec((1, g, tq, 1), lambda b, h, t: (h, 0, 0, 0))],
        out_specs=pl.BlockSpec((1, tq, g * HEAD_DIM), lambda b, h, t: (b, t, h)),
        compiler_params=_cparams("arbitrary", "arbitrary", "arbitrary"),
        name="window_attention",
    )(q, k, v, sink_rows)


WINDOW_SUB_ROWS = 128
DENSE_KEY_BLOCK = 2048


def _dense_attn_kernel(q_ref, k_ref, v_ref, o_ref, *, seq_len, tq, tk):
    t = pl.program_id(2)
    n_lat = seq_len // tq
    g = q_ref.shape[1]
    ctx_len = k_ref.shape[2] - seq_len
    q = q_ref[0].reshape(g * tq, HEAD_DIM)

    def step(carry, start, size):
        m, acc = carry
        s = lax.dot_general(q, k_ref[0, 0, start:start + size, :], (((1,), (1,)), ((), ())),
                            preferred_element_type=F32)
        m_new = jnp.maximum(m, jnp.max(s, axis=-1, keepdims=True))
        p = jnp.exp(s - m_new).astype(BF16)
        pv = jnp.dot(p, v_ref[0, 0, start:start + size, :], preferred_element_type=F32)
        return m_new, jnp.exp(m - m_new) * acc + pv

    def finish(carry):
        _, acc = carry
        out = acc[:, :HEAD_DIM] / acc[:, HEAD_DIM:HEAD_DIM + 1]
        o_ref[0] = jnp.concatenate([out[h * tq:(h + 1) * tq] for h in range(g)], axis=-1)

    init = (jnp.full((g * tq, 1), NEG_INF, F32), jnp.zeros((g * tq, v_ref.shape[-1]), F32))

    @pl.when(t < n_lat)
    def _():
        carry = step(init, seq_len, ctx_len)
        for j in range(seq_len // tk):
            carry = step(carry, j * tk, tk)
        finish(carry)

    @pl.when(t >= n_lat)
    def _():
        finish(step(init, seq_len, ctx_len))


def _dense_attention(q, k, v, seq_len):
    bsz, nq, s, _ = q.shape
    nkv = k.shape[1]
    g = nq // nkv
    tq = ROW_TILE
    tk = math.gcd(seq_len, DENSE_KEY_BLOCK)
    kv = lambda a: pl.BlockSpec((1, 1, s, a.shape[-1]), lambda b, h, t: (b, h, 0, 0))
    return pl.pallas_call(
        functools.partial(_dense_attn_kernel, seq_len=seq_len, tq=tq, tk=tk),
        out_shape=jax.ShapeDtypeStruct((bsz, s, nq * HEAD_DIM), F32),
        grid=(bsz, nkv, s // tq),
        in_specs=[pl.BlockSpec((1, g, tq, HEAD_DIM), lambda b, h, t: (b, h, t, 0)), kv(k), kv(v)],
        out_specs=pl.BlockSpec((1, tq, g * HEAD_DIM), lambda b, h, t: (b, t, h)),
        compiler_params=_cparams("arbitrary", "arbitrary", "arbitrary"),
        name="dense_attention",
    )(q, k, v)


def _softplus(x):
    return jnp.maximum(x, 0.0) + jnp.log(1.0 + jnp.exp(-jnp.abs(x)))


def _split(a):
    bits = lax.bitcast_convert_type(a, jnp.uint32) & jnp.uint32(0xFFFF0000)
    hi = lax.bitcast_convert_type(bits, F32)
    return hi, a - hi


def _rhs3(b):
    hi, lo = _split(b)
    return jnp.concatenate([hi, hi, lo], axis=-2).astype(BF16)


def _split3(a):
    hi, rest = _split(a)
    mid, lo = _split(rest)
    return hi, mid, lo


def _mm_exact_lhs(tbl3, x):
    return jnp.dot(tbl3, jnp.concatenate(_split3(x), axis=0).astype(BF16), preferred_element_type=F32)


def _mm_exact_rhs(x, tbl3):
    return jnp.dot(jnp.concatenate(_split3(x), axis=-1).astype(BF16), tbl3, preferred_element_type=F32)


def _mm_split(a, b):
    ah, al = _split(a)
    return jnp.dot(jnp.concatenate([ah, al, ah], axis=-1).astype(BF16), _rhs3(b), preferred_element_type=F32)


def _bmm(a, bf):
    return lax.dot_general(a.astype(BF16), bf, (((2,), (1,)), ((0,), (0,))), preferred_element_type=F32)


def _rwkv_par_kernel(z_ref, w0_ref, wup_ref, a0_ref, aup_ref, kk_ref, ka_ref, bd_ref, cum_ref,
                     rp3_ref, yvq_ref, rvk_ref, *, w, lora):
    tc = RW_CHUNK
    nc, nh = ROW_TILE // tc, w // HEAD_DIM
    assert tc == HEAD_DIM
    z = z_ref[0]
    r, k, v = z[:, :w], z[:, w:2 * w], z[:, 2 * w:3 * w]
    w_low = jnp.tanh(z[:, 3 * w:3 * w + lora])
    a_low = z[:, 3 * w + lora:]
    kk = k * kk_ref[...]
    kk = kk / jnp.maximum(jnp.sqrt(_mm_exact_rhs(kk * kk, bd_ref[...])), 1e-12)
    ksum = jnp.zeros_like(k)

    tpos = lax.broadcasted_iota(jnp.int32, (tc, w), 0)
    spos = lax.broadcasted_iota(jnp.int32, (tc, w), 1) % tc
    eye = tpos == spos
    packed = lambda a: a.reshape(nc, tc, w)

    def level_mask(s):
        return jnp.logical_and(tpos // (2 * s) == spos // (2 * s), tpos // s != spos // s)

    n_lane_tiles = w // LANES
    lane_head = lax.broadcasted_iota(jnp.int32, (tc, LANES), 1) // tc

    def head_lanes(h, tile):
        return jnp.where(lane_head == h % (LANES // tc), tile, jnp.zeros_like(tile))

    def blockdiag(m):
        mb = m.astype(BF16)
        zeros = jnp.zeros(mb.shape[:2] + (LANES,), BF16)
        rows = []
        for h in range(nh):
            j = h * tc // LANES
            kept = head_lanes(h, mb[:, :, j * LANES:(j + 1) * LANES])
            rows.append(jnp.concatenate([kept if jj == j else zeros for jj in range(n_lane_tiles)], axis=2))
        return jnp.concatenate(rows, axis=1)

    def own_blocks(full):
        tiles = []
        for j in range(n_lane_tiles):
            heads = [h for h in range(nh) if h * tc // LANES == j]
            tiles.append(sum(head_lanes(h, full[:, h * tc:(h + 1) * tc, j * LANES:(j + 1) * LANES]) for h in heads))
        return jnp.concatenate(tiles, axis=2)

    per_dir = []
    for d in range(2):
        w_log = -_softplus(-(w0_ref[d:d + 1, :] + _mm_split(w_low, wup_ref[d]))) - 0.5
        lw = -jnp.exp(w_log)
        a = 1.0 / (1.0 + jnp.exp(-(a0_ref[d:d + 1, :] + _mm_split(a_low, aup_ref[d]))))
        kd = k * (1.0 + (a - 1.0) * ka_ref[...])
        ksum = ksum + kd
        sums = _mm_exact_lhs(cum_ref[d], lw)
        c, ctot = sums[:ROW_TILE], sums[ROW_TILE:]
        e_neg, e_rem = jnp.exp(-c), jnp.exp(ctot - c)
        b = kk * a
        before = tpos > spos if d == 0 else tpos < spos
        per_dir.append((packed(-kk * jnp.exp(c - lw)), packed(r * jnp.exp(c)), packed(b * e_neg),
                        packed(kd * e_neg), packed(b * e_rem), packed(kd * e_rem), packed(jnp.exp(ctot)),
                        jnp.broadcast_to(before, (nc, tc, w))))
    rvk_ref[0] = jnp.concatenate([r, v, ksum], axis=-1)

    at_p, rt_p, bh_p, kh_p, bc_p, kc_p, wt_p, before = (jnp.concatenate(parts, axis=0) for parts in zip(*per_dir))
    v_p = jnp.concatenate([packed(v)] * 2, axis=0)
    upto = jnp.logical_or(before, eye)
    big = lax.dot_general(jnp.concatenate([at_p, rt_p], axis=1).astype(BF16),
                          jnp.concatenate([blockdiag(bh_p), blockdiag(kh_p)], axis=1),
                          (((2,), (2,)), ((0,), (0,))), preferred_element_type=F32)
    a_ab = jnp.where(before, big[:, :tc, :w], 0.0)
    a_ak = jnp.where(before, big[:, :tc, w:], 0.0)
    a_rb = jnp.where(upto, big[:, tc:, :w], 0.0)
    a_rk = jnp.where(upto, big[:, tc:, w:], 0.0)
    x = jnp.where(eye, 1.0, jnp.where(level_mask(1), a_ab, 0.0))
    a_ab16 = a_ab.astype(BF16)
    s = 2
    while s < tc:
        half = _bmm(x, blockdiag(jnp.where(level_mask(s), a_ab16, jnp.zeros_like(a_ab16))))
        x = x + _bmm(half, blockdiag(x))
        s *= 2
    akrk = _bmm(jnp.concatenate([a_ak, a_rk], axis=1), blockdiag(v_p))
    xa = _bmm(x, jnp.concatenate([blockdiag(at_p), blockdiag(akrk[:, :tc])], axis=2))
    ra = _bmm(a_rb, jnp.concatenate([blockdiag(xa[:, :, :w]), blockdiag(xa[:, :, w:])], axis=2))
    rp = rt_p + ra[:, :, :w]
    yv = ra[:, :, w:] + akrk[:, tc:]
    lhs_t = jnp.concatenate([bc_p, kc_p], axis=1).astype(BF16)
    rhs_t = jnp.concatenate([xa, jnp.concatenate([jnp.zeros_like(v_p), v_p], axis=2)], axis=1).astype(BF16)
    full = lax.dot_general(lhs_t, rhs_t, (((1,), (1,)), ((0,), (0,))), preferred_element_type=F32)
    p = own_blocks(full[:, :, :w]) + jnp.where(eye, wt_p[:, 0:1, :], 0.0)
    q = own_blocks(full[:, :, w:])
    rp3_ref[0] = jnp.concatenate([rp, p], axis=1).astype(BF16).reshape(2, nc, tc + HEAD_DIM, w)
    yvq_ref[0] = jnp.concatenate([yv, q], axis=1).reshape(2, nc, tc + HEAD_DIM, w)


def _chunk_matrices():
    t = np.arange(ROW_TILE)
    same = (t[:, None] // RW_CHUNK) == (t[None, :] // RW_CHUNK)
    tabs = []
    for run in (same & (t[None, :] <= t[:, None]), same & (t[None, :] >= t[:, None])):
        m = np.concatenate([run, same], axis=0).astype(np.float32).astype(BF16)
        tabs.append(np.concatenate([m, m, m], axis=1))
    return np.stack(tabs)


def _rwkv_par(rw, w0, w_up, a0, a_up, k_k, k_a, w):
    bsz, s, width = rw.shape
    nt = s // ROW_TILE
    nh, nc = w // HEAD_DIM, ROW_TILE // RW_CHUNK
    lora = w_up.shape[1]
    full = lambda a: pl.BlockSpec(a.shape, lambda b, t: (0,) * a.ndim)
    cum = _chunk_matrices()
    bd = _head_block_diag(w, 1.0)
    vec = lambda a: a.reshape(1, w)
    n_chunks = s // RW_CHUNK
    tc, hd = RW_CHUNK, HEAD_DIM
    consts = (w0, w_up, a0, a_up, vec(k_k), vec(k_a), bd, cum)
    return pl.pallas_call(
        functools.partial(_rwkv_par_kernel, w=w, lora=lora),
        out_shape=[jax.ShapeDtypeStruct((bsz, 2, n_chunks, tc + hd, w), BF16),
                   jax.ShapeDtypeStruct((bsz, 2, n_chunks, tc + hd, w), F32),
                   jax.ShapeDtypeStruct((bsz, s, 3 * w), F32)],
        grid=(bsz, nt),
        in_specs=[pl.BlockSpec((1, ROW_TILE, width), lambda b, t: (b, t, 0))] + [full(a) for a in consts],
        out_specs=[pl.BlockSpec((1, 2, nc, tc + hd, w), lambda b, t: (b, 0, t, 0, 0)),
                   pl.BlockSpec((1, 2, nc, tc + hd, w), lambda b, t: (b, 0, t, 0, 0)),
                   pl.BlockSpec((1, ROW_TILE, 3 * w), lambda b, t: (b, t, 0))],
        compiler_params=_cparams("arbitrary", "arbitrary"),
        name="rwkv_chunk_prep",
    )(rw, *consts)


def _rwkv_seq_kernel(rp3f_ref, yvqf_ref, rp3b_ref, yvqb_ref, yf_ref, yb_ref, g_ref):
    @pl.when(pl.program_id(1) == 0)
    def _():
        g_ref[...] = jnp.zeros_like(g_ref)

    w = g_ref.shape[-1]
    tc = RW_CHUNK
    nh = w // HEAD_DIM
    grp = rp3f_ref.shape[2]
    row_head = lax.broadcasted_iota(jnp.int32, (w, w), 0) // HEAD_DIM
    col_head = lax.broadcasted_iota(jnp.int32, (w, w), 1) // HEAD_DIM
    on_diag = row_head == col_head
    for step in range(grp):
        for d, (rp3, yvq, y) in enumerate(((rp3f_ref, yvqf_ref, yf_ref), (rp3b_ref, yvqb_ref, yb_ref))):
            ci = step if d == 0 else grp - 1 - step
            out = jnp.dot(rp3[0, 0, ci], g_ref[d].astype(BF16), preferred_element_type=F32) + yvq[0, 0, ci]
            y[0, ci * tc:(ci + 1) * tc, :] = out[:tc]
            g_ref[d] = jnp.where(on_diag, jnp.concatenate([out[tc:]] * nh, axis=0), 0.0)


RW_SCAN_GROUP = 4


def _rwkv_seq(rp3, yvq, n_lat_chunks):
    bsz, _, n_chunks = rp3.shape[:3]
    w = yvq.shape[-1]
    grp = RW_SCAN_GROUP
    assert n_lat_chunks % grp == 0 and n_chunks % grp == 0
    n_groups, n_lat, n_ctx = n_chunks // grp, n_lat_chunks // grp, (n_chunks - n_lat_chunks) // grp
    order = (lambda i: jnp.where(i < n_ctx, n_lat + i, i - n_ctx),
             lambda i: n_groups - 1 - i)
    blk = lambda d, a: pl.BlockSpec((1, 1, grp) + a.shape[3:], lambda b, i: (b, d, order[d](i), 0, 0))
    out = lambda d: pl.BlockSpec((1, grp * RW_CHUNK, w), lambda b, i: (b, order[d](i), 0))
    shp = jax.ShapeDtypeStruct((bsz, n_chunks * RW_CHUNK, w), F32)
    return pl.pallas_call(
        _rwkv_seq_kernel,
        out_shape=[shp, shp],
        grid=(bsz, n_groups),
        in_specs=[blk(0, rp3), blk(0, yvq), blk(1, rp3), blk(1, yvq)],
        out_specs=[out(0), out(1)],
        scratch_shapes=[pltpu.VMEM((2, w, w), F32)],
        compiler_params=_cparams("arbitrary", "arbitrary"),
        name="rwkv_state_scan",
    )(rp3, yvq, rp3, yvq)


def _rwkv_readout_tile(y, rvk, r_k, ln_g, ln_b, head_mean, head_sum):
    w = r_k.shape[-1]
    r, v, ksum = rvk[:, :w], rvk[:, w:2 * w], rvk[:, 2 * w:]
    yc = y - _mm_exact_rhs(y, head_mean)
    var = _mm_exact_rhs(yc * yc, head_mean)
    bonus = _mm_exact_rhs(r * ksum * r_k, head_sum) * v
    return yc * lax.rsqrt(var + RW_GN_EPS) * ln_g + ln_b + bonus


@functools.lru_cache(maxsize=None)
def _filter_position_features(seq_len, bands, width):
    n = np.arange(2 * seq_len)
    pos = np.where(n < seq_len, n, 2 * seq_len - n) % seq_len
    t = np.linspace(0.0, 1.0, seq_len)[pos][:, None]
    wpos = (2.0 * math.pi / seq_len) * pos[:, None]
    f = np.linspace(1e-4, bands - 1, bands)[None, :]
    z = np.concatenate([t, np.cos(f * wpos), np.sin(f * wpos)], axis=-1)
    return np.pad(z, ((0, 0), (0, width - z.shape[1]))).astype(np.float32)


def _hyena_filter_kernel(z_ref, fw1_ref, fb1_ref, freq_ref, fw2_ref, fb2_ref, fw3_ref, delta_ref, o_ref, *, seq_len):
    freq = freq_ref[...]
    rows = ROW_TILE
    n_orders, _, c = o_ref.shape

    def taps(i, norms):
        blk = pl.ds(pl.multiple_of(i * rows, rows), rows)
        z = z_ref[blk, :]
        h = jnp.sin(freq * (_mm_split(z, fw1_ref[...]) + fb1_ref[...]))
        h = jnp.sin(freq * (_mm_split(h, fw2_ref[...]) + fb2_ref[...]))
        h = _mm_split(h, fw3_ref[...])
        n = i * rows + lax.broadcasted_iota(jnp.int32, (rows, 1), 0)
        decay = jnp.exp(-z[:, 0:1] * jnp.abs(delta_ref[...]))
        out = []
        for o in range(n_orders):
            ho = jnp.where(n < seq_len, h[:, 2 * o * c:(2 * o + 1) * c], h[:, (2 * o + 1) * c:(2 * o + 2) * c])
            ho = ho * decay
            o_ref[o, blk, :] = jnp.where(n != seq_len, ho, 0.0)
            out.append(norms[o] + jnp.sum(jnp.abs(ho), axis=0, keepdims=True))
        return tuple(out)
    zero = jnp.zeros((1, c), F32)
    norms = lax.fori_loop(0, z_ref.shape[0] // rows, taps, (zero,) * n_orders)

    def normalise(i, carry):
        blk = pl.ds(pl.multiple_of(i * rows, rows), rows)
        for o in range(n_orders):
            o_ref[o, blk, :] = o_ref[o, blk, :] / norms[o]
        return carry
    lax.fori_loop(0, z_ref.shape[0] // rows, normalise, 0)


def _hyena_two_sided_filters(seq_len, fw1, fb1, freq, fw2, fb2, fw3, width):
    ffn = fw1.shape[1]
    bands = (fw1.shape[0] - 1) // 2
    z = _filter_position_features(seq_len, bands, ffn)
    fw1p = jnp.pad(fw1, ((0, ffn - fw1.shape[0]), (0, 0)))
    max_decay = math.log(HY_DECAY_TARGET) / HY_FAST_DECAY
    min_decay = math.log(HY_DECAY_TARGET) / HY_SLOW_DECAY
    deltas = jnp.linspace(min_decay, max_decay, width, dtype=F32).reshape(1, width)
    row = lambda a: a.reshape(1, ffn)
    full = lambda shape: pl.BlockSpec(shape, lambda i: (0,) * len(shape))
    return pl.pallas_call(
        functools.partial(_hyena_filter_kernel, seq_len=seq_len),
        out_shape=jax.ShapeDtypeStruct((2, 2 * seq_len, width), F32),
        grid=(1,),
        in_specs=[full(z.shape), full((ffn, ffn)), full((1, ffn)), full((1, ffn)), full((ffn, ffn)),
                  full((1, ffn)), full(fw3.shape), full((1, width))],
        out_specs=full((2, 2 * seq_len, width)),
        compiler_params=_cparams("arbitrary"),
        name="hyena_filter",
    )(z, fw1p, row(fb1), row(freq), fw2, row(fb2), fw3, deltas)


def kernel(x, c, ctx, c_ctx, mod_w, mod_b, norm_g, w_in, w_out, hy_conv, hy_fw1, hy_fb1, hy_freq, hy_fw2,
           hy_fb2, hy_fw3, hy_bias, rw_conv, rw_w0, rw_w_up, rw_a0, rw_a_up, rw_k_k, rw_k_a, rw_r_k,
           rw_ln_g, rw_ln_b, wa_sink, fa_q_norm, fa_k_norm, final_g):
    bsz, seq_len, d = x.shape
    ctx_len = ctx.shape[1]
    depth = w_in.shape[0]
    w_hy = hy_bias.shape[-1]
    w_rw = rw_w0.shape[-1]
    n_wa_heads = wa_sink.shape[-1]
    w_q = n_wa_heads * HEAD_DIM
    w_kv = w_q // 2
    lora = rw_w_up.shape[2] + rw_a_up.shape[2]
    branch_w = (3 * w_hy, 3 * w_rw + lora, w_q + 2 * w_kv, w_q + 2 * w_kv)
    gate_w = (w_hy, w_rw, w_q, w_q)
    assert seq_len % ROW_TILE == 0 and ctx_len % ROW_TILE == 0 and bsz % 2 == 0
    n_lat_tiles = seq_len // ROW_TILE

    splits = tuple(wd for pair in zip(branch_w, gate_w) for wd in pair)
    assert all(wd % LANES == 0 for wd in splits)
    w_in_b = w_in.astype(BF16)
    w_out_b = w_out.astype(BF16)

    pad_rows = (-(bsz + 1)) % SUBLANES
    cond = jnp.concatenate([c, c_ctx[None], jnp.zeros((pad_rows, d), F32)], axis=0)
    mod = _modulation(cond, mod_w, mod_b)
    mod_lat = mod[:, :bsz].reshape(depth, bsz, 3, d)
    mod_ctx = jnp.broadcast_to(mod[:, bsz].reshape(depth, 1, 3, d), (depth, bsz, 3, d))
    mods = jnp.stack([mod_lat, mod_ctx], axis=2)

    rope = _rope_tables(seq_len, ctx_len)
    hy_group = 2 * seq_len // FFT_N1
    assert ctx_len % hy_group == 0 and ROW_TILE % hy_group == 0
    xs = jnp.concatenate([x, ctx], axis=1)
    for l in range(depth):
        last = l == depth - 1
        hyc, rwc, gates, qw, kw, vw, qf, kf, vf = _inproj(
            xs, mods[l], norm_g[l], w_in_b[l], hy_conv[l], rw_conv[l], splits, n_lat_tiles, rope,
            fa_q_norm[l], fa_k_norm[l], w_q, w_kv, hy_group)

        filt = functools.partial(_hyena_two_sided_filters, fw1=hy_fw1[l], fb1=hy_fb1[l], freq=hy_freq[l],
                                 fw2=hy_fw2[l], fb2=hy_fb2[l], fw3=hy_fw3[l], width=w_hy)
        bias = hy_bias[l].reshape(2, 1, w_hy)
        spec = _filter_spectrum(filt(seq_len))
        nt_hy = w_hy // LANES
        y1 = _fftconv_gated(hyc, 0, hyc, nt_hy, spec, 0, bias, seq_len)
        a_lat = _fftconv_gated(y1, 0, hyc, 2 * nt_hy, spec, 1, bias, seq_len)
        a_ctx = (a_lat[:, :ctx_len] if last else
                 _ctx_hyena(hyc, seq_len // ctx_len, filt(ctx_len), bias, ctx_len, w_hy, hy_group))

        rp3, yvq, rvk = _rwkv_par(rwc, rw_w0[l], rw_w_up[l], rw_a0[l], rw_a_up[l], rw_k_k[l], rw_k_a[l], w_rw)
        yf, yb = _rwkv_seq(rp3, yvq, seq_len // RW_CHUNK)

        c_mix = _window_attention(qw, kw, vw, wa_sink[l], seq_len, ctx_len)
        d_mix = _dense_attention(qf, kf, vf, seq_len)

        xs = _outproj(a_lat, a_ctx, yf, yb, rvk, rw_r_k[l], rw_ln_g[l], rw_ln_b[l], c_mix, d_mix, gates, xs,
                      mods[l], w_out_b[l], final_g, n_lat_tiles, last, hy_group)
    return xs
```

```python
import functools
import math

import jax
import jax.numpy as jnp
import numpy as np
from jax import lax
from jax.experimental import pallas as pl
from jax.experimental.pallas import tpu as pltpu

HEAD_DIM = 64
GRID_W = 64
WINDOW = 128
NORM_EPS = 1e-6
RW_GN_EPS = 64e-5
NEG_INF = -1e30
ROPE_THETA = 10000.0
HY_FAST_DECAY = 0.3
HY_SLOW_DECAY = 1.5
HY_DECAY_TARGET = 1e-2

ROW_TILE = 256
LANES = 128
SUBLANES = 8
FFT_N1 = 64
FFT_UNROLL = 16
FFT_PLANE_PAD = 8
RW_CHUNK = 64
VMEM_LIMIT = 60 * 1024 * 1024

F32 = jnp.float32
BF16 = jnp.bfloat16
HI = lax.Precision.HIGHEST


def _dot(a, b):
    return jnp.dot(a, b, preferred_element_type=F32, precision=HI)


def _cparams(*sem):
    return pltpu.CompilerParams(dimension_semantics=sem, vmem_limit_bytes=VMEM_LIMIT)


def _const_spec(shape):
    return pl.BlockSpec(shape, lambda *_: (0,) * len(shape), pipeline_mode=pl.Buffered(1))


def _silu(x):
    return x * (1.0 / (1.0 + jnp.exp(-x)))


def _pad_groups(x, group):
    gap = jnp.zeros((FFT_PLANE_PAD, x.shape[1]), x.dtype)
    pieces = []
    for g in range(x.shape[0] // group):
        pieces += [x[g * group:(g + 1) * group], gap]
    return jnp.concatenate(pieces, axis=0)


def _unpad_groups(x, group):
    pitch = group + FFT_PLANE_PAD
    return jnp.concatenate([x[g * pitch:g * pitch + group] for g in range(x.shape[0] // pitch)], axis=0)


def _padded_rows(rows, group):
    return rows // group * (group + FFT_PLANE_PAD)


def _mod_kernel(c_ref, w_ref, b_ref, o_ref):
    o_ref[0] = _dot(_silu(c_ref[...]), w_ref[0]) + b_ref[0]


def _modulation(cond, mod_w, mod_b):
    depth, d, d3 = mod_w.shape
    rows = cond.shape[0]
    return pl.pallas_call(
        _mod_kernel,
        out_shape=jax.ShapeDtypeStruct((depth, rows, d3), F32),
        grid=(depth,),
        in_specs=[pl.BlockSpec((rows, d), lambda l: (0, 0)),
                  pl.BlockSpec((1, d, d3), lambda l: (l, 0, 0)),
                  pl.BlockSpec((1, 1, d3), lambda l: (l, 0, 0))],
        out_specs=pl.BlockSpec((1, rows, d3), lambda l: (l, 0, 0)),
        compiler_params=_cparams("arbitrary"),
        name="modulation",
    )(cond, mod_w, mod_b.reshape(depth, 1, d3))


def _blockreal(m):
    return np.block([[m.real, -m.imag], [m.imag, m.real]])


@functools.lru_cache(maxsize=None)
def _fft_tables(seq_len):
    n = 2 * seq_len
    n1, n2 = FFT_N1, n // FFT_N1
    h1 = n1 // 2
    j2 = np.arange(n2)[:, None, None]
    k1 = np.arange(n1)[None, :, None]
    t1 = np.exp(-2j * np.pi * (j2 * k1 / n + k1 * np.arange(n1)[None, None, :] / n1))
    t1_data = np.stack([_blockreal(t1[j][:, :h1]) for j in range(n2)])
    t1_real = np.concatenate([t1.real, t1.imag], axis=1)
    f2 = np.exp(-2j * np.pi * np.outer(np.arange(n2), np.arange(n2)) / n2)
    f2_fwd = _blockreal(f2)
    f2_inv = _blockreal(np.conj(f2))
    t4 = np.exp(2j * np.pi * (np.arange(h1)[None, :, None] * np.arange(n1)[None, None, :] / n1
                              + j2 * np.arange(n1)[None, None, :] / n)) / n
    t4 = np.stack([_blockreal(t4[j]) for j in range(n2)])
    f2_parts = np.concatenate([f2.real, f2.imag], axis=0)
    return tuple(_lhs3_table(t) for t in (t1_data, t1_real, f2_fwd, f2_parts, t4))


def _lhs3_table(m):
    hi = m.astype(np.float32).astype(BF16)
    lo = (m - hi.astype(np.float64)).astype(np.float32).astype(BF16)
    return np.concatenate([hi, lo, hi], axis=-1)


def _mm3(tbl3, x):
    return jnp.dot(tbl3, _rhs3(x), preferred_element_type=F32)


def _spectrum_kernel(k_ref, t1_ref, f2_ref, o_ref, a_ref, *, n1, n2):
    def stage1(j, carry):
        rows = k_ref[0, pl.ds(j, n1, stride=n2), :]
        a_ref[pl.ds(j, 2 * n1, stride=n2 + FFT_PLANE_PAD), :] = _mm3(t1_ref[j], rows)
        return carry
    lax.fori_loop(0, n2, stage1, 0, unroll=FFT_UNROLL)

    def stage2(i, carry):
        pitch = n2 + FFT_PLANE_PAD
        re = a_ref[pl.ds(pl.multiple_of(i * pitch, 8), n2), :]
        im = a_ref[pl.ds(pl.multiple_of((n1 + i) * pitch, 8), n2), :]
        o_ref[0, i] = _mm3(f2_ref[...], jnp.concatenate([re, im], axis=0))
        return carry
    lax.fori_loop(0, n1, stage2, 0, unroll=8)


def _filter_spectrum(kfilt):
    g, n, w = kfilt.shape
    n1, n2 = FFT_N1, n // FFT_N1
    _, t1_real, f2_fwd, _, _ = _fft_tables(n // 2)
    const = _const_spec
    return pl.pallas_call(
        functools.partial(_spectrum_kernel, n1=n1, n2=n2),
        out_shape=jax.ShapeDtypeStruct((g, n1, 2 * n2, w), F32),
        grid=(g, w // LANES),
        in_specs=[pl.BlockSpec((1, n, LANES), lambda gi, j: (gi, 0, j)),
                  const(t1_real.shape), const(f2_fwd.shape)],
        out_specs=pl.BlockSpec((1, n1, 2 * n2, LANES), lambda gi, j: (gi, 0, 0, j)),
        scratch_shapes=[pltpu.VMEM((2 * n1 * (n2 + FFT_PLANE_PAD), LANES), F32)],
        compiler_params=_cparams("arbitrary", "arbitrary"),
        name="hyena_filter_spectrum",
    )(kfilt, t1_real, f2_fwd)


def _fftconv_kernel(u_ref, m_ref, spec_ref, bias_ref, t1_ref, f2p_ref, t4_ref, o_ref, ar_ref, ai_ref,
                    *, n1, n2):
    h1 = n1 // 2
    c = u_ref.shape[-1]
    pitch = n2 + FFT_PLANE_PAD

    def stage1(j, carry):
        za = u_ref[0, pl.ds(j, h1, stride=pitch), :]
        zb = u_ref[1, pl.ds(j, h1, stride=pitch), :]
        res = _mm3(t1_ref[j], jnp.concatenate([za, zb], axis=0))
        ar_ref[pl.ds(j, n1, stride=pitch), :] = res[:n1]
        ai_ref[pl.ds(j, n1, stride=pitch), :] = res[n1:]
        return carry
    lax.fori_loop(0, n2, stage1, 0, unroll=FFT_UNROLL)

    def stage2(i, carry):
        rows = pl.ds(pl.multiple_of(i * pitch, SUBLANES), n2)
        p = _mm3(f2p_ref[...], jnp.concatenate([ar_ref[rows, :], ai_ref[rows, :]], axis=-1))
        xr = p[:n2, :c] - p[n2:, c:]
        xi = p[:n2, c:] + p[n2:, :c]
        kr, ki = spec_ref[0, i, :n2, :], spec_ref[0, i, n2:, :]
        y = jnp.concatenate([xr * kr - xi * ki, xr * ki + xi * kr], axis=-1)
        q = _mm3(f2p_ref[...], y)
        ar_ref[rows, :] = q[:n2, :c] + q[n2:, c:]
        ai_ref[rows, :] = q[:n2, c:] - q[n2:, :c]
        return carry
    lax.fori_loop(0, n1, stage2, 0, unroll=16)

    bias = bias_ref[0]

    def stage4(j, carry):
        planes = pl.ds(j, n1, stride=pitch)
        y = _mm3(t4_ref[j], jnp.concatenate([ar_ref[planes, :], ai_ref[planes, :]], axis=0))
        rows = pl.ds(j, h1, stride=pitch)
        for p in range(2):
            u = u_ref[p, rows, :]
            o_ref[p, rows, :] = m_ref[p, rows, :] * (y[p * h1:(p + 1) * h1] + bias * u)
        return carry
    lax.fori_loop(0, n2, stage4, 0, unroll=FFT_UNROLL)
    for r in range(n2, pitch):
        for p in range(2):
            o_ref[p, pl.ds(r, h1, stride=pitch), :] = jnp.zeros((h1, c), F32)


def _fftconv_gated(u, u_col, mult, mult_col, spec, conv_idx, bias, seq_len):
    bsz = u.shape[0]
    w = spec.shape[-1]
    n = 2 * seq_len
    n1, n2 = FFT_N1, n // FFT_N1
    rows = (n1 // 2) * (n2 + FFT_PLANE_PAD)
    t1_data, _, _, f2_parts, t4 = _fft_tables(seq_len)
    const = _const_spec
    return pl.pallas_call(
        functools.partial(_fftconv_kernel, n1=n1, n2=n2),
        out_shape=jax.ShapeDtypeStruct((bsz, rows, w), F32),
        grid=(w // LANES, bsz // 2),
        in_specs=[pl.BlockSpec((2, rows, LANES), lambda j, p: (p, 0, u_col + j)),
                  pl.BlockSpec((2, rows, LANES), lambda j, p: (p, 0, mult_col + j)),
                  pl.BlockSpec((1, n1, 2 * n2, LANES), lambda j, p: (conv_idx, 0, 0, j),
                               pipeline_mode=pl.Buffered(1)),
                  pl.BlockSpec((1, 1, LANES), lambda j, p: (conv_idx, 0, j)),
                  const(t1_data.shape), const(f2_parts.shape), const(t4.shape)],
        out_specs=pl.BlockSpec((2, rows, LANES), lambda j, p: (p, 0, j)),
        scratch_shapes=[pltpu.VMEM((n1 * (n2 + FFT_PLANE_PAD), LANES), F32)] * 2,
        compiler_params=_cparams("arbitrary", "arbitrary"),
        name="hyena_fftconv",
    )(u, mult, spec, bias, t1_data, f2_parts, t4)


@functools.lru_cache(maxsize=None)
def _small_fft_tables(seq_len):
    n = 2 * seq_len
    f = np.exp(-2j * np.pi * np.outer(np.arange(n), np.arange(n)) / n)
    fwd = _blockreal(f[:, :seq_len])
    real = np.concatenate([f.real, f.imag], axis=0)
    inv = _blockreal(np.conj(f)[:seq_len, :] / n)
    return _lhs3_table(fwd), _lhs3_table(real), _lhs3_table(inv)


def _ctx_hyena_kernel(v_ref, x1_ref, x2_ref, k_ref, bias_ref, fwd_ref, real_ref, inv_ref, o_ref,
                      *, seq_len, group):
    n = 2 * seq_len
    tokens = lambda ref, p: _unpad_groups(ref[p], group)

    def conv(ua, ub, g):
        spec = _mm3(real_ref[...], k_ref[g])
        x = _mm3(fwd_ref[...], jnp.concatenate([ua, ub], axis=0))
        xr, xi, kr, ki = x[:n], x[n:], spec[:n], spec[n:]
        y = _mm3(inv_ref[...], jnp.concatenate([xr * kr - xi * ki, xr * ki + xi * kr], axis=0))
        b = bias_ref[g]
        return y[:seq_len] + b * ua, y[seq_len:] + b * ub

    c1a, c1b = conv(tokens(v_ref, 0), tokens(v_ref, 1), 0)
    y1a, y1b = tokens(x1_ref, 0) * c1a, tokens(x1_ref, 1) * c1b
    c2a, c2b = conv(y1a, y1b, 1)
    o_ref[0] = tokens(x2_ref, 0) * c2a
    o_ref[1] = tokens(x2_ref, 1) * c2b


def _ctx_hyena(hyc, row_block, kfilt, bias, seq_len, w, group):
    bsz = hyc.shape[0]
    n = 2 * seq_len
    fwd, real, inv = _small_fft_tables(seq_len)
    nt = w // LANES
    col = lambda c0: pl.BlockSpec((2, _padded_rows(seq_len, group), LANES), lambda j, p: (p, row_block, c0 + j))
    return pl.pallas_call(
        functools.partial(_ctx_hyena_kernel, seq_len=seq_len, group=group),
        out_shape=jax.ShapeDtypeStruct((bsz, seq_len, w), F32),
        grid=(nt, bsz // 2),
        in_specs=[col(0), col(nt), col(2 * nt),
                  pl.BlockSpec((2, n, LANES), lambda j, p: (0, 0, j)),
                  pl.BlockSpec((2, 1, LANES), lambda j, p: (0, 0, j)),
                  _const_spec(fwd.shape), _const_spec(real.shape), _const_spec(inv.shape)],
        out_specs=pl.BlockSpec((2, seq_len, LANES), lambda j, p: (p, 0, j)),
        compiler_params=_cparams("arbitrary", "arbitrary"),
        name="hyena_ctx",
    )(hyc, hyc, hyc, kfilt, bias, fwd, real, inv)


def _rms(x, g):
    return x * lax.rsqrt(jnp.mean(x * x, axis=-1, keepdims=True) + NORM_EPS) * g


def _inproj_kernel(x_ref, xprev_ref, xnext_ref, mod_ref, g_ref, w_ref, hycw_ref, rwcw_ref,
                   cos_ref, shi_ref, slo_ref, qg_ref, kg_ref, bdq_ref, bdk_ref,
                   hy_ref, rw_ref, gt_ref, qw_ref, kw_ref, vw_ref, qf_ref, kf_ref, vf_ref,
                   *, splits, wq, wk, n_lat_tiles, n_tiles, hy_group):
    shift, scale = mod_ref[0, 0, 0:1, :], mod_ref[0, 0, 1:2, :]
    norm_mod = lambda x: (_rms(x, g_ref[...]) * (1.0 + scale) + shift).astype(BF16)
    z = jnp.dot(norm_mod(x_ref[0]), w_ref[...], preferred_element_type=F32)
    offs = np.cumsum((0,) + tuple(splits))
    seg = lambda i: z[:, offs[i]:offs[i + 1]]
    halo_h = norm_mod(jnp.concatenate([xprev_ref[0], xnext_ref[0]], axis=0))
    for i, cw_ref, o_ref in ((0, hycw_ref, hy_ref), (2, rwcw_ref, rw_ref)):
        halo = jnp.dot(halo_h, w_ref[:, offs[i]:offs[i + 1]], preferred_element_type=F32)
        conv = _short_conv_tile(seg(i), halo[:SUBLANES], halo[SUBLANES:], cw_ref[...], pl.program_id(1),
                                n_lat_tiles, n_tiles)
        o_ref[0] = _pad_groups(conv, hy_group) if o_ref is hy_ref else conv
    gt_ref[0] = _silu(jnp.concatenate([seg(1), seg(3), seg(5), seg(7)], axis=-1))
    _qk_emit(seg(4), seg(6), cos_ref[...], shi_ref[...], slo_ref[...], qg_ref[...], kg_ref[...],
             bdq_ref[...], bdk_ref[...], qw_ref, kw_ref, vw_ref, qf_ref, kf_ref, vf_ref, wq, wk)


def _inproj(xs, mods, norm_g, w_in, hy_conv, rw_conv, splits, n_lat_tiles, rope, q_gain, k_gain, wq, wk,
            hy_group):
    widths = (splits[0], splits[2], None, None, sum(splits[1::2]))
    assert ROW_TILE % hy_group == 0
    hy_tile = _padded_rows(ROW_TILE, hy_group)
    bsz, s, d = xs.shape
    nt = s // ROW_TILE
    nq, nk = wq // HEAD_DIM, wk // HEAD_DIM
    prev, nxt = _halo_specs(d, nt)
    tile = lambda wd: pl.BlockSpec((1, ROW_TILE, wd), lambda b, t: (b, t, 0))
    tab = pl.BlockSpec((ROW_TILE, HEAD_DIM), lambda b, t: (t, 0))
    hm = lambda n, wd=HEAD_DIM: pl.BlockSpec((1, n, ROW_TILE, wd), lambda b, t: (b, 0, t, 0))
    flat = lambda wd: jax.ShapeDtypeStruct((bsz, s, wd), F32)
    heads = lambda n, wd=HEAD_DIM: jax.ShapeDtypeStruct((bsz, n, s, wd), BF16)
    return pl.pallas_call(
        functools.partial(_inproj_kernel, splits=splits, wq=wq, wk=wk, n_lat_tiles=n_lat_tiles, n_tiles=nt,
                          hy_group=hy_group),
        out_shape=[jax.ShapeDtypeStruct((bsz, nt * hy_tile, widths[0]), F32), flat(widths[1]), flat(widths[4]),
                   heads(nq), heads(nk), heads(nk, LANES), heads(nq), heads(nk), heads(nk, LANES)],
        grid=(bsz, nt),
        in_specs=[tile(d), prev, nxt,
                  pl.BlockSpec((1, 1, 3, d), lambda b, t: (b, t // n_lat_tiles, 0, 0)),
                  pl.BlockSpec((1, d), lambda b, t: (0, 0)),
                  _const_spec(w_in.shape), pl.BlockSpec(hy_conv.shape, lambda b, t: (0, 0)),
                  pl.BlockSpec(rw_conv.shape, lambda b, t: (0, 0)), tab, tab, tab,
                  pl.BlockSpec((1, wq), lambda b, t: (0, 0)), pl.BlockSpec((1, wk), lambda b, t: (0, 0)),
                  _const_spec((3 * wq, wq)), _const_spec((3 * wk, wk))],
        out_specs=[pl.BlockSpec((1, hy_tile, widths[0]), lambda b, t: (b, t, 0)), tile(widths[1]), tile(widths[4]),
                   hm(nq), hm(nk), hm(nk, LANES), hm(nq), hm(nk), hm(nk, LANES)],
        compiler_params=_cparams("arbitrary", "arbitrary"),
        name="in_projection",
    )(xs, xs, xs, mods, norm_g.reshape(1, d), w_in, hy_conv, rw_conv, *rope,
      jnp.tile(q_gain, nq).reshape(1, wq), jnp.tile(k_gain, nk).reshape(1, wk),
      _head_block_diag(wq, 1.0 / HEAD_DIM), _head_block_diag(wk, 1.0 / HEAD_DIM))


def _halo_specs(width, n_tiles):
    per = ROW_TILE // SUBLANES
    prev = pl.BlockSpec((1, SUBLANES, width), lambda b, t: (b, jnp.maximum(t * per - 1, 0), 0))
    nxt = pl.BlockSpec((1, SUBLANES, width), lambda b, t: (b, jnp.minimum((t + 1) * per, n_tiles * per - 1), 0))
    return prev, nxt


def _short_conv_tile(z, prev8, next8, w, t, n_lat_tiles, n_tiles):
    first = jnp.logical_or(t == 0, t == n_lat_tiles)
    last = jnp.logical_or(t == n_lat_tiles - 1, t == n_tiles - 1)
    above = jnp.where(first, 0.0, prev8[SUBLANES - 1:SUBLANES, :])
    below = jnp.where(last, 0.0, next8[0:1, :])
    row = lax.broadcasted_iota(jnp.int32, z.shape, 0)
    zm1 = jnp.where(row == 0, above, pltpu.roll(z, 1, 0))
    zp1 = jnp.where(row == z.shape[0] - 1, below, pltpu.roll(z, z.shape[0] - 1, 0))
    return zm1 * w[0:1, :] + z * w[1:2, :] + zp1 * w[2:3, :]


def _outproj_kernel(a_lat_ref, a_ctx_ref, yf_ref, yb_ref, rvk_ref, rk_ref, lng_ref, lnb_ref, hmean_ref, hsum_ref,
                    c_ref, d_ref, gt_ref, x_ref, mod_ref, w_ref, fg_ref, o_ref, *, n_lat_tiles, final, hy_group):
    is_ctx = pl.program_id(1) >= n_lat_tiles
    a = jnp.where(is_ctx, a_ctx_ref[0], _unpad_groups(a_lat_ref[0], hy_group))
    b = _rwkv_readout_tile(yf_ref[0] + yb_ref[0], rvk_ref[0], rk_ref[...], lng_ref[...], lnb_ref[...],
                           hmean_ref[...], hsum_ref[...])
    mix = jnp.concatenate([a, b, c_ref[0], d_ref[0]], axis=-1) * gt_ref[0]
    y = jnp.dot(mix.astype(BF16), w_ref[...], preferred_element_type=F32)
    x = x_ref[0] + mod_ref[0, 0, 2:3, :] * y
    o_ref[0] = _rms(x, fg_ref[...]) if final else x


def _outproj(a_lat, a_ctx, yf, yb, rvk, r_k, ln_g, ln_b, cmix, dmix, gates, xs, mods, w_out, final_g,
             n_lat_tiles, final, hy_group):
    bsz, s, d = xs.shape
    wb = a_lat.shape[-1]
    nt = n_lat_tiles if final else s // ROW_TILE
    tile = lambda wd: pl.BlockSpec((1, ROW_TILE, wd), lambda b, t: (b, t, 0))
    vec = pl.BlockSpec((1, wb), lambda b, t: (0, 0))
    head_tab = _const_spec((3 * wb, wb))
    return pl.pallas_call(
        functools.partial(_outproj_kernel, n_lat_tiles=n_lat_tiles, final=final, hy_group=hy_group),
        out_shape=jax.ShapeDtypeStruct((bsz, nt * ROW_TILE, d), F32),
        grid=(bsz, nt),
        in_specs=[pl.BlockSpec((1, _padded_rows(ROW_TILE, hy_group), wb),
                               lambda b, t: (b, jnp.minimum(t, n_lat_tiles - 1), 0)),
                  pl.BlockSpec((1, ROW_TILE, wb), lambda b, t: (b, 0, 0)),
                  tile(wb), tile(wb), tile(3 * wb), vec, vec, vec, head_tab, head_tab,
                  tile(wb), tile(wb), tile(4 * wb), tile(d),
                  pl.BlockSpec((1, 1, 3, d), lambda b, t: (b, t // n_lat_tiles, 0, 0)),
                  _const_spec(w_out.shape),
                  pl.BlockSpec((1, d), lambda b, t: (0, 0))],
        out_specs=tile(d),
        compiler_params=_cparams("arbitrary", "arbitrary"),
        name="out_projection",
    )(a_lat, a_ctx, yf, yb, rvk, r_k.reshape(1, wb), ln_g.reshape(1, wb), ln_b.reshape(1, wb),
      _head_block_diag(wb, 1.0 / HEAD_DIM), _head_block_diag(wb, 1.0), cmix, dmix, gates, xs, mods, w_out, final_g.reshape(1, d))


def _rope_tables(seq_len, ctx_len):
    rows = seq_len // GRID_W
    row = jnp.repeat(jnp.arange(rows, dtype=F32), GRID_W)
    col = jnp.tile(jnp.arange(GRID_W, dtype=F32), rows)
    n_freq = HEAD_DIM // 4
    inv_freq = ROPE_THETA ** (-jnp.arange(n_freq, dtype=F32) / n_freq)
    ar, ac = row[:, None] * inv_freq, col[:, None] * inv_freq
    zero = jnp.zeros_like(ar)
    cos = jnp.concatenate([jnp.cos(ar), jnp.cos(ar), jnp.cos(ac), jnp.cos(ac)], axis=-1)
    sin_hi = jnp.concatenate([-jnp.sin(ar), zero, -jnp.sin(ac), zero], axis=-1)
    sin_lo = jnp.concatenate([zero, jnp.sin(ar), zero, jnp.sin(ac)], axis=-1)
    pad = lambda t, v: jnp.concatenate([t, jnp.full((ctx_len, HEAD_DIM), v, F32)], axis=0)
    return pad(cos, 1.0), pad(sin_hi, 0.0), pad(sin_lo, 0.0)


def _rope(x, cos, sin_hi, sin_lo):
    q = HEAD_DIM // 4
    w = x.shape[-1]
    return x * cos + pltpu.roll(x, w - q, 1) * sin_hi + pltpu.roll(x, q, 1) * sin_lo


def _head_mean_sq(x, bd3):
    return _mm_exact_rhs(x * x, bd3)


def _qk_emit(wa, fa, cos, shi, slo, q_gain, k_gain, bdq, bdk, qw_ref, kw_ref, vw_ref, qf_ref, kf_ref, vf_ref,
             wq, wk):
    nq, nk = wq // HEAD_DIM, wk // HEAD_DIM
    tab = lambda t, n: jnp.concatenate([t] * n, axis=-1)
    cq, hq, lq = tab(cos, nq), tab(shi, nq), tab(slo, nq)
    ck, hk, lk = tab(cos, nk), tab(shi, nk), tab(slo, nk)
    scale = HEAD_DIM ** -0.5

    def emit(ref, val, n):
        for h in range(n):
            ref[0, h] = val[:, h * HEAD_DIM:(h + 1) * HEAD_DIM].astype(ref.dtype)

    def emit_values(ref, v):
        lane = lax.broadcasted_iota(jnp.int32, (v.shape[0], LANES - HEAD_DIM), 1)
        ones_pad = jnp.where(lane == 0, 1.0, 0.0)
        for h in range(nk):
            ref[0, h] = jnp.concatenate([v[:, h * HEAD_DIM:(h + 1) * HEAD_DIM], ones_pad],
                                        axis=-1).astype(ref.dtype)

    emit(qw_ref, _rope(wa[:, :wq], cq, hq, lq) * scale, nq)
    emit(kw_ref, _rope(wa[:, wq:wq + wk], ck, hk, lk), nk)
    emit_values(vw_ref, wa[:, wq + wk:])
    q, k = fa[:, :wq], fa[:, wq:wq + wk]
    q = q * lax.rsqrt(_head_mean_sq(q, bdq) + NORM_EPS) * q_gain
    k = k * lax.rsqrt(_head_mean_sq(k, bdk) + NORM_EPS) * k_gain
    emit(qf_ref, _rope(q, cq, hq, lq) * scale, nq)
    emit(kf_ref, _rope(k, ck, hk, lk), nk)
    emit_values(vf_ref, fa[:, wq + wk:])


def _head_block_diag(width, value):
    h = np.arange(width) // HEAD_DIM
    bd = ((h[:, None] == h[None, :]) * value).astype(np.float32).astype(BF16)
    assert np.all(bd.astype(np.float32) == (h[:, None] == h[None, :]) * value)
    return np.concatenate([bd, bd, bd], axis=0)


def _window_attn_kernel(q_ref, k_ref, v_ref, sink_ref, o_ref, *, seq_len, ctx_len, tq):
    t = pl.program_id(2)
    n_lat = seq_len // tq
    g = q_ref.shape[1]
    sub = WINDOW_SUB_ROWS
    band = 2 * WINDOW + sub
    kc, vc = k_ref[0, 0, seq_len:seq_len + ctx_len, :], v_ref[0, 0, seq_len:seq_len + ctx_len, :]
    nt_dot = lambda a, b: lax.dot_general(a, b, (((1,), (1,)), ((), ())), preferred_element_type=F32)
    scores = []
    for j in range(tq // sub):
        q = q_ref[0, :, j * sub:(j + 1) * sub, :].reshape(g * sub, HEAD_DIM)
        first = t * tq + j * sub
        start = pl.multiple_of(jnp.clip(first - WINDOW, 0, seq_len - band), sub)
        s_ctx = nt_dot(q, kc)
        s_loc = nt_dot(q, k_ref[0, 0, pl.ds(start, band), :])
        qpos = first + lax.broadcasted_iota(jnp.int32, (g, sub, band), 1).reshape(g * sub, band)
        kpos = start + lax.broadcasted_iota(jnp.int32, (g * sub, band), 1)
        valid = jnp.logical_and(jnp.abs(qpos - kpos) <= WINDOW, t < n_lat)
        scores.append((s_ctx, jnp.where(valid, s_loc, NEG_INF), start))
    probs = []
    for j, (s_ctx, s_loc, start) in enumerate(scores):
        sink = sink_ref[0, :, j * sub:(j + 1) * sub, :].reshape(g * sub, 1)
        m = jnp.maximum(jnp.maximum(jnp.max(s_ctx, axis=-1, keepdims=True),
                                    jnp.max(s_loc, axis=-1, keepdims=True)), sink)
        probs.append((jnp.exp(s_ctx - m).astype(BF16), jnp.exp(s_loc - m).astype(BF16), jnp.exp(sink - m), start))
    for j, (p_ctx, p_loc, p_sink, start) in enumerate(probs):
        acc = (jnp.dot(p_ctx, vc, preferred_element_type=F32)
               + jnp.dot(p_loc, v_ref[0, 0, pl.ds(start, band), :], preferred_element_type=F32))
        out = acc[:, :HEAD_DIM] / (acc[:, HEAD_DIM:HEAD_DIM + 1] + p_sink)
        o_ref[0, j * sub:(j + 1) * sub, :] = jnp.concatenate(
            [out[h * sub:(h + 1) * sub] for h in range(g)], axis=-1)


def _window_attention(q, k, v, sink, seq_len, ctx_len):
    bsz, nq, s, _ = q.shape
    nkv = k.shape[1]
    g = nq // nkv
    tq = ROW_TILE
    sink_rows = jnp.broadcast_to(sink.astype(F32).reshape(nkv, g, 1, 1), (nkv, g, tq, 1))
    kv = lambda a: pl.BlockSpec((1, 1, s, a.shape[-1]), lambda b, h, t: (b, h, 0, 0))
    return pl.pallas_call(
        functools.partial(_window_attn_kernel, seq_len=seq_len, ctx_len=ctx_len, tq=tq),
        out_shape=jax.ShapeDtypeStruct((bsz, s, nq * HEAD_DIM), F32),
        grid=(bsz, nkv, s // tq),
        in_specs=[pl.BlockSpec((1, g, tq, HEAD_DIM), lambda b, h, t: (b, h, t, 0)), kv(k), kv(v),
                  pl.BlockSpec((1, g, tq, 1), lambda b, h, t: (h, 0, 0, 0))],
        out_specs=pl.BlockSpec((1, tq, g * HEAD_DIM), lambda b, h, t: (b, t, h)),
        compiler_params=_cparams("arbitrary", "arbitrary", "arbitrary"),
        name="window_attention",
    )(q, k, v, sink_rows)


WINDOW_SUB_ROWS = 64
DENSE_KEY_BLOCK = 1024


def _dense_attn_kernel(q_ref, k_ref, v_ref, o_ref, *, seq_len, tq, tk):
    t = pl.program_id(2)
    n_lat = seq_len // tq
    g = q_ref.shape[1]
    ctx_len = k_ref.shape[2] - seq_len
    q = q_ref[0].reshape(g * tq, HEAD_DIM)

    def score(start, size):
        return lax.dot_general(q, k_ref[0, 0, start:start + size, :], (((1,), (1,)), ((), ())),
                               preferred_element_type=F32)

    def softmax(m, s):
        m_new = jnp.maximum(m, jnp.max(s, axis=-1, keepdims=True))
        return m_new, jnp.exp(m - m_new), jnp.exp(s - m_new).astype(BF16)

    def values(acc, alpha, p, start, size):
        return alpha * acc + jnp.dot(p, v_ref[0, 0, start:start + size, :], preferred_element_type=F32)

    def attend(blocks):
        m = jnp.full((g * tq, 1), NEG_INF, F32)
        acc = jnp.zeros((g * tq, v_ref.shape[-1]), F32)
        n = len(blocks)
        scores, probs = {}, {}
        for i in range(n + 2):
            if i < n:
                scores[i] = score(*blocks[i])
            if 0 <= i - 1 < n:
                m, alpha, p = softmax(m, scores.pop(i - 1))
                probs[i - 1] = (alpha, p)
            if 0 <= i - 2 < n:
                acc = values(acc, *probs.pop(i - 2), *blocks[i - 2])
        out = acc[:, :HEAD_DIM] / acc[:, HEAD_DIM:HEAD_DIM + 1]
        o_ref[0] = jnp.concatenate([out[h * tq:(h + 1) * tq] for h in range(g)], axis=-1)

    @pl.when(t < n_lat)
    def _():
        attend([(seq_len, ctx_len)] + [(j * tk, tk) for j in range(seq_len // tk)])

    @pl.when(t >= n_lat)
    def _():
        attend([(seq_len, ctx_len)])


def _dense_attention(q, k, v, seq_len):
    bsz, nq, s, _ = q.shape
    nkv = k.shape[1]
    g = nq // nkv
    tq = ROW_TILE
    tk = math.gcd(seq_len, DENSE_KEY_BLOCK)
    kv = lambda a: pl.BlockSpec((1, 1, s, a.shape[-1]), lambda b, h, t: (b, h, 0, 0))
    return pl.pallas_call(
        functools.partial(_dense_attn_kernel, seq_len=seq_len, tq=tq, tk=tk),
        out_shape=jax.ShapeDtypeStruct((bsz, s, nq * HEAD_DIM), F32),
        grid=(bsz, nkv, s // tq),
        in_specs=[pl.BlockSpec((1, g, tq, HEAD_DIM), lambda b, h, t: (b, h, t, 0)), kv(k), kv(v)],
        out_specs=pl.BlockSpec((1, tq, g * HEAD_DIM), lambda b, h, t: (b, t, h)),
        compiler_params=_cparams("arbitrary", "arbitrary", "arbitrary"),
        name="dense_attention",
    )(q, k, v)


def _softplus(x):
    return jnp.maximum(x, 0.0) + jnp.log(1.0 + jnp.exp(-jnp.abs(x)))


def _split(a):
    bits = lax.bitcast_convert_type(a, jnp.uint32) & jnp.uint32(0xFFFF0000)
    hi = lax.bitcast_convert_type(bits, F32)
    return hi, a - hi


def _rhs3(b):
    hi, lo = _split(b)
    return jnp.concatenate([hi, hi, lo], axis=-2).astype(BF16)


def _split3(a):
    hi, rest = _split(a)
    mid, lo = _split(rest)
    return hi, mid, lo


def _mm_exact_lhs(tbl3, x):
    return jnp.dot(tbl3, jnp.concatenate(_split3(x), axis=0).astype(BF16), preferred_element_type=F32)


def _mm_exact_rhs(x, tbl3):
    return jnp.dot(jnp.concatenate(_split3(x), axis=-1).astype(BF16), tbl3, preferred_element_type=F32)


def _mm_split(a, b):
    ah, al = _split(a)
    return jnp.dot(jnp.concatenate([ah, al, ah], axis=-1).astype(BF16), _rhs3(b), preferred_element_type=F32)


def _bmm(a, bf):
    return lax.dot_general(a.astype(BF16), bf, (((2,), (1,)), ((0,), (0,))), preferred_element_type=F32)


def _rwkv_par_kernel(z_ref, w0_ref, wup_ref, a0_ref, aup_ref, kk_ref, ka_ref, bd_ref, cum_ref,
                     rp3_ref, yvq_ref, rvk_ref, *, w, lora):
    tc = RW_CHUNK
    nc, nh = ROW_TILE // tc, w // HEAD_DIM
    assert tc == HEAD_DIM
    z = z_ref[0]
    r, k, v = z[:, :w], z[:, w:2 * w], z[:, 2 * w:3 * w]
    w_low = jnp.tanh(z[:, 3 * w:3 * w + lora])
    a_low = z[:, 3 * w + lora:]
    kk = k * kk_ref[...]
    kk = kk / jnp.maximum(jnp.sqrt(_mm_exact_rhs(kk * kk, bd_ref[...])), 1e-12)
    ksum = jnp.zeros_like(k)

    tpos = lax.broadcasted_iota(jnp.int32, (tc, w), 0)
    spos = lax.broadcasted_iota(jnp.int32, (tc, w), 1) % tc
    eye = tpos == spos
    packed = lambda a: a.reshape(nc, tc, w)

    def level_mask(s):
        return jnp.logical_and(tpos // (2 * s) == spos // (2 * s), tpos // s != spos // s)

    n_lane_tiles = w // LANES
    lane_head = lax.broadcasted_iota(jnp.int32, (tc, LANES), 1) // tc

    def head_lanes(h, tile):
        return jnp.where(lane_head == h % (LANES // tc), tile, jnp.zeros_like(tile))

    def blockdiag(m):
        mb = m.astype(BF16)
        zeros = jnp.zeros(mb.shape[:2] + (LANES,), BF16)
        rows = []
        for h in range(nh):
            j = h * tc // LANES
            kept = head_lanes(h, mb[:, :, j * LANES:(j + 1) * LANES])
            rows.append(jnp.concatenate([kept if jj == j else zeros for jj in range(n_lane_tiles)], axis=2))
        return jnp.concatenate(rows, axis=1)

    def own_blocks(full):
        tiles = []
        for j in range(n_lane_tiles):
            heads = [h for h in range(nh) if h * tc // LANES == j]
            tiles.append(sum(head_lanes(h, full[:, h * tc:(h + 1) * tc, j * LANES:(j + 1) * LANES]) for h in heads))
        return jnp.concatenate(tiles, axis=2)

    per_dir = []
    for d in range(2):
        w_log = -_softplus(-(w0_ref[d:d + 1, :] + _mm_split(w_low, wup_ref[d]))) - 0.5
        lw = -jnp.exp(w_log)
        a = 1.0 / (1.0 + jnp.exp(-(a0_ref[d:d + 1, :] + _mm_split(a_low, aup_ref[d]))))
        kd = k * (1.0 + (a - 1.0) * ka_ref[...])
        ksum = ksum + kd
        sums = _mm_exact_lhs(cum_ref[d], lw)
        c, ctot = sums[:ROW_TILE], sums[ROW_TILE:]
        e_neg, e_rem = jnp.exp(-c), jnp.exp(ctot - c)
        b = kk * a
        before = tpos > spos if d == 0 else tpos < spos
        per_dir.append((packed(-kk * jnp.exp(c - lw)), packed(r * jnp.exp(c)), packed(b * e_neg),
                        packed(kd * e_neg), packed(b * e_rem), packed(kd * e_rem), packed(jnp.exp(ctot)),
                        jnp.broadcast_to(before, (nc, tc, w))))
    rvk_ref[0] = jnp.concatenate([r, v, ksum], axis=-1)

    at_p, rt_p, bh_p, kh_p, bc_p, kc_p, wt_p, before = (jnp.concatenate(parts, axis=0) for parts in zip(*per_dir))
    v_p = jnp.concatenate([packed(v)] * 2, axis=0)
    upto = jnp.logical_or(before, eye)
    big = lax.dot_general(jnp.concatenate([at_p, rt_p], axis=1).astype(BF16),
                          jnp.concatenate([blockdiag(bh_p), blockdiag(kh_p)], axis=1),
                          (((2,), (2,)), ((0,), (0,))), preferred_element_type=F32)
    a_ab = jnp.where(before, big[:, :tc, :w], 0.0)
    a_ak = jnp.where(before, big[:, :tc, w:], 0.0)
    a_rb = jnp.where(upto, big[:, tc:, :w], 0.0)
    a_rk = jnp.where(upto, big[:, tc:, w:], 0.0)
    x = jnp.where(eye, 1.0, jnp.where(level_mask(1), a_ab, 0.0))
    a_ab16 = a_ab.astype(BF16)
    s = 2
    while s < tc:
        half = _bmm(x, blockdiag(jnp.where(level_mask(s), a_ab16, jnp.zeros_like(a_ab16))))
        x = x + _bmm(half, blockdiag(x))
        s *= 2
    akrk = _bmm(jnp.concatenate([a_ak, a_rk], axis=1), blockdiag(v_p))
    xa = _bmm(x, jnp.concatenate([blockdiag(at_p), blockdiag(akrk[:, :tc])], axis=2))
    ra = _bmm(a_rb, jnp.concatenate([blockdiag(xa[:, :, :w]), blockdiag(xa[:, :, w:])], axis=2))
    rp = rt_p + ra[:, :, :w]
    yv = ra[:, :, w:] + akrk[:, tc:]
    lhs_t = jnp.concatenate([bc_p, kc_p], axis=1).astype(BF16)
    rhs_t = jnp.concatenate([xa, jnp.concatenate([jnp.zeros_like(v_p), v_p], axis=2)], axis=1).astype(BF16)
    full = lax.dot_general(lhs_t, rhs_t, (((1,), (1,)), ((0,), (0,))), preferred_element_type=F32)
    p = own_blocks(full[:, :, :w]) + jnp.where(eye, wt_p[:, 0:1, :], 0.0)
    q = own_blocks(full[:, :, w:])
    rp3_ref[0] = jnp.concatenate([rp, p], axis=1).astype(BF16).reshape(2, nc, tc + HEAD_DIM, w)
    yvq_ref[0] = jnp.concatenate([yv, q], axis=1).reshape(2, nc, tc + HEAD_DIM, w)


def _chunk_matrices():
    t = np.arange(ROW_TILE)
    same = (t[:, None] // RW_CHUNK) == (t[None, :] // RW_CHUNK)
    tabs = []
    for run in (same & (t[None, :] <= t[:, None]), same & (t[None, :] >= t[:, None])):
        m = np.concatenate([run, same], axis=0).astype(np.float32).astype(BF16)
        tabs.append(np.concatenate([m, m, m], axis=1))
    return np.stack(tabs)


def _rwkv_par(rw, w0, w_up, a0, a_up, k_k, k_a, w):
    bsz, s, width = rw.shape
    nt = s // ROW_TILE
    nh, nc = w // HEAD_DIM, ROW_TILE // RW_CHUNK
    lora = w_up.shape[1]
    full = lambda a: pl.BlockSpec(a.shape, lambda b, t: (0,) * a.ndim)
    cum = _chunk_matrices()
    bd = _head_block_diag(w, 1.0)
    vec = lambda a: a.reshape(1, w)
    n_chunks = s // RW_CHUNK
    tc, hd = RW_CHUNK, HEAD_DIM
    consts = (w0, w_up, a0, a_up, vec(k_k), vec(k_a), bd, cum)
    return pl.pallas_call(
        functools.partial(_rwkv_par_kernel, w=w, lora=lora),
        out_shape=[jax.ShapeDtypeStruct((bsz, 2, n_chunks, tc + hd, w), BF16),
                   jax.ShapeDtypeStruct((bsz, 2, n_chunks, tc + hd, w), F32),
                   jax.ShapeDtypeStruct((bsz, s, 3 * w), F32)],
        grid=(bsz, nt),
        in_specs=[pl.BlockSpec((1, ROW_TILE, width), lambda b, t: (b, t, 0))] + [full(a) for a in consts],
        out_specs=[pl.BlockSpec((1, 2, nc, tc + hd, w), lambda b, t: (b, 0, t, 0, 0)),
                   pl.BlockSpec((1, 2, nc, tc + hd, w), lambda b, t: (b, 0, t, 0, 0)),
                   pl.BlockSpec((1, ROW_TILE, 3 * w), lambda b, t: (b, t, 0))],
        compiler_params=_cparams("arbitrary", "arbitrary"),
        name="rwkv_chunk_prep",
    )(rw, *consts)


def _rwkv_seq_kernel(rp3f_ref, yvqf_ref, rp3b_ref, yvqb_ref, yf_ref, yb_ref, g_ref):
    @pl.when(pl.program_id(1) == 0)
    def _():
        g_ref[...] = jnp.zeros_like(g_ref)

    w = g_ref.shape[-1]
    tc = RW_CHUNK
    nh = w // HEAD_DIM
    grp = rp3f_ref.shape[2]
    row_head = lax.broadcasted_iota(jnp.int32, (w, w), 0) // HEAD_DIM
    col_head = lax.broadcasted_iota(jnp.int32, (w, w), 1) // HEAD_DIM
    on_diag = row_head == col_head
    for step in range(grp):
        for d, (rp3, yvq, y) in enumerate(((rp3f_ref, yvqf_ref, yf_ref), (rp3b_ref, yvqb_ref, yb_ref))):
            ci = step if d == 0 else grp - 1 - step
            out = jnp.dot(rp3[0, 0, ci], g_ref[d].astype(BF16), preferred_element_type=F32) + yvq[0, 0, ci]
            y[0, ci * tc:(ci + 1) * tc, :] = out[:tc]
            g_ref[d] = jnp.where(on_diag, jnp.concatenate([out[tc:]] * nh, axis=0), 0.0)


RW_SCAN_GROUP = 4


def _rwkv_seq(rp3, yvq, n_lat_chunks):
    bsz, _, n_chunks = rp3.shape[:3]
    w = yvq.shape[-1]
    grp = RW_SCAN_GROUP
    assert n_lat_chunks % grp == 0 and n_chunks % grp == 0
    n_groups, n_lat, n_ctx = n_chunks // grp, n_lat_chunks // grp, (n_chunks - n_lat_chunks) // grp
    order = (lambda i: jnp.where(i < n_ctx, n_lat + i, i - n_ctx),
             lambda i: n_groups - 1 - i)
    blk = lambda d, a: pl.BlockSpec((1, 1, grp) + a.shape[3:], lambda b, i: (b, d, order[d](i), 0, 0))
    out = lambda d: pl.BlockSpec((1, grp * RW_CHUNK, w), lambda b, i: (b, order[d](i), 0))
    shp = jax.ShapeDtypeStruct((bsz, n_chunks * RW_CHUNK, w), F32)
    return pl.pallas_call(
        _rwkv_seq_kernel,
        out_shape=[shp, shp],
        grid=(bsz, n_groups),
        in_specs=[blk(0, rp3), blk(0, yvq), blk(1, rp3), blk(1, yvq)],
        out_specs=[out(0), out(1)],
        scratch_shapes=[pltpu.VMEM((2, w, w), F32)],
        compiler_params=_cparams("arbitrary", "arbitrary"),
        name="rwkv_state_scan",
    )(rp3, yvq, rp3, yvq)


def _rwkv_readout_tile(y, rvk, r_k, ln_g, ln_b, head_mean, head_sum):
    w = r_k.shape[-1]
    r, v, ksum = rvk[:, :w], rvk[:, w:2 * w], rvk[:, 2 * w:]
    yc = y - _mm_exact_rhs(y, head_mean)
    var = _mm_exact_rhs(yc * yc, head_mean)
    bonus = _mm_exact_rhs(r * ksum * r_k, head_sum) * v
    return yc * lax.rsqrt(var + RW_GN_EPS) * ln_g + ln_b + bonus


@functools.lru_cache(maxsize=None)
def _filter_position_features(seq_len, bands, width):
    n = np.arange(2 * seq_len)
    pos = np.where(n < seq_len, n, 2 * seq_len - n) % seq_len
    t = np.linspace(0.0, 1.0, seq_len)[pos][:, None]
    wpos = (2.0 * math.pi / seq_len) * pos[:, None]
    f = np.linspace(1e-4, bands - 1, bands)[None, :]
    z = np.concatenate([t, np.cos(f * wpos), np.sin(f * wpos)], axis=-1)
    return np.pad(z, ((0, 0), (0, width - z.shape[1]))).astype(np.float32)


def _hyena_filter_kernel(z_ref, fw1_ref, fb1_ref, freq_ref, fw2_ref, fb2_ref, fw3_ref, delta_ref, o_ref, *, seq_len):
    freq = freq_ref[...]
    rows = ROW_TILE
    n_orders, _, c = o_ref.shape

    def taps(i, norms):
        blk = pl.ds(pl.multiple_of(i * rows, rows), rows)
        z = z_ref[blk, :]
        h = jnp.sin(freq * (_mm_split(z, fw1_ref[...]) + fb1_ref[...]))
        h = jnp.sin(freq * (_mm_split(h, fw2_ref[...]) + fb2_ref[...]))
        h = _mm_split(h, fw3_ref[...])
        n = i * rows + lax.broadcasted_iota(jnp.int32, (rows, 1), 0)
        decay = jnp.exp(-z[:, 0:1] * jnp.abs(delta_ref[...]))
        out = []
        for o in range(n_orders):
            ho = jnp.where(n < seq_len, h[:, 2 * o * c:(2 * o + 1) * c], h[:, (2 * o + 1) * c:(2 * o + 2) * c])
            ho = ho * decay
            o_ref[o, blk, :] = jnp.where(n != seq_len, ho, 0.0)
            out.append(norms[o] + jnp.sum(jnp.abs(ho), axis=0, keepdims=True))
        return tuple(out)
    zero = jnp.zeros((1, c), F32)
    norms = lax.fori_loop(0, z_ref.shape[0] // rows, taps, (zero,) * n_orders)

    def normalise(i, carry):
        blk = pl.ds(pl.multiple_of(i * rows, rows), rows)
        for o in range(n_orders):
            o_ref[o, blk, :] = o_ref[o, blk, :] / norms[o]
        return carry
    lax.fori_loop(0, z_ref.shape[0] // rows, normalise, 0)


def _hyena_two_sided_filters(seq_len, fw1, fb1, freq, fw2, fb2, fw3, width):
    ffn = fw1.shape[1]
    bands = (fw1.shape[0] - 1) // 2
    z = _filter_position_features(seq_len, bands, ffn)
    fw1p = jnp.pad(fw1, ((0, ffn - fw1.shape[0]), (0, 0)))
    max_decay = math.log(HY_DECAY_TARGET) / HY_FAST_DECAY
    min_decay = math.log(HY_DECAY_TARGET) / HY_SLOW_DECAY
    deltas = jnp.linspace(min_decay, max_decay, width, dtype=F32).reshape(1, width)
    row = lambda a: a.reshape(1, ffn)
    full = lambda shape: pl.BlockSpec(shape, lambda i: (0,) * len(shape))
    return pl.pallas_call(
        functools.partial(_hyena_filter_kernel, seq_len=seq_len),
        out_shape=jax.ShapeDtypeStruct((2, 2 * seq_len, width), F32),
        grid=(1,),
        in_specs=[full(z.shape), full((ffn, ffn)), full((1, ffn)), full((1, ffn)), full((ffn, ffn)),
                  full((1, ffn)), full(fw3.shape), full((1, width))],
        out_specs=full((2, 2 * seq_len, width)),
        compiler_params=_cparams("arbitrary"),
        name="hyena_filter",
    )(z, fw1p, row(fb1), row(freq), fw2, row(fb2), fw3, deltas)


def kernel(x, c, ctx, c_ctx, mod_w, mod_b, norm_g, w_in, w_out, hy_conv, hy_fw1, hy_fb1, hy_freq, hy_fw2,
           hy_fb2, hy_fw3, hy_bias, rw_conv, rw_w0, rw_w_up, rw_a0, rw_a_up, rw_k_k, rw_k_a, rw_r_k,
           rw_ln_g, rw_ln_b, wa_sink, fa_q_norm, fa_k_norm, final_g):
    bsz, seq_len, d = x.shape
    ctx_len = ctx.shape[1]
    depth = w_in.shape[0]
    w_hy = hy_bias.shape[-1]
    w_rw = rw_w0.shape[-1]
    n_wa_heads = wa_sink.shape[-1]
    w_q = n_wa_heads * HEAD_DIM
    w_kv = w_q // 2
    lora = rw_w_up.shape[2] + rw_a_up.shape[2]
    branch_w = (3 * w_hy, 3 * w_rw + lora, w_q + 2 * w_kv, w_q + 2 * w_kv)
    gate_w = (w_hy, w_rw, w_q, w_q)
    assert seq_len % ROW_TILE == 0 and ctx_len % ROW_TILE == 0 and bsz % 2 == 0
    n_lat_tiles = seq_len // ROW_TILE

    splits = tuple(wd for pair in zip(branch_w, gate_w) for wd in pair)
    assert all(wd % LANES == 0 for wd in splits)
    w_in_b = w_in.astype(BF16)
    w_out_b = w_out.astype(BF16)

    pad_rows = (-(bsz + 1)) % SUBLANES
    cond = jnp.concatenate([c, c_ctx[None], jnp.zeros((pad_rows, d), F32)], axis=0)
    mod = _modulation(cond, mod_w, mod_b)
    mod_lat = mod[:, :bsz].reshape(depth, bsz, 3, d)
    mod_ctx = jnp.broadcast_to(mod[:, bsz].reshape(depth, 1, 3, d), (depth, bsz, 3, d))
    mods = jnp.stack([mod_lat, mod_ctx], axis=2)

    rope = _rope_tables(seq_len, ctx_len)
    hy_group = 2 * seq_len // FFT_N1
    assert ctx_len % hy_group == 0 and ROW_TILE % hy_group == 0
    xs = jnp.concatenate([x, ctx], axis=1)
    for l in range(depth):
        last = l == depth - 1
        hyc, rwc, gates, qw, kw, vw, qf, kf, vf = _inproj(
            xs, mods[l], norm_g[l], w_in_b[l], hy_conv[l], rw_conv[l], splits, n_lat_tiles, rope,
            fa_q_norm[l], fa_k_norm[l], w_q, w_kv, hy_group)

        filt = functools.partial(_hyena_two_sided_filters, fw1=hy_fw1[l], fb1=hy_fb1[l], freq=hy_freq[l],
                                 fw2=hy_fw2[l], fb2=hy_fb2[l], fw3=hy_fw3[l], width=w_hy)
        bias = hy_bias[l].reshape(2, 1, w_hy)
        spec = _filter_spectrum(filt(seq_len))
        nt_hy = w_hy // LANES
        y1 = _fftconv_gated(hyc, 0, hyc, nt_hy, spec, 0, bias, seq_len)
        a_lat = _fftconv_gated(y1, 0, hyc, 2 * nt_hy, spec, 1, bias, seq_len)
        a_ctx = (a_lat[:, :ctx_len] if last else
                 _ctx_hyena(hyc, seq_len // ctx_len, filt(ctx_len), bias, ctx_len, w_hy, hy_group))

        rp3, yvq, rvk = _rwkv_par(rwc, rw_w0[l], rw_w_up[l], rw_a0[l], rw_a_up[l], rw_k_k[l], rw_k_a[l], w_rw)
        yf, yb = _rwkv_seq(rp3, yvq, seq_len // RW_CHUNK)

        c_mix = _window_attention(qw, kw, vw, wa_sink[l], seq_len, ctx_len)
        d_mix = _dense_attention(qf, kf, vf, seq_len)

        xs = _outproj(a_lat, a_ctx, yf, yb, rvk, rw_r_k[l], rw_ln_g[l], rw_ln_b[l], c_mix, d_mix, gates, xs,
                      mods[l], w_out_b[l], final_g, n_lat_tiles, last, hy_group)
    return xs
```

```python
import functools
import math

import jax
import jax.numpy as jnp
import numpy as np
from jax import lax
from jax.experimental import pallas as pl
from jax.experimental.pallas import tpu as pltpu

HEAD_DIM = 64
GRID_W = 64
WINDOW = 128
NORM_EPS = 1e-6
RW_GN_EPS = 64e-5
NEG_INF = -1e30
ROPE_THETA = 10000.0
HY_FAST_DECAY = 0.3
HY_SLOW_DECAY = 1.5
HY_DECAY_TARGET = 1e-2

ROW_TILE = 256
LANES = 128
SUBLANES = 8
FFT_N1 = 64
FFT_UNROLL = 16
FFT_PLANE_PAD = 8
RW_CHUNK = 64
VMEM_LIMIT = 60 * 1024 * 1024

F32 = jnp.float32
BF16 = jnp.bfloat16
HI = lax.Precision.HIGHEST


def _dot(a, b):
    return jnp.dot(a, b, preferred_element_type=F32, precision=HI)


def _cparams(*sem):
    return pltpu.CompilerParams(dimension_semantics=sem, vmem_limit_bytes=VMEM_LIMIT)


def _const_spec(shape):
    return pl.BlockSpec(shape, lambda *_: (0,) * len(shape), pipeline_mode=pl.Buffered(1))


def _silu(x):
    return x * (1.0 / (1.0 + jnp.exp(-x)))


def _pad_groups(x, group):
    gap = jnp.zeros((FFT_PLANE_PAD, x.shape[1]), x.dtype)
    pieces = []
    for g in range(x.shape[0] // group):
        pieces += [x[g * group:(g + 1) * group], gap]
    return jnp.concatenate(pieces, axis=0)


def _unpad_groups(x, group):
    pitch = group + FFT_PLANE_PAD
    return jnp.concatenate([x[g * pitch:g * pitch + group] for g in range(x.shape[0] // pitch)], axis=0)


def _padded_rows(rows, group):
    return rows // group * (group + FFT_PLANE_PAD)


def _mod_kernel(c_ref, w_ref, b_ref, o_ref):
    o_ref[0] = _dot(_silu(c_ref[...]), w_ref[0]) + b_ref[0]


def _modulation(cond, mod_w, mod_b):
    depth, d, d3 = mod_w.shape
    rows = cond.shape[0]
    return pl.pallas_call(
        _mod_kernel,
        out_shape=jax.ShapeDtypeStruct((depth, rows, d3), F32),
        grid=(depth,),
        in_specs=[pl.BlockSpec((rows, d), lambda l: (0, 0)),
                  pl.BlockSpec((1, d, d3), lambda l: (l, 0, 0)),
                  pl.BlockSpec((1, 1, d3), lambda l: (l, 0, 0))],
        out_specs=pl.BlockSpec((1, rows, d3), lambda l: (l, 0, 0)),
        compiler_params=_cparams("arbitrary"),
        name="modulation",
    )(cond, mod_w, mod_b.reshape(depth, 1, d3))


def _blockreal(m):
    return np.block([[m.real, -m.imag], [m.imag, m.real]])


@functools.lru_cache(maxsize=None)
def _fft_tables(seq_len):
    n = 2 * seq_len
    n1, n2 = FFT_N1, n // FFT_N1
    h1 = n1 // 2
    j2 = np.arange(n2)[:, None, None]
    k1 = np.arange(n1)[None, :, None]
    t1 = np.exp(-2j * np.pi * (j2 * k1 / n + k1 * np.arange(n1)[None, None, :] / n1))
    t1_data = np.stack([_blockreal(t1[j][:, :h1]) for j in range(n2)])
    t1_real = np.concatenate([t1.real, t1.imag], axis=1)
    f2 = np.exp(-2j * np.pi * np.outer(np.arange(n2), np.arange(n2)) / n2)
    f2_fwd = _blockreal(f2)
    f2_inv = _blockreal(np.conj(f2))
    t4 = np.exp(2j * np.pi * (np.arange(h1)[None, :, None] * np.arange(n1)[None, None, :] / n1
                              + j2 * np.arange(n1)[None, None, :] / n)) / n
    t4 = np.stack([_blockreal(t4[j]) for j in range(n2)])
    f2_parts = np.concatenate([f2.real, f2.imag], axis=0)
    return tuple(_lhs3_table(t) for t in (t1_data, t1_real, f2_fwd, f2_parts, t4))


def _lhs3_table(m):
    hi = m.astype(np.float32).astype(BF16)
    lo = (m - hi.astype(np.float64)).astype(np.float32).astype(BF16)
    return np.concatenate([hi, lo, hi], axis=-1)


def _mm3(tbl3, x):
    return jnp.dot(tbl3, _rhs3(x), preferred_element_type=F32)


def _spectrum_kernel(k_ref, t1_ref, f2_ref, o_ref, a_ref, *, n1, n2):
    def stage1(j, carry):
        rows = k_ref[0, pl.ds(j, n1, stride=n2), :]
        a_ref[pl.ds(j, 2 * n1, stride=n2 + FFT_PLANE_PAD), :] = _mm3(t1_ref[j], rows)
        return carry
    lax.fori_loop(0, n2, stage1, 0, unroll=FFT_UNROLL)

    def stage2(i, carry):
        pitch = n2 + FFT_PLANE_PAD
        re = a_ref[pl.ds(pl.multiple_of(i * pitch, 8), n2), :]
        im = a_ref[pl.ds(pl.multiple_of((n1 + i) * pitch, 8), n2), :]
        o_ref[0, i] = _mm3(f2_ref[...], jnp.concatenate([re, im], axis=0))
        return carry
    lax.fori_loop(0, n1, stage2, 0, unroll=8)


def _filter_spectrum(kfilt):
    g, n, w = kfilt.shape
    n1, n2 = FFT_N1, n // FFT_N1
    _, t1_real, f2_fwd, _, _ = _fft_tables(n // 2)
    const = _const_spec
    return pl.pallas_call(
        functools.partial(_spectrum_kernel, n1=n1, n2=n2),
        out_shape=jax.ShapeDtypeStruct((g, n1, 2 * n2, w), F32),
        grid=(g, w // LANES),
        in_specs=[pl.BlockSpec((1, n, LANES), lambda gi, j: (gi, 0, j)),
                  const(t1_real.shape), const(f2_fwd.shape)],
        out_specs=pl.BlockSpec((1, n1, 2 * n2, LANES), lambda gi, j: (gi, 0, 0, j)),
        scratch_shapes=[pltpu.VMEM((2 * n1 * (n2 + FFT_PLANE_PAD), LANES), F32)],
        compiler_params=_cparams("arbitrary", "arbitrary"),
        name="hyena_filter_spectrum",
    )(kfilt, t1_real, f2_fwd)


def _fftconv_kernel(u_ref, m_ref, spec_ref, bias_ref, t1_ref, f2p_ref, t4_ref, o_ref, ar_ref, ai_ref,
                    *, n1, n2):
    h1 = n1 // 2
    c = u_ref.shape[-1]
    pitch = n2 + FFT_PLANE_PAD

    def stage1(j, carry):
        za = u_ref[0, pl.ds(j, h1, stride=pitch), :]
        zb = u_ref[1, pl.ds(j, h1, stride=pitch), :]
        res = _mm3(t1_ref[j], jnp.concatenate([za, zb], axis=0))
        ar_ref[pl.ds(j, n1, stride=pitch), :] = res[:n1]
        ai_ref[pl.ds(j, n1, stride=pitch), :] = res[n1:]
        return carry
    lax.fori_loop(0, n2, stage1, 0, unroll=FFT_UNROLL)

    def stage2(i, carry):
        rows = pl.ds(pl.multiple_of(i * pitch, SUBLANES), n2)
        p = _mm3(f2p_ref[...], jnp.concatenate([ar_ref[rows, :], ai_ref[rows, :]], axis=-1))
        xr = p[:n2, :c] - p[n2:, c:]
        xi = p[:n2, c:] + p[n2:, :c]
        kr, ki = spec_ref[0, i, :n2, :], spec_ref[0, i, n2:, :]
        y = jnp.concatenate([xr * kr - xi * ki, xr * ki + xi * kr], axis=-1)
        q = _mm3(f2p_ref[...], y)
        ar_ref[rows, :] = q[:n2, :c] + q[n2:, c:]
        ai_ref[rows, :] = q[:n2, c:] - q[n2:, :c]
        return carry
    lax.fori_loop(0, n1, stage2, 0, unroll=16)

    bias = bias_ref[0]

    def stage4(j, carry):
        planes = pl.ds(j, n1, stride=pitch)
        y = _mm3(t4_ref[j], jnp.concatenate([ar_ref[planes, :], ai_ref[planes, :]], axis=0))
        rows = pl.ds(j, h1, stride=pitch)
        for p in range(2):
            u = u_ref[p, rows, :]
            o_ref[p, rows, :] = m_ref[p, rows, :] * (y[p * h1:(p + 1) * h1] + bias * u)
        return carry
    lax.fori_loop(0, n2, stage4, 0, unroll=FFT_UNROLL)
    for r in range(n2, pitch):
        for p in range(2):
            o_ref[p, pl.ds(r, h1, stride=pitch), :] = jnp.zeros((h1, c), F32)


def _fftconv_gated(u, u_col, mult, mult_col, spec, conv_idx, bias, seq_len):
    bsz = u.shape[0]
    w = spec.shape[-1]
    n = 2 * seq_len
    n1, n2 = FFT_N1, n // FFT_N1
    rows = (n1 // 2) * (n2 + FFT_PLANE_PAD)
    t1_data, _, _, f2_parts, t4 = _fft_tables(seq_len)
    const = _const_spec
    return pl.pallas_call(
        functools.partial(_fftconv_kernel, n1=n1, n2=n2),
        out_shape=jax.ShapeDtypeStruct((bsz, rows, w), F32),
        grid=(w // LANES, bsz // 2),
        in_specs=[pl.BlockSpec((2, rows, LANES), lambda j, p: (p, 0, u_col + j)),
                  pl.BlockSpec((2, rows, LANES), lambda j, p: (p, 0, mult_col + j)),
                  pl.BlockSpec((1, n1, 2 * n2, LANES), lambda j, p: (conv_idx, 0, 0, j),
                               pipeline_mode=pl.Buffered(1)),
                  pl.BlockSpec((1, 1, LANES), lambda j, p: (conv_idx, 0, j)),
                  const(t1_data.shape), const(f2_parts.shape), const(t4.shape)],
        out_specs=pl.BlockSpec((2, rows, LANES), lambda j, p: (p, 0, j)),
        scratch_shapes=[pltpu.VMEM((n1 * (n2 + FFT_PLANE_PAD), LANES), F32)] * 2,
        compiler_params=_cparams("arbitrary", "arbitrary"),
        name="hyena_fftconv",
    )(u, mult, spec, bias, t1_data, f2_parts, t4)


@functools.lru_cache(maxsize=None)
def _small_fft_tables(seq_len):
    n = 2 * seq_len
    f = np.exp(-2j * np.pi * np.outer(np.arange(n), np.arange(n)) / n)
    fwd = _blockreal(f[:, :seq_len])
    real = np.concatenate([f.real, f.imag], axis=0)
    inv = _blockreal(np.conj(f)[:seq_len, :] / n)
    return _lhs3_table(fwd), _lhs3_table(real), _lhs3_table(inv)


def _ctx_hyena_kernel(v_ref, x1_ref, x2_ref, k_ref, bias_ref, fwd_ref, real_ref, inv_ref, o_ref,
                      *, seq_len, group):
    n = 2 * seq_len
    tokens = lambda ref, p: _unpad_groups(ref[p], group)

    def conv(ua, ub, g):
        spec = _mm3(real_ref[...], k_ref[g])
        x = _mm3(fwd_ref[...], jnp.concatenate([ua, ub], axis=0))
        xr, xi, kr, ki = x[:n], x[n:], spec[:n], spec[n:]
        y = _mm3(inv_ref[...], jnp.concatenate([xr * kr - xi * ki, xr * ki + xi * kr], axis=0))
        b = bias_ref[g]
        return y[:seq_len] + b * ua, y[seq_len:] + b * ub

    c1a, c1b = conv(tokens(v_ref, 0), tokens(v_ref, 1), 0)
    y1a, y1b = tokens(x1_ref, 0) * c1a, tokens(x1_ref, 1) * c1b
    c2a, c2b = conv(y1a, y1b, 1)
    o_ref[0] = tokens(x2_ref, 0) * c2a
    o_ref[1] = tokens(x2_ref, 1) * c2b


def _ctx_hyena(hyc, row_block, kfilt, bias, seq_len, w, group):
    bsz = hyc.shape[0]
    n = 2 * seq_len
    fwd, real, inv = _small_fft_tables(seq_len)
    nt = w // LANES
    col = lambda c0: pl.BlockSpec((2, _padded_rows(seq_len, group), LANES), lambda j, p: (p, row_block, c0 + j))
    return pl.pallas_call(
        functools.partial(_ctx_hyena_kernel, seq_len=seq_len, group=group),
        out_shape=jax.ShapeDtypeStruct((bsz, seq_len, w), F32),
        grid=(nt, bsz // 2),
        in_specs=[col(0), col(nt), col(2 * nt),
                  pl.BlockSpec((2, n, LANES), lambda j, p: (0, 0, j)),
                  pl.BlockSpec((2, 1, LANES), lambda j, p: (0, 0, j)),
                  _const_spec(fwd.shape), _const_spec(real.shape), _const_spec(inv.shape)],
        out_specs=pl.BlockSpec((2, seq_len, LANES), lambda j, p: (p, 0, j)),
        compiler_params=_cparams("arbitrary", "arbitrary"),
        name="hyena_ctx",
    )(hyc, hyc, hyc, kfilt, bias, fwd, real, inv)


def _rms(x, g):
    return x * lax.rsqrt(jnp.mean(x * x, axis=-1, keepdims=True) + NORM_EPS) * g


def _inproj_kernel(x_ref, xprev_ref, xnext_ref, mod_ref, g_ref, w_ref, hycw_ref, rwcw_ref,
                   cos_ref, shi_ref, slo_ref, qg_ref, kg_ref, bdq_ref, bdk_ref,
                   hy_ref, rw_ref, gt_ref, qw_ref, kw_ref, vw_ref, qf_ref, kf_ref, vf_ref,
                   *, splits, wq, wk, n_lat_tiles, n_tiles, hy_group):
    shift, scale = mod_ref[0, 0, 0:1, :], mod_ref[0, 0, 1:2, :]
    norm_mod = lambda x: (_rms(x, g_ref[...]) * (1.0 + scale) + shift).astype(BF16)
    z = jnp.dot(norm_mod(x_ref[0]), w_ref[...], preferred_element_type=F32)
    offs = np.cumsum((0,) + tuple(splits))
    seg = lambda i: z[:, offs[i]:offs[i + 1]]
    halo_h = norm_mod(jnp.concatenate([xprev_ref[0], xnext_ref[0]], axis=0))
    for i, cw_ref, o_ref in ((0, hycw_ref, hy_ref), (2, rwcw_ref, rw_ref)):
        halo = jnp.dot(halo_h, w_ref[:, offs[i]:offs[i + 1]], preferred_element_type=F32)
        conv = _short_conv_tile(seg(i), halo[:SUBLANES], halo[SUBLANES:], cw_ref[...], pl.program_id(1),
                                n_lat_tiles, n_tiles)
        o_ref[0] = _pad_groups(conv, hy_group) if o_ref is hy_ref else conv
    gt_ref[0] = _silu(jnp.concatenate([seg(1), seg(3), seg(5), seg(7)], axis=-1))
    _qk_emit(seg(4), seg(6), cos_ref[...], shi_ref[...], slo_ref[...], qg_ref[...], kg_ref[...],
             bdq_ref[...], bdk_ref[...], qw_ref, kw_ref, vw_ref, qf_ref, kf_ref, vf_ref, wq, wk)


def _inproj(xs, mods, norm_g, w_in, hy_conv, rw_conv, splits, n_lat_tiles, rope, q_gain, k_gain, wq, wk,
            hy_group):
    widths = (splits[0], splits[2], None, None, sum(splits[1::2]))
    assert ROW_TILE % hy_group == 0
    hy_tile = _padded_rows(ROW_TILE, hy_group)
    bsz, s, d = xs.shape
    nt = s // ROW_TILE
    nq, nk = wq // HEAD_DIM, wk // HEAD_DIM
    prev, nxt = _halo_specs(d, nt)
    tile = lambda wd: pl.BlockSpec((1, ROW_TILE, wd), lambda b, t: (b, t, 0))
    tab = pl.BlockSpec((ROW_TILE, HEAD_DIM), lambda b, t: (t, 0))
    hm = lambda n, wd=HEAD_DIM: pl.BlockSpec((1, n, ROW_TILE, wd), lambda b, t: (b, 0, t, 0))
    flat = lambda wd: jax.ShapeDtypeStruct((bsz, s, wd), F32)
    heads = lambda n, wd=HEAD_DIM: jax.ShapeDtypeStruct((bsz, n, s, wd), BF16)
    return pl.pallas_call(
        functools.partial(_inproj_kernel, splits=splits, wq=wq, wk=wk, n_lat_tiles=n_lat_tiles, n_tiles=nt,
                          hy_group=hy_group),
        out_shape=[jax.ShapeDtypeStruct((bsz, nt * hy_tile, widths[0]), F32), flat(widths[1]), flat(widths[4]),
                   heads(nq), heads(nk), heads(nk, LANES), heads(nq), heads(nk), heads(nk, LANES)],
        grid=(bsz, nt),
        in_specs=[tile(d), prev, nxt,
                  pl.BlockSpec((1, 1, 3, d), lambda b, t: (b, t // n_lat_tiles, 0, 0)),
                  pl.BlockSpec((1, d), lambda b, t: (0, 0)),
                  _const_spec(w_in.shape), pl.BlockSpec(hy_conv.shape, lambda b, t: (0, 0)),
                  pl.BlockSpec(rw_conv.shape, lambda b, t: (0, 0)), tab, tab, tab,
                  pl.BlockSpec((1, wq), lambda b, t: (0, 0)), pl.BlockSpec((1, wk), lambda b, t: (0, 0)),
                  _const_spec((3 * wq, wq)), _const_spec((3 * wk, wk))],
        out_specs=[pl.BlockSpec((1, hy_tile, widths[0]), lambda b, t: (b, t, 0)), tile(widths[1]), tile(widths[4]),
                   hm(nq), hm(nk), hm(nk, LANES), hm(nq), hm(nk), hm(nk, LANES)],
        compiler_params=_cparams("arbitrary", "arbitrary"),
        name="in_projection",
    )(xs, xs, xs, mods, norm_g.reshape(1, d), w_in, hy_conv, rw_conv, *rope,
      jnp.tile(q_gain, nq).reshape(1, wq), jnp.tile(k_gain, nk).reshape(1, wk),
      _head_block_diag(wq, 1.0 / HEAD_DIM), _head_block_diag(wk, 1.0 / HEAD_DIM))


def _halo_specs(width, n_tiles):
    per = ROW_TILE // SUBLANES
    prev = pl.BlockSpec((1, SUBLANES, width), lambda b, t: (b, jnp.maximum(t * per - 1, 0), 0))
    nxt = pl.BlockSpec((1, SUBLANES, width), lambda b, t: (b, jnp.minimum((t + 1) * per, n_tiles * per - 1), 0))
    return prev, nxt


def _short_conv_tile(z, prev8, next8, w, t, n_lat_tiles, n_tiles):
    first = jnp.logical_or(t == 0, t == n_lat_tiles)
    last = jnp.logical_or(t == n_lat_tiles - 1, t == n_tiles - 1)
    above = jnp.where(first, 0.0, prev8[SUBLANES - 1:SUBLANES, :])
    below = jnp.where(last, 0.0, next8[0:1, :])
    row = lax.broadcasted_iota(jnp.int32, z.shape, 0)
    zm1 = jnp.where(row == 0, above, pltpu.roll(z, 1, 0))
    zp1 = jnp.where(row == z.shape[0] - 1, below, pltpu.roll(z, z.shape[0] - 1, 0))
    return zm1 * w[0:1, :] + z * w[1:2, :] + zp1 * w[2:3, :]


def _outproj_kernel(a_lat_ref, a_ctx_ref, yf_ref, yb_ref, bonus_ref, lng_ref, lnb_ref, hmean_ref,
                    c_ref, d_ref, gt_ref, x_ref, mod_ref, w_ref, fg_ref, o_ref, *, n_lat_tiles, final, hy_group):
    is_ctx = pl.program_id(1) >= n_lat_tiles
    a = jnp.where(is_ctx, a_ctx_ref[0], _unpad_groups(a_lat_ref[0], hy_group))
    b = _rwkv_readout_tile(yf_ref[0] + yb_ref[0], bonus_ref[0], lng_ref[...], lnb_ref[...], hmean_ref[...])
    mix = jnp.concatenate([a, b, c_ref[0], d_ref[0]], axis=-1) * gt_ref[0]
    y = jnp.dot(mix.astype(BF16), w_ref[...], preferred_element_type=F32)
    x = x_ref[0] + mod_ref[0, 0, 2:3, :] * y
    o_ref[0] = _rms(x, fg_ref[...]) if final else x


def _outproj(a_lat, a_ctx, yf, yb, bonus, ln_g, ln_b, cmix, dmix, gates, xs, mods, w_out, final_g,
             n_lat_tiles, final, hy_group):
    bsz, s, d = xs.shape
    wb = a_lat.shape[-1]
    nt = n_lat_tiles if final else s // ROW_TILE
    tile = lambda wd: pl.BlockSpec((1, ROW_TILE, wd), lambda b, t: (b, t, 0))
    vec = pl.BlockSpec((1, wb), lambda b, t: (0, 0))
    head_tab = _const_spec((3 * wb, wb))
    return pl.pallas_call(
        functools.partial(_outproj_kernel, n_lat_tiles=n_lat_tiles, final=final, hy_group=hy_group),
        out_shape=jax.ShapeDtypeStruct((bsz, nt * ROW_TILE, d), F32),
        grid=(bsz, nt),
        in_specs=[pl.BlockSpec((1, _padded_rows(ROW_TILE, hy_group), wb),
                               lambda b, t: (b, jnp.minimum(t, n_lat_tiles - 1), 0)),
                  pl.BlockSpec((1, ROW_TILE, wb), lambda b, t: (b, 0, 0)),
                  tile(wb), tile(wb), tile(wb), vec, vec, head_tab,
                  tile(wb), tile(wb), tile(4 * wb), tile(d),
                  pl.BlockSpec((1, 1, 3, d), lambda b, t: (b, t // n_lat_tiles, 0, 0)),
                  _const_spec(w_out.shape),
                  pl.BlockSpec((1, d), lambda b, t: (0, 0))],
        out_specs=tile(d),
        compiler_params=_cparams("arbitrary", "arbitrary"),
        name="out_projection",
    )(a_lat, a_ctx, yf, yb, bonus, ln_g.reshape(1, wb), ln_b.reshape(1, wb),
      _head_block_diag(wb, 1.0 / HEAD_DIM), cmix, dmix, gates, xs, mods, w_out, final_g.reshape(1, d))


def _rope_tables(seq_len, ctx_len):
    rows = seq_len // GRID_W
    row = jnp.repeat(jnp.arange(rows, dtype=F32), GRID_W)
    col = jnp.tile(jnp.arange(GRID_W, dtype=F32), rows)
    n_freq = HEAD_DIM // 4
    inv_freq = ROPE_THETA ** (-jnp.arange(n_freq, dtype=F32) / n_freq)
    ar, ac = row[:, None] * inv_freq, col[:, None] * inv_freq
    zero = jnp.zeros_like(ar)
    cos = jnp.concatenate([jnp.cos(ar), jnp.cos(ar), jnp.cos(ac), jnp.cos(ac)], axis=-1)
    sin_hi = jnp.concatenate([-jnp.sin(ar), zero, -jnp.sin(ac), zero], axis=-1)
    sin_lo = jnp.concatenate([zero, jnp.sin(ar), zero, jnp.sin(ac)], axis=-1)
    pad = lambda t, v: jnp.concatenate([t, jnp.full((ctx_len, HEAD_DIM), v, F32)], axis=0)
    return pad(cos, 1.0), pad(sin_hi, 0.0), pad(sin_lo, 0.0)


def _rope(x, cos, sin_hi, sin_lo):
    q = HEAD_DIM // 4
    w = x.shape[-1]
    return x * cos + pltpu.roll(x, w - q, 1) * sin_hi + pltpu.roll(x, q, 1) * sin_lo


def _head_mean_sq(x, bd3):
    return _mm_exact_rhs(x * x, bd3)


def _qk_emit(wa, fa, cos, shi, slo, q_gain, k_gain, bdq, bdk, qw_ref, kw_ref, vw_ref, qf_ref, kf_ref, vf_ref,
             wq, wk):
    nq, nk = wq // HEAD_DIM, wk // HEAD_DIM
    tab = lambda t, n: jnp.concatenate([t] * n, axis=-1)
    cq, hq, lq = tab(cos, nq), tab(shi, nq), tab(slo, nq)
    ck, hk, lk = tab(cos, nk), tab(shi, nk), tab(slo, nk)
    scale = HEAD_DIM ** -0.5

    def emit(ref, val, n):
        for h in range(n):
            ref[0, h] = val[:, h * HEAD_DIM:(h + 1) * HEAD_DIM].astype(ref.dtype)

    def emit_values(ref, v):
        lane = lax.broadcasted_iota(jnp.int32, (v.shape[0], LANES - HEAD_DIM), 1)
        ones_pad = jnp.where(lane == 0, 1.0, 0.0)
        for h in range(nk):
            ref[0, h] = jnp.concatenate([v[:, h * HEAD_DIM:(h + 1) * HEAD_DIM], ones_pad],
                                        axis=-1).astype(ref.dtype)

    emit(qw_ref, _rope(wa[:, :wq], cq, hq, lq) * scale, nq)
    emit(kw_ref, _rope(wa[:, wq:wq + wk], ck, hk, lk), nk)
    emit_values(vw_ref, wa[:, wq + wk:])
    q, k = fa[:, :wq], fa[:, wq:wq + wk]
    q = q * lax.rsqrt(_head_mean_sq(q, bdq) + NORM_EPS) * q_gain
    k = k * lax.rsqrt(_head_mean_sq(k, bdk) + NORM_EPS) * k_gain
    emit(qf_ref, _rope(q, cq, hq, lq) * scale, nq)
    emit(kf_ref, _rope(k, ck, hk, lk), nk)
    emit_values(vf_ref, fa[:, wq + wk:])


def _head_block_diag(width, value):
    h = np.arange(width) // HEAD_DIM
    bd = ((h[:, None] == h[None, :]) * value).astype(np.float32).astype(BF16)
    assert np.all(bd.astype(np.float32) == (h[:, None] == h[None, :]) * value)
    return np.concatenate([bd, bd, bd], axis=0)


def _window_attn_kernel(q_ref, k_ref, v_ref, sink_ref, o_ref, *, seq_len, ctx_len, tq):
    t = pl.program_id(2)
    n_lat = seq_len // tq
    g = q_ref.shape[1]
    sub = WINDOW_SUB_ROWS
    band = 2 * WINDOW + sub
    kc, vc = k_ref[0, 0, seq_len:seq_len + ctx_len, :], v_ref[0, 0, seq_len:seq_len + ctx_len, :]
    nt_dot = lambda a, b: lax.dot_general(a, b, (((1,), (1,)), ((), ())), preferred_element_type=F32)
    scores = []
    for j in range(tq // sub):
        q = q_ref[0, :, j * sub:(j + 1) * sub, :].reshape(g * sub, HEAD_DIM)
        first = t * tq + j * sub
        start = pl.multiple_of(jnp.clip(first - WINDOW, 0, seq_len - band), sub)
        s_ctx = nt_dot(q, kc)
        s_loc = nt_dot(q, k_ref[0, 0, pl.ds(start, band), :])
        qpos = first + lax.broadcasted_iota(jnp.int32, (g, sub, band), 1).reshape(g * sub, band)
        kpos = start + lax.broadcasted_iota(jnp.int32, (g * sub, band), 1)
        valid = jnp.logical_and(jnp.abs(qpos - kpos) <= WINDOW, t < n_lat)
        scores.append((s_ctx, jnp.where(valid, s_loc, NEG_INF), start))
    probs = []
    for j, (s_ctx, s_loc, start) in enumerate(scores):
        sink = sink_ref[0, :, j * sub:(j + 1) * sub, :].reshape(g * sub, 1)
        m = jnp.maximum(jnp.maximum(jnp.max(s_ctx, axis=-1, keepdims=True),
                                    jnp.max(s_loc, axis=-1, keepdims=True)), sink)
        probs.append((jnp.exp(s_ctx - m).astype(BF16), jnp.exp(s_loc - m).astype(BF16), jnp.exp(sink - m), start))
    for j, (p_ctx, p_loc, p_sink, start) in enumerate(probs):
        acc = (jnp.dot(p_ctx, vc, preferred_element_type=F32)
               + jnp.dot(p_loc, v_ref[0, 0, pl.ds(start, band), :], preferred_element_type=F32))
        out = acc[:, :HEAD_DIM] / (acc[:, HEAD_DIM:HEAD_DIM + 1] + p_sink)
        o_ref[0, j * sub:(j + 1) * sub, :] = jnp.concatenate(
            [out[h * sub:(h + 1) * sub] for h in range(g)], axis=-1)


def _window_attention(q, k, v, sink, seq_len, ctx_len):
    bsz, nq, s, _ = q.shape
    nkv = k.shape[1]
    g = nq // nkv
    tq = ROW_TILE
    sink_rows = jnp.broadcast_to(sink.astype(F32).reshape(nkv, g, 1, 1), (nkv, g, tq, 1))
    kv = lambda a: pl.BlockSpec((1, 1, s, a.shape[-1]), lambda b, h, t: (b, h, 0, 0))
    return pl.pallas_call(
        functools.partial(_window_attn_kernel, seq_len=seq_len, ctx_len=ctx_len, tq=tq),
        out_shape=jax.ShapeDtypeStruct((bsz, s, nq * HEAD_DIM), F32),
        grid=(bsz, nkv, s // tq),
        in_specs=[pl.BlockSpec((1, g, tq, HEAD_DIM), lambda b, h, t: (b, h, t, 0)), kv(k), kv(v),
                  pl.BlockSpec((1, g, tq, 1), lambda b, h, t: (h, 0, 0, 0))],
        out_specs=pl.BlockSpec((1, tq, g * HEAD_DIM), lambda b, h, t: (b, t, h)),
        compiler_params=_cparams("arbitrary", "arbitrary", "arbitrary"),
        name="window_attention",
    )(q, k, v, sink_rows)


WINDOW_SUB_ROWS = 64
DENSE_KEY_BLOCK = 1024


def _dense_attn_kernel(q_ref, k_ref, v_ref, o_ref, *, seq_len, tq, tk):
    t = pl.program_id(2)
    n_lat = seq_len // tq
    g = q_ref.shape[1]
    ctx_len = k_ref.shape[2] - seq_len
    q = q_ref[0].reshape(g * tq, HEAD_DIM)

    def score(start, size):
        return lax.dot_general(q, k_ref[0, 0, start:start + size, :], (((1,), (1,)), ((), ())),
                               preferred_element_type=F32)

    def softmax(m, s):
        m_new = jnp.maximum(m, jnp.max(s, axis=-1, keepdims=True))
        return m_new, jnp.exp(m - m_new), jnp.exp(s - m_new).astype(BF16)

    def values(acc, alpha, p, start, size):
        return alpha * acc + jnp.dot(p, v_ref[0, 0, start:start + size, :], preferred_element_type=F32)

    def attend(blocks):
        m = jnp.full((g * tq, 1), NEG_INF, F32)
        acc = jnp.zeros((g * tq, v_ref.shape[-1]), F32)
        n = len(blocks)
        scores, probs = {}, {}
        for i in range(n + 2):
            if i < n:
                scores[i] = score(*blocks[i])
            if 0 <= i - 1 < n:
                m, alpha, p = softmax(m, scores.pop(i - 1))
                probs[i - 1] = (alpha, p)
            if 0 <= i - 2 < n:
                acc = values(acc, *probs.pop(i - 2), *blocks[i - 2])
        out = acc[:, :HEAD_DIM] / acc[:, HEAD_DIM:HEAD_DIM + 1]
        o_ref[0] = jnp.concatenate([out[h * tq:(h + 1) * tq] for h in range(g)], axis=-1)

    @pl.when(t < n_lat)
    def _():
        attend([(seq_len, ctx_len)] + [(j * tk, tk) for j in range(seq_len // tk)])

    @pl.when(t >= n_lat)
    def _():
        attend([(seq_len, ctx_len)])


def _dense_attention(q, k, v, seq_len):
    bsz, nq, s, _ = q.shape
    nkv = k.shape[1]
    g = nq // nkv
    tq = ROW_TILE
    tk = math.gcd(seq_len, DENSE_KEY_BLOCK)
    kv = lambda a: pl.BlockSpec((1, 1, s, a.shape[-1]), lambda b, h, t: (b, h, 0, 0))
    return pl.pallas_call(
        functools.partial(_dense_attn_kernel, seq_len=seq_len, tq=tq, tk=tk),
        out_shape=jax.ShapeDtypeStruct((bsz, s, nq * HEAD_DIM), F32),
        grid=(bsz, nkv, s // tq),
        in_specs=[pl.BlockSpec((1, g, tq, HEAD_DIM), lambda b, h, t: (b, h, t, 0)), kv(k), kv(v)],
        out_specs=pl.BlockSpec((1, tq, g * HEAD_DIM), lambda b, h, t: (b, t, h)),
        compiler_params=_cparams("arbitrary", "arbitrary", "arbitrary"),
        name="dense_attention",
    )(q, k, v)


def _softplus(x):
    return jnp.maximum(x, 0.0) + jnp.log(1.0 + jnp.exp(-jnp.abs(x)))


def _split(a):
    bits = lax.bitcast_convert_type(a, jnp.uint32) & jnp.uint32(0xFFFF0000)
    hi = lax.bitcast_convert_type(bits, F32)
    return hi, a - hi


def _rhs3(b):
    hi, lo = _split(b)
    return jnp.concatenate([hi, hi, lo], axis=-2).astype(BF16)


def _split3(a):
    hi, rest = _split(a)
    mid, lo = _split(rest)
    return hi, mid, lo


def _mm_exact_lhs(tbl3, x):
    return jnp.dot(tbl3, jnp.concatenate(_split3(x), axis=0).astype(BF16), preferred_element_type=F32)


def _mm_exact_rhs(x, tbl3):
    return jnp.dot(jnp.concatenate(_split3(x), axis=-1).astype(BF16), tbl3, preferred_element_type=F32)


def _mm_split(a, b):
    ah, al = _split(a)
    return jnp.dot(jnp.concatenate([ah, al, ah], axis=-1).astype(BF16), _rhs3(b), preferred_element_type=F32)


def _bmm(a, bf):
    return lax.dot_general(a.astype(BF16), bf, (((2,), (1,)), ((0,), (0,))), preferred_element_type=F32)


def _rwkv_par_kernel(z_ref, w0_ref, wup_ref, a0_ref, aup_ref, kk_ref, ka_ref, rk_ref, bd_ref, cum_ref,
                     rp3_ref, yvq_ref, bonus_ref, *, w, lora):
    tc = RW_CHUNK
    nc, nh = ROW_TILE // tc, w // HEAD_DIM
    assert tc == HEAD_DIM
    z = z_ref[0]
    r, k, v = z[:, :w], z[:, w:2 * w], z[:, 2 * w:3 * w]
    w_low = jnp.tanh(z[:, 3 * w:3 * w + lora])
    a_low = z[:, 3 * w + lora:]
    kk = k * kk_ref[...]
    kk = kk / jnp.maximum(jnp.sqrt(_mm_exact_rhs(kk * kk, bd_ref[...])), 1e-12)
    ksum = jnp.zeros_like(k)

    tpos = lax.broadcasted_iota(jnp.int32, (tc, w), 0)
    spos = lax.broadcasted_iota(jnp.int32, (tc, w), 1) % tc
    eye = tpos == spos
    packed = lambda a: a.reshape(nc, tc, w)

    def level_mask(s):
        return jnp.logical_and(tpos // (2 * s) == spos // (2 * s), tpos // s != spos // s)

    n_lane_tiles = w // LANES
    lane_head = lax.broadcasted_iota(jnp.int32, (tc, LANES), 1) // tc

    def head_lanes(h, tile):
        return jnp.where(lane_head == h % (LANES // tc), tile, jnp.zeros_like(tile))

    def blockdiag(m):
        mb = m.astype(BF16)
        zeros = jnp.zeros(mb.shape[:2] + (LANES,), BF16)
        rows = []
        for h in range(nh):
            j = h * tc // LANES
            kept = head_lanes(h, mb[:, :, j * LANES:(j + 1) * LANES])
            rows.append(jnp.concatenate([kept if jj == j else zeros for jj in range(n_lane_tiles)], axis=2))
        return jnp.concatenate(rows, axis=1)

    def own_blocks(full):
        tiles = []
        for j in range(n_lane_tiles):
            heads = [h for h in range(nh) if h * tc // LANES == j]
            tiles.append(sum(head_lanes(h, full[:, h * tc:(h + 1) * tc, j * LANES:(j + 1) * LANES]) for h in heads))
        return jnp.concatenate(tiles, axis=2)

    per_dir = []
    for d in range(2):
        w_log = -_softplus(-(w0_ref[d:d + 1, :] + _mm_split(w_low, wup_ref[d]))) - 0.5
        lw = -jnp.exp(w_log)
        a = 1.0 / (1.0 + jnp.exp(-(a0_ref[d:d + 1, :] + _mm_split(a_low, aup_ref[d]))))
        kd = k * (1.0 + (a - 1.0) * ka_ref[...])
        ksum = ksum + kd
        sums = _mm_exact_lhs(cum_ref[d], lw)
        c, ctot = sums[:ROW_TILE], sums[ROW_TILE:]
        e_neg, e_rem = jnp.exp(-c), jnp.exp(ctot - c)
        b = kk * a
        before = tpos > spos if d == 0 else tpos < spos
        per_dir.append((packed(-kk * jnp.exp(c - lw)), packed(r * jnp.exp(c)), packed(b * e_neg),
                        packed(kd * e_neg), packed(b * e_rem), packed(kd * e_rem), packed(jnp.exp(ctot)),
                        jnp.broadcast_to(before, (nc, tc, w))))
    bonus_ref[0] = _mm_exact_rhs(r * ksum * rk_ref[...], bd_ref[...]) * v

    at_p, rt_p, bh_p, kh_p, bc_p, kc_p, wt_p, before = (jnp.concatenate(parts, axis=0) for parts in zip(*per_dir))
    v_p = jnp.concatenate([packed(v)] * 2, axis=0)
    upto = jnp.logical_or(before, eye)
    big = lax.dot_general(jnp.concatenate([at_p, rt_p], axis=1).astype(BF16),
                          jnp.concatenate([blockdiag(bh_p), blockdiag(kh_p)], axis=1),
                          (((2,), (2,)), ((0,), (0,))), preferred_element_type=F32)
    a_ab = jnp.where(before, big[:, :tc, :w], 0.0)
    a_ak = jnp.where(before, big[:, :tc, w:], 0.0)
    a_rb = jnp.where(upto, big[:, tc:, :w], 0.0)
    a_rk = jnp.where(upto, big[:, tc:, w:], 0.0)
    x = jnp.where(eye, 1.0, jnp.where(level_mask(1), a_ab, 0.0))
    a_ab16 = a_ab.astype(BF16)
    s = 2
    while s < tc:
        half = _bmm(x, blockdiag(jnp.where(level_mask(s), a_ab16, jnp.zeros_like(a_ab16))))
        x = x + _bmm(half, blockdiag(x))
        s *= 2
    akrk = _bmm(jnp.concatenate([a_ak, a_rk], axis=1), blockdiag(v_p))
    xa = _bmm(x, jnp.concatenate([blockdiag(at_p), blockdiag(akrk[:, :tc])], axis=2))
    ra = _bmm(a_rb, jnp.concatenate([blockdiag(xa[:, :, :w]), blockdiag(xa[:, :, w:])], axis=2))
    rp = rt_p + ra[:, :, :w]
    yv = ra[:, :, w:] + akrk[:, tc:]
    lhs_t = jnp.concatenate([bc_p, kc_p], axis=1).astype(BF16)
    rhs_t = jnp.concatenate([xa, jnp.concatenate([jnp.zeros_like(v_p), v_p], axis=2)], axis=1).astype(BF16)
    full = lax.dot_general(lhs_t, rhs_t, (((1,), (1,)), ((0,), (0,))), preferred_element_type=F32)
    p = own_blocks(full[:, :, :w]) + jnp.where(eye, wt_p[:, 0:1, :], 0.0)
    q = own_blocks(full[:, :, w:])
    rp3_ref[0] = jnp.concatenate([rp, p], axis=1).astype(BF16).reshape(2, nc, tc + HEAD_DIM, w)
    yvq_ref[0] = jnp.concatenate([yv, q], axis=1).reshape(2, nc, tc + HEAD_DIM, w)


def _chunk_matrices():
    t = np.arange(ROW_TILE)
    same = (t[:, None] // RW_CHUNK) == (t[None, :] // RW_CHUNK)
    tabs = []
    for run in (same & (t[None, :] <= t[:, None]), same & (t[None, :] >= t[:, None])):
        m = np.concatenate([run, same], axis=0).astype(np.float32).astype(BF16)
        tabs.append(np.concatenate([m, m, m], axis=1))
    return np.stack(tabs)


def _rwkv_par(rw, w0, w_up, a0, a_up, k_k, k_a, r_k, w):
    bsz, s, width = rw.shape
    nt = s // ROW_TILE
    nh, nc = w // HEAD_DIM, ROW_TILE // RW_CHUNK
    lora = w_up.shape[1]
    full = lambda a: pl.BlockSpec(a.shape, lambda b, t: (0,) * a.ndim)
    cum = _chunk_matrices()
    bd = _head_block_diag(w, 1.0)
    vec = lambda a: a.reshape(1, w)
    n_chunks = s // RW_CHUNK
    tc, hd = RW_CHUNK, HEAD_DIM
    consts = (w0, w_up, a0, a_up, vec(k_k), vec(k_a), vec(r_k), bd, cum)
    return pl.pallas_call(
        functools.partial(_rwkv_par_kernel, w=w, lora=lora),
        out_shape=[jax.ShapeDtypeStruct((bsz, 2, n_chunks, tc + hd, w), BF16),
                   jax.ShapeDtypeStruct((bsz, 2, n_chunks, tc + hd, w), F32),
                   jax.ShapeDtypeStruct((bsz, s, w), F32)],
        grid=(bsz, nt),
        in_specs=[pl.BlockSpec((1, ROW_TILE, width), lambda b, t: (b, t, 0))] + [full(a) for a in consts],
        out_specs=[pl.BlockSpec((1, 2, nc, tc + hd, w), lambda b, t: (b, 0, t, 0, 0)),
                   pl.BlockSpec((1, 2, nc, tc + hd, w), lambda b, t: (b, 0, t, 0, 0)),
                   pl.BlockSpec((1, ROW_TILE, w), lambda b, t: (b, t, 0))],
        compiler_params=_cparams("arbitrary", "arbitrary"),
        name="rwkv_chunk_prep",
    )(rw, *consts)


def _rwkv_seq_kernel(rp3f_ref, yvqf_ref, rp3b_ref, yvqb_ref, yf_ref, yb_ref, g_ref):
    @pl.when(pl.program_id(1) == 0)
    def _():
        g_ref[...] = jnp.zeros_like(g_ref)

    w = g_ref.shape[-1]
    tc = RW_CHUNK
    nh = w // HEAD_DIM
    grp = rp3f_ref.shape[2]
    row_head = lax.broadcasted_iota(jnp.int32, (w, w), 0) // HEAD_DIM
    col_head = lax.broadcasted_iota(jnp.int32, (w, w), 1) // HEAD_DIM
    on_diag = row_head == col_head
    for step in range(grp):
        for d, (rp3, yvq, y) in enumerate(((rp3f_ref, yvqf_ref, yf_ref), (rp3b_ref, yvqb_ref, yb_ref))):
            ci = step if d == 0 else grp - 1 - step
            out = jnp.dot(rp3[0, 0, ci], g_ref[d].astype(BF16), preferred_element_type=F32) + yvq[0, 0, ci]
            y[0, ci * tc:(ci + 1) * tc, :] = out[:tc]
            g_ref[d] = jnp.where(on_diag, jnp.concatenate([out[tc:]] * nh, axis=0), 0.0)


RW_SCAN_GROUP = 4


def _rwkv_seq(rp3, yvq, n_lat_chunks):
    bsz, _, n_chunks = rp3.shape[:3]
    w = yvq.shape[-1]
    grp = RW_SCAN_GROUP
    assert n_lat_chunks % grp == 0 and n_chunks % grp == 0
    n_groups, n_lat, n_ctx = n_chunks // grp, n_lat_chunks // grp, (n_chunks - n_lat_chunks) // grp
    order = (lambda i: jnp.where(i < n_ctx, n_lat + i, i - n_ctx),
             lambda i: n_groups - 1 - i)
    blk = lambda d, a: pl.BlockSpec((1, 1, grp) + a.shape[3:], lambda b, i: (b, d, order[d](i), 0, 0))
    out = lambda d: pl.BlockSpec((1, grp * RW_CHUNK, w), lambda b, i: (b, order[d](i), 0))
    shp = jax.ShapeDtypeStruct((bsz, n_chunks * RW_CHUNK, w), F32)
    return pl.pallas_call(
        _rwkv_seq_kernel,
        out_shape=[shp, shp],
        grid=(bsz, n_groups),
        in_specs=[blk(0, rp3), blk(0, yvq), blk(1, rp3), blk(1, yvq)],
        out_specs=[out(0), out(1)],
        scratch_shapes=[pltpu.VMEM((2, w, w), F32)],
        compiler_params=_cparams("arbitrary", "arbitrary"),
        name="rwkv_state_scan",
    )(rp3, yvq, rp3, yvq)


def _rwkv_readout_tile(y, bonus, ln_g, ln_b, head_mean):
    yc = y - _mm_exact_rhs(y, head_mean)
    var = _mm_exact_rhs(yc * yc, head_mean)
    return yc * lax.rsqrt(var + RW_GN_EPS) * ln_g + ln_b + bonus


@functools.lru_cache(maxsize=None)
def _filter_position_features(seq_len, bands, width):
    n = np.arange(2 * seq_len)
    pos = np.where(n < seq_len, n, 2 * seq_len - n) % seq_len
    t = np.linspace(0.0, 1.0, seq_len)[pos][:, None]
    wpos = (2.0 * math.pi / seq_len) * pos[:, None]
    f = np.linspace(1e-4, bands - 1, bands)[None, :]
    z = np.concatenate([t, np.cos(f * wpos), np.sin(f * wpos)], axis=-1)
    return np.pad(z, ((0, 0), (0, width - z.shape[1]))).astype(np.float32)


def _hyena_filter_kernel(z_ref, fw1_ref, fb1_ref, freq_ref, fw2_ref, fb2_ref, fw3_ref, delta_ref, o_ref, *, seq_len):
    freq = freq_ref[...]
    rows = ROW_TILE
    n_orders, _, c = o_ref.shape

    def taps(i, norms):
        blk = pl.ds(pl.multiple_of(i * rows, rows), rows)
        z = z_ref[blk, :]
        h = jnp.sin(freq * (_mm_split(z, fw1_ref[...]) + fb1_ref[...]))
        h = jnp.sin(freq * (_mm_split(h, fw2_ref[...]) + fb2_ref[...]))
        h = _mm_split(h, fw3_ref[...])
        n = i * rows + lax.broadcasted_iota(jnp.int32, (rows, 1), 0)
        decay = jnp.exp(-z[:, 0:1] * jnp.abs(delta_ref[...]))
        out = []
        for o in range(n_orders):
            ho = jnp.where(n < seq_len, h[:, 2 * o * c:(2 * o + 1) * c], h[:, (2 * o + 1) * c:(2 * o + 2) * c])
            ho = ho * decay
            o_ref[o, blk, :] = jnp.where(n != seq_len, ho, 0.0)
            out.append(norms[o] + jnp.sum(jnp.abs(ho), axis=0, keepdims=True))
        return tuple(out)
    zero = jnp.zeros((1, c), F32)
    norms = lax.fori_loop(0, z_ref.shape[0] // rows, taps, (zero,) * n_orders)

    def normalise(i, carry):
        blk = pl.ds(pl.multiple_of(i * rows, rows), rows)
        for o in range(n_orders):
            o_ref[o, blk, :] = o_ref[o, blk, :] / norms[o]
        return carry
    lax.fori_loop(0, z_ref.shape[0] // rows, normalise, 0)


def _hyena_two_sided_filters(seq_len, fw1, fb1, freq, fw2, fb2, fw3, width):
    ffn = fw1.shape[1]
    bands = (fw1.shape[0] - 1) // 2
    z = _filter_position_features(seq_len, bands, ffn)
    fw1p = jnp.pad(fw1, ((0, ffn - fw1.shape[0]), (0, 0)))
    max_decay = math.log(HY_DECAY_TARGET) / HY_FAST_DECAY
    min_decay = math.log(HY_DECAY_TARGET) / HY_SLOW_DECAY
    deltas = jnp.linspace(min_decay, max_decay, width, dtype=F32).reshape(1, width)
    row = lambda a: a.reshape(1, ffn)
    full = lambda shape: pl.BlockSpec(shape, lambda i: (0,) * len(shape))
    return pl.pallas_call(
        functools.partial(_hyena_filter_kernel, seq_len=seq_len),
        out_shape=jax.ShapeDtypeStruct((2, 2 * seq_len, width), F32),
        grid=(1,),
        in_specs=[full(z.shape), full((ffn, ffn)), full((1, ffn)), full((1, ffn)), full((ffn, ffn)),
                  full((1, ffn)), full(fw3.shape), full((1, width))],
        out_specs=full((2, 2 * seq_len, width)),
        compiler_params=_cparams("arbitrary"),
        name="hyena_filter",
    )(z, fw1p, row(fb1), row(freq), fw2, row(fb2), fw3, deltas)


def kernel(x, c, ctx, c_ctx, mod_w, mod_b, norm_g, w_in, w_out, hy_conv, hy_fw1, hy_fb1, hy_freq, hy_fw2,
           hy_fb2, hy_fw3, hy_bias, rw_conv, rw_w0, rw_w_up, rw_a0, rw_a_up, rw_k_k, rw_k_a, rw_r_k,
           rw_ln_g, rw_ln_b, wa_sink, fa_q_norm, fa_k_norm, final_g):
    bsz, seq_len, d = x.shape
    ctx_len = ctx.shape[1]
    depth = w_in.shape[0]
    w_hy = hy_bias.shape[-1]
    w_rw = rw_w0.shape[-1]
    n_wa_heads = wa_sink.shape[-1]
    w_q = n_wa_heads * HEAD_DIM
    w_kv = w_q // 2
    lora = rw_w_up.shape[2] + rw_a_up.shape[2]
    branch_w = (3 * w_hy, 3 * w_rw + lora, w_q + 2 * w_kv, w_q + 2 * w_kv)
    gate_w = (w_hy, w_rw, w_q, w_q)
    assert seq_len % ROW_TILE == 0 and ctx_len % ROW_TILE == 0 and bsz % 2 == 0
    n_lat_tiles = seq_len // ROW_TILE

    splits = tuple(wd for pair in zip(branch_w, gate_w) for wd in pair)
    assert all(wd % LANES == 0 for wd in splits)
    w_in_b = w_in.astype(BF16)
    w_out_b = w_out.astype(BF16)

    pad_rows = (-(bsz + 1)) % SUBLANES
    cond = jnp.concatenate([c, c_ctx[None], jnp.zeros((pad_rows, d), F32)], axis=0)
    mod = _modulation(cond, mod_w, mod_b)
    mod_lat = mod[:, :bsz].reshape(depth, bsz, 3, d)
    mod_ctx = jnp.broadcast_to(mod[:, bsz].reshape(depth, 1, 3, d), (depth, bsz, 3, d))
    mods = jnp.stack([mod_lat, mod_ctx], axis=2)

    rope = _rope_tables(seq_len, ctx_len)
    hy_group = 2 * seq_len // FFT_N1
    assert ctx_len % hy_group == 0 and ROW_TILE % hy_group == 0
    xs = jnp.concatenate([x, ctx], axis=1)
    for l in range(depth):
        last = l == depth - 1
        hyc, rwc, gates, qw, kw, vw, qf, kf, vf = _inproj(
            xs, mods[l], norm_g[l], w_in_b[l], hy_conv[l], rw_conv[l], splits, n_lat_tiles, rope,
            fa_q_norm[l], fa_k_norm[l], w_q, w_kv, hy_group)

        filt = functools.partial(_hyena_two_sided_filters, fw1=hy_fw1[l], fb1=hy_fb1[l], freq=hy_freq[l],
                                 fw2=hy_fw2[l], fb2=hy_fb2[l], fw3=hy_fw3[l], width=w_hy)
        bias = hy_bias[l].reshape(2, 1, w_hy)
        spec = _filter_spectrum(filt(seq_len))
        nt_hy = w_hy // LANES
        y1 = _fftconv_gated(hyc, 0, hyc, nt_hy, spec, 0, bias, seq_len)
        a_lat = _fftconv_gated(y1, 0, hyc, 2 * nt_hy, spec, 1, bias, seq_len)
        a_ctx = (a_lat[:, :ctx_len] if last else
                 _ctx_hyena(hyc, seq_len // ctx_len, filt(ctx_len), bias, ctx_len, w_hy, hy_group))

        rp3, yvq, bonus = _rwkv_par(rwc, rw_w0[l], rw_w_up[l], rw_a0[l], rw_a_up[l], rw_k_k[l], rw_k_a[l],
                                    rw_r_k[l], w_rw)
        yf, yb = _rwkv_seq(rp3, yvq, seq_len // RW_CHUNK)

        c_mix = _window_attention(qw, kw, vw, wa_sink[l], seq_len, ctx_len)
        d_mix = _dense_attention(qf, kf, vf, seq_len)

        xs = _outproj(a_lat, a_ctx, yf, yb, bonus, rw_ln_g[l], rw_ln_b[l], c_mix, d_mix, gates, xs,
                      mods[l], w_out_b[l], final_g, n_lat_tiles, last, hy_group)
    return xs
```
